```python
import jax, jax.numpy as jnp
from jax import lax
import numpy as np

D_MODEL = 1024
BATCH = 8
SEQ = 16384
DEPTH = 4

HG_HEADS = 8
HG_DK = 128
HG_DV = 128
HG_WIDTH = HG_HEADS * HG_DK
HG_CHUNK = 64
ATT_Q_HEADS = 16
ATT_KV_HEADS = 4
ATT_GROUP = ATT_Q_HEADS // ATT_KV_HEADS
ATT_HEAD_DIM = 64
ATT_WIDTH = ATT_Q_HEADS * ATT_HEAD_DIM
ATT_KV_WIDTH = ATT_KV_HEADS * ATT_HEAD_DIM
WINDOW = 128
ATT_BLOCK = 128
ROPE_THETA = 500000.0
ROPE_DIM = ATT_HEAD_DIM // 4
FFN_HIDDEN = ((8 * D_MODEL // 3 + 255) // 256) * 256
EPS = 1e-6
MIN_F = 1e-30
IN_SIZES = (HG_WIDTH, HG_WIDTH, HG_WIDTH, HG_WIDTH, ATT_WIDTH, ATT_KV_WIDTH, ATT_KV_WIDTH, D_MODEL, D_MODEL)
IN_COLS = 4 * HG_WIDTH + ATT_WIDTH + 2 * ATT_KV_WIDTH + 2 * D_MODEL

kernel_name = "hgrn2_swa_sink_gated_hybrid"


def rmsnorm(x, g):
    xf = x.astype(jnp.float32)
    y = xf * lax.rsqrt(jnp.mean(xf * xf, axis=-1, keepdims=True) + EPS)
    return (y * g.astype(jnp.float32)).astype(x.dtype)


def split_cols(proj):
    points, acc = [], 0
    for s in IN_SIZES[:-1]:
        acc += s
        points.append(acc)
    return jnp.split(proj, points, axis=-1)


def rope_partial(x, pos):
    half = ROPE_DIM // 2
    inv = ROPE_THETA ** (-jnp.arange(half, dtype=jnp.float32) * 2.0 / ROPE_DIM)
    ang = pos.astype(jnp.float32)[:, None] * inv[None, :]
    cos = jnp.cos(ang)[None, :, None, :]
    sin = jnp.sin(ang)[None, :, None, :]
    xr = x[..., :ROPE_DIM].astype(jnp.float32)
    x1, x2 = xr[..., :half], xr[..., half:]
    rot = jnp.concatenate([x1 * cos - x2 * sin, x2 * cos + x1 * sin], axis=-1)
    return jnp.concatenate([rot.astype(x.dtype), x[..., ROPE_DIM:]], axis=-1)


def hgrn2_chunked(q, k, v, logf):
    B, S, H, DK = q.shape
    DV = v.shape[-1]
    C = HG_CHUNK
    n = S // C

    def to_chunks(t):
        return t.reshape(B, n, C, H, t.shape[-1]).transpose(1, 0, 3, 2, 4)

    qc, kc, vc, fc = to_chunks(q), to_chunks(k), to_chunks(v), to_chunks(logf)
    causal = jnp.tril(jnp.ones((C, C), dtype=bool))

    def step(state, inp):
        qi, ki, vi, fi = inp
        b = jnp.cumsum(fi, axis=2)
        o_inter = jnp.einsum('bhtk,bhkv->bhtv', qi * jnp.exp(b), state)
        diff = b[:, :, :, None, :] - b[:, :, None, :, :]
        decay = jnp.exp(jnp.where(causal[:, :, None], diff, -jnp.inf))
        scores = jnp.einsum('bhtk,bhsk,bhtsk->bhts', qi, ki, decay)
        o_intra = jnp.einsum('bhts,bhsv->bhtv', scores, vi)
        b_last = b[:, :, -1:, :]
        new_state = jnp.exp(b_last[:, :, 0, :])[..., None] * state + jnp.einsum(
            'bhsk,bhsv->bhkv', ki * jnp.exp(b_last - b), vi)
        return new_state, o_inter + o_intra

    init = jnp.zeros((B, H, DK, DV), jnp.float32)
    _, o = lax.scan(step, init, (qc, kc, vc, fc))
    return o.transpose(1, 0, 3, 2, 4).reshape(B, S, H, DV)


def swa_with_sinks(q, k, v, sinks):
    B, S, Hq, hd = q.shape
    W = ATT_BLOCK
    n = S // W
    qb = q.reshape(B, n, W, ATT_KV_HEADS, ATT_GROUP, hd)

    def with_prev(t):
        tb = t.reshape(B, n, W, ATT_KV_HEADS, hd)
        prev = jnp.pad(tb, ((0, 0), (1, 0), (0, 0), (0, 0), (0, 0)))[:, :-1]
        return jnp.concatenate([prev, tb], axis=2)

    kw, vw = with_prev(k), with_prev(v)
    s = jnp.einsum('bnqhgd,bnshd->bnhgqs', qb, kw).astype(jnp.float32) * (hd ** -0.5)
    blk = jnp.arange(n)[:, None, None] * W
    qpos = blk + jnp.arange(W)[None, :, None]
    kpos = blk - W + jnp.arange(2 * W)[None, None, :]
    delta = qpos - kpos
    mask = (delta >= 0) & (delta < WINDOW) & (kpos >= 0)
    s = jnp.where(mask[None, :, None, None], s, -jnp.inf)
    sink = sinks.astype(jnp.float32).reshape(1, 1, ATT_KV_HEADS, ATT_GROUP, 1, 1)
    m = jnp.maximum(jnp.max(s, axis=-1, keepdims=True), sink)
    p = jnp.exp(s - m)
    p = p / (jnp.sum(p, axis=-1, keepdims=True) + jnp.exp(sink - m))
    o = jnp.einsum('bnhgqs,bnshd->bnqhgd', p.astype(v.dtype), vw)
    return o.reshape(B, S, Hq * hd)


def _fwd_setup_inputs(seed: int = 0) -> dict:
    key = jax.random.key(seed)
    ks = jax.random.split(key, 16)
    f32 = jnp.float32
    nrm = lambda k, shape, scale: jax.random.normal(k, shape, f32) * scale
    return {
        "x": nrm(ks[0], (BATCH, SEQ, D_MODEL), 1.0),
        "norm1": 1.0 + nrm(ks[1], (DEPTH, D_MODEL), 0.01),
        "w_in": nrm(ks[2], (DEPTH, D_MODEL, IN_COLS), D_MODEL ** -0.5),
        "lb_logits": nrm(ks[3], (DEPTH, HG_WIDTH), 0.5),
        "hg_norm": 1.0 + nrm(ks[4], (DEPTH, HG_WIDTH), 0.01),
        "attn_sinks": nrm(ks[5], (DEPTH, ATT_Q_HEADS), 0.5),
        "w_pa": nrm(ks[6], (DEPTH, HG_WIDTH, D_MODEL), HG_WIDTH ** -0.5),
        "w_pb": nrm(ks[7], (DEPTH, ATT_WIDTH, D_MODEL), ATT_WIDTH ** -0.5),
        "w_o": nrm(ks[8], (DEPTH, D_MODEL, D_MODEL), D_MODEL ** -0.5),
        "norm2": 1.0 + nrm(ks[9], (DEPTH, D_MODEL), 0.01),
        "w_gate": nrm(ks[10], (DEPTH, D_MODEL, FFN_HIDDEN), D_MODEL ** -0.5),
        "w_up": nrm(ks[11], (DEPTH, D_MODEL, FFN_HIDDEN), D_MODEL ** -0.5),
        "w_down": nrm(ks[12], (DEPTH, FFN_HIDDEN, D_MODEL), FFN_HIDDEN ** -0.5),
        "final_norm": 1.0 + nrm(ks[13], (D_MODEL,), 0.01),
    }


def _fwd_reference(x, norm1, w_in, lb_logits, hg_norm, attn_sinks, w_pa, w_pb, w_o,
              norm2, w_gate, w_up, w_down, final_norm):
    B, S, _ = x.shape
    pos = jnp.arange(S)
    lb_p = jax.nn.softmax(lb_logits.astype(jnp.float32), axis=0)
    lb_all = jnp.cumsum(lb_p, axis=0) - lb_p[0:1]

    for l in range(DEPTH):
        h = rmsnorm(x, norm1[l])
        proj = h @ w_in[l]
        hq, hf, hi, hg, aq, ak, av, ga, gb = split_cols(proj)

        q = jax.nn.silu(hq).reshape(B, S, HG_HEADS, HG_DK).astype(jnp.float32)
        z = hf.reshape(B, S, HG_HEADS, HG_DK).astype(jnp.float32)
        lb = lb_all[l].reshape(HG_HEADS, HG_DK)
        f = lb + (1.0 - lb) * jax.nn.sigmoid(z)
        logf = jnp.log(jnp.maximum(f, MIN_F))
        kk = 1.0 - f
        vi = hi.reshape(B, S, HG_HEADS, HG_DV).astype(jnp.float32)
        o_hg = hgrn2_chunked(q, kk, vi, logf)
        o_hg = rmsnorm(o_hg, hg_norm[l].reshape(HG_HEADS, HG_DV)).astype(x.dtype)
        o_hg = o_hg.reshape(B, S, HG_WIDTH) * jax.nn.silu(hg)
        y_a = o_hg @ w_pa[l]

        qa = rope_partial(aq.reshape(B, S, ATT_Q_HEADS, ATT_HEAD_DIM), pos)
        ka = rope_partial(ak.reshape(B, S, ATT_KV_HEADS, ATT_HEAD_DIM), pos)
        va = av.reshape(B, S, ATT_KV_HEADS, ATT_HEAD_DIM)
        y_b = swa_with_sinks(qa, ka, va, attn_sinks[l]) @ w_pb[l]

        mix = jax.nn.sigmoid(ga) * y_a + jax.nn.sigmoid(gb) * y_b
        x = x + mix @ w_o[l]

        h2 = rmsnorm(x, norm2[l])
        x = x + (jax.nn.silu(h2 @ w_gate[l]) * (h2 @ w_up[l])) @ w_down[l]

    return rmsnorm(x, final_norm)


import jax as _jax
import jax.numpy as _jnp

TWIN_FORMAT = 'train_step'
FWD_PARAMS = ['x', 'norm1', 'w_in', 'lb_logits', 'hg_norm', 'attn_sinks', 'w_pa', 'w_pb', 'w_o', 'norm2', 'w_gate', 'w_up', 'w_down', 'final_norm']
TWIN_WEIGHTS = ['norm1', 'w_in', 'lb_logits', 'hg_norm', 'attn_sinks', 'w_pa', 'w_pb', 'w_o', 'norm2', 'w_gate', 'w_up', 'w_down', 'final_norm']
TWIN_DIFF_INPUT = 'x'
TWIN_INPUTS = ['x', 'norm1', 'w_in', 'lb_logits', 'hg_norm', 'attn_sinks', 'w_pa', 'w_pb', 'w_o', 'norm2', 'w_gate', 'w_up', 'w_down', 'final_norm', 'loss_target', 'm_norm1', 'm_w_in', 'm_lb_logits', 'm_hg_norm', 'm_attn_sinks', 'm_w_pa', 'm_w_pb', 'm_w_o', 'm_norm2', 'm_w_gate', 'm_w_up', 'm_w_down', 'm_final_norm', 'v_norm1', 'v_w_in', 'v_lb_logits', 'v_hg_norm', 'v_attn_sinks', 'v_w_pa', 'v_w_pb', 'v_w_o', 'v_norm2', 'v_w_gate', 'v_w_up', 'v_w_down', 'v_final_norm']
TWIN_OUTPUTS = ['loss', 'grad_x', 'grad_norm1', 'grad_w_in', 'grad_lb_logits', 'grad_hg_norm', 'grad_attn_sinks', 'grad_w_pa', 'grad_w_pb', 'grad_w_o', 'grad_norm2', 'grad_w_gate', 'grad_w_up', 'grad_w_down', 'grad_final_norm', 'delta_norm1', 'delta_w_in', 'delta_lb_logits', 'delta_hg_norm', 'delta_attn_sinks', 'delta_w_pa', 'delta_w_pb', 'delta_w_o', 'delta_norm2', 'delta_w_gate', 'delta_w_up', 'delta_w_down', 'delta_final_norm', 'new_m_norm1', 'new_m_w_in', 'new_m_lb_logits', 'new_m_hg_norm', 'new_m_attn_sinks', 'new_m_w_pa', 'new_m_w_pb', 'new_m_w_o', 'new_m_norm2', 'new_m_w_gate', 'new_m_w_up', 'new_m_w_down', 'new_m_final_norm', 'new_v_norm1', 'new_v_w_in', 'new_v_lb_logits', 'new_v_hg_norm', 'new_v_attn_sinks', 'new_v_w_pa', 'new_v_w_pb', 'new_v_w_o', 'new_v_norm2', 'new_v_w_gate', 'new_v_w_up', 'new_v_w_down', 'new_v_final_norm']
TWIN_LEAF_KINDS = {'loss': 'loss', 'grad_x': 'grad_x', 'grad_norm1': 'grad_w', 'grad_w_in': 'grad_w', 'grad_lb_logits': 'grad_w', 'grad_hg_norm': 'grad_w', 'grad_attn_sinks': 'grad_w', 'grad_w_pa': 'grad_w', 'grad_w_pb': 'grad_w', 'grad_w_o': 'grad_w', 'grad_norm2': 'grad_w', 'grad_w_gate': 'grad_w', 'grad_w_up': 'grad_w', 'grad_w_down': 'grad_w', 'grad_final_norm': 'grad_w', 'delta_norm1': 'delta_w', 'delta_w_in': 'delta_w', 'delta_lb_logits': 'delta_w', 'delta_hg_norm': 'delta_w', 'delta_attn_sinks': 'delta_w', 'delta_w_pa': 'delta_w', 'delta_w_pb': 'delta_w', 'delta_w_o': 'delta_w', 'delta_norm2': 'delta_w', 'delta_w_gate': 'delta_w', 'delta_w_up': 'delta_w', 'delta_w_down': 'delta_w', 'delta_final_norm': 'delta_w', 'new_m_norm1': 'new_m', 'new_m_w_in': 'new_m', 'new_m_lb_logits': 'new_m', 'new_m_hg_norm': 'new_m', 'new_m_attn_sinks': 'new_m', 'new_m_w_pa': 'new_m', 'new_m_w_pb': 'new_m', 'new_m_w_o': 'new_m', 'new_m_norm2': 'new_m', 'new_m_w_gate': 'new_m', 'new_m_w_up': 'new_m', 'new_m_w_down': 'new_m', 'new_m_final_norm': 'new_m', 'new_v_norm1': 'new_v', 'new_v_w_in': 'new_v', 'new_v_lb_logits': 'new_v', 'new_v_hg_norm': 'new_v', 'new_v_attn_sinks': 'new_v', 'new_v_w_pa': 'new_v', 'new_v_w_pb': 'new_v', 'new_v_w_o': 'new_v', 'new_v_norm2': 'new_v', 'new_v_w_gate': 'new_v', 'new_v_w_up': 'new_v', 'new_v_w_down': 'new_v', 'new_v_final_norm': 'new_v'}


def _forward(args):
    return _fwd_reference(*[args[k] for k in FWD_PARAMS])


def _output_shape():
    def fwd():
        inp = _fwd_setup_inputs(0)
        return _fwd_reference(*[inp[k] for k in FWD_PARAMS])
    out = _jax.eval_shape(fwd)
    return out.shape, out.dtype

N_MICROBATCH = 1
ADAM_LR = 0.001
ADAM_B1 = 0.9
ADAM_B2 = 0.999
ADAM_EPS = 1e-08
ADAM_WD = 0.01
ADAM_STEP = 10
PER_EXAMPLE_BATCH_AXIS = {'x': 0, 'loss_target': 0}
SHARED_INPUTS = []
_WEIGHT_DTYPES = {'norm1': _jnp.float32, 'w_in': _jnp.float32, 'lb_logits': _jnp.float32, 'hg_norm': _jnp.float32, 'attn_sinks': _jnp.float32, 'w_pa': _jnp.float32, 'w_pb': _jnp.float32, 'w_o': _jnp.float32, 'norm2': _jnp.float32, 'w_gate': _jnp.float32, 'w_up': _jnp.float32, 'w_down': _jnp.float32, 'final_norm': _jnp.float32}
MOMENT_SCALE = {'norm1': 1.936208e-01, 'w_in': 7.088908e-02, 'lb_logits': 9.406989e-03, 'hg_norm': 1.196536e-01, 'attn_sinks': 3.545681e-02, 'w_pa': 1.219423e-01, 'w_pb': 4.611003e-02, 'w_o': 1.291890e-01, 'norm2': 2.591660e-01, 'w_gate': 1.124224e-01, 'w_up': 1.089777e-01, 'w_down': 1.808605e-01, 'final_norm': 1.278566e+02}


def _to_microbatches(a, axis):
    t = _jnp.moveaxis(a, axis, 0)
    t = t.reshape((N_MICROBATCH, t.shape[0] // N_MICROBATCH) + t.shape[1:])
    return _jnp.moveaxis(t, 1, axis + 1)


def setup_inputs(seed: int = 0) -> dict:
    inp = _fwd_setup_inputs(seed)
    key = _jax.random.fold_in(_jax.random.key(seed), 7919)
    shape, _ = _output_shape()
    out = dict(inp)
    out["loss_target"] = _jax.random.normal(_jax.random.fold_in(key, 0), shape, _jnp.float32)
    for i, name in enumerate(TWIN_WEIGHTS):
        w = inp[name].astype(_jnp.float32)
        if MOMENT_SCALE is None:
            s = _jnp.sqrt(_jnp.mean(_jnp.square(w)) + 1e-30)
        else:
            s = MOMENT_SCALE[name]
        km, kv = _jax.random.split(_jax.random.fold_in(key, i + 1))
        out[name] = w
        out["m_" + name] = s * _jax.random.normal(km, w.shape, _jnp.float32)
        out["v_" + name] = (s * s) * _jax.random.uniform(kv, w.shape, _jnp.float32, 0.5, 1.5)
    if N_MICROBATCH > 1:
        for name, axis in PER_EXAMPLE_BATCH_AXIS.items():
            out[name] = _to_microbatches(out[name], axis)
    return {'x': out['x'], 'norm1': out['norm1'], 'w_in': out['w_in'], 'lb_logits': out['lb_logits'], 'hg_norm': out['hg_norm'], 'attn_sinks': out['attn_sinks'], 'w_pa': out['w_pa'], 'w_pb': out['w_pb'], 'w_o': out['w_o'], 'norm2': out['norm2'], 'w_gate': out['w_gate'], 'w_up': out['w_up'], 'w_down': out['w_down'], 'final_norm': out['final_norm'], 'loss_target': out['loss_target'], 'm_norm1': out['m_norm1'], 'm_w_in': out['m_w_in'], 'm_lb_logits': out['m_lb_logits'], 'm_hg_norm': out['m_hg_norm'], 'm_attn_sinks': out['m_attn_sinks'], 'm_w_pa': out['m_w_pa'], 'm_w_pb': out['m_w_pb'], 'm_w_o': out['m_w_o'], 'm_norm2': out['m_norm2'], 'm_w_gate': out['m_w_gate'], 'm_w_up': out['m_w_up'], 'm_w_down': out['m_w_down'], 'm_final_norm': out['m_final_norm'], 'v_norm1': out['v_norm1'], 'v_w_in': out['v_w_in'], 'v_lb_logits': out['v_lb_logits'], 'v_hg_norm': out['v_hg_norm'], 'v_attn_sinks': out['v_attn_sinks'], 'v_w_pa': out['v_w_pa'], 'v_w_pb': out['v_w_pb'], 'v_w_o': out['v_w_o'], 'v_norm2': out['v_norm2'], 'v_w_gate': out['v_w_gate'], 'v_w_up': out['v_w_up'], 'v_w_down': out['v_w_down'], 'v_final_norm': out['v_final_norm']}


def _loss(weights, diff, rest, loss_target):
    with _jax.named_scope("forward"):
        args = {**rest, TWIN_DIFF_INPUT: diff, **{k: w.astype(_WEIGHT_DTYPES[k]) for k, w in weights.items()}}
        y = _forward(args)
    with _jax.named_scope("loss_head"):
        err = _jnp.square(y.astype(_jnp.float32) - loss_target)
        return 0.5 * _jnp.sum(_jnp.mean(err, axis=-1)) if err.ndim else 0.5 * err


def _adamw(w, g, m, v):
    m = ADAM_B1 * m + (1.0 - ADAM_B1) * g
    v = ADAM_B2 * v + (1.0 - ADAM_B2) * _jnp.square(g)
    m_hat = m / (1.0 - ADAM_B1 ** ADAM_STEP)
    v_hat = v / (1.0 - ADAM_B2 ** ADAM_STEP)
    delta = -ADAM_LR * (m_hat / (_jnp.sqrt(v_hat) + ADAM_EPS) + ADAM_WD * w)
    return delta, m, v


def reference(x, norm1, w_in, lb_logits, hg_norm, attn_sinks, w_pa, w_pb, w_o, norm2, w_gate, w_up, w_down, final_norm, loss_target, m_norm1, m_w_in, m_lb_logits, m_hg_norm, m_attn_sinks, m_w_pa, m_w_pb, m_w_o, m_norm2, m_w_gate, m_w_up, m_w_down, m_final_norm, v_norm1, v_w_in, v_lb_logits, v_hg_norm, v_attn_sinks, v_w_pa, v_w_pb, v_w_o, v_norm2, v_w_gate, v_w_up, v_w_down, v_final_norm):
    given = dict(x=x, norm1=norm1, w_in=w_in, lb_logits=lb_logits, hg_norm=hg_norm, attn_sinks=attn_sinks, w_pa=w_pa, w_pb=w_pb, w_o=w_o, norm2=norm2, w_gate=w_gate, w_up=w_up, w_down=w_down, final_norm=final_norm, loss_target=loss_target, m_norm1=m_norm1, m_w_in=m_w_in, m_lb_logits=m_lb_logits, m_hg_norm=m_hg_norm, m_attn_sinks=m_attn_sinks, m_w_pa=m_w_pa, m_w_pb=m_w_pb, m_w_o=m_w_o, m_norm2=m_norm2, m_w_gate=m_w_gate, m_w_up=m_w_up, m_w_down=m_w_down, m_final_norm=m_final_norm, v_norm1=v_norm1, v_w_in=v_w_in, v_lb_logits=v_lb_logits, v_hg_norm=v_hg_norm, v_attn_sinks=v_attn_sinks, v_w_pa=v_w_pa, v_w_pb=v_w_pb, v_w_o=v_w_o, v_norm2=v_norm2, v_w_gate=v_w_gate, v_w_up=v_w_up, v_w_down=v_w_down, v_final_norm=v_final_norm)
    weights = {n: given[n] for n in TWIN_WEIGHTS}
    shared = {n: given[n] for n in SHARED_INPUTS}
    per_example = {n: given[n] for n in ['x']}
    grad_fn = _jax.value_and_grad(_loss, argnums=(0, 1))

    def one_microbatch(ex, loss_target):
        ex = dict(ex)
        diff = ex.pop(TWIN_DIFF_INPUT)
        return grad_fn(weights, diff, {**shared, **ex}, loss_target)

    if N_MICROBATCH == 1:
        loss, (grad_w, grad_x) = one_microbatch(per_example, given["loss_target"])
    else:
        def body(carry, xs):
            loss_sum, grad_sum = carry
            l_k, (gw_k, gx_k) = one_microbatch(xs[0], xs[1])
            with _jax.named_scope("update"):
                return (loss_sum + l_k, _jax.tree.map(_jnp.add, grad_sum, gw_k)), gx_k

        init = (_jnp.zeros((), _jnp.float32), _jax.tree.map(_jnp.zeros_like, weights))
        (loss, grad_w), grad_x = _jax.lax.scan(body, init, (per_example, given["loss_target"]))
    with _jax.named_scope("update"):
        delta_w, new_m, new_v = {}, {}, {}
        for n in TWIN_WEIGHTS:
            delta_w[n], new_m[n], new_v[n] = _adamw(weights[n], grad_w[n], given["m_" + n], given["v_" + n])
    return (loss, grad_x, *[grad_w[n] for n in TWIN_WEIGHTS], *[delta_w[n] for n in TWIN_WEIGHTS],
            *[new_m[n] for n in TWIN_WEIGHTS], *[new_v[n] for n in TWIN_WEIGHTS])
```

```python
import functools

import jax
import jax.numpy as jnp
from jax import lax
from jax.experimental import pallas as pl
from jax.experimental.pallas import tpu as pltpu

F32, BF16 = jnp.float32, jnp.bfloat16

D_MODEL = 1024
DEPTH = 4
N_DEV = 8
HG_HEADS = 8
HG_DK = 128
HG_CHUNK = 64
HG_SUB = 16
HG_BLOCK = 256
HG_EXP_CLAMP = 60.0
ATT_Q_HEADS = 16
ATT_HEAD_DIM = 64
ATT_BLOCK = 128
ROPE_THETA = 500000.0
ROPE_DIM = 16
FFN_HIDDEN = 2816
EPS = 1e-6
MIN_F = 1e-30
ADAM_LR, ADAM_B1, ADAM_B2, ADAM_EPS, ADAM_WD, ADAM_STEP = 0.001, 0.9, 0.999, 1e-08, 0.01, 10

COL_HQ, COL_HF, COL_HI, COL_HG = 0, 1024, 2048, 3072
COL_AQ, COL_AK, COL_AV, COL_GA, COL_GB = 4096, 5120, 5376, 5632, 6656
IN_COLS = 7680

SHARD_ROWS = (960, 128, 128, 128, 352, 352, 352)
SLOT_OFF = (0, 960, 1088, 1216, 1344, 1696, 2048)
SLOT_ROWS = 2400
SMALL_ROWS = 24

VMEM_LIMIT_BYTES = 56 * 1024 * 1024

NN = ((1,), (0,))
NT = ((1,), (1,))
TN = ((0,), (0,))
HIGHEST = lax.Precision.HIGHEST


def _dot(a, b, dims, precision=None):
    return lax.dot_general(a, b, (dims, ((), ())), preferred_element_type=F32, precision=precision)


def _cp(*sem):
    return pltpu.CompilerParams(dimension_semantics=sem if sem else None, vmem_limit_bytes=VMEM_LIMIT_BYTES)


def _sigmoid(x):
    return 1.0 / (1.0 + jnp.exp(-x))


def _matmul_nt(a, w, row_off, n, out_dtype, name, tm=1024, tn=512):
    t, k = a.shape
    tm = min(tm, t)
    assert n % tn == 0 and row_off % tn == 0 and t % tm == 0

    def body(a_ref, w_ref, o_ref):
        o_ref[...] = _dot(a_ref[...].astype(BF16), w_ref[...], NT).astype(o_ref.dtype)

    return pl.pallas_call(
        body, name=name, grid=(n // tn, t // tm),
        in_specs=[pl.BlockSpec((tm, k), lambda j, i: (i, 0)),
                  pl.BlockSpec((tn, k), lambda j, i: (row_off // tn + j, 0))],
        out_specs=pl.BlockSpec((tm, tn), lambda j, i: (i, j)),
        out_shape=jax.ShapeDtypeStruct((t, n), out_dtype),
        compiler_params=_cp("parallel", "parallel"))(a, w)


def _matmul_nn(a, w, row_off, res, name, tm=512, tk=None):
    t, k = a.shape
    n = w.shape[1]
    tm = min(tm, t)
    tk = tk or k
    nk = k // tk
    assert k % tk == 0 and row_off % tk == 0 and t % tm == 0

    def body(*refs):
        if res is None:
            a_ref, w_ref, o_ref, acc = refs
        else:
            a_ref, w_ref, r_ref, o_ref, acc = refs
        kk = pl.program_id(1)
        part = _dot(a_ref[...].astype(BF16), w_ref[...], NN)

        @pl.when(kk == 0)
        def _():
            acc[...] = part

        @pl.when(kk > 0)
        def _():
            acc[...] += part

        @pl.when(kk == nk - 1)
        def _():
            o_ref[...] = acc[...] if res is None else acc[...] + r_ref[...]

    in_specs = [pl.BlockSpec((tm, tk), lambda i, kk: (i, kk)),
                pl.BlockSpec((tk, n), lambda i, kk: (row_off // tk + kk, 0))]
    args = [a, w]
    if res is not None:
        in_specs.append(pl.BlockSpec((tm, n), lambda i, kk: (i, 0)))
        args.append(res)
    return pl.pallas_call(
        body, name=name, grid=(t // tm, nk), in_specs=in_specs,
        out_specs=pl.BlockSpec((tm, n), lambda i, kk: (i, 0)),
        out_shape=jax.ShapeDtypeStruct((t, n), F32),
        scratch_shapes=[pltpu.VMEM((tm, n), F32)],
        compiler_params=_cp("parallel", "arbitrary"))(*args)


def _matmul_tn(a, b, name, tm=512, tk=1024):
    t, m = a.shape
    n = b.shape[1]
    tk = min(tk, t)
    assert m % tm == 0 and t % tk == 0

    def body(a_ref, b_ref, o_ref):
        kk = pl.program_id(1)
        part = _dot(a_ref[...].astype(BF16), b_ref[...].astype(BF16), TN)

        @pl.when(kk == 0)
        def _():
            o_ref[...] = part

        @pl.when(kk > 0)
        def _():
            o_ref[...] += part

    return pl.pallas_call(
        body, name=name, grid=(m // tm, t // tk),
        in_specs=[pl.BlockSpec((tk, tm), lambda i, kk: (kk, i)),
                  pl.BlockSpec((tk, n), lambda i, kk: (kk, 0))],
        out_specs=pl.BlockSpec((tm, n), lambda i, kk: (i, 0)),
        out_shape=jax.ShapeDtypeStruct((m, n), F32),
        compiler_params=_cp("parallel", "arbitrary"))(a, b)


def _rms(x, g):
    return x * lax.rsqrt(jnp.mean(x * x, axis=-1, keepdims=True) + EPS) * g


def _rms_fwd(x, g, name, tm=512):
    t, d = x.shape
    tm = min(tm, t)

    def body(x_ref, g_ref, o_ref):
        o_ref[...] = _rms(x_ref[...], g_ref[...]).astype(BF16)

    return pl.pallas_call(
        body, name=name, grid=(t // tm,),
        in_specs=[pl.BlockSpec((tm, d), lambda i: (i, 0)), pl.BlockSpec((1, d), lambda i: (0, 0))],
        out_specs=pl.BlockSpec((tm, d), lambda i: (i, 0)),
        out_shape=jax.ShapeDtypeStruct((t, d), BF16),
        compiler_params=_cp("parallel"))(x, g)


def _rms_bwd(x, g, dh, dres, name, tm=512):
    t, d = x.shape
    tm = min(tm, t)

    def body(x_ref, g_ref, dh_ref, dres_ref, dx_ref, dg_ref):
        _, vjp = jax.vjp(_rms, x_ref[...], g_ref[...])
        dx, dg = vjp(dh_ref[...])
        dx_ref[...] = dres_ref[...] + dx

        @pl.when(pl.program_id(0) == 0)
        def _():
            dg_ref[...] = jnp.zeros_like(dg_ref)

        dg_ref[...] += dg

    row = pl.BlockSpec((tm, d), lambda i: (i, 0))
    vec = pl.BlockSpec((1, d), lambda i: (0, 0))
    return pl.pallas_call(
        body, name=name, grid=(t // tm,), in_specs=[row, vec, row, row], out_specs=[row, vec],
        out_shape=[jax.ShapeDtypeStruct((t, d), F32), jax.ShapeDtypeStruct((1, d), F32)],
        compiler_params=_cp("arbitrary"))(x, g, dh, dres)


def _mix(ya, yb, ga, gb):
    return _sigmoid(ga) * ya + _sigmoid(gb) * yb


def _gate_specs(tm):
    half = D_MODEL // 2
    return [pl.BlockSpec((tm, half), lambda i, c=c: (i, c))
            for c in (COL_GA // half, COL_GA // half + 1, COL_GB // half, COL_GB // half + 1)]


def _mix_fwd(ya, yb, proj, name, tm=256):
    t, d = ya.shape
    tm = min(tm, t)

    def body(ya_ref, yb_ref, ga0, ga1, gb0, gb1, o_ref):
        ga = jnp.concatenate([ga0[...], ga1[...]], axis=1)
        gb = jnp.concatenate([gb0[...], gb1[...]], axis=1)
        o_ref[...] = _mix(ya_ref[...], yb_ref[...], ga, gb).astype(BF16)

    row = pl.BlockSpec((tm, d), lambda i: (i, 0))
    return pl.pallas_call(
        body, name=name, grid=(t // tm,), in_specs=[row, row] + _gate_specs(tm), out_specs=row,
        out_shape=jax.ShapeDtypeStruct((t, d), BF16),
        compiler_params=_cp("parallel"))(ya, yb, proj, proj, proj, proj)


def _mix_bwd(ya, yb, proj, dmix, name, tm=256):
    t, d = ya.shape
    tm = min(tm, t)

    def body(ya_ref, yb_ref, ga0, ga1, gb0, gb1, dm_ref, dya_ref, dyb_ref, dg_ref):
        ga = jnp.concatenate([ga0[...], ga1[...]], axis=1)
        gb = jnp.concatenate([gb0[...], gb1[...]], axis=1)
        _, vjp = jax.vjp(_mix, ya_ref[...], yb_ref[...], ga, gb)
        dya, dyb, dga, dgb = vjp(dm_ref[...])
        dya_ref[...] = dya.astype(BF16)
        dyb_ref[...] = dyb.astype(BF16)
        dg_ref[:, :d] = dga.astype(BF16)
        dg_ref[:, d:] = dgb.astype(BF16)

    row = pl.BlockSpec((tm, d), lambda i: (i, 0))
    wide = pl.BlockSpec((tm, 2 * d), lambda i: (i, 0))
    return pl.pallas_call(
        body, name=name, grid=(t // tm,), in_specs=[row, row] + _gate_specs(tm) + [row],
        out_specs=[row, row, wide],
        out_shape=[jax.ShapeDtypeStruct((t, d), BF16), jax.ShapeDtypeStruct((t, d), BF16),
                   jax.ShapeDtypeStruct((t, 2 * d), BF16)],
        compiler_params=_cp("parallel"))(ya, yb, proj, proj, proj, proj, dmix)


def _swiglu(g, u):
    return g * _sigmoid(g) * u


def _swiglu_fwd(gu, name, tm=256):
    t = gu.shape[0]
    tm = min(tm, t)
    f = FFN_HIDDEN

    def body(gu_ref, o_ref):
        o_ref[...] = _swiglu(gu_ref[:, :f], gu_ref[:, f:]).astype(BF16)

    return pl.pallas_call(
        body, name=name, grid=(t // tm,),
        in_specs=[pl.BlockSpec((tm, 2 * f), lambda i: (i, 0))],
        out_specs=pl.BlockSpec((tm, f), lambda i: (i, 0)),
        out_shape=jax.ShapeDtypeStruct((t, f), BF16),
        compiler_params=_cp("parallel"))(gu)


def _swiglu_bwd(gu, dact, name, tm=256):
    t = gu.shape[0]
    tm = min(tm, t)
    f = FFN_HIDDEN

    def body(gu_ref, da_ref, o_ref):
        _, vjp = jax.vjp(_swiglu, gu_ref[:, :f], gu_ref[:, f:])
        dg, du = vjp(da_ref[...])
        o_ref[:, :f] = dg.astype(BF16)
        o_ref[:, f:] = du.astype(BF16)

    return pl.pallas_call(
        body, name=name, grid=(t // tm,),
        in_specs=[pl.BlockSpec((tm, 2 * f), lambda i: (i, 0)), pl.BlockSpec((tm, f), lambda i: (i, 0))],
        out_specs=pl.BlockSpec((tm, 2 * f), lambda i: (i, 0)),
        out_shape=jax.ShapeDtypeStruct((t, 2 * f), BF16),
        compiler_params=_cp("parallel"))(gu, dact)


def _loss_head(x, g, target, name, tm=512):
    t, d = x.shape
    tm = min(tm, t)

    def body(x_ref, g_ref, t_ref, dx_ref, dg_ref, loss_ref):
        tgt = t_ref[...]

        def f(xv, gv):
            err = _rms(xv, gv) - tgt
            return 0.5 * jnp.sum(jnp.mean(err * err, axis=-1, keepdims=True))

        loss, vjp = jax.vjp(f, x_ref[...], g_ref[...])
        dx, dg = vjp(jnp.ones((), F32))
        dx_ref[...] = dx

        @pl.when(pl.program_id(0) == 0)
        def _():
            dg_ref[...] = jnp.zeros_like(dg_ref)
            loss_ref[...] = jnp.zeros_like(loss_ref)

        dg_ref[...] += dg
        loss_ref[...] += jnp.full(loss_ref.shape, loss, F32)

    row = pl.BlockSpec((tm, d), lambda i: (i, 0))
    vec = pl.BlockSpec((1, d), lambda i: (0, 0))
    lane = pl.BlockSpec((1, 128), lambda i: (0, 0))
    return pl.pallas_call(
        body, name=name, grid=(t // tm,), in_specs=[row, vec, row], out_specs=[row, vec, lane],
        out_shape=[jax.ShapeDtypeStruct((t, d), F32), jax.ShapeDtypeStruct((1, d), F32),
                   jax.ShapeDtypeStruct((1, 128), F32)],
        compiler_params=_cp("arbitrary"))(x, g, target)


def _lb_rows(l0, l1, l2, l3):
    mx = jnp.maximum(jnp.maximum(l0, l1), jnp.maximum(l2, l3))
    e0, e1, e2, e3 = jnp.exp(l0 - mx), jnp.exp(l1 - mx), jnp.exp(l2 - mx), jnp.exp(l3 - mx)
    s = e0 + e1 + e2 + e3
    p0, p1, p2, p3 = e0 / s, e1 / s, e2 / s, e3 / s
    c1 = p0 + p1
    c2 = c1 + p2
    c3 = c2 + p3
    return p0 - p0, c1 - p0, c2 - p0, c3 - p0


def _lb_fwd(lb_logits):
    def body(l_ref, o_ref):
        rows = _lb_rows(*[l_ref[pl.ds(i, 1), :] for i in range(DEPTH)])
        for i in range(DEPTH):
            o_ref[pl.ds(i, 1), :] = rows[i]

    return pl.pallas_call(body, name="lb_fwd", out_shape=jax.ShapeDtypeStruct(lb_logits.shape, F32))(lb_logits)


def _lb_bwd(lb_logits, dlb):
    def body(l_ref, d_ref, o_ref):
        _, vjp = jax.vjp(_lb_rows, *[l_ref[pl.ds(i, 1), :] for i in range(DEPTH)])
        grads = vjp(tuple(d_ref[pl.ds(i, 1), :] for i in range(DEPTH)))
        for i in range(DEPTH):
            o_ref[pl.ds(i, 1), :] = grads[i]

    return pl.pallas_call(body, name="lb_bwd", out_shape=jax.ShapeDtypeStruct(lb_logits.shape, F32))(lb_logits, dlb)


def _hg_consts():
    c = HG_CHUNK
    r = lax.broadcasted_iota(jnp.int32, (c, c), 0)
    s = lax.broadcasted_iota(jnp.int32, (c, c), 1)
    group = r // HG_SUB
    mats = [s <= r, s < group * HG_SUB] + [s < i * HG_SUB for i in range(1, c // HG_SUB)] + [s >= 0]
    cum = jnp.concatenate([m.astype(F32) for m in mats], axis=0)
    return cum, r, s, group


def _hg_chunk(zq, zf, lb, cum_mat):
    c = HG_CHUNK
    sq = _sigmoid(zq)
    q = zq * sq
    sg = _sigmoid(zf)
    f = lb + (1.0 - lb) * sg
    logf = jnp.log(jnp.maximum(f, MIN_F))
    k = 1.0 - f
    cum = _dot(cum_mat, logf, NN, HIGHEST)
    b = cum[0:c]
    bs = cum[c:2 * c]
    refs = [jnp.zeros_like(b)] + [cum[(2 + i) * c:(3 + i) * c] for i in range(c // HG_SUB - 1)]
    bc = cum[(1 + c // HG_SUB) * c:(2 + c // HG_SUB) * c]
    return sq, q, sg, f, k, b, bs, refs, bc


def _hg_gate(o, zg, gn):
    return o * lax.rsqrt(jnp.mean(o * o, axis=-1, keepdims=True) + EPS) * gn * (zg * _sigmoid(zg))


def _hgrn2_fwd(proj, lb, gn, name):
    t = proj.shape[0]
    bs_tok = min(HG_BLOCK, t)
    n_chunks = bs_tok // HG_CHUNK
    w = HG_HEADS * HG_DK

    def body(hq_ref, hf_ref, hi_ref, hg_ref, lb_ref, gn_ref, o_ref, og_ref, sall_ref, st_ref):
        @pl.when(pl.program_id(0) == 0)
        def _():
            st_ref[...] = jnp.zeros_like(st_ref)

        cum_mat, r, s, group = _hg_consts()
        causal = s <= r

        def chunk(ci, carry):
            rows = pl.ds(pl.multiple_of(ci * HG_CHUNK, HG_CHUNK), HG_CHUNK)
            for h in range(HG_HEADS):
                cols = slice(h * HG_DK, (h + 1) * HG_DK)
                v = hi_ref[rows, cols]
                zg = hg_ref[rows, cols]
                _, q, _, _, k, b, bstart, refs, bc = _hg_chunk(hq_ref[rows, cols], hf_ref[rows, cols],
                                                               lb_ref[:, cols], cum_mat)
                st0 = st_ref[h]
                sall_ref[ci, h] = st0
                vb = v.astype(BF16)
                o = _dot((q * jnp.exp(b)).astype(BF16), st0.astype(BF16), NT)
                qt = (q * jnp.exp(b - bstart)).astype(BF16)
                a = jnp.zeros((HG_CHUNK, HG_CHUNK), F32)
                for i, ref in enumerate(refs):
                    kref = (k * jnp.exp(jnp.minimum(ref - b, HG_EXP_CLAMP))).astype(BF16)
                    a = a + jnp.where((group == i) & causal, _dot(qt, kref, NT), 0.0)
                o = o + _dot(a.astype(BF16), vb, NN)
                kdec = (k * jnp.exp(bc - b)).astype(BF16)
                ebc = jnp.exp(bc)
                st_ref[h] = st0 * jnp.concatenate([ebc, ebc], axis=0) + _dot(vb, kdec, TN)
                o_ref[rows, cols] = o
                og_ref[rows, cols] = _hg_gate(o, zg, gn_ref[:, cols]).astype(BF16)
            return carry

        lax.fori_loop(0, n_chunks, chunk, 0)

    def col(j):
        return pl.BlockSpec((bs_tok, w), lambda n, j=j: (n, j))

    vec = pl.BlockSpec((1, w), lambda n: (0, 0))
    return pl.pallas_call(
        body, name=name, grid=(t // bs_tok,),
        in_specs=[col(COL_HQ // w), col(COL_HF // w), col(COL_HI // w), col(COL_HG // w), vec, vec],
        out_specs=[col(0), col(0),
                   pl.BlockSpec((n_chunks, HG_HEADS, HG_DK, HG_DK), lambda n: (n, 0, 0, 0))],
        out_shape=[jax.ShapeDtypeStruct((t, w), F32), jax.ShapeDtypeStruct((t, w), BF16),
                   jax.ShapeDtypeStruct((t // HG_CHUNK, HG_HEADS, HG_DK, HG_DK), F32)],
        scratch_shapes=[pltpu.VMEM((HG_HEADS, HG_DK, HG_DK), F32)],
        compiler_params=_cp("arbitrary"))(proj, proj, proj, proj, lb, gn)


def _hgrn2_bwd(proj, lb, gn, o_hg, sall, dog, name):
    t = proj.shape[0]
    bs_tok = min(HG_BLOCK, t)
    n_chunks = bs_tok // HG_CHUNK
    n_blocks = t // bs_tok
    w = HG_HEADS * HG_DK

    def body(hq_ref, hf_ref, hi_ref, hg_ref, lb_ref, gn_ref, o_ref, sall_ref, dog_ref,
             da_ref, dlb_ref, dgn_ref, dst_ref):
        @pl.when(pl.program_id(0) == 0)
        def _():
            dst_ref[...] = jnp.zeros_like(dst_ref)
            dlb_ref[...] = jnp.zeros_like(dlb_ref)
            dgn_ref[...] = jnp.zeros_like(dgn_ref)

        cum_mat, r, s, group = _hg_consts()
        causal = s <= r
        rev_cum = (s >= r).astype(F32)

        def chunk(cj, carry):
            ci = n_chunks - 1 - cj
            rows = pl.ds(pl.multiple_of(ci * HG_CHUNK, HG_CHUNK), HG_CHUNK)
            for h in range(HG_HEADS):
                cols = slice(h * HG_DK, (h + 1) * HG_DK)
                zq = hq_ref[rows, cols]
                v = hi_ref[rows, cols]
                zg = hg_ref[rows, cols]
                lbv = lb_ref[:, cols]
                sq, q, sg, f, k, b, bstart, refs, bc = _hg_chunk(zq, hf_ref[rows, cols], lbv, cum_mat)
                st0 = st_ref_load = sall_ref[ci, h]
                dst1 = dst_ref[h]
                vb = v.astype(BF16)
                eb = jnp.exp(b)
                ebs = jnp.exp(b - bstart)
                qg = (q * eb).astype(BF16)
                qt = (q * ebs).astype(BF16)
                ebcb = jnp.exp(bc - b)
                kdec = (k * ebcb).astype(BF16)
                ebc = jnp.exp(bc)
                ebc2 = jnp.concatenate([ebc, ebc], axis=0)
                st1 = st0 * ebc2 + _dot(vb, kdec, TN)

                _, gate_vjp = jax.vjp(_hg_gate, o_ref[rows, cols], zg, gn_ref[:, cols])
                do, dzg, dgn = gate_vjp(dog_ref[rows, cols])
                dob = do.astype(BF16)
                dam = jnp.where(causal, _dot(dob, vb, NT), 0.0)

                erefs, a = [], jnp.zeros((HG_CHUNK, HG_CHUNK), F32)
                dq_in = jnp.zeros((HG_CHUNK, HG_DK), F32)
                dk = ebcb * _dot(vb, dst1.astype(BF16), NN)
                for i, ref in enumerate(refs):
                    e_i = jnp.exp(jnp.minimum(ref - b, HG_EXP_CLAMP))
                    kref = (k * e_i).astype(BF16)
                    sel = group == i
                    a = a + jnp.where(sel & causal, _dot(qt, kref, NT), 0.0)
                    da_i = jnp.where(sel, dam, 0.0).astype(BF16)
                    dq_in = dq_in + _dot(da_i, kref, NN)
                    dk = dk + e_i * _dot(da_i, qt, TN)
                dq = eb * _dot(dob, st0.astype(BF16), NN) + ebs * dq_in
                dv = _dot(a.astype(BF16), dob, TN) + _dot(kdec, dst1.astype(BF16), NT)
                dst_ref[h] = dst1 * ebc2 + _dot(dob, qg, TN)

                dbx = jnp.sum(dst1 * st1, axis=0, keepdims=True)
                dlogf = _dot(rev_cum, q * dq - k * dk, NN, HIGHEST) + dbx
                df = jnp.where(f > MIN_F, dlogf / f, 0.0) - dk
                dzf = df * (1.0 - lbv) * sg * (1.0 - sg)
                dzq = dq * (sq * (1.0 + zq * (1.0 - sq)))
                da_ref[rows, pl.ds(COL_HQ + h * HG_DK, HG_DK)] = dzq.astype(BF16)
                da_ref[rows, pl.ds(COL_HF + h * HG_DK, HG_DK)] = dzf.astype(BF16)
                da_ref[rows, pl.ds(COL_HI + h * HG_DK, HG_DK)] = dv.astype(BF16)
                da_ref[rows, pl.ds(COL_HG + h * HG_DK, HG_DK)] = dzg.astype(BF16)
                dlb_ref[:, cols] += jnp.sum(df * (1.0 - sg), axis=0, keepdims=True)
                dgn_ref[:, cols] += dgn
            return carry

        lax.fori_loop(0, n_chunks, chunk, 0)

    def col(j):
        return pl.BlockSpec((bs_tok, w), lambda n, j=j: (n_blocks - 1 - n, j))

    vec = pl.BlockSpec((1, w), lambda n: (0, 0))
    return pl.pallas_call(
        body, name=name, grid=(n_blocks,),
        in_specs=[col(COL_HQ // w), col(COL_HF // w), col(COL_HI // w), col(COL_HG // w), vec, vec, col(0),
                  pl.BlockSpec((n_chunks, HG_HEADS, HG_DK, HG_DK), lambda n: (n_blocks - 1 - n, 0, 0, 0)),
                  col(0)],
        out_specs=[pl.BlockSpec((bs_tok, 4 * w), lambda n: (n_blocks - 1 - n, 0)), vec, vec],
        out_shape=[jax.ShapeDtypeStruct((t, 4 * w), BF16), jax.ShapeDtypeStruct((1, w), F32),
                   jax.ShapeDtypeStruct((1, w), F32)],
        scratch_shapes=[pltpu.VMEM((HG_HEADS, HG_DK, HG_DK), F32)],
        compiler_params=_cp("arbitrary"))(proj, proj, proj, proj, lb, gn, o_hg, sall, dog)


def _rope_tables(t):
    half = ROPE_DIM // 2
    inv = ROPE_THETA ** (-jnp.arange(half, dtype=F32) * 2.0 / ROPE_DIM)
    ang = jnp.arange(t).astype(F32)[:, None] * inv[None, :]
    cos, sin = jnp.cos(ang), jnp.sin(ang)
    pad = ATT_HEAD_DIM - ROPE_DIM
    c = jnp.concatenate([cos, cos, jnp.ones((t, pad), F32)], axis=1)
    su = jnp.concatenate([-sin, jnp.zeros((t, half + pad), F32)], axis=1)
    sd = jnp.concatenate([jnp.zeros((t, half), F32), sin, jnp.zeros((t, pad), F32)], axis=1)
    return tuple(jnp.concatenate([m, m], axis=1) for m in (c, su, sd))


def _rope(x, tabs):
    c, su, sd = tabs
    n = x.shape[1]
    half = ROPE_DIM // 2
    return x * c + pltpu.roll(x, n - half, 1) * su + pltpu.roll(x, half, 1) * sd


def _rope_t(dy, tabs):
    c, su, sd = tabs
    n = dy.shape[1]
    half = ROPE_DIM // 2
    return dy * c + pltpu.roll(dy * su, half, 1) + pltpu.roll(dy * sd, n - half, 1)


def _swa_specs(n_blocks, clamp):
    blk = ATT_BLOCK

    def cur(n):
        return jnp.minimum(n, n_blocks - 1) if clamp else n

    def prev(n):
        return jnp.maximum(cur(n) - 1, 0)

    q_spec = pl.BlockSpec((blk, 512), lambda m, n: (cur(n), COL_AQ // 512 + m))
    kv = [pl.BlockSpec((blk, 128), lambda m, n, c=c, f=f: (f(n), c + m))
          for c in (COL_AK // 128, COL_AV // 128) for f in (cur, prev)]
    tabs = [pl.BlockSpec((blk, 128), lambda m, n, f=f: (f(n), 0)) for f in (cur, prev) for _ in range(3)]
    return q_spec, kv, tabs, cur, prev


def _swa_scores(qm, kd, sink, mask):
    s = _dot(qm, kd, NT) * (ATT_HEAD_DIM ** -0.5)
    s = jnp.where(mask, s, -jnp.inf)
    mx = jnp.maximum(jnp.max(s, axis=-1, keepdims=True), sink)
    p = jnp.exp(s - mx)
    es = jnp.exp(sink - mx)
    den = jnp.sum(p, axis=-1, keepdims=True) + es
    return p / den, es / den


def _swa_window(kc_ref, kp_ref, vc_ref, vp_ref, tabs_c, tabs_p, n):
    k2 = jnp.concatenate([_rope(kp_ref[...], tabs_p), _rope(kc_ref[...], tabs_c)], axis=0)
    v2 = jnp.concatenate([vp_ref[...], vc_ref[...]], axis=0)
    blk = ATT_BLOCK
    qi = lax.broadcasted_iota(jnp.int32, (blk, 2 * blk), 0)
    kj = lax.broadcasted_iota(jnp.int32, (blk, 2 * blk), 1)
    delta = qi + blk - kj
    mask = (delta >= 0) & (delta < blk) & ((kj >= blk) | (n > 0))
    return k2, v2, mask


def _swa_fwd(proj, sinks, tabs, name):
    t = proj.shape[0]
    n_blocks = t // ATT_BLOCK
    q_spec, kv_specs, tab_specs, _, _ = _swa_specs(n_blocks, clamp=False)

    def body(q_ref, kc_ref, kp_ref, vc_ref, vp_ref, c0, c1, c2, p0, p1, p2, sink_ref, o_ref):
        m, n = pl.program_id(0), pl.program_id(1)
        tabs_c = (c0[...], c1[...], c2[...])
        tabs_p = (p0[...], p1[...], p2[...])
        k2, v2, mask = _swa_window(kc_ref, kp_ref, vc_ref, vp_ref, tabs_c, tabs_p, n)
        k2r, v2r = pltpu.roll(k2, 64, 1), pltpu.roll(v2, 64, 1)
        upper_k = lax.broadcasted_iota(jnp.int32, k2.shape, 1) >= 64
        upper_q = lax.broadcasted_iota(jnp.int32, (ATT_BLOCK, 128), 1) >= 64
        for jj in range(2):
            own = upper_k if jj else ~upper_k
            kd = jnp.where(own, k2, k2r).astype(BF16)
            vd = jnp.where(own, v2, v2r).astype(BF16)
            for pi in range(2):
                cols = slice(256 * jj + 128 * pi, 256 * jj + 128 * pi + 128)
                qp = _rope(q_ref[:, cols], tabs_c)
                outs = []
                for e in range(2):
                    sink = sink_ref[0, 8 * m + 4 * jj + 2 * pi + e]
                    qm = jnp.where(upper_q if e else ~upper_q, qp, 0.0).astype(BF16)
                    pn, _ = _swa_scores(qm, kd, sink, mask)
                    outs.append(_dot(pn.astype(BF16), vd, NN))
                o_ref[:, cols] = jnp.where(upper_q, outs[1], outs[0]).astype(BF16)

    return pl.pallas_call(
        body, name=name, grid=(2, n_blocks),
        in_specs=[q_spec] + kv_specs + tab_specs + [pl.BlockSpec(memory_space=pltpu.SMEM)],
        out_specs=pl.BlockSpec((ATT_BLOCK, 512), lambda m, n: (n, m)),
        out_shape=jax.ShapeDtypeStruct((t, ATT_Q_HEADS * ATT_HEAD_DIM), BF16),
        compiler_params=_cp("parallel", "arbitrary"))(proj, proj, proj, proj, proj, *tabs, *tabs, sinks)


def _swa_bwd(proj, sinks, tabs, o_att, do_att, name):
    t = proj.shape[0]
    n_blocks = t // ATT_BLOCK
    blk = ATT_BLOCK
    q_spec, kv_specs, tab_specs, cur, prev = _swa_specs(n_blocks, clamp=True)

    def body(q_ref, kc_ref, kp_ref, vc_ref, vp_ref, c0, c1, c2, p0, p1, p2, sink_ref, o_ref, do_ref,
             dq_ref, dk_ref, dv_ref, ds_ref, ck_ref, cv_ref):
        m, n = pl.program_id(0), pl.program_id(1)

        @pl.when(n == 0)
        def _():
            ds_ref[...] = jnp.zeros_like(ds_ref)
            ck_ref[...] = jnp.zeros_like(ck_ref)
            cv_ref[...] = jnp.zeros_like(cv_ref)

        @pl.when(n < n_blocks)
        def _():
            tabs_c = (c0[...], c1[...], c2[...])
            tabs_p = (p0[...], p1[...], p2[...])
            k2, v2, mask = _swa_window(kc_ref, kp_ref, vc_ref, vp_ref, tabs_c, tabs_p, n)
            k2r, v2r = pltpu.roll(k2, 64, 1), pltpu.roll(v2, 64, 1)
            upper_k = lax.broadcasted_iota(jnp.int32, k2.shape, 1) >= 64
            upper_q = lax.broadcasted_iota(jnp.int32, (blk, 128), 1) >= 64
            lane = lax.broadcasted_iota(jnp.int32, (8, 128), 1)
            dk2 = jnp.zeros(k2.shape, F32)
            dv2 = jnp.zeros(k2.shape, F32)
            dsv = jnp.zeros((8, 128), F32)
            for jj in range(2):
                own = upper_k if jj else ~upper_k
                kd = jnp.where(own, k2, k2r).astype(BF16)
                vd = jnp.where(own, v2, v2r).astype(BF16)
                dkd = jnp.zeros(k2.shape, F32)
                dvd = jnp.zeros(k2.shape, F32)
                for pi in range(2):
                    cols = slice(256 * jj + 128 * pi, 256 * jj + 128 * pi + 128)
                    qp = _rope(q_ref[:, cols], tabs_c)
                    do_pair = do_ref[:, cols]
                    o_pair = o_ref[:, cols].astype(F32)
                    dqs = []
                    for e in range(2):
                        hl = 4 * jj + 2 * pi + e
                        sink = sink_ref[0, 8 * m + hl]
                        half = upper_q if e else ~upper_q
                        qm = jnp.where(half, qp, 0.0).astype(BF16)
                        pn, ps = _swa_scores(qm, kd, sink, mask)
                        dom = jnp.where(half, do_pair, 0.0)
                        delta = jnp.sum(dom * o_pair, axis=-1, keepdims=True)
                        domb = dom.astype(BF16)
                        dp = _dot(domb, vd, NT)
                        dsb = (pn * (dp - delta) * (ATT_HEAD_DIM ** -0.5)).astype(BF16)
                        dqs.append(_dot(dsb, kd, NN))
                        dkd = dkd + _dot(dsb, qm, TN)
                        dvd = dvd + _dot(pn.astype(BF16), domb, TN)
                        dsv = dsv + jnp.where(lane == hl, -jnp.sum(ps * delta), 0.0)
                    dq_ref[:, cols] = _rope_t(jnp.where(upper_q, dqs[1], dqs[0]), tabs_c).astype(BF16)
                dk2 = dk2 + jnp.where(own, dkd + pltpu.roll(dkd, 64, 1), 0.0)
                dv2 = dv2 + jnp.where(own, dvd + pltpu.roll(dvd, 64, 1), 0.0)
            dk_ref[...] = (ck_ref[...] + _rope_t(dk2[:blk], tabs_p)).astype(BF16)
            dv_ref[...] = (cv_ref[...] + dv2[:blk]).astype(BF16)
            ck_ref[...] = _rope_t(dk2[blk:], tabs_c)
            cv_ref[...] = dv2[blk:]
            ds_ref[...] += dsv

        @pl.when(n == n_blocks)
        def _():
            dk_ref[...] = ck_ref[...].astype(BF16)
            dv_ref[...] = cv_ref[...].astype(BF16)

    wide = pl.BlockSpec((blk, 512), lambda m, n: (cur(n), m))
    lagged = pl.BlockSpec((blk, 128), lambda m, n: (jnp.maximum(n - 1, 0), m))
    return pl.pallas_call(
        body, name=name, grid=(2, n_blocks + 1),
        in_specs=[q_spec] + kv_specs + tab_specs + [pl.BlockSpec(memory_space=pltpu.SMEM), wide, wide],
        out_specs=[wide, lagged, lagged, pl.BlockSpec((None, 8, 128), lambda m, n: (m, 0, 0))],
        out_shape=[jax.ShapeDtypeStruct((t, 1024), BF16), jax.ShapeDtypeStruct((t, 256), BF16),
                   jax.ShapeDtypeStruct((t, 256), BF16), jax.ShapeDtypeStruct((2, 8, 128), F32)],
        scratch_shapes=[pltpu.VMEM((blk, 128), F32), pltpu.VMEM((blk, 128), F32)],
        compiler_params=_cp("arbitrary", "arbitrary"))(proj, proj, proj, proj, proj, *tabs, *tabs, sinks, o_att, do_att)


def _local_step(x, target, weights, norm1, lb_logits, hg_norm, attn_sinks, norm2, final_norm):
    t = x.shape[0]
    win_t, w_pa, w_pb, w_o, wgu_t, w_d = weights
    tabs = _rope_tables(t)
    lb_all = _lb_fwd(lb_logits)
    saved = []
    for l in range(DEPTH):
        n1, n2 = norm1[l][None, :], norm2[l][None, :]
        lb, gn, sinks = lb_all[l][None, :], hg_norm[l][None, :], attn_sinks[l][None, :]
        h = _rms_fwd(x, n1, "rms1_fwd")
        proj = _matmul_nt(h, win_t[l], 0, IN_COLS, F32, "proj_fwd", tn=1280)
        o_hg, o_g, sall = _hgrn2_fwd(proj, lb, gn, "hgrn2_fwd")
        o_att = _swa_fwd(proj, sinks, tabs, "swa_fwd")
        ya = _matmul_nn(o_g, w_pa[l], 0, None, "ya_fwd")
        yb = _matmul_nn(o_att, w_pb[l], 0, None, "yb_fwd")
        mix = _mix_fwd(ya, yb, proj, "mix_fwd")
        x1 = _matmul_nn(mix, w_o[l], 0, x, "wo_fwd")
        h2 = _rms_fwd(x1, n2, "rms2_fwd")
        gu = _matmul_nt(h2, wgu_t[l], 0, 2 * FFN_HIDDEN, F32, "gu_fwd", tn=1408)
        act = _swiglu_fwd(gu, "swiglu_fwd")
        x2 = _matmul_nn(act, w_d[l], 0, x1, "wd_fwd")
        saved.append((x, h, proj, o_hg, o_g, sall, o_att, ya, yb, mix, x1, h2, gu, act, n1, n2, lb, gn, sinks))
        x = x2

    dx, d_fn, loss = _loss_head(x, final_norm[None, :], target, "loss_head")

    wgrads = [None] * DEPTH
    d_n1, d_n2, d_lb, d_gn, d_sinks = ([None] * DEPTH for _ in range(5))
    for l in reversed(range(DEPTH)):
        x0, h, proj, o_hg, o_g, sall, o_att, ya, yb, mix, x1, h2, gu, act, n1, n2, lb, gn, sinks = saved[l]
        dact = _matmul_nt(dx, w_d[l], 0, FFN_HIDDEN, F32, "wd_bwd", tn=1408)
        dgu = _swiglu_bwd(gu, dact, "swiglu_bwd")
        g_wd = _matmul_tn(act, dx, "wd_grad", tm=1408)
        g_wgu = _matmul_tn(dgu, h2, "wgu_grad", tm=1408)
        dh2 = _matmul_nn(dgu, wgu_t[l], 0, None, "gu_bwd", tk=FFN_HIDDEN)
        dx1, d_n2[l] = _rms_bwd(x1, n2, dh2, dx, "rms2_bwd")
        dmix = _matmul_nt(dx1, w_o[l], 0, D_MODEL, F32, "wo_bwd", tn=1024)
        g_wo = _matmul_tn(mix, dx1, "wo_grad")
        dya, dyb, dgab = _mix_bwd(ya, yb, proj, dmix, "mix_bwd")
        g_wpa = _matmul_tn(o_g, dya, "wpa_grad")
        g_wpb = _matmul_tn(o_att, dyb, "wpb_grad")
        dog = _matmul_nt(dya, w_pa[l], 0, D_MODEL, F32, "wpa_bwd", tn=1024)
        doatt = _matmul_nt(dyb, w_pb[l], 0, D_MODEL, F32, "wpb_bwd", tn=1024)
        dhg, d_lb[l], d_gn[l] = _hgrn2_bwd(proj, lb, gn, o_hg, sall, dog, "hgrn2_bwd")
        daq, dak, dav, d_sinks[l] = _swa_bwd(proj, sinks, tabs, o_att, doatt, "swa_bwd")
        dakv = jnp.concatenate([dak, dav], axis=1)
        g_win = jnp.concatenate([_matmul_tn(dhg, h, "win_grad_hg"), _matmul_tn(daq, h, "win_grad_aq"),
                                 _matmul_tn(dakv, h, "win_grad_akv"), _matmul_tn(dgab, h, "win_grad_gates")], axis=0)
        dh = _matmul_nn(dhg, win_t[l], COL_HQ, None, "win_bwd_hg", tk=2048)
        dh = _matmul_nn(daq, win_t[l], COL_AQ, dh, "win_bwd_aq")
        dh = _matmul_nn(dakv, win_t[l], COL_AK, dh, "win_bwd_akv")
        dh = _matmul_nn(dgab, win_t[l], COL_GA, dh, "win_bwd_gates", tk=512)
        dx, d_n1[l] = _rms_bwd(x0, n1, dh, dx1, "rms1_bwd")
        wgrads[l] = (g_win, g_wpa, g_wpb, g_wo, g_wgu, g_wd)

    d_sink_rows = [jnp.concatenate([d[0, 0, :8], d[1, 0, :8]]) for d in d_sinks]
    small = (jnp.concatenate(d_n1, axis=0), jnp.concatenate(d_lb, axis=0), jnp.concatenate(d_gn, axis=0),
             jnp.concatenate(d_n2, axis=0), d_fn, jnp.stack(d_sink_rows, axis=0))
    return loss, dx, wgrads, small


MESH = pl.DeviceIdType.MESH
ANY = pl.BlockSpec(memory_space=pl.ANY)


def _position():
    x, y, c = lax.axis_index("x"), lax.axis_index("y"), lax.axis_index("c")
    return x, y, c, 4 * x + 2 * y + c


def _peer(x, y, c, r):
    px = 1 - x if r & 4 else x
    py = 1 - y if r & 2 else y
    pc = 1 - c if r & 1 else c
    return (px, py, pc), 4 * px + 2 * py + pc


def _all_gather_weights(shards):
    n_t = len(shards)
    dest = [(0, 0), (1, 0), (2, 0), (3, 0), (4, 0), (4, N_DEV * SHARD_ROWS[4]), (5, 0)]
    out_rows = [N_DEV * SHARD_ROWS[0], 1024, 1024, 1024, 2 * N_DEV * SHARD_ROWS[4], N_DEV * SHARD_ROWS[6]]

    def body(*refs):
        ins, outs = refs[:n_t], refs[n_t:n_t + 6]
        send_sems, recv_sems, local_sems = refs[n_t + 6:]
        x, y, c, me = _position()

        def window(ti, dev):
            oi, base = dest[ti]
            return outs[oi].at[:, pl.ds(base + dev * SHARD_ROWS[ti], SHARD_ROWS[ti]), :]

        local = [pltpu.make_async_copy(ins[ti], window(ti, me), local_sems.at[ti]) for ti in range(n_t)]
        for cp in local:
            cp.start()
        sends = []
        for r in range(1, N_DEV):
            peer, _ = _peer(x, y, c, r)
            for ti in range(n_t):
                sends.append(pltpu.make_async_remote_copy(
                    src_ref=ins[ti], dst_ref=window(ti, me), send_sem=send_sems.at[r, ti],
                    recv_sem=recv_sems.at[r, ti], device_id=peer, device_id_type=MESH))
        for cp in sends:
            cp.start()
        for r in range(1, N_DEV):
            peer, pid = _peer(x, y, c, r)
            for ti in range(n_t):
                pltpu.make_async_remote_copy(
                    src_ref=ins[ti], dst_ref=window(ti, pid), send_sem=send_sems.at[r, ti],
                    recv_sem=recv_sems.at[r, ti], device_id=peer, device_id_type=MESH).wait_recv()
        for cp in sends:
            cp.wait_send()
        for cp in local:
            cp.wait()

    return pl.pallas_call(
        body, name="gather_weights",
        in_specs=[ANY] * n_t, out_specs=[ANY] * 6,
        out_shape=[jax.ShapeDtypeStruct((DEPTH, rows, D_MODEL), BF16) for rows in out_rows],
        scratch_shapes=[pltpu.SemaphoreType.DMA((N_DEV, n_t)), pltpu.SemaphoreType.DMA((N_DEV, n_t)),
                        pltpu.SemaphoreType.DMA((n_t,))],
    )(*shards)


def _scatter_grads(grads):
    src = [(0, 0), (1, 0), (2, 0), (3, 0), (4, 0), (4, N_DEV * SHARD_ROWS[4]), (5, 0)]
    n_t = len(src)

    def body(*refs):
        ins, land = refs[:6], refs[6]
        send_sems, recv_sems, local_sems = refs[7:]
        x, y, c, me = _position()

        def piece(ti, dev):
            ii, base = src[ti]
            return ins[ii].at[pl.ds(base + dev * SHARD_ROWS[ti], SHARD_ROWS[ti]), :]

        def slot(ti, dev):
            return land.at[dev, pl.ds(SLOT_OFF[ti], SHARD_ROWS[ti]), :]

        local = [pltpu.make_async_copy(piece(ti, me), slot(ti, me), local_sems.at[ti]) for ti in range(n_t)]
        for cp in local:
            cp.start()
        sends = []
        for r in range(1, N_DEV):
            peer, pid = _peer(x, y, c, r)
            for ti in range(n_t):
                sends.append(pltpu.make_async_remote_copy(
                    src_ref=piece(ti, pid), dst_ref=slot(ti, me), send_sem=send_sems.at[r, ti],
                    recv_sem=recv_sems.at[r, ti], device_id=peer, device_id_type=MESH))
        for cp in sends:
            cp.start()
        for r in range(1, N_DEV):
            peer, pid = _peer(x, y, c, r)
            for ti in range(n_t):
                pltpu.make_async_remote_copy(
                    src_ref=piece(ti, me), dst_ref=slot(ti, pid), send_sem=send_sems.at[r, ti],
                    recv_sem=recv_sems.at[r, ti], device_id=peer, device_id_type=MESH).wait_recv()
        for cp in sends:
            cp.wait_send()
        for cp in local:
            cp.wait()

    return pl.pallas_call(
        body, name="scatter_grads",
        in_specs=[ANY] * 6, out_specs=ANY,
        out_shape=jax.ShapeDtypeStruct((N_DEV, SLOT_ROWS, D_MODEL), F32),
        scratch_shapes=[pltpu.SemaphoreType.DMA((N_DEV, n_t)), pltpu.SemaphoreType.DMA((N_DEV, n_t)),
                        pltpu.SemaphoreType.DMA((n_t,))],
    )(*grads)


def _sum_slots(land, name, tr=480):
    _, rows, d = land.shape

    def body(l_ref, o_ref):
        acc = l_ref[0]
        for k in range(1, N_DEV):
            acc = acc + l_ref[k]
        o_ref[...] = acc

    return pl.pallas_call(
        body, name=name, grid=(rows // tr,),
        in_specs=[pl.BlockSpec((N_DEV, tr, d), lambda i: (0, i, 0))],
        out_specs=pl.BlockSpec((tr, d), lambda i: (i, 0)),
        out_shape=jax.ShapeDtypeStruct((rows, d), F32),
        compiler_params=_cp("parallel"))(land)


def _all_reduce_small(part):
    rows, d = part.shape

    def body(p_ref, o_ref, buf, send_sems, recv_sems):
        x, y, c, me = _position()
        buf[me] = p_ref[...]
        sends = []
        for r in range(1, N_DEV):
            peer, _ = _peer(x, y, c, r)
            sends.append(pltpu.make_async_remote_copy(
                src_ref=p_ref, dst_ref=buf.at[me], send_sem=send_sems.at[r], recv_sem=recv_sems.at[r],
                device_id=peer, device_id_type=MESH))
        for cp in sends:
            cp.start()
        for r in range(1, N_DEV):
            peer, pid = _peer(x, y, c, r)
            pltpu.make_async_remote_copy(
                src_ref=p_ref, dst_ref=buf.at[pid], send_sem=send_sems.at[r], recv_sem=recv_sems.at[r],
                device_id=peer, device_id_type=MESH).wait_recv()
        for cp in sends:
            cp.wait_send()
        acc = buf[0]
        for k in range(1, N_DEV):
            acc = acc + buf[k]
        o_ref[...] = acc

    vmem = pl.BlockSpec(memory_space=pltpu.VMEM)
    return pl.pallas_call(
        body, name="all_reduce_small", in_specs=[vmem], out_specs=vmem,
        out_shape=jax.ShapeDtypeStruct((rows, d), F32),
        scratch_shapes=[pltpu.VMEM((N_DEV, rows, d), F32), pltpu.SemaphoreType.DMA((N_DEV,)),
                        pltpu.SemaphoreType.DMA((N_DEV,))],
    )(part)


def _adamw(w, g, m, v, name):
    shape = w.shape
    c = shape[-1]
    rows = w.size // c
    tr = rows
    for cand in (512, 352, 128):
        if rows % cand == 0:
            tr = cand
            break
    c1 = 1.0 / (1.0 - ADAM_B1 ** ADAM_STEP)
    c2 = 1.0 / (1.0 - ADAM_B2 ** ADAM_STEP)

    def body(w_ref, g_ref, m_ref, v_ref, d_ref, nm_ref, nv_ref):
        gv = g_ref[...]
        nm = ADAM_B1 * m_ref[...] + (1.0 - ADAM_B1) * gv
        nv = ADAM_B2 * v_ref[...] + (1.0 - ADAM_B2) * (gv * gv)
        d_ref[...] = -ADAM_LR * ((nm * c1) / (jnp.sqrt(nv * c2) + ADAM_EPS) + ADAM_WD * w_ref[...])
        nm_ref[...] = nm
        nv_ref[...] = nv

    spec = pl.BlockSpec((tr, c), lambda i: (i, 0))
    outs = pl.pallas_call(
        body, name=name, grid=(rows // tr,), in_specs=[spec] * 4, out_specs=[spec] * 3,
        out_shape=[jax.ShapeDtypeStruct((rows, c), F32)] * 3,
        compiler_params=_cp("parallel"))(*[a.reshape(rows, c) for a in (w, g, m, v)])
    return tuple(o.reshape(shape) for o in outs)


def kernel(x, norm1, w_in, lb_logits, hg_norm, attn_sinks, w_pa, w_pb, w_o, norm2, w_gate, w_up, w_down, final_norm, loss_target, m_norm1, m_w_in, m_lb_logits, m_hg_norm, m_attn_sinks, m_w_pa, m_w_pb, m_w_o, m_norm2, m_w_gate, m_w_up, m_w_down, m_final_norm, v_norm1, v_w_in, v_lb_logits, v_hg_norm, v_attn_sinks, v_w_pa, v_w_pb, v_w_o, v_norm2, v_w_gate, v_w_up, v_w_down, v_final_norm):
    t = x.shape[1]
    shards = [jnp.swapaxes(w_in, 1, 2).astype(BF16), w_pa.astype(BF16), w_pb.astype(BF16), w_o.astype(BF16),
              jnp.swapaxes(w_gate, 1, 2).astype(BF16), jnp.swapaxes(w_up, 1, 2).astype(BF16), w_down.astype(BF16)]
    weights = _all_gather_weights(shards)

    loss_lanes, grad_x, wgrads, small = _local_step(
        x.reshape(t, D_MODEL), loss_target.reshape(t, D_MODEL), weights,
        norm1, lb_logits, hg_norm, attn_sinks, norm2, final_norm)

    owned = [_sum_slots(_scatter_grads(wgrads[l]), "sum_slots") for l in range(DEPTH)]
    owned = jnp.stack(owned, axis=0)

    def rows_of(ti, transpose):
        g = owned[:, SLOT_OFF[ti]:SLOT_OFF[ti] + SHARD_ROWS[ti], :]
        return jnp.swapaxes(g, 1, 2) if transpose else g

    g_big = {"w_in": rows_of(0, True), "w_pa": rows_of(1, False), "w_pb": rows_of(2, False), "w_o": rows_of(3, False),
             "w_gate": rows_of(4, True), "w_up": rows_of(5, True), "w_down": rows_of(6, False)}

    d_n1, d_lb, d_gn, d_n2, d_fn, d_sinks = small
    pad = jnp.zeros((DEPTH, D_MODEL - ATT_Q_HEADS), F32)
    packed = jnp.concatenate([
        d_n1, d_lb, d_gn, d_n2, d_fn, jnp.concatenate([d_sinks, pad], axis=1),
        jnp.concatenate([loss_lanes, jnp.zeros((1, D_MODEL - 128), F32)], axis=1),
        jnp.zeros((SMALL_ROWS - 22, D_MODEL), F32)], axis=0)
    total = _all_reduce_small(packed)
    loss = total[21, 0]
    g_small = {"norm1": total[0:4], "lb_logits": _lb_bwd(lb_logits, total[4:8]), "hg_norm": total[8:12],
               "norm2": total[12:16], "final_norm": total[16], "attn_sinks": total[17:21, :ATT_Q_HEADS]}

    params = {"norm1": (norm1, m_norm1, v_norm1), "w_in": (w_in, m_w_in, v_w_in),
              "lb_logits": (lb_logits, m_lb_logits, v_lb_logits), "hg_norm": (hg_norm, m_hg_norm, v_hg_norm),
              "attn_sinks": (attn_sinks, m_attn_sinks, v_attn_sinks), "w_pa": (w_pa, m_w_pa, v_w_pa),
              "w_pb": (w_pb, m_w_pb, v_w_pb), "w_o": (w_o, m_w_o, v_w_o), "norm2": (norm2, m_norm2, v_norm2),
              "w_gate": (w_gate, m_w_gate, v_w_gate), "w_up": (w_up, m_w_up, v_w_up),
              "w_down": (w_down, m_w_down, v_w_down), "final_norm": (final_norm, m_final_norm, v_final_norm)}
    order = ["norm1", "w_in", "lb_logits", "hg_norm", "attn_sinks", "w_pa", "w_pb", "w_o", "norm2",
             "w_gate", "w_up", "w_down", "final_norm"]
    grads, deltas, new_m, new_v = [], [], [], []
    for name in order:
        w, m, v = params[name]
        g = (g_big[name] if name in g_big else g_small[name]).reshape(w.shape)
        w2 = w.reshape(1, -1) if w.ndim == 1 else w
        d, nm, nv = _adamw(w2, g.reshape(w2.shape), m.reshape(w2.shape), v.reshape(w2.shape), "adamw_" + name)
        grads.append(g)
        deltas.append(d.reshape(w.shape))
        new_m.append(nm.reshape(w.shape))
        new_v.append(nv.reshape(w.shape))
    return (loss, grad_x.reshape(x.shape), *grads, *deltas, *new_m, *new_v)
```

```python
import functools
from typing import Callable, NamedTuple

import jax
import jax.numpy as jnp
from jax import lax
from jax.experimental import pallas as pl
from jax.experimental.pallas import tpu as pltpu

F32, BF16 = jnp.float32, jnp.bfloat16

D_MODEL = 1024
DEPTH = 4
N_DEV = 8
HG_HEADS = 8
HG_DK = 128
HG_CHUNK = 64
HG_SUB = 16
HG_BLOCK = 256
HG_EXP_CLAMP = 60.0
ATT_Q_HEADS = 16
ATT_HEAD_DIM = 64
ATT_BLOCK = 128
ROPE_THETA = 500000.0
ROPE_DIM = 16
FFN_HIDDEN = 2816
EPS = 1e-6
MIN_F = 1e-30
ADAM_LR, ADAM_B1, ADAM_B2, ADAM_EPS, ADAM_WD, ADAM_STEP = 0.001, 0.9, 0.999, 1e-08, 0.01, 10

COL_HQ, COL_HF, COL_HI, COL_HG = 0, 1024, 2048, 3072
COL_AQ, COL_AK, COL_AV, COL_GA, COL_GB = 4096, 5120, 5376, 5632, 6656
IN_COLS = 7680

SHARD_ROWS = (960, 128, 128, 128, 352, 352, 352)
SLOT_OFF = (0, 960, 1088, 1216, 1344, 1696, 2048)
SLOT_ROWS = 2400
SMALL_ROWS = 24

VMEM_LIMIT_BYTES = 56 * 1024 * 1024

NN = ((1,), (0,))
NT = ((1,), (1,))
TN = ((0,), (0,))


def _dot(a, b, dims):
    return lax.dot_general(a, b, (dims, ((), ())), preferred_element_type=F32)


def _cp(*sem):
    return pltpu.CompilerParams(dimension_semantics=sem if sem else None, vmem_limit_bytes=VMEM_LIMIT_BYTES)


def _sigmoid(x):
    return 1.0 / (1.0 + jnp.exp(-x))


def _matmul_nt(a, w, row_off, n, out_dtype, name, tm=1024, tn=512):
    t, k = a.shape
    tm = min(tm, t)
    assert n % tn == 0 and row_off % tn == 0 and t % tm == 0

    def body(a_ref, w_ref, o_ref):
        o_ref[...] = _dot(a_ref[...].astype(BF16), w_ref[...], NT).astype(o_ref.dtype)

    return pl.pallas_call(
        body, name=name, grid=(n // tn, t // tm),
        in_specs=[pl.BlockSpec((tm, k), lambda j, i: (i, 0)),
                  pl.BlockSpec((tn, k), lambda j, i: (row_off // tn + j, 0))],
        out_specs=pl.BlockSpec((tm, tn), lambda j, i: (i, j)),
        out_shape=jax.ShapeDtypeStruct((t, n), out_dtype),
        compiler_params=_cp("parallel", "parallel"))(a, w)


def _matmul_nn(a, w, row_off, res, name, tm=512, tk=None):
    t, k = a.shape
    n = w.shape[1]
    tm = min(tm, t)
    tk = tk or k
    nk = k // tk
    assert k % tk == 0 and row_off % tk == 0 and t % tm == 0

    def body(*refs):
        if res is None:
            a_ref, w_ref, o_ref, acc = refs
        else:
            a_ref, w_ref, r_ref, o_ref, acc = refs
        kk = pl.program_id(1)
        part = _dot(a_ref[...].astype(BF16), w_ref[...], NN)

        @pl.when(kk == 0)
        def _():
            acc[...] = part

        @pl.when(kk > 0)
        def _():
            acc[...] += part

        @pl.when(kk == nk - 1)
        def _():
            o_ref[...] = acc[...] if res is None else acc[...] + r_ref[...]

    in_specs = [pl.BlockSpec((tm, tk), lambda i, kk: (i, kk)),
                pl.BlockSpec((tk, n), lambda i, kk: (row_off // tk + kk, 0))]
    args = [a, w]
    if res is not None:
        in_specs.append(pl.BlockSpec((tm, n), lambda i, kk: (i, 0)))
        args.append(res)
    return pl.pallas_call(
        body, name=name, grid=(t // tm, nk), in_specs=in_specs,
        out_specs=pl.BlockSpec((tm, n), lambda i, kk: (i, 0)),
        out_shape=jax.ShapeDtypeStruct((t, n), F32),
        scratch_shapes=[pltpu.VMEM((tm, n), F32)],
        compiler_params=_cp("parallel", "arbitrary"))(*args)


def _matmul_tn(a, b, name, tm=512, tk=1024):
    t, m = a.shape
    n = b.shape[1]
    tk = min(tk, t)
    nk = t // tk
    assert m % tm == 0 and t % tk == 0

    def body(a_ref, b_ref, o_ref, acc):
        kk = pl.program_id(1)
        part = _dot(a_ref[...].astype(BF16), b_ref[...].astype(BF16), TN)

        @pl.when(kk == 0)
        def _():
            acc[...] = part

        @pl.when(kk > 0)
        def _():
            acc[...] += part

        @pl.when(kk == nk - 1)
        def _():
            o_ref[...] = acc[...].astype(BF16)

    return pl.pallas_call(
        body, name=name, grid=(m // tm, nk),
        in_specs=[pl.BlockSpec((tk, tm), lambda i, kk: (kk, i)),
                  pl.BlockSpec((tk, n), lambda i, kk: (kk, 0))],
        out_specs=pl.BlockSpec((tm, n), lambda i, kk: (i, 0)),
        out_shape=jax.ShapeDtypeStruct((m, n), BF16),
        scratch_shapes=[pltpu.VMEM((tm, n), F32)],
        compiler_params=_cp("parallel", "arbitrary"))(a, b)


def _rms(x, g):
    return x * lax.rsqrt(jnp.mean(x * x, axis=-1, keepdims=True) + EPS) * g


def _rms_fwd(x, g, name, tm=512):
    t, d = x.shape
    tm = min(tm, t)

    def body(x_ref, g_ref, o_ref):
        o_ref[...] = _rms(x_ref[...], g_ref[...]).astype(BF16)

    return pl.pallas_call(
        body, name=name, grid=(t // tm,),
        in_specs=[pl.BlockSpec((tm, d), lambda i: (i, 0)), pl.BlockSpec((1, d), lambda i: (0, 0))],
        out_specs=pl.BlockSpec((tm, d), lambda i: (i, 0)),
        out_shape=jax.ShapeDtypeStruct((t, d), BF16),
        compiler_params=_cp("parallel"))(x, g)


def _rms_bwd(x, g, dh, dres, name, tm=512):
    t, d = x.shape
    tm = min(tm, t)

    def body(x_ref, g_ref, dh_ref, dres_ref, dx_ref, dg_ref):
        _, vjp = jax.vjp(_rms, x_ref[...], g_ref[...])
        dx, dg = vjp(dh_ref[...])
        dx_ref[...] = dres_ref[...] + dx

        @pl.when(pl.program_id(0) == 0)
        def _():
            dg_ref[...] = jnp.zeros_like(dg_ref)

        dg_ref[...] += dg

    row = pl.BlockSpec((tm, d), lambda i: (i, 0))
    vec = pl.BlockSpec((1, d), lambda i: (0, 0))
    return pl.pallas_call(
        body, name=name, grid=(t // tm,), in_specs=[row, vec, row, row], out_specs=[row, vec],
        out_shape=[jax.ShapeDtypeStruct((t, d), F32), jax.ShapeDtypeStruct((1, d), F32)],
        compiler_params=_cp("arbitrary"))(x, g, dh, dres)


def _mix(ya, yb, ga, gb):
    return _sigmoid(ga) * ya + _sigmoid(gb) * yb


def _gate_specs(tm):
    half = D_MODEL // 2
    return [pl.BlockSpec((tm, half), lambda i, c=c: (i, c))
            for c in (COL_GA // half, COL_GA // half + 1, COL_GB // half, COL_GB // half + 1)]


def _mix_fwd(ya, yb, proj, name, tm=256):
    t, d = ya.shape
    tm = min(tm, t)

    def body(ya_ref, yb_ref, ga0, ga1, gb0, gb1, o_ref):
        ga = jnp.concatenate([ga0[...], ga1[...]], axis=1)
        gb = jnp.concatenate([gb0[...], gb1[...]], axis=1)
        o_ref[...] = _mix(ya_ref[...], yb_ref[...], ga, gb).astype(BF16)

    row = pl.BlockSpec((tm, d), lambda i: (i, 0))
    return pl.pallas_call(
        body, name=name, grid=(t // tm,), in_specs=[row, row] + _gate_specs(tm), out_specs=row,
        out_shape=jax.ShapeDtypeStruct((t, d), BF16),
        compiler_params=_cp("parallel"))(ya, yb, proj, proj, proj, proj)


def _mix_bwd(ya, yb, proj, dmix, name, tm=256):
    t, d = ya.shape
    tm = min(tm, t)

    def body(ya_ref, yb_ref, ga0, ga1, gb0, gb1, dm_ref, dya_ref, dyb_ref, dg_ref):
        ga = jnp.concatenate([ga0[...], ga1[...]], axis=1)
        gb = jnp.concatenate([gb0[...], gb1[...]], axis=1)
        _, vjp = jax.vjp(_mix, ya_ref[...], yb_ref[...], ga, gb)
        dya, dyb, dga, dgb = vjp(dm_ref[...])
        dya_ref[...] = dya.astype(BF16)
        dyb_ref[...] = dyb.astype(BF16)
        dg_ref[:, :d] = dga.astype(BF16)
        dg_ref[:, d:] = dgb.astype(BF16)

    row = pl.BlockSpec((tm, d), lambda i: (i, 0))
    wide = pl.BlockSpec((tm, 2 * d), lambda i: (i, 0))
    return pl.pallas_call(
        body, name=name, grid=(t // tm,), in_specs=[row, row] + _gate_specs(tm) + [row],
        out_specs=[row, row, wide],
        out_shape=[jax.ShapeDtypeStruct((t, d), BF16), jax.ShapeDtypeStruct((t, d), BF16),
                   jax.ShapeDtypeStruct((t, 2 * d), BF16)],
        compiler_params=_cp("parallel"))(ya, yb, proj, proj, proj, proj, dmix)


def _swiglu(g, u):
    return g * _sigmoid(g) * u


def _swiglu_fwd(gu, name, tm=256):
    t = gu.shape[0]
    tm = min(tm, t)
    f = FFN_HIDDEN

    def body(gu_ref, o_ref):
        o_ref[...] = _swiglu(gu_ref[:, :f], gu_ref[:, f:]).astype(BF16)

    return pl.pallas_call(
        body, name=name, grid=(t // tm,),
        in_specs=[pl.BlockSpec((tm, 2 * f), lambda i: (i, 0))],
        out_specs=pl.BlockSpec((tm, f), lambda i: (i, 0)),
        out_shape=jax.ShapeDtypeStruct((t, f), BF16),
        compiler_params=_cp("parallel"))(gu)


def _swiglu_bwd(gu, dact, name, tm=256):
    t = gu.shape[0]
    tm = min(tm, t)
    f = FFN_HIDDEN

    def body(gu_ref, da_ref, o_ref):
        _, vjp = jax.vjp(_swiglu, gu_ref[:, :f], gu_ref[:, f:])
        dg, du = vjp(da_ref[...])
        o_ref[:, :f] = dg.astype(BF16)
        o_ref[:, f:] = du.astype(BF16)

    return pl.pallas_call(
        body, name=name, grid=(t // tm,),
        in_specs=[pl.BlockSpec((tm, 2 * f), lambda i: (i, 0)), pl.BlockSpec((tm, f), lambda i: (i, 0))],
        out_specs=pl.BlockSpec((tm, 2 * f), lambda i: (i, 0)),
        out_shape=jax.ShapeDtypeStruct((t, 2 * f), BF16),
        compiler_params=_cp("parallel"))(gu, dact)


def _loss_head(x, g, target, name, tm=512):
    t, d = x.shape
    tm = min(tm, t)

    def body(x_ref, g_ref, t_ref, dx_ref, dg_ref, loss_ref):
        tgt = t_ref[...]

        def f(xv, gv):
            err = _rms(xv, gv) - tgt
            return 0.5 * jnp.sum(jnp.mean(err * err, axis=-1, keepdims=True))

        loss, vjp = jax.vjp(f, x_ref[...], g_ref[...])
        dx, dg = vjp(jnp.ones((), F32))
        dx_ref[...] = dx

        @pl.when(pl.program_id(0) == 0)
        def _():
            dg_ref[...] = jnp.zeros_like(dg_ref)
            loss_ref[...] = jnp.zeros_like(loss_ref)

        dg_ref[...] += dg
        loss_ref[...] += jnp.full(loss_ref.shape, loss, F32)

    row = pl.BlockSpec((tm, d), lambda i: (i, 0))
    vec = pl.BlockSpec((1, d), lambda i: (0, 0))
    lane = pl.BlockSpec((1, 128), lambda i: (0, 0))
    return pl.pallas_call(
        body, name=name, grid=(t // tm,), in_specs=[row, vec, row], out_specs=[row, vec, lane],
        out_shape=[jax.ShapeDtypeStruct((t, d), F32), jax.ShapeDtypeStruct((1, d), F32),
                   jax.ShapeDtypeStruct((1, 128), F32)],
        compiler_params=_cp("arbitrary"))(x, g, target)


def _lb_rows(l0, l1, l2, l3):
    mx = jnp.maximum(jnp.maximum(l0, l1), jnp.maximum(l2, l3))
    e0, e1, e2, e3 = jnp.exp(l0 - mx), jnp.exp(l1 - mx), jnp.exp(l2 - mx), jnp.exp(l3 - mx)
    s = e0 + e1 + e2 + e3
    p0, p1, p2, p3 = e0 / s, e1 / s, e2 / s, e3 / s
    c1 = p0 + p1
    c2 = c1 + p2
    c3 = c2 + p3
    return p0 - p0, c1 - p0, c2 - p0, c3 - p0


def _lb_fwd(lb_logits):
    def body(l_ref, o_ref):
        rows = _lb_rows(*[l_ref[pl.ds(i, 1), :] for i in range(DEPTH)])
        for i in range(DEPTH):
            o_ref[pl.ds(i, 1), :] = rows[i]

    return pl.pallas_call(body, name="lb_fwd", out_shape=jax.ShapeDtypeStruct(lb_logits.shape, F32))(lb_logits)


def _lb_bwd(lb_logits, dlb):
    def body(l_ref, d_ref, o_ref):
        _, vjp = jax.vjp(_lb_rows, *[l_ref[pl.ds(i, 1), :] for i in range(DEPTH)])
        grads = vjp(tuple(d_ref[pl.ds(i, 1), :] for i in range(DEPTH)))
        for i in range(DEPTH):
            o_ref[pl.ds(i, 1), :] = grads[i]

    return pl.pallas_call(body, name="lb_bwd", out_shape=jax.ShapeDtypeStruct(lb_logits.shape, F32))(lb_logits, dlb)


MESH = pl.DeviceIdType.MESH
ANY = pl.BlockSpec(memory_space=pl.ANY)
N_KINDS = len(SHARD_ROWS)
KIND_PLACE = ((0, 0), (1, 0), (2, 0), (3, 0), (4, 0), (4, N_DEV * SHARD_ROWS[4]), (5, 0))
FULL_ROWS = (N_DEV * SHARD_ROWS[0], D_MODEL, D_MODEL, D_MODEL, 2 * N_DEV * SHARD_ROWS[4], N_DEV * SHARD_ROWS[6])


def _position():
    x, y, c = lax.axis_index("x"), lax.axis_index("y"), lax.axis_index("c")
    return x, y, c, 4 * x + 2 * y + c


def _peer(x, y, c, r):
    px = 1 - x if r & 4 else x
    py = 1 - y if r & 2 else y
    pc = 1 - c if r & 1 else c
    return (px, py, pc), 4 * px + 2 * py + pc


class _Exchange(NamedTuple):
    operands: tuple
    out_shape: tuple
    copies: Callable


EXCHANGE_SCRATCH = (pltpu.SemaphoreType.DMA((N_DEV, N_KINDS)), pltpu.SemaphoreType.DMA((N_DEV, N_KINDS)),
                    pltpu.SemaphoreType.DMA((N_KINDS,)))


def _all_pairs(ends, send_sems, recv_sems):
    x, y, c, me = _position()
    out = []
    for r in range(1, N_DEV):
        peer, pid = _peer(x, y, c, r)
        for ti in range(N_KINDS):
            src, dst = ends(ti, me, pid)
            out.append(pltpu.make_async_remote_copy(
                src_ref=src, dst_ref=dst, send_sem=send_sems.at[r, ti], recv_sem=recv_sems.at[r, ti],
                device_id=peer, device_id_type=MESH))
    return out


def _gather_exchange(shards):
    def copies(ins, outs, send_sems, recv_sems, local_sems, arrivals):
        def window(ti, dev):
            oi, base = KIND_PLACE[ti]
            return outs[oi].at[pl.ds(base + dev * SHARD_ROWS[ti], SHARD_ROWS[ti]), :]

        if arrivals:
            return _all_pairs(lambda ti, me, pid: (ins[ti], window(ti, pid)), send_sems, recv_sems)
        _, _, _, me = _position()
        local = [pltpu.make_async_copy(ins[ti], window(ti, me), local_sems.at[ti]) for ti in range(N_KINDS)]
        return local + _all_pairs(lambda ti, me, pid: (ins[ti], window(ti, me)), send_sems, recv_sems)

    return _Exchange(tuple(shards), tuple(jax.ShapeDtypeStruct((rows, D_MODEL), BF16) for rows in FULL_ROWS), copies)


def _scatter_exchange(grads):
    def copies(ins, outs, send_sems, recv_sems, local_sems, arrivals):
        land = outs[0]

        def piece(ti, dev):
            ii, base = KIND_PLACE[ti]
            return ins[ii].at[pl.ds(base + dev * SHARD_ROWS[ti], SHARD_ROWS[ti]), :]

        def slot(ti, dev):
            return land.at[dev, pl.ds(SLOT_OFF[ti], SHARD_ROWS[ti]), :]

        if arrivals:
            return _all_pairs(lambda ti, me, pid: (piece(ti, me), slot(ti, pid)), send_sems, recv_sems)
        _, _, _, me = _position()
        local = [pltpu.make_async_copy(piece(ti, me), slot(ti, me), local_sems.at[ti]) for ti in range(N_KINDS)]
        return local + _all_pairs(lambda ti, me, pid: (piece(ti, pid), slot(ti, me)), send_sems, recv_sems)

    return _Exchange(tuple(grads), (jax.ShapeDtypeStruct((N_DEV, SLOT_ROWS, D_MODEL), BF16),), copies)


def _exchange_start(ex, ins, outs, sems):
    for cp in ex.copies(ins, outs, *sems, False):
        cp.start()


def _exchange_finish(ex, ins, outs, sems):
    for cp in ex.copies(ins, outs, *sems, True):
        cp.wait_recv()
    mine = ex.copies(ins, outs, *sems, False)
    for cp in mine[:N_KINDS]:
        cp.wait()
    for cp in mine[N_KINDS:]:
        cp.wait_send()


def _run_exchange(ex, name):
    n_in, n_out = len(ex.operands), len(ex.out_shape)

    def body(*refs):
        ins, outs, sems = refs[:n_in], refs[n_in:n_in + n_out], refs[n_in + n_out:]
        _exchange_start(ex, ins, outs, sems)
        _exchange_finish(ex, ins, outs, sems)

    return pl.pallas_call(body, name=name, in_specs=[ANY] * n_in, out_specs=[ANY] * n_out,
                          out_shape=list(ex.out_shape), scratch_shapes=list(EXCHANGE_SCRATCH))(*ex.operands)


def _carry(body, ex, n_in, n_out, n_scratch, n_steps):
    if ex is None:
        return body
    e_in, e_out = len(ex.operands), len(ex.out_shape)

    def carrying(*refs):
        own_in, ex_in = refs[:n_in], refs[n_in:n_in + e_in]
        rest = refs[n_in + e_in:]
        own_out, ex_out = rest[:n_out], rest[n_out:n_out + e_out]
        own_scratch, sems = rest[n_out + e_out:n_out + e_out + n_scratch], rest[n_out + e_out + n_scratch:]

        @pl.when(pl.program_id(0) == 0)
        def _():
            _exchange_start(ex, ex_in, ex_out, sems)

        body(*own_in, *own_out, *own_scratch)

        @pl.when(pl.program_id(0) == n_steps - 1)
        def _():
            _exchange_finish(ex, ex_in, ex_out, sems)

    return carrying


def _small_sum_body(p_ref, o_ref, buf, send_sems, recv_sems):
    x, y, c, me = _position()
    buf[me] = p_ref[...]
    sends = []
    for r in range(1, N_DEV):
        peer, _ = _peer(x, y, c, r)
        sends.append(pltpu.make_async_remote_copy(
            src_ref=p_ref, dst_ref=buf.at[me], send_sem=send_sems.at[r], recv_sem=recv_sems.at[r],
            device_id=peer, device_id_type=MESH))
    for cp in sends:
        cp.start()
    for r in range(1, N_DEV):
        peer, pid = _peer(x, y, c, r)
        pltpu.make_async_remote_copy(
            src_ref=p_ref, dst_ref=buf.at[pid], send_sem=send_sems.at[r], recv_sem=recv_sems.at[r],
            device_id=peer, device_id_type=MESH).wait_recv()
    for cp in sends:
        cp.wait_send()
    acc = buf[0]
    for k in range(1, N_DEV):
        acc = acc + buf[k]
    o_ref[...] = acc


def _all_reduce_small(part):
    rows, d = part.shape
    vmem = pl.BlockSpec(memory_space=pltpu.VMEM)
    return pl.pallas_call(
        functools.partial(_small_sum_body), name="all_reduce_small", in_specs=[vmem], out_specs=vmem,
        out_shape=jax.ShapeDtypeStruct((rows, d), F32),
        scratch_shapes=[pltpu.VMEM((N_DEV, rows, d), F32), pltpu.SemaphoreType.DMA((N_DEV,)),
                        pltpu.SemaphoreType.DMA((N_DEV,))],
    )(part)


def _hg_consts():
    c = HG_CHUNK
    r = lax.broadcasted_iota(jnp.int32, (c, c), 0)
    s = lax.broadcasted_iota(jnp.int32, (c, c), 1)
    return r, s, r // HG_SUB


def _split3(x):
    hi = x.astype(BF16)
    r1 = x - hi.astype(F32)
    mid = r1.astype(BF16)
    lo = (r1 - mid.astype(F32)).astype(BF16)
    return jnp.concatenate([hi, mid, lo], axis=1)


def _cumsum_rows(tri, x):
    w = x.shape[1]
    y = _dot(tri, _split3(x), NN)
    return y[:, :w] + y[:, w:2 * w] + y[:, 2 * w:]


def _hg_chunk(zq, zf, lb, tril, b_ref):
    c = HG_CHUNK
    sq = _sigmoid(zq)
    q = zq * sq
    sg = _sigmoid(zf)
    f = lb + (1.0 - lb) * sg
    logf = jnp.log(jnp.maximum(f, MIN_F))
    k = 1.0 - f
    b = _cumsum_rows(tril, logf)
    b_ref[...] = b
    refs = [jnp.zeros((1, b.shape[1]), F32)] + [b_ref[pl.ds(i * HG_SUB - 1, 1), :] for i in range(1, c // HG_SUB)]
    bc = b_ref[pl.ds(c - 1, 1), :]
    bs = jnp.concatenate([jnp.broadcast_to(ref, (HG_SUB, b.shape[1])) for ref in refs], axis=0)
    return sq, q, sg, f, k, b, bs, refs, bc


def _hg_gate(o, zg, gn):
    return o * lax.rsqrt(jnp.mean(o * o, axis=-1, keepdims=True) + EPS) * gn * (zg * _sigmoid(zg))


def _hgrn2_fwd(proj, lb, gn, name, ex=None):
    t = proj.shape[0]
    bs_tok = min(HG_BLOCK, t)
    n_chunks = bs_tok // HG_CHUNK
    w = HG_HEADS * HG_DK

    def body(hq_ref, hf_ref, hi_ref, hg_ref, lb_ref, gn_ref, o_ref, og_ref, sall_ref, st_ref, b_ref):
        @pl.when(pl.program_id(0) == 0)
        def _():
            st_ref[...] = jnp.zeros_like(st_ref)

        r, s, group = _hg_consts()
        causal = s <= r
        tril = causal.astype(BF16)

        def chunk(ci, carry):
            rows = pl.ds(pl.multiple_of(ci * HG_CHUNK, HG_CHUNK), HG_CHUNK)
            for h in range(HG_HEADS):
                cols = slice(h * HG_DK, (h + 1) * HG_DK)
                v = hi_ref[rows, cols]
                zg = hg_ref[rows, cols]
                _, q, _, _, k, b, bstart, refs, bc = _hg_chunk(hq_ref[rows, cols], hf_ref[rows, cols],
                                                               lb_ref[:, cols], tril, b_ref.at[h])
                st0 = st_ref[h]
                sall_ref[ci, h] = st0
                vb = v.astype(BF16)
                o = _dot((q * jnp.exp(b)).astype(BF16), st0.astype(BF16), NT)
                qt = (q * jnp.exp(b - bstart)).astype(BF16)
                a = jnp.zeros((HG_CHUNK, HG_CHUNK), F32)
                for i, ref in enumerate(refs):
                    kref = (k * jnp.exp(jnp.minimum(ref - b, HG_EXP_CLAMP))).astype(BF16)
                    a = a + jnp.where((group == i) & causal, _dot(qt, kref, NT), 0.0)
                o = o + _dot(a.astype(BF16), vb, NN)
                kdec = (k * jnp.exp(bc - b)).astype(BF16)
                st_ref[h] = st0 * jnp.exp(bc) + _dot(vb, kdec, TN)
                o_ref[rows, cols] = o
                og_ref[rows, cols] = _hg_gate(o, zg, gn_ref[:, cols]).astype(BF16)
            return carry

        lax.fori_loop(0, n_chunks, chunk, 0)

    def col(j):
        return pl.BlockSpec((bs_tok, w), lambda n, j=j: (n, j))

    vec = pl.BlockSpec((1, w), lambda n: (0, 0))
    ex_in, ex_out = (ex.operands, ex.out_shape) if ex else ((), ())
    outs = pl.pallas_call(
        _carry(body, ex, 6, 3, 2, t // bs_tok), name=name, grid=(t // bs_tok,),
        in_specs=[col(COL_HQ // w), col(COL_HF // w), col(COL_HI // w), col(COL_HG // w), vec, vec] + [ANY] * len(ex_in),
        out_specs=[col(0), col(0),
                   pl.BlockSpec((n_chunks, HG_HEADS, HG_DK, HG_DK), lambda n: (n, 0, 0, 0))] + [ANY] * len(ex_out),
        out_shape=[jax.ShapeDtypeStruct((t, w), F32), jax.ShapeDtypeStruct((t, w), BF16),
                   jax.ShapeDtypeStruct((t // HG_CHUNK, HG_HEADS, HG_DK, HG_DK), F32)] + list(ex_out),
        scratch_shapes=[pltpu.VMEM((HG_HEADS, HG_DK, HG_DK), F32), pltpu.VMEM((HG_HEADS, HG_CHUNK, HG_DK), F32)]
        + (list(EXCHANGE_SCRATCH) if ex else []),
        compiler_params=_cp("arbitrary"))(proj, proj, proj, proj, lb, gn, *ex_in)
    return outs[:3], outs[3:]


def _hgrn2_bwd(proj, lb, gn, o_hg, sall, dog, name, ex=None):
    t = proj.shape[0]
    bs_tok = min(HG_BLOCK, t)
    n_chunks = bs_tok // HG_CHUNK
    n_blocks = t // bs_tok
    w = HG_HEADS * HG_DK

    def body(hq_ref, hf_ref, hi_ref, hg_ref, lb_ref, gn_ref, o_ref, sall_ref, dog_ref,
             da_ref, dlb_ref, dgn_ref, dst_ref, b_ref):
        @pl.when(pl.program_id(0) == 0)
        def _():
            dst_ref[...] = jnp.zeros_like(dst_ref)
            dlb_ref[...] = jnp.zeros_like(dlb_ref)
            dgn_ref[...] = jnp.zeros_like(dgn_ref)

        r, s, group = _hg_consts()
        causal = s <= r
        tril = causal.astype(BF16)
        rev_tril = (s >= r).astype(BF16)

        def chunk(cj, carry):
            ci = n_chunks - 1 - cj
            rows = pl.ds(pl.multiple_of(ci * HG_CHUNK, HG_CHUNK), HG_CHUNK)
            for h in range(HG_HEADS):
                cols = slice(h * HG_DK, (h + 1) * HG_DK)
                zq = hq_ref[rows, cols]
                v = hi_ref[rows, cols]
                zg = hg_ref[rows, cols]
                lbv = lb_ref[:, cols]
                sq, q, sg, f, k, b, bstart, refs, bc = _hg_chunk(zq, hf_ref[rows, cols], lbv, tril, b_ref.at[h])
                st0 = sall_ref[ci, h]
                dst1 = dst_ref[h]
                vb = v.astype(BF16)
                eb = jnp.exp(b)
                ebs = jnp.exp(b - bstart)
                qg = (q * eb).astype(BF16)
                qt = (q * ebs).astype(BF16)
                ebcb = jnp.exp(bc - b)
                kdec = (k * ebcb).astype(BF16)
                ebc = jnp.exp(bc)
                st1 = st0 * ebc + _dot(vb, kdec, TN)

                _, gate_vjp = jax.vjp(_hg_gate, o_ref[rows, cols], zg, gn_ref[:, cols])
                do, dzg, dgn = gate_vjp(dog_ref[rows, cols])
                dob = do.astype(BF16)
                dam = jnp.where(causal, _dot(dob, vb, NT), 0.0)

                a = jnp.zeros((HG_CHUNK, HG_CHUNK), F32)
                dq_in = jnp.zeros((HG_CHUNK, HG_DK), F32)
                dk = ebcb * _dot(vb, dst1.astype(BF16), NN)
                for i, ref in enumerate(refs):
                    e_i = jnp.exp(jnp.minimum(ref - b, HG_EXP_CLAMP))
                    kref = (k * e_i).astype(BF16)
                    sel = group == i
                    a = a + jnp.where(sel & causal, _dot(qt, kref, NT), 0.0)
                    da_i = jnp.where(sel, dam, 0.0).astype(BF16)
                    dq_in = dq_in + _dot(da_i, kref, NN)
                    dk = dk + e_i * _dot(da_i, qt, TN)
                dq = eb * _dot(dob, st0.astype(BF16), NN) + ebs * dq_in
                dv = _dot(a.astype(BF16), dob, TN) + _dot(kdec, dst1.astype(BF16), NT)
                dst_ref[h] = dst1 * ebc + _dot(dob, qg, TN)

                dbx = jnp.sum(dst1 * st1, axis=0, keepdims=True)
                dlogf = _cumsum_rows(rev_tril, q * dq - k * dk) + dbx
                df = jnp.where(f > MIN_F, dlogf / f, 0.0) - dk
                dzf = df * (1.0 - lbv) * sg * (1.0 - sg)
                dzq = dq * (sq * (1.0 + zq * (1.0 - sq)))
                da_ref[rows, pl.ds(COL_HQ + h * HG_DK, HG_DK)] = dzq.astype(BF16)
                da_ref[rows, pl.ds(COL_HF + h * HG_DK, HG_DK)] = dzf.astype(BF16)
                da_ref[rows, pl.ds(COL_HI + h * HG_DK, HG_DK)] = dv.astype(BF16)
                da_ref[rows, pl.ds(COL_HG + h * HG_DK, HG_DK)] = dzg.astype(BF16)
                dlb_ref[:, cols] += jnp.sum(df * (1.0 - sg), axis=0, keepdims=True)
                dgn_ref[:, cols] += dgn
            return carry

        lax.fori_loop(0, n_chunks, chunk, 0)

    def col(j):
        return pl.BlockSpec((bs_tok, w), lambda n, j=j: (n_blocks - 1 - n, j))

    vec = pl.BlockSpec((1, w), lambda n: (0, 0))
    ex_in, ex_out = (ex.operands, ex.out_shape) if ex else ((), ())
    outs = pl.pallas_call(
        _carry(body, ex, 9, 3, 2, n_blocks), name=name, grid=(n_blocks,),
        in_specs=[col(COL_HQ // w), col(COL_HF // w), col(COL_HI // w), col(COL_HG // w), vec, vec, col(0),
                  pl.BlockSpec((n_chunks, HG_HEADS, HG_DK, HG_DK), lambda n: (n_blocks - 1 - n, 0, 0, 0)),
                  col(0)] + [ANY] * len(ex_in),
        out_specs=[pl.BlockSpec((bs_tok, 4 * w), lambda n: (n_blocks - 1 - n, 0)), vec, vec] + [ANY] * len(ex_out),
        out_shape=[jax.ShapeDtypeStruct((t, 4 * w), BF16), jax.ShapeDtypeStruct((1, w), F32),
                   jax.ShapeDtypeStruct((1, w), F32)] + list(ex_out),
        scratch_shapes=[pltpu.VMEM((HG_HEADS, HG_DK, HG_DK), F32), pltpu.VMEM((HG_HEADS, HG_CHUNK, HG_DK), F32)]
        + (list(EXCHANGE_SCRATCH) if ex else []),
        compiler_params=_cp("arbitrary"))(proj, proj, proj, proj, lb, gn, o_hg, sall, dog, *ex_in)
    return outs[:3], outs[3:]


def _rope_tables(t):
    half = ROPE_DIM // 2
    inv = ROPE_THETA ** (-jnp.arange(half, dtype=F32) * 2.0 / ROPE_DIM)
    ang = jnp.arange(t).astype(F32)[:, None] * inv[None, :]
    cos, sin = jnp.cos(ang), jnp.sin(ang)
    pad = ATT_HEAD_DIM - ROPE_DIM
    c = jnp.concatenate([cos, cos, jnp.ones((t, pad), F32)], axis=1)
    su = jnp.concatenate([-sin, jnp.zeros((t, half + pad), F32)], axis=1)
    sd = jnp.concatenate([jnp.zeros((t, half), F32), sin, jnp.zeros((t, pad), F32)], axis=1)
    return tuple(jnp.concatenate([m, m], axis=1) for m in (c, su, sd))


def _rope(x, tabs):
    c, su, sd = tabs
    n = x.shape[1]
    half = ROPE_DIM // 2
    return x * c + pltpu.roll(x, n - half, 1) * su + pltpu.roll(x, half, 1) * sd


def _rope_t(dy, tabs):
    c, su, sd = tabs
    n = dy.shape[1]
    half = ROPE_DIM // 2
    return dy * c + pltpu.roll(dy * su, half, 1) + pltpu.roll(dy * sd, n - half, 1)


def _swa_specs(n_blocks, clamp):
    blk = ATT_BLOCK

    def cur(n):
        return jnp.minimum(n, n_blocks - 1) if clamp else n

    def prev(n):
        return jnp.maximum(cur(n) - 1, 0)

    q_spec = pl.BlockSpec((blk, 512), lambda m, n: (cur(n), COL_AQ // 512 + m))
    kv = [pl.BlockSpec((blk, 128), lambda m, n, c=c, f=f: (f(n), c + m))
          for c in (COL_AK // 128, COL_AV // 128) for f in (cur, prev)]
    tabs = [pl.BlockSpec((blk, 128), lambda m, n, f=f: (f(n), 0)) for f in (cur, prev) for _ in range(3)]
    return q_spec, kv, tabs, cur, prev


def _swa_scores(qm, kd, sink, mask):
    s = _dot(qm, kd, NT) * (ATT_HEAD_DIM ** -0.5)
    s = jnp.where(mask, s, -jnp.inf)
    mx = jnp.maximum(jnp.max(s, axis=-1, keepdims=True), sink)
    p = jnp.exp(s - mx)
    es = jnp.exp(sink - mx)
    den = jnp.sum(p, axis=-1, keepdims=True) + es
    return p / den, es / den


def _swa_window(kc_ref, kp_ref, vc_ref, vp_ref, tabs_c, tabs_p, n):
    k2 = jnp.concatenate([_rope(kp_ref[...], tabs_p), _rope(kc_ref[...], tabs_c)], axis=0)
    v2 = jnp.concatenate([vp_ref[...], vc_ref[...]], axis=0)
    blk = ATT_BLOCK
    qi = lax.broadcasted_iota(jnp.int32, (blk, 2 * blk), 0)
    kj = lax.broadcasted_iota(jnp.int32, (blk, 2 * blk), 1)
    delta = qi + blk - kj
    mask = (delta >= 0) & (delta < blk) & ((kj >= blk) | (n > 0))
    return k2, v2, mask


def _swa_fwd(proj, sinks, tabs, name):
    t = proj.shape[0]
    n_blocks = t // ATT_BLOCK
    q_spec, kv_specs, tab_specs, _, _ = _swa_specs(n_blocks, clamp=False)

    def body(q_ref, kc_ref, kp_ref, vc_ref, vp_ref, c0, c1, c2, p0, p1, p2, sink_ref, o_ref):
        m, n = pl.program_id(0), pl.program_id(1)
        tabs_c = (c0[...], c1[...], c2[...])
        tabs_p = (p0[...], p1[...], p2[...])
        k2, v2, mask = _swa_window(kc_ref, kp_ref, vc_ref, vp_ref, tabs_c, tabs_p, n)
        k2r, v2r = pltpu.roll(k2, 64, 1), pltpu.roll(v2, 64, 1)
        upper_k = lax.broadcasted_iota(jnp.int32, k2.shape, 1) >= 64
        upper_q = lax.broadcasted_iota(jnp.int32, (ATT_BLOCK, 128), 1) >= 64
        for jj in range(2):
            own = upper_k if jj else ~upper_k
            kd = jnp.where(own, k2, k2r).astype(BF16)
            vd = jnp.where(own, v2, v2r).astype(BF16)
            for pi in range(2):
                cols = slice(256 * jj + 128 * pi, 256 * jj + 128 * pi + 128)
                qp = _rope(q_ref[:, cols], tabs_c)
                outs = []
                for e in range(2):
                    sink = sink_ref[0, 8 * m + 4 * jj + 2 * pi + e]
                    qm = jnp.where(upper_q if e else ~upper_q, qp, 0.0).astype(BF16)
                    pn, _ = _swa_scores(qm, kd, sink, mask)
                    outs.append(_dot(pn.astype(BF16), vd, NN))
                o_ref[:, cols] = jnp.where(upper_q, outs[1], outs[0]).astype(BF16)

    return pl.pallas_call(
        body, name=name, grid=(2, n_blocks),
        in_specs=[q_spec] + kv_specs + tab_specs + [pl.BlockSpec(memory_space=pltpu.SMEM)],
        out_specs=pl.BlockSpec((ATT_BLOCK, 512), lambda m, n: (n, m)),
        out_shape=jax.ShapeDtypeStruct((t, ATT_Q_HEADS * ATT_HEAD_DIM), BF16),
        compiler_params=_cp("parallel", "arbitrary"))(proj, proj, proj, proj, proj, *tabs, *tabs, sinks)


def _swa_bwd(proj, sinks, tabs, o_att, do_att, name):
    t = proj.shape[0]
    n_blocks = t // ATT_BLOCK
    blk = ATT_BLOCK
    q_spec, kv_specs, tab_specs, cur, prev = _swa_specs(n_blocks, clamp=True)

    def body(q_ref, kc_ref, kp_ref, vc_ref, vp_ref, c0, c1, c2, p0, p1, p2, sink_ref, o_ref, do_ref,
             dq_ref, dk_ref, dv_ref, ds_ref, ck_ref, cv_ref):
        m, n = pl.program_id(0), pl.program_id(1)

        @pl.when(n == 0)
        def _():
            ds_ref[...] = jnp.zeros_like(ds_ref)
            ck_ref[...] = jnp.zeros_like(ck_ref)
            cv_ref[...] = jnp.zeros_like(cv_ref)

        @pl.when(n < n_blocks)
        def _():
            tabs_c = (c0[...], c1[...], c2[...])
            tabs_p = (p0[...], p1[...], p2[...])
            k2, v2, mask = _swa_window(kc_ref, kp_ref, vc_ref, vp_ref, tabs_c, tabs_p, n)
            k2r, v2r = pltpu.roll(k2, 64, 1), pltpu.roll(v2, 64, 1)
            upper_k = lax.broadcasted_iota(jnp.int32, k2.shape, 1) >= 64
            upper_q = lax.broadcasted_iota(jnp.int32, (blk, 128), 1) >= 64
            lane = lax.broadcasted_iota(jnp.int32, (8, 128), 1)
            dk2 = jnp.zeros(k2.shape, F32)
            dv2 = jnp.zeros(k2.shape, F32)
            dsv = jnp.zeros((8, 128), F32)
            for jj in range(2):
                own = upper_k if jj else ~upper_k
                kd = jnp.where(own, k2, k2r).astype(BF16)
                vd = jnp.where(own, v2, v2r).astype(BF16)
                dkd = jnp.zeros(k2.shape, F32)
                dvd = jnp.zeros(k2.shape, F32)
                for pi in range(2):
                    cols = slice(256 * jj + 128 * pi, 256 * jj + 128 * pi + 128)
                    qp = _rope(q_ref[:, cols], tabs_c)
                    do_pair = do_ref[:, cols]
                    o_pair = o_ref[:, cols].astype(F32)
                    dqs = []
                    for e in range(2):
                        hl = 4 * jj + 2 * pi + e
                        sink = sink_ref[0, 8 * m + hl]
                        half = upper_q if e else ~upper_q
                        qm = jnp.where(half, qp, 0.0).astype(BF16)
                        pn, ps = _swa_scores(qm, kd, sink, mask)
                        dom = jnp.where(half, do_pair, 0.0)
                        delta = jnp.sum(dom * o_pair, axis=-1, keepdims=True)
                        domb = dom.astype(BF16)
                        dp = _dot(domb, vd, NT)
                        dsb = (pn * (dp - delta) * (ATT_HEAD_DIM ** -0.5)).astype(BF16)
                        dqs.append(_dot(dsb, kd, NN))
                        dkd = dkd + _dot(dsb, qm, TN)
                        dvd = dvd + _dot(pn.astype(BF16), domb, TN)
                        dsv = dsv + jnp.where(lane == hl, -jnp.sum(ps * delta), 0.0)
                    dq_ref[:, cols] = _rope_t(jnp.where(upper_q, dqs[1], dqs[0]), tabs_c).astype(BF16)
                dk2 = dk2 + jnp.where(own, dkd + pltpu.roll(dkd, 64, 1), 0.0)
                dv2 = dv2 + jnp.where(own, dvd + pltpu.roll(dvd, 64, 1), 0.0)
            dk_ref[...] = (ck_ref[...] + _rope_t(dk2[:blk], tabs_p)).astype(BF16)
            dv_ref[...] = (cv_ref[...] + dv2[:blk]).astype(BF16)
            ck_ref[...] = _rope_t(dk2[blk:], tabs_c)
            cv_ref[...] = dv2[blk:]
            ds_ref[...] += dsv

        @pl.when(n == n_blocks)
        def _():
            dk_ref[...] = ck_ref[...].astype(BF16)
            dv_ref[...] = cv_ref[...].astype(BF16)

    wide = pl.BlockSpec((blk, 512), lambda m, n: (cur(n), m))
    lagged = pl.BlockSpec((blk, 128), lambda m, n: (jnp.maximum(n - 1, 0), m))
    return pl.pallas_call(
        body, name=name, grid=(2, n_blocks + 1),
        in_specs=[q_spec] + kv_specs + tab_specs + [pl.BlockSpec(memory_space=pltpu.SMEM), wide, wide],
        out_specs=[wide, lagged, lagged, pl.BlockSpec((None, 8, 128), lambda m, n: (m, 0, 0))],
        out_shape=[jax.ShapeDtypeStruct((t, 1024), BF16), jax.ShapeDtypeStruct((t, 256), BF16),
                   jax.ShapeDtypeStruct((t, 256), BF16), jax.ShapeDtypeStruct((2, 8, 128), F32)],
        scratch_shapes=[pltpu.VMEM((blk, 128), F32), pltpu.VMEM((blk, 128), F32)],
        compiler_params=_cp("arbitrary", "arbitrary"))(proj, proj, proj, proj, proj, *tabs, *tabs, sinks, o_att, do_att)


def _local_step(x, target, shards, norm1, lb_logits, hg_norm, attn_sinks, norm2, final_norm):
    t = x.shape[0]
    tabs = _rope_tables(t)
    lb_all = _lb_fwd(lb_logits)
    saved = []
    weights = _run_exchange(_gather_exchange([s[0] for s in shards]), "gather_weights")
    for l in range(DEPTH):
        win_t, w_pa, w_pb, w_o, wgu_t, w_d = weights
        n1, n2 = norm1[l][None, :], norm2[l][None, :]
        lb, gn, sinks = lb_all[l][None, :], hg_norm[l][None, :], attn_sinks[l][None, :]
        h = _rms_fwd(x, n1, "rms1_fwd")
        proj = _matmul_nt(h, win_t, 0, IN_COLS, F32, "proj_fwd", tn=1280)
        ex = _gather_exchange([s[l + 1] for s in shards]) if l + 1 < DEPTH else None
        (o_hg, o_g, sall), next_weights = _hgrn2_fwd(proj, lb, gn, "hgrn2_fwd", ex)
        o_att = _swa_fwd(proj, sinks, tabs, "swa_fwd")
        ya = _matmul_nn(o_g, w_pa, 0, None, "ya_fwd")
        yb = _matmul_nn(o_att, w_pb, 0, None, "yb_fwd")
        mix = _mix_fwd(ya, yb, proj, "mix_fwd")
        x1 = _matmul_nn(mix, w_o, 0, x, "wo_fwd")
        h2 = _rms_fwd(x1, n2, "rms2_fwd")
        gu = _matmul_nt(h2, wgu_t, 0, 2 * FFN_HIDDEN, F32, "gu_fwd", tn=1408)
        act = _swiglu_fwd(gu, "swiglu_fwd")
        x2 = _matmul_nn(act, w_d, 0, x1, "wd_fwd")
        saved.append((x, h, proj, o_hg, o_g, sall, o_att, ya, yb, mix, x1, h2, gu, act, n1, n2, lb, gn, sinks, weights))
        x, weights = x2, next_weights

    dx, d_fn, loss = _loss_head(x, final_norm[None, :], target, "loss_head")

    owned = [None] * DEPTH
    pending = None
    d_n1, d_n2, d_lb, d_gn, d_sinks = ([None] * DEPTH for _ in range(5))
    for l in reversed(range(DEPTH)):
        x0, h, proj, o_hg, o_g, sall, o_att, ya, yb, mix, x1, h2, gu, act, n1, n2, lb, gn, sinks, weights = saved[l]
        win_t, w_pa, w_pb, w_o, wgu_t, w_d = weights
        dact = _matmul_nt(dx, w_d, 0, FFN_HIDDEN, F32, "wd_bwd", tn=1408)
        dgu = _swiglu_bwd(gu, dact, "swiglu_bwd")
        g_wd = _matmul_tn(act, dx, "wd_grad", tm=1408)
        g_wgu = _matmul_tn(dgu, h2, "wgu_grad", tm=1408)
        dh2 = _matmul_nn(dgu, wgu_t, 0, None, "gu_bwd", tk=FFN_HIDDEN)
        dx1, d_n2[l] = _rms_bwd(x1, n2, dh2, dx, "rms2_bwd")
        dmix = _matmul_nt(dx1, w_o, 0, D_MODEL, F32, "wo_bwd", tn=1024)
        g_wo = _matmul_tn(mix, dx1, "wo_grad")
        dya, dyb, dgab = _mix_bwd(ya, yb, proj, dmix, "mix_bwd")
        g_wpa = _matmul_tn(o_g, dya, "wpa_grad")
        g_wpb = _matmul_tn(o_att, dyb, "wpb_grad")
        dog = _matmul_nt(dya, w_pa, 0, D_MODEL, F32, "wpa_bwd", tn=1024)
        doatt = _matmul_nt(dyb, w_pb, 0, D_MODEL, F32, "wpb_bwd", tn=1024)
        ex = _scatter_exchange(pending) if pending is not None else None
        (dhg, d_lb[l], d_gn[l]), land = _hgrn2_bwd(proj, lb, gn, o_hg, sall, dog, "hgrn2_bwd", ex)
        if pending is not None:
            owned[l + 1] = _sum_slots(land[0], "sum_slots")
        daq, dak, dav, d_sinks[l] = _swa_bwd(proj, sinks, tabs, o_att, doatt, "swa_bwd")
        dakv = jnp.concatenate([dak, dav], axis=1)
        g_win = jnp.concatenate([_matmul_tn(dhg, h, "win_grad_hg"), _matmul_tn(daq, h, "win_grad_aq"),
                                 _matmul_tn(dakv, h, "win_grad_akv"), _matmul_tn(dgab, h, "win_grad_gates")], axis=0)
        dh = _matmul_nn(dhg, win_t, COL_HQ, None, "win_bwd_hg", tk=2048)
        dh = _matmul_nn(daq, win_t, COL_AQ, dh, "win_bwd_aq")
        dh = _matmul_nn(dakv, win_t, COL_AK, dh, "win_bwd_akv")
        dh = _matmul_nn(dgab, win_t, COL_GA, dh, "win_bwd_gates", tk=512)
        dx, d_n1[l] = _rms_bwd(x0, n1, dh, dx1, "rms1_bwd")
        pending = (g_win, g_wpa, g_wpb, g_wo, g_wgu, g_wd)
    owned[0] = _sum_slots(_run_exchange(_scatter_exchange(pending), "scatter_grads")[0], "sum_slots")

    d_sink_rows = [jnp.concatenate([d[0, 0, :8], d[1, 0, :8]]) for d in d_sinks]
    small = (jnp.concatenate(d_n1, axis=0), jnp.concatenate(d_lb, axis=0), jnp.concatenate(d_gn, axis=0),
             jnp.concatenate(d_n2, axis=0), d_fn, jnp.stack(d_sink_rows, axis=0))
    return loss, dx, jnp.stack(owned, axis=0), small


def _sum_slots(land, name, tr=480):
    _, rows, d = land.shape

    def body(l_ref, o_ref):
        acc = l_ref[0].astype(F32)
        for k in range(1, N_DEV):
            acc = acc + l_ref[k].astype(F32)
        o_ref[...] = acc

    return pl.pallas_call(
        body, name=name, grid=(rows // tr,),
        in_specs=[pl.BlockSpec((N_DEV, tr, d), lambda i: (0, i, 0))],
        out_specs=pl.BlockSpec((tr, d), lambda i: (i, 0)),
        out_shape=jax.ShapeDtypeStruct((rows, d), F32),
        compiler_params=_cp("parallel"))(land)


def _adamw(w, g, m, v, name):
    shape = w.shape
    c = shape[-1]
    rows = w.size // c
    tr = rows
    for cand in (512, 352, 128):
        if rows % cand == 0:
            tr = cand
            break
    c1 = 1.0 / (1.0 - ADAM_B1 ** ADAM_STEP)
    c2 = 1.0 / (1.0 - ADAM_B2 ** ADAM_STEP)

    def body(w_ref, g_ref, m_ref, v_ref, d_ref, nm_ref, nv_ref):
        gv = g_ref[...]
        nm = ADAM_B1 * m_ref[...] + (1.0 - ADAM_B1) * gv
        nv = ADAM_B2 * v_ref[...] + (1.0 - ADAM_B2) * (gv * gv)
        d_ref[...] = -ADAM_LR * ((nm * c1) / (jnp.sqrt(nv * c2) + ADAM_EPS) + ADAM_WD * w_ref[...])
        nm_ref[...] = nm
        nv_ref[...] = nv

    spec = pl.BlockSpec((tr, c), lambda i: (i, 0))
    outs = pl.pallas_call(
        body, name=name, grid=(rows // tr,), in_specs=[spec] * 4, out_specs=[spec] * 3,
        out_shape=[jax.ShapeDtypeStruct((rows, c), F32)] * 3,
        compiler_params=_cp("parallel"))(*[a.reshape(rows, c) for a in (w, g, m, v)])
    return tuple(o.reshape(shape) for o in outs)


def kernel(x, norm1, w_in, lb_logits, hg_norm, attn_sinks, w_pa, w_pb, w_o, norm2, w_gate, w_up, w_down, final_norm, loss_target, m_norm1, m_w_in, m_lb_logits, m_hg_norm, m_attn_sinks, m_w_pa, m_w_pb, m_w_o, m_norm2, m_w_gate, m_w_up, m_w_down, m_final_norm, v_norm1, v_w_in, v_lb_logits, v_hg_norm, v_attn_sinks, v_w_pa, v_w_pb, v_w_o, v_norm2, v_w_gate, v_w_up, v_w_down, v_final_norm):
    t = x.shape[1]
    shards = [jnp.swapaxes(w_in, 1, 2).astype(BF16), w_pa.astype(BF16), w_pb.astype(BF16), w_o.astype(BF16),
              jnp.swapaxes(w_gate, 1, 2).astype(BF16), jnp.swapaxes(w_up, 1, 2).astype(BF16), w_down.astype(BF16)]
    loss_lanes, grad_x, owned, small = _local_step(
        x.reshape(t, D_MODEL), loss_target.reshape(t, D_MODEL), shards,
        norm1, lb_logits, hg_norm, attn_sinks, norm2, final_norm)

    def rows_of(ti, transpose):
        g = owned[:, SLOT_OFF[ti]:SLOT_OFF[ti] + SHARD_ROWS[ti], :]
        return jnp.swapaxes(g, 1, 2) if transpose else g

    g_big = {"w_in": rows_of(0, True), "w_pa": rows_of(1, False), "w_pb": rows_of(2, False), "w_o": rows_of(3, False),
             "w_gate": rows_of(4, True), "w_up": rows_of(5, True), "w_down": rows_of(6, False)}

    d_n1, d_lb, d_gn, d_n2, d_fn, d_sinks = small
    pad = jnp.zeros((DEPTH, D_MODEL - ATT_Q_HEADS), F32)
    packed = jnp.concatenate([
        d_n1, d_lb, d_gn, d_n2, d_fn, jnp.concatenate([d_sinks, pad], axis=1),
        jnp.concatenate([loss_lanes, jnp.zeros((1, D_MODEL - 128), F32)], axis=1),
        jnp.zeros((SMALL_ROWS - 22, D_MODEL), F32)], axis=0)
    total = _all_reduce_small(packed)
    loss = total[21, 0]
    g_small = {"norm1": total[0:4], "lb_logits": _lb_bwd(lb_logits, total[4:8]), "hg_norm": total[8:12],
               "norm2": total[12:16], "final_norm": total[16], "attn_sinks": total[17:21, :ATT_Q_HEADS]}

    params = {"norm1": (norm1, m_norm1, v_norm1), "w_in": (w_in, m_w_in, v_w_in),
              "lb_logits": (lb_logits, m_lb_logits, v_lb_logits), "hg_norm": (hg_norm, m_hg_norm, v_hg_norm),
              "attn_sinks": (attn_sinks, m_attn_sinks, v_attn_sinks), "w_pa": (w_pa, m_w_pa, v_w_pa),
              "w_pb": (w_pb, m_w_pb, v_w_pb), "w_o": (w_o, m_w_o, v_w_o), "norm2": (norm2, m_norm2, v_norm2),
              "w_gate": (w_gate, m_w_gate, v_w_gate), "w_up": (w_up, m_w_up, v_w_up),
              "w_down": (w_down, m_w_down, v_w_down), "final_norm": (final_norm, m_final_norm, v_final_norm)}
    order = ["norm1", "w_in", "lb_logits", "hg_norm", "attn_sinks", "w_pa", "w_pb", "w_o", "norm2",
             "w_gate", "w_up", "w_down", "final_norm"]
    grads, deltas, new_m, new_v = [], [], [], []
    for name in order:
        w, m, v = params[name]
        g = (g_big[name] if name in g_big else g_small[name]).reshape(w.shape)
        w2 = w.reshape(1, -1) if w.ndim == 1 else w
        d, nm, nv = _adamw(w2, g.reshape(w2.shape), m.reshape(w2.shape), v.reshape(w2.shape), "adamw_" + name)
        grads.append(g)
        deltas.append(d.reshape(w.shape))
        new_m.append(nm.reshape(w.shape))
        new_v.append(nv.reshape(w.shape))
    return (loss, grad_x.reshape(x.shape), *grads, *deltas, *new_m, *new_v)
```

```python
import functools
from typing import Callable, NamedTuple

import jax
import jax.numpy as jnp
from jax import lax
from jax.experimental import pallas as pl
from jax.experimental.pallas import tpu as pltpu

F32, BF16 = jnp.float32, jnp.bfloat16

D_MODEL = 1024
DEPTH = 4
N_DEV = 8
HG_HEADS = 8
HG_DK = 128
HG_CHUNK = 64
HG_BLOCK = 256
HG_EXP_CLAMP = 60.0
ATT_Q_HEADS = 16
ATT_HEAD_DIM = 64
ATT_BLOCK = 128
ROPE_THETA = 500000.0
ROPE_DIM = 16
FFN_HIDDEN = 2816
EPS = 1e-6
MIN_F = 1e-30
ADAM_LR, ADAM_B1, ADAM_B2, ADAM_EPS, ADAM_WD, ADAM_STEP = 0.001, 0.9, 0.999, 1e-08, 0.01, 10

COL_HQ, COL_HF, COL_HI, COL_HG = 0, 1024, 2048, 3072
COL_AQ, COL_AK, COL_AV, COL_GA, COL_GB = 4096, 5120, 5376, 5632, 6656
IN_COLS = 7680

SHARD_ROWS = (960, 128, 128, 128, 352, 352, 352)
SLOT_OFF = (0, 960, 1088, 1216, 1344, 1696, 2048)
SLOT_ROWS = 2400
SMALL_ROWS = 24

VMEM_LIMIT_BYTES = 56 * 1024 * 1024

NN = ((1,), (0,))
NT = ((1,), (1,))
TN = ((0,), (0,))


def _dot(a, b, dims):
    return lax.dot_general(a, b, (dims, ((), ())), preferred_element_type=F32)


def _cp(*sem):
    return pltpu.CompilerParams(dimension_semantics=sem if sem else None, vmem_limit_bytes=VMEM_LIMIT_BYTES)


def _sigmoid(x):
    return 1.0 / (1.0 + jnp.exp(-x))


def _matmul_nt(a, w, row_off, n, out_dtype, name, tm=1024, tn=512):
    t, k = a.shape
    tm = min(tm, t)
    assert n % tn == 0 and row_off % tn == 0 and t % tm == 0

    def body(a_ref, w_ref, o_ref):
        o_ref[...] = _dot(a_ref[...].astype(BF16), w_ref[...], NT).astype(o_ref.dtype)

    return pl.pallas_call(
        body, name=name, grid=(n // tn, t // tm),
        in_specs=[pl.BlockSpec((tm, k), lambda j, i: (i, 0)),
                  pl.BlockSpec((tn, k), lambda j, i: (row_off // tn + j, 0))],
        out_specs=pl.BlockSpec((tm, tn), lambda j, i: (i, j)),
        out_shape=jax.ShapeDtypeStruct((t, n), out_dtype),
        compiler_params=_cp("parallel", "parallel"))(a, w)


def _matmul_nn(a, w, row_off, res, name, tm=512, tk=None):
    t, k = a.shape
    n = w.shape[1]
    tm = min(tm, t)
    tk = tk or k
    nk = k // tk
    assert k % tk == 0 and row_off % tk == 0 and t % tm == 0

    def body(*refs):
        if res is None:
            a_ref, w_ref, o_ref, acc = refs
        else:
            a_ref, w_ref, r_ref, o_ref, acc = refs
        kk = pl.program_id(1)
        part = _dot(a_ref[...].astype(BF16), w_ref[...], NN)

        @pl.when(kk == 0)
        def _():
            acc[...] = part

        @pl.when(kk > 0)
        def _():
            acc[...] += part

        @pl.when(kk == nk - 1)
        def _():
            o_ref[...] = acc[...] if res is None else acc[...] + r_ref[...]

    in_specs = [pl.BlockSpec((tm, tk), lambda i, kk: (i, kk)),
                pl.BlockSpec((tk, n), lambda i, kk: (row_off // tk + kk, 0))]
    args = [a, w]
    if res is not None:
        in_specs.append(pl.BlockSpec((tm, n), lambda i, kk: (i, 0)))
        args.append(res)
    return pl.pallas_call(
        body, name=name, grid=(t // tm, nk), in_specs=in_specs,
        out_specs=pl.BlockSpec((tm, n), lambda i, kk: (i, 0)),
        out_shape=jax.ShapeDtypeStruct((t, n), F32),
        scratch_shapes=[pltpu.VMEM((tm, n), F32)],
        compiler_params=_cp("parallel", "arbitrary"))(*args)


def _matmul_tn(a, b, name, tm=512, tk=1024):
    t, m = a.shape
    n = b.shape[1]
    tk = min(tk, t)
    nk = t // tk
    assert m % tm == 0 and t % tk == 0

    def body(a_ref, b_ref, o_ref, acc):
        kk = pl.program_id(1)
        part = _dot(a_ref[...].astype(BF16), b_ref[...].astype(BF16), TN)

        @pl.when(kk == 0)
        def _():
            acc[...] = part

        @pl.when(kk > 0)
        def _():
            acc[...] += part

        @pl.when(kk == nk - 1)
        def _():
            o_ref[...] = acc[...].astype(BF16)

    return pl.pallas_call(
        body, name=name, grid=(m // tm, nk),
        in_specs=[pl.BlockSpec((tk, tm), lambda i, kk: (kk, i)),
                  pl.BlockSpec((tk, n), lambda i, kk: (kk, 0))],
        out_specs=pl.BlockSpec((tm, n), lambda i, kk: (i, 0)),
        out_shape=jax.ShapeDtypeStruct((m, n), BF16),
        scratch_shapes=[pltpu.VMEM((tm, n), F32)],
        compiler_params=_cp("parallel", "arbitrary"))(a, b)


def _rms(x, g):
    return x * lax.rsqrt(jnp.mean(x * x, axis=-1, keepdims=True) + EPS) * g


def _rms_fwd(x, g, name, tm=512):
    t, d = x.shape
    tm = min(tm, t)

    def body(x_ref, g_ref, o_ref):
        o_ref[...] = _rms(x_ref[...], g_ref[...]).astype(BF16)

    return pl.pallas_call(
        body, name=name, grid=(t // tm,),
        in_specs=[pl.BlockSpec((tm, d), lambda i: (i, 0)), pl.BlockSpec((1, d), lambda i: (0, 0))],
        out_specs=pl.BlockSpec((tm, d), lambda i: (i, 0)),
        out_shape=jax.ShapeDtypeStruct((t, d), BF16),
        compiler_params=_cp("parallel"))(x, g)


def _mix(ya, yb, ga, gb):
    return _sigmoid(ga) * ya + _sigmoid(gb) * yb


def _gate_specs(tm):
    half = D_MODEL // 2
    return [pl.BlockSpec((tm, half), lambda i, c=c: (i, c))
            for c in (COL_GA // half, COL_GA // half + 1, COL_GB // half, COL_GB // half + 1)]


def _mix_bwd(ya, yb, proj, dmix, name, tm=256):
    t, d = ya.shape
    tm = min(tm, t)

    def body(ya_ref, yb_ref, ga0, ga1, gb0, gb1, dm_ref, dya_ref, dyb_ref, dg_ref):
        ga = jnp.concatenate([ga0[...], ga1[...]], axis=1)
        gb = jnp.concatenate([gb0[...], gb1[...]], axis=1)
        _, vjp = jax.vjp(_mix, ya_ref[...].astype(F32), yb_ref[...].astype(F32), ga, gb)
        dya, dyb, dga, dgb = vjp(dm_ref[...])
        dya_ref[...] = dya.astype(BF16)
        dyb_ref[...] = dyb.astype(BF16)
        dg_ref[:, :d] = dga.astype(BF16)
        dg_ref[:, d:] = dgb.astype(BF16)

    row = pl.BlockSpec((tm, d), lambda i: (i, 0))
    wide = pl.BlockSpec((tm, 2 * d), lambda i: (i, 0))
    return pl.pallas_call(
        body, name=name, grid=(t // tm,), in_specs=[row, row] + _gate_specs(tm) + [row],
        out_specs=[row, row, wide],
        out_shape=[jax.ShapeDtypeStruct((t, d), BF16), jax.ShapeDtypeStruct((t, d), BF16),
                   jax.ShapeDtypeStruct((t, 2 * d), BF16)],
        compiler_params=_cp("parallel"))(ya, yb, proj, proj, proj, proj, dmix)


def _swiglu(g, u):
    return g * _sigmoid(g) * u


def _ffn_up_fwd(h2, wgu_t, name, tm=512):
    t, d = h2.shape
    tm = min(tm, t)
    fh = FFN_HIDDEN // 2

    def body(a_ref, w_ref, gu_ref, act_ref):
        r = _dot(a_ref[...], w_ref[...], NT)
        gu_ref[...] = r.astype(BF16)
        act_ref[...] = _swiglu(r[:, :fh], r[:, fh:]).astype(BF16)

    return pl.pallas_call(
        body, name=name, grid=(2, t // tm),
        in_specs=[pl.BlockSpec((tm, d), lambda j, i: (i, 0)), pl.BlockSpec((2 * fh, d), lambda j, i: (j, 0))],
        out_specs=[pl.BlockSpec((tm, 2 * fh), lambda j, i: (i, j)), pl.BlockSpec((tm, fh), lambda j, i: (i, j))],
        out_shape=[jax.ShapeDtypeStruct((t, 4 * fh), BF16), jax.ShapeDtypeStruct((t, 2 * fh), BF16)],
        compiler_params=_cp("parallel", "parallel"))(h2, wgu_t)


def _ffn_down_bwd(dx, w_d, gu, name, tm=512):
    t, d = dx.shape
    tm = min(tm, t)
    fh = FFN_HIDDEN // 2

    def body(a_ref, w_ref, gu_ref, o_ref):
        dact = _dot(a_ref[...].astype(BF16), w_ref[...], NT)
        _, vjp = jax.vjp(_swiglu, gu_ref[:, :fh].astype(F32), gu_ref[:, fh:].astype(F32))
        dg, du = vjp(dact)
        o_ref[:, :fh] = dg.astype(BF16)
        o_ref[:, fh:] = du.astype(BF16)

    wide = pl.BlockSpec((tm, 2 * fh), lambda j, i: (i, j))
    return pl.pallas_call(
        body, name=name, grid=(2, t // tm),
        in_specs=[pl.BlockSpec((tm, d), lambda j, i: (i, 0)), pl.BlockSpec((fh, d), lambda j, i: (j, 0)), wide],
        out_specs=wide,
        out_shape=jax.ShapeDtypeStruct((t, 4 * fh), BF16),
        compiler_params=_cp("parallel", "parallel"))(dx, w_d, gu)


def _rows_bwd(pieces, w, x, g, dres, name, tm=256):
    t, d = x.shape
    tm = min(tm, t)
    widths = [p.shape[1] for p in pieces]
    starts = [sum(widths[:i]) for i in range(len(widths))]
    assert sum(widths) == w.shape[0]
    n_p = len(pieces)

    def body(*refs):
        p_refs, (w_ref, x_ref, g_ref, dres_ref, dx_ref, dg_ref) = refs[:n_p], refs[n_p:]
        dh = _dot(p_refs[0][...], w_ref[pl.ds(starts[0], widths[0]), :], NN)
        for i in range(1, n_p):
            dh = dh + _dot(p_refs[i][...], w_ref[pl.ds(starts[i], widths[i]), :], NN)
        _, vjp = jax.vjp(_rms, x_ref[...], g_ref[...])
        dx, dg = vjp(dh)
        dx_ref[...] = dres_ref[...] + dx

        @pl.when(pl.program_id(0) == 0)
        def _():
            dg_ref[...] = jnp.zeros_like(dg_ref)

        dg_ref[...] += dg

    row = pl.BlockSpec((tm, d), lambda i: (i, 0))
    vec = pl.BlockSpec((1, d), lambda i: (0, 0))
    return pl.pallas_call(
        body, name=name, grid=(t // tm,),
        in_specs=[pl.BlockSpec((tm, k), lambda i: (i, 0)) for k in widths]
        + [pl.BlockSpec(w.shape, lambda i: (0, 0)), row, vec, row],
        out_specs=[row, vec],
        out_shape=[jax.ShapeDtypeStruct((t, d), F32), jax.ShapeDtypeStruct((1, d), F32)],
        compiler_params=_cp("arbitrary"))(*pieces, w, x, g, dres)


def _merge_fwd(o_g, o_att, proj, x, w_pa, w_pb, w_o, name, tm=256):
    t, d = x.shape
    tm = min(tm, t)

    def body(og_ref, oa_ref, ga0, ga1, gb0, gb1, x_ref, wpa_ref, wpb_ref, wo_ref, ya_ref, yb_ref, mix_ref, x1_ref):
        ya = _dot(og_ref[...], wpa_ref[...], NN)
        yb = _dot(oa_ref[...], wpb_ref[...], NN)
        ga = jnp.concatenate([ga0[...], ga1[...]], axis=1)
        gb = jnp.concatenate([gb0[...], gb1[...]], axis=1)
        mix = _mix(ya, yb, ga, gb).astype(BF16)
        ya_ref[...] = ya.astype(BF16)
        yb_ref[...] = yb.astype(BF16)
        mix_ref[...] = mix
        x1_ref[...] = x_ref[...] + _dot(mix, wo_ref[...], NN)

    row = pl.BlockSpec((tm, d), lambda i: (i, 0))
    mat = pl.BlockSpec((d, d), lambda i: (0, 0))
    return pl.pallas_call(
        body, name=name, grid=(t // tm,),
        in_specs=[row, row] + _gate_specs(tm) + [row, mat, mat, mat], out_specs=[row, row, row, row],
        out_shape=[jax.ShapeDtypeStruct((t, d), BF16)] * 3 + [jax.ShapeDtypeStruct((t, d), F32)],
        compiler_params=_cp("parallel"))(o_g, o_att, proj, proj, proj, proj, x, w_pa, w_pb, w_o)


def _loss_head(x, g, target, name, tm=512):
    t, d = x.shape
    tm = min(tm, t)

    def body(x_ref, g_ref, t_ref, dx_ref, dg_ref, loss_ref):
        tgt = t_ref[...]

        def f(xv, gv):
            err = _rms(xv, gv) - tgt
            return 0.5 * jnp.sum(jnp.mean(err * err, axis=-1, keepdims=True))

        loss, vjp = jax.vjp(f, x_ref[...], g_ref[...])
        dx, dg = vjp(jnp.ones((), F32))
        dx_ref[...] = dx

        @pl.when(pl.program_id(0) == 0)
        def _():
            dg_ref[...] = jnp.zeros_like(dg_ref)
            loss_ref[...] = jnp.zeros_like(loss_ref)

        dg_ref[...] += dg
        loss_ref[...] += jnp.full(loss_ref.shape, loss, F32)

    row = pl.BlockSpec((tm, d), lambda i: (i, 0))
    vec = pl.BlockSpec((1, d), lambda i: (0, 0))
    lane = pl.BlockSpec((1, 128), lambda i: (0, 0))
    return pl.pallas_call(
        body, name=name, grid=(t // tm,), in_specs=[row, vec, row], out_specs=[row, vec, lane],
        out_shape=[jax.ShapeDtypeStruct((t, d), F32), jax.ShapeDtypeStruct((1, d), F32),
                   jax.ShapeDtypeStruct((1, 128), F32)],
        compiler_params=_cp("arbitrary"))(x, g, target)


def _lb_rows(l0, l1, l2, l3):
    mx = jnp.maximum(jnp.maximum(l0, l1), jnp.maximum(l2, l3))
    e0, e1, e2, e3 = jnp.exp(l0 - mx), jnp.exp(l1 - mx), jnp.exp(l2 - mx), jnp.exp(l3 - mx)
    s = e0 + e1 + e2 + e3
    p0, p1, p2, p3 = e0 / s, e1 / s, e2 / s, e3 / s
    c1 = p0 + p1
    c2 = c1 + p2
    c3 = c2 + p3
    return p0 - p0, c1 - p0, c2 - p0, c3 - p0


def _lb_fwd(lb_logits):
    def body(l_ref, o_ref):
        rows = _lb_rows(*[l_ref[pl.ds(i, 1), :] for i in range(DEPTH)])
        for i in range(DEPTH):
            o_ref[pl.ds(i, 1), :] = rows[i]

    return pl.pallas_call(body, name="lb_fwd", out_shape=jax.ShapeDtypeStruct(lb_logits.shape, F32))(lb_logits)


def _lb_bwd(lb_logits, dlb):
    def body(l_ref, d_ref, o_ref):
        _, vjp = jax.vjp(_lb_rows, *[l_ref[pl.ds(i, 1), :] for i in range(DEPTH)])
        grads = vjp(tuple(d_ref[pl.ds(i, 1), :] for i in range(DEPTH)))
        for i in range(DEPTH):
            o_ref[pl.ds(i, 1), :] = grads[i]

    return pl.pallas_call(body, name="lb_bwd", out_shape=jax.ShapeDtypeStruct(lb_logits.shape, F32))(lb_logits, dlb)


MESH = pl.DeviceIdType.MESH
ANY = pl.BlockSpec(memory_space=pl.ANY)
N_KINDS = len(SHARD_ROWS)
FFN_HALF = FFN_HIDDEN // 2
KIND_PLACE = ((0, 0), (1, 0), (2, 0), (3, 0), (4, 0), (4, FFN_HALF), (5, 0))
KIND_HALF_SKIP = (0, 0, 0, 0, FFN_HALF, FFN_HALF, 0)
FULL_ROWS = (N_DEV * SHARD_ROWS[0], D_MODEL, D_MODEL, D_MODEL, 2 * N_DEV * SHARD_ROWS[4], N_DEV * SHARD_ROWS[6])


def _kind_rows(ti, dev):
    oi, base = KIND_PLACE[ti]
    start = base + dev * SHARD_ROWS[ti]
    if KIND_HALF_SKIP[ti]:
        start = start + (dev // (N_DEV // 2)) * KIND_HALF_SKIP[ti]
    return oi, pl.ds(start, SHARD_ROWS[ti])


def _position():
    x, y, c = lax.axis_index("x"), lax.axis_index("y"), lax.axis_index("c")
    return x, y, c, 4 * x + 2 * y + c


def _peer(x, y, c, r):
    px = 1 - x if r & 4 else x
    py = 1 - y if r & 2 else y
    pc = 1 - c if r & 1 else c
    return (px, py, pc), 4 * px + 2 * py + pc


class _Exchange(NamedTuple):
    operands: tuple
    out_shape: tuple
    copies: Callable


EXCHANGE_SCRATCH = (pltpu.SemaphoreType.DMA((N_DEV, N_KINDS)), pltpu.SemaphoreType.DMA((N_DEV, N_KINDS)),
                    pltpu.SemaphoreType.DMA((N_KINDS,)))


def _all_pairs(ends, send_sems, recv_sems):
    x, y, c, me = _position()
    out = []
    for r in range(1, N_DEV):
        peer, pid = _peer(x, y, c, r)
        for ti in range(N_KINDS):
            src, dst = ends(ti, me, pid)
            out.append(pltpu.make_async_remote_copy(
                src_ref=src, dst_ref=dst, send_sem=send_sems.at[r, ti], recv_sem=recv_sems.at[r, ti],
                device_id=peer, device_id_type=MESH))
    return out


def _gather_exchange(shards):
    def copies(ins, outs, send_sems, recv_sems, local_sems, arrivals):
        def window(ti, dev):
            oi, rows = _kind_rows(ti, dev)
            return outs[oi].at[rows, :]

        if arrivals:
            return _all_pairs(lambda ti, me, pid: (ins[ti], window(ti, pid)), send_sems, recv_sems)
        _, _, _, me = _position()
        local = [pltpu.make_async_copy(ins[ti], window(ti, me), local_sems.at[ti]) for ti in range(N_KINDS)]
        return local + _all_pairs(lambda ti, me, pid: (ins[ti], window(ti, me)), send_sems, recv_sems)

    return _Exchange(tuple(shards), tuple(jax.ShapeDtypeStruct((rows, D_MODEL), BF16) for rows in FULL_ROWS), copies)


def _scatter_exchange(grads):
    def copies(ins, outs, send_sems, recv_sems, local_sems, arrivals):
        land = outs[0]

        def piece(ti, dev):
            ii, rows = _kind_rows(ti, dev)
            return ins[ii].at[rows, :]

        def slot(ti, dev):
            return land.at[dev, pl.ds(SLOT_OFF[ti], SHARD_ROWS[ti]), :]

        if arrivals:
            return _all_pairs(lambda ti, me, pid: (piece(ti, me), slot(ti, pid)), send_sems, recv_sems)
        _, _, _, me = _position()
        local = [pltpu.make_async_copy(piece(ti, me), slot(ti, me), local_sems.at[ti]) for ti in range(N_KINDS)]
        return local + _all_pairs(lambda ti, me, pid: (piece(ti, pid), slot(ti, me)), send_sems, recv_sems)

    return _Exchange(tuple(grads), (jax.ShapeDtypeStruct((N_DEV, SLOT_ROWS, D_MODEL), BF16),), copies)


def _exchange_start(ex, ins, outs, sems):
    for cp in ex.copies(ins, outs, *sems, False):
        cp.start()


def _exchange_finish(ex, ins, outs, sems):
    for cp in ex.copies(ins, outs, *sems, True):
        cp.wait_recv()
    mine = ex.copies(ins, outs, *sems, False)
    for cp in mine[:N_KINDS]:
        cp.wait()
    for cp in mine[N_KINDS:]:
        cp.wait_send()


def _run_exchange(ex, name):
    n_in, n_out = len(ex.operands), len(ex.out_shape)

    def body(*refs):
        ins, outs, sems = refs[:n_in], refs[n_in:n_in + n_out], refs[n_in + n_out:]
        _exchange_start(ex, ins, outs, sems)
        _exchange_finish(ex, ins, outs, sems)

    return pl.pallas_call(body, name=name, in_specs=[ANY] * n_in, out_specs=[ANY] * n_out,
                          out_shape=list(ex.out_shape), scratch_shapes=list(EXCHANGE_SCRATCH))(*ex.operands)


def _carry(body, ex, n_in, n_out, n_scratch, n_steps):
    if ex is None:
        return body
    e_in, e_out = len(ex.operands), len(ex.out_shape)

    def carrying(*refs):
        own_in, ex_in = refs[:n_in], refs[n_in:n_in + e_in]
        rest = refs[n_in + e_in:]
        own_out, ex_out = rest[:n_out], rest[n_out:n_out + e_out]
        own_scratch, sems = rest[n_out + e_out:n_out + e_out + n_scratch], rest[n_out + e_out + n_scratch:]

        @pl.when(pl.program_id(0) == 0)
        def _():
            _exchange_start(ex, ex_in, ex_out, sems)

        body(*own_in, *own_out, *own_scratch)

        @pl.when(pl.program_id(0) == n_steps - 1)
        def _():
            _exchange_finish(ex, ex_in, ex_out, sems)

    return carrying


def _small_sum_body(p_ref, o_ref, buf, send_sems, recv_sems):
    x, y, c, me = _position()
    buf[me] = p_ref[...]
    sends = []
    for r in range(1, N_DEV):
        peer, _ = _peer(x, y, c, r)
        sends.append(pltpu.make_async_remote_copy(
            src_ref=p_ref, dst_ref=buf.at[me], send_sem=send_sems.at[r], recv_sem=recv_sems.at[r],
            device_id=peer, device_id_type=MESH))
    for cp in sends:
        cp.start()
    for r in range(1, N_DEV):
        peer, pid = _peer(x, y, c, r)
        pltpu.make_async_remote_copy(
            src_ref=p_ref, dst_ref=buf.at[pid], send_sem=send_sems.at[r], recv_sem=recv_sems.at[r],
            device_id=peer, device_id_type=MESH).wait_recv()
    for cp in sends:
        cp.wait_send()
    acc = buf[0]
    for k in range(1, N_DEV):
        acc = acc + buf[k]
    o_ref[...] = acc


def _all_reduce_small(part):
    rows, d = part.shape
    vmem = pl.BlockSpec(memory_space=pltpu.VMEM)
    return pl.pallas_call(
        functools.partial(_small_sum_body), name="all_reduce_small", in_specs=[vmem], out_specs=vmem,
        out_shape=jax.ShapeDtypeStruct((rows, d), F32),
        scratch_shapes=[pltpu.VMEM((N_DEV, rows, d), F32), pltpu.SemaphoreType.DMA((N_DEV,)),
                        pltpu.SemaphoreType.DMA((N_DEV,))],
    )(part)


def _hg_consts():
    c = HG_CHUNK
    r = lax.broadcasted_iota(jnp.int32, (c, c), 0)
    s = lax.broadcasted_iota(jnp.int32, (c, c), 1)
    return r, s


def _split3(x):
    hi = x.astype(BF16)
    r1 = x - hi.astype(F32)
    mid = r1.astype(BF16)
    lo = (r1 - mid.astype(F32)).astype(BF16)
    return jnp.concatenate([hi, mid, lo], axis=1)


def _cumsum_rows(tri, x):
    w = x.shape[1]
    y = _dot(tri, _split3(x), NN)
    return y[:, :w] + y[:, w:2 * w] + y[:, 2 * w:]


def _hg_chunk(zq, zf, lb, tril, b_ref):
    c = HG_CHUNK
    sq = _sigmoid(zq)
    q = zq * sq
    sg = _sigmoid(zf)
    f = lb + (1.0 - lb) * sg
    logf = jnp.log(jnp.maximum(f, MIN_F))
    k = 1.0 - f
    b = _cumsum_rows(tril, logf)
    b_ref[...] = b
    mid = b_ref[pl.ds(c // 2 - 1, 1), :]
    bc = b_ref[pl.ds(c - 1, 1), :]
    em = jnp.exp(jnp.minimum(b - mid, HG_EXP_CLAMP))
    en = jnp.exp(jnp.minimum(mid - b, HG_EXP_CLAMP))
    return sq, q, sg, f, k, b, em, en, bc


def _hg_gate(o, zg, gn):
    return o * lax.rsqrt(jnp.mean(o * o, axis=-1, keepdims=True) + EPS) * gn * (zg * _sigmoid(zg))


def _hgrn2_fwd(proj, lb, gn, name, ex=None):
    t = proj.shape[0]
    bs_tok = min(HG_BLOCK, t)
    n_chunks = bs_tok // HG_CHUNK
    w = HG_HEADS * HG_DK

    def body(hq_ref, hf_ref, hi_ref, hg_ref, lb_ref, gn_ref, o_ref, og_ref, sall_ref, st_ref, b_ref):
        @pl.when(pl.program_id(0) == 0)
        def _():
            st_ref[...] = jnp.zeros_like(st_ref)

        r, s = _hg_consts()
        causal = s <= r
        tril = causal.astype(BF16)

        def chunk(ci, carry):
            rows = pl.ds(pl.multiple_of(ci * HG_CHUNK, HG_CHUNK), HG_CHUNK)
            for h in range(HG_HEADS):
                cols = slice(h * HG_DK, (h + 1) * HG_DK)
                v = hi_ref[rows, cols]
                zg = hg_ref[rows, cols]
                _, q, _, _, k, b, em, en, bc = _hg_chunk(hq_ref[rows, cols], hf_ref[rows, cols],
                                                         lb_ref[:, cols], tril, b_ref.at[h])
                st0 = st_ref[h]
                sall_ref[ci, h] = st0
                vb = v.astype(BF16)
                o = _dot((q * jnp.exp(b)).astype(BF16), st0.astype(BF16), NT)
                a = jnp.where(causal, _dot((q * em).astype(BF16), (k * en).astype(BF16), NT), 0.0)
                o = o + _dot(a.astype(BF16), vb, NN)
                kdec = (k * jnp.exp(bc - b)).astype(BF16)
                st_ref[h] = st0 * jnp.exp(bc) + _dot(vb, kdec, TN)
                o_ref[rows, cols] = o
                og_ref[rows, cols] = _hg_gate(o, zg, gn_ref[:, cols]).astype(BF16)
            return carry

        lax.fori_loop(0, n_chunks, chunk, 0)

    def col(j):
        return pl.BlockSpec((bs_tok, w), lambda n, j=j: (n, j))

    vec = pl.BlockSpec((1, w), lambda n: (0, 0))
    ex_in, ex_out = (ex.operands, ex.out_shape) if ex else ((), ())
    outs = pl.pallas_call(
        _carry(body, ex, 6, 3, 2, t // bs_tok), name=name, grid=(t // bs_tok,),
        in_specs=[col(COL_HQ // w), col(COL_HF // w), col(COL_HI // w), col(COL_HG // w), vec, vec] + [ANY] * len(ex_in),
        out_specs=[col(0), col(0),
                   pl.BlockSpec((n_chunks, HG_HEADS, HG_DK, HG_DK), lambda n: (n, 0, 0, 0))] + [ANY] * len(ex_out),
        out_shape=[jax.ShapeDtypeStruct((t, w), F32), jax.ShapeDtypeStruct((t, w), BF16),
                   jax.ShapeDtypeStruct((t // HG_CHUNK, HG_HEADS, HG_DK, HG_DK), F32)] + list(ex_out),
        scratch_shapes=[pltpu.VMEM((HG_HEADS, HG_DK, HG_DK), F32), pltpu.VMEM((HG_HEADS, HG_CHUNK, HG_DK), F32)]
        + (list(EXCHANGE_SCRATCH) if ex else []),
        compiler_params=_cp("arbitrary"))(proj, proj, proj, proj, lb, gn, *ex_in)
    return outs[:3], outs[3:]


def _hgrn2_bwd(proj, lb, gn, o_hg, sall, dog, name, ex=None):
    t = proj.shape[0]
    bs_tok = min(HG_BLOCK, t)
    n_chunks = bs_tok // HG_CHUNK
    n_blocks = t // bs_tok
    w = HG_HEADS * HG_DK

    def body(hq_ref, hf_ref, hi_ref, hg_ref, lb_ref, gn_ref, o_ref, sall_ref, dog_ref,
             da_ref, dlb_ref, dgn_ref, dst_ref, b_ref):
        @pl.when(pl.program_id(0) == 0)
        def _():
            dst_ref[...] = jnp.zeros_like(dst_ref)
            dlb_ref[...] = jnp.zeros_like(dlb_ref)
            dgn_ref[...] = jnp.zeros_like(dgn_ref)

        r, s = _hg_consts()
        causal = s <= r
        tril = causal.astype(BF16)
        rev_tril = (s >= r).astype(BF16)

        def chunk(cj, carry):
            ci = n_chunks - 1 - cj
            rows = pl.ds(pl.multiple_of(ci * HG_CHUNK, HG_CHUNK), HG_CHUNK)
            for h in range(HG_HEADS):
                cols = slice(h * HG_DK, (h + 1) * HG_DK)
                zq = hq_ref[rows, cols]
                v = hi_ref[rows, cols]
                zg = hg_ref[rows, cols]
                lbv = lb_ref[:, cols]
                sq, q, sg, f, k, b, em, en, bc = _hg_chunk(zq, hf_ref[rows, cols], lbv, tril, b_ref.at[h])
                st0 = sall_ref[ci, h]
                dst1 = dst_ref[h]
                vb = v.astype(BF16)
                eb = jnp.exp(b)
                qg = (q * eb).astype(BF16)
                qt = (q * em).astype(BF16)
                kref = (k * en).astype(BF16)
                ebcb = jnp.exp(bc - b)
                kdec = (k * ebcb).astype(BF16)
                ebc = jnp.exp(bc)
                st1 = st0 * ebc + _dot(vb, kdec, TN)

                _, gate_vjp = jax.vjp(_hg_gate, o_ref[rows, cols], zg, gn_ref[:, cols])
                do, dzg, dgn = gate_vjp(dog_ref[rows, cols])
                dob = do.astype(BF16)
                dam = jnp.where(causal, _dot(dob, vb, NT), 0.0).astype(BF16)
                a = jnp.where(causal, _dot(qt, kref, NT), 0.0)
                dk = ebcb * _dot(vb, dst1.astype(BF16), NN) + en * _dot(dam, qt, TN)
                dq = eb * _dot(dob, st0.astype(BF16), NN) + em * _dot(dam, kref, NN)
                dv = _dot(a.astype(BF16), dob, TN) + _dot(kdec, dst1.astype(BF16), NT)
                dst_ref[h] = dst1 * ebc + _dot(dob, qg, TN)

                dbx = jnp.sum(dst1 * st1, axis=0, keepdims=True)
                dlogf = _cumsum_rows(rev_tril, q * dq - k * dk) + dbx
                df = jnp.where(f > MIN_F, dlogf / f, 0.0) - dk
                dzf = df * (1.0 - lbv) * sg * (1.0 - sg)
                dzq = dq * (sq * (1.0 + zq * (1.0 - sq)))
                da_ref[rows, pl.ds(COL_HQ + h * HG_DK, HG_DK)] = dzq.astype(BF16)
                da_ref[rows, pl.ds(COL_HF + h * HG_DK, HG_DK)] = dzf.astype(BF16)
                da_ref[rows, pl.ds(COL_HI + h * HG_DK, HG_DK)] = dv.astype(BF16)
                da_ref[rows, pl.ds(COL_HG + h * HG_DK, HG_DK)] = dzg.astype(BF16)
                dlb_ref[:, cols] += jnp.sum(df * (1.0 - sg), axis=0, keepdims=True)
                dgn_ref[:, cols] += dgn
            return carry

        lax.fori_loop(0, n_chunks, chunk, 0)

    def col(j):
        return pl.BlockSpec((bs_tok, w), lambda n, j=j: (n_blocks - 1 - n, j))

    vec = pl.BlockSpec((1, w), lambda n: (0, 0))
    ex_in, ex_out = (ex.operands, ex.out_shape) if ex else ((), ())
    outs = pl.pallas_call(
        _carry(body, ex, 9, 3, 2, n_blocks), name=name, grid=(n_blocks,),
        in_specs=[col(COL_HQ // w), col(COL_HF // w), col(COL_HI // w), col(COL_HG // w), vec, vec, col(0),
                  pl.BlockSpec((n_chunks, HG_HEADS, HG_DK, HG_DK), lambda n: (n_blocks - 1 - n, 0, 0, 0)),
                  col(0)] + [ANY] * len(ex_in),
        out_specs=[pl.BlockSpec((bs_tok, 4 * w), lambda n: (n_blocks - 1 - n, 0)), vec, vec] + [ANY] * len(ex_out),
        out_shape=[jax.ShapeDtypeStruct((t, 4 * w), BF16), jax.ShapeDtypeStruct((1, w), F32),
                   jax.ShapeDtypeStruct((1, w), F32)] + list(ex_out),
        scratch_shapes=[pltpu.VMEM((HG_HEADS, HG_DK, HG_DK), F32), pltpu.VMEM((HG_HEADS, HG_CHUNK, HG_DK), F32)]
        + (list(EXCHANGE_SCRATCH) if ex else []),
        compiler_params=_cp("arbitrary"))(proj, proj, proj, proj, lb, gn, o_hg, sall, dog, *ex_in)
    return outs[:3], outs[3:]


def _rope_tables(t):
    half = ROPE_DIM // 2
    inv = ROPE_THETA ** (-jnp.arange(half, dtype=F32) * 2.0 / ROPE_DIM)
    ang = jnp.arange(t).astype(F32)[:, None] * inv[None, :]
    cos, sin = jnp.cos(ang), jnp.sin(ang)
    pad = ATT_HEAD_DIM - ROPE_DIM
    c = jnp.concatenate([cos, cos, jnp.ones((t, pad), F32)], axis=1)
    su = jnp.concatenate([-sin, jnp.zeros((t, half + pad), F32)], axis=1)
    sd = jnp.concatenate([jnp.zeros((t, half), F32), sin, jnp.zeros((t, pad), F32)], axis=1)
    return tuple(jnp.concatenate([m, m], axis=1) for m in (c, su, sd))


def _rope(x, tabs):
    c, su, sd = tabs
    n = x.shape[1]
    half = ROPE_DIM // 2
    return x * c + pltpu.roll(x, n - half, 1) * su + pltpu.roll(x, half, 1) * sd


def _rope_t(dy, tabs):
    c, su, sd = tabs
    n = dy.shape[1]
    half = ROPE_DIM // 2
    return dy * c + pltpu.roll(dy * su, half, 1) + pltpu.roll(dy * sd, n - half, 1)


def _swa_specs(n_blocks, clamp):
    blk = ATT_BLOCK

    def cur(n):
        return jnp.minimum(n, n_blocks - 1) if clamp else n

    def prev(n):
        return jnp.maximum(cur(n) - 1, 0)

    q_spec = pl.BlockSpec((blk, 512), lambda m, n: (cur(n), COL_AQ // 512 + m))
    kv = [pl.BlockSpec((blk, 128), lambda m, n, c=c, f=f: (f(n), c + m))
          for c in (COL_AK // 128, COL_AV // 128) for f in (cur, prev)]
    tabs = [pl.BlockSpec((blk, 128), lambda m, n, f=f: (f(n), 0)) for f in (cur, prev) for _ in range(3)]
    return q_spec, kv, tabs, cur, prev


ATT_SCALE = ATT_HEAD_DIM ** -0.5


def _swa_scores(qm, kd, sink, mask, ones):
    s = jnp.where(mask, _dot(qm, kd, NT), -jnp.inf)
    mx = jnp.maximum(jnp.max(s, axis=-1, keepdims=True), sink)
    pb = jnp.exp(s - mx).astype(BF16)
    es = jnp.exp(sink - mx)
    return pb, 1.0 / (_dot(pb, ones, NN) + es), es


def _swa_window(kc_ref, kp_ref, vc_ref, vp_ref, tabs_c, tabs_p, n):
    k2 = jnp.concatenate([_rope(kp_ref[...], tabs_p), _rope(kc_ref[...], tabs_c)], axis=0)
    v2 = jnp.concatenate([vp_ref[...], vc_ref[...]], axis=0)
    blk = ATT_BLOCK
    qi = lax.broadcasted_iota(jnp.int32, (blk, 2 * blk), 0)
    kj = lax.broadcasted_iota(jnp.int32, (blk, 2 * blk), 1)
    delta = qi + blk - kj
    mask = (delta >= 0) & (delta < blk) & ((kj >= blk) | (n > 0))
    return k2, v2, mask


def _swa_fwd(proj, sinks, tabs, name):
    t = proj.shape[0]
    n_blocks = t // ATT_BLOCK
    q_spec, kv_specs, tab_specs, _, _ = _swa_specs(n_blocks, clamp=False)

    def body(q_ref, kc_ref, kp_ref, vc_ref, vp_ref, c0, c1, c2, p0, p1, p2, sink_ref, o_ref):
        m, n = pl.program_id(0), pl.program_id(1)
        tabs_c = (c0[...], c1[...], c2[...])
        tabs_p = (p0[...], p1[...], p2[...])
        k2, v2, mask = _swa_window(kc_ref, kp_ref, vc_ref, vp_ref, tabs_c, tabs_p, n)
        k2r, v2r = pltpu.roll(k2, 64, 1), pltpu.roll(v2, 64, 1)
        upper_k = lax.broadcasted_iota(jnp.int32, k2.shape, 1) >= 64
        upper_q = lax.broadcasted_iota(jnp.int32, (ATT_BLOCK, 128), 1) >= 64
        ones = jnp.ones((2 * ATT_BLOCK, 128), BF16)
        for jj in range(2):
            own = upper_k if jj else ~upper_k
            kd = jnp.where(own, k2, k2r).astype(BF16)
            vd = jnp.where(own, v2, v2r).astype(BF16)
            for pi in range(2):
                cols = slice(256 * jj + 128 * pi, 256 * jj + 128 * pi + 128)
                qp = _rope(q_ref[:, cols], tabs_c) * ATT_SCALE
                outs = []
                for e in range(2):
                    sink = sink_ref[0, 8 * m + 4 * jj + 2 * pi + e]
                    qm = jnp.where(upper_q if e else ~upper_q, qp, 0.0).astype(BF16)
                    pb, rinv, _ = _swa_scores(qm, kd, sink, mask, ones)
                    outs.append(_dot(pb, vd, NN) * rinv)
                o_ref[:, cols] = jnp.where(upper_q, outs[1], outs[0]).astype(BF16)

    return pl.pallas_call(
        body, name=name, grid=(2, n_blocks),
        in_specs=[q_spec] + kv_specs + tab_specs + [pl.BlockSpec(memory_space=pltpu.SMEM)],
        out_specs=pl.BlockSpec((ATT_BLOCK, 512), lambda m, n: (n, m)),
        out_shape=jax.ShapeDtypeStruct((t, ATT_Q_HEADS * ATT_HEAD_DIM), BF16),
        compiler_params=_cp("parallel", "arbitrary"))(proj, proj, proj, proj, proj, *tabs, *tabs, sinks)


def _swa_bwd(proj, sinks, tabs, o_att, do_att, name):
    t = proj.shape[0]
    n_blocks = t // ATT_BLOCK
    blk = ATT_BLOCK
    q_spec, kv_specs, tab_specs, cur, prev = _swa_specs(n_blocks, clamp=True)

    def body(q_ref, kc_ref, kp_ref, vc_ref, vp_ref, c0, c1, c2, p0, p1, p2, sink_ref, o_ref, do_ref,
             dq_ref, dk_ref, dv_ref, ds_ref, ck_ref, cv_ref):
        m, n = pl.program_id(0), pl.program_id(1)

        @pl.when(n == 0)
        def _():
            ds_ref[...] = jnp.zeros_like(ds_ref)
            ck_ref[...] = jnp.zeros_like(ck_ref)
            cv_ref[...] = jnp.zeros_like(cv_ref)

        @pl.when(n < n_blocks)
        def _():
            tabs_c = (c0[...], c1[...], c2[...])
            tabs_p = (p0[...], p1[...], p2[...])
            k2, v2, mask = _swa_window(kc_ref, kp_ref, vc_ref, vp_ref, tabs_c, tabs_p, n)
            k2r, v2r = pltpu.roll(k2, 64, 1), pltpu.roll(v2, 64, 1)
            upper_k = lax.broadcasted_iota(jnp.int32, k2.shape, 1) >= 64
            upper_q = lax.broadcasted_iota(jnp.int32, (blk, 128), 1) >= 64
            lane = lax.broadcasted_iota(jnp.int32, (8, 128), 1)
            ones = jnp.ones((2 * blk, 128), BF16)
            dk2 = jnp.zeros(k2.shape, F32)
            dv2 = jnp.zeros(k2.shape, F32)
            dsv = jnp.zeros((8, 128), F32)
            for jj in range(2):
                own = upper_k if jj else ~upper_k
                kd = jnp.where(own, k2, k2r).astype(BF16)
                vd = jnp.where(own, v2, v2r).astype(BF16)
                dkd = jnp.zeros(k2.shape, F32)
                dvd = jnp.zeros(k2.shape, F32)
                for pi in range(2):
                    cols = slice(256 * jj + 128 * pi, 256 * jj + 128 * pi + 128)
                    qp = _rope(q_ref[:, cols], tabs_c) * ATT_SCALE
                    do_pair = do_ref[:, cols]
                    o_pair = o_ref[:, cols].astype(F32)
                    dqs = []
                    for e in range(2):
                        hl = 4 * jj + 2 * pi + e
                        sink = sink_ref[0, 8 * m + hl]
                        half = upper_q if e else ~upper_q
                        qm = jnp.where(half, qp, 0.0).astype(BF16)
                        pb, rinv, es = _swa_scores(qm, kd, sink, mask, ones)
                        pn = pb.astype(F32) * jnp.concatenate([rinv, rinv], axis=1)
                        dom = jnp.where(half, do_pair, 0.0)
                        delta = jnp.sum(dom * o_pair, axis=-1, keepdims=True)
                        domb = dom.astype(BF16)
                        dp = _dot(domb, vd, NT)
                        dsb = (pn * (dp - delta)).astype(BF16)
                        dqs.append(_dot(dsb, kd, NN) * ATT_SCALE)
                        dkd = dkd + _dot(dsb, qm, TN)
                        dvd = dvd + _dot(pn.astype(BF16), domb, TN)
                        dsv = dsv + jnp.where(lane == hl, -jnp.sum(es * rinv * delta) * (1.0 / 128), 0.0)
                    dq_ref[:, cols] = _rope_t(jnp.where(upper_q, dqs[1], dqs[0]), tabs_c).astype(BF16)
                dk2 = dk2 + jnp.where(own, dkd + pltpu.roll(dkd, 64, 1), 0.0)
                dv2 = dv2 + jnp.where(own, dvd + pltpu.roll(dvd, 64, 1), 0.0)
            dk_ref[...] = (ck_ref[...] + _rope_t(dk2[:blk], tabs_p)).astype(BF16)
            dv_ref[...] = (cv_ref[...] + dv2[:blk]).astype(BF16)
            ck_ref[...] = _rope_t(dk2[blk:], tabs_c)
            cv_ref[...] = dv2[blk:]
            ds_ref[...] += dsv

        @pl.when(n == n_blocks)
        def _():
            dk_ref[...] = ck_ref[...].astype(BF16)
            dv_ref[...] = cv_ref[...].astype(BF16)

    wide = pl.BlockSpec((blk, 512), lambda m, n: (cur(n), m))
    lagged = pl.BlockSpec((blk, 128), lambda m, n: (jnp.maximum(n - 1, 0), m))
    return pl.pallas_call(
        body, name=name, grid=(2, n_blocks + 1),
        in_specs=[q_spec] + kv_specs + tab_specs + [pl.BlockSpec(memory_space=pltpu.SMEM), wide, wide],
        out_specs=[wide, lagged, lagged, pl.BlockSpec((None, 8, 128), lambda m, n: (m, 0, 0))],
        out_shape=[jax.ShapeDtypeStruct((t, 1024), BF16), jax.ShapeDtypeStruct((t, 256), BF16),
                   jax.ShapeDtypeStruct((t, 256), BF16), jax.ShapeDtypeStruct((2, 8, 128), F32)],
        scratch_shapes=[pltpu.VMEM((blk, 128), F32), pltpu.VMEM((blk, 128), F32)],
        compiler_params=_cp("arbitrary", "arbitrary"))(proj, proj, proj, proj, proj, *tabs, *tabs, sinks, o_att, do_att)


def _local_step(x, target, shards, norm1, lb_logits, hg_norm, attn_sinks, norm2, final_norm):
    t = x.shape[0]
    tabs = _rope_tables(t)
    lb_all = _lb_fwd(lb_logits)
    saved = []
    weights = _run_exchange(_gather_exchange([s[0] for s in shards]), "gather_weights")
    for l in range(DEPTH):
        win_t, w_pa, w_pb, w_o, wgu_t, w_d = weights
        n1, n2 = norm1[l][None, :], norm2[l][None, :]
        lb, gn, sinks = lb_all[l][None, :], hg_norm[l][None, :], attn_sinks[l][None, :]
        h = _rms_fwd(x, n1, "rms1_fwd")
        proj = _matmul_nt(h, win_t, 0, IN_COLS, F32, "proj_fwd", tn=1280)
        ex = _gather_exchange([s[l + 1] for s in shards]) if l + 1 < DEPTH else None
        (o_hg, o_g, sall), next_weights = _hgrn2_fwd(proj, lb, gn, "hgrn2_fwd", ex)
        o_att = _swa_fwd(proj, sinks, tabs, "swa_fwd")
        ya, yb, mix, x1 = _merge_fwd(o_g, o_att, proj, x, w_pa, w_pb, w_o, "merge_fwd")
        h2 = _rms_fwd(x1, n2, "rms2_fwd")
        gu, act = _ffn_up_fwd(h2, wgu_t, "ffn_up_fwd")
        x2 = _matmul_nn(act, w_d, 0, x1, "wd_fwd")
        saved.append((x, h, proj, o_hg, o_g, sall, o_att, ya, yb, mix, x1, h2, gu, act, n1, n2, lb, gn, sinks, weights))
        x, weights = x2, next_weights

    dx, d_fn, loss = _loss_head(x, final_norm[None, :], target, "loss_head")

    owned = [None] * DEPTH
    pending = None
    d_n1, d_n2, d_lb, d_gn, d_sinks = ([None] * DEPTH for _ in range(5))
    for l in reversed(range(DEPTH)):
        x0, h, proj, o_hg, o_g, sall, o_att, ya, yb, mix, x1, h2, gu, act, n1, n2, lb, gn, sinks, weights = saved[l]
        win_t, w_pa, w_pb, w_o, wgu_t, w_d = weights
        dgu = _ffn_down_bwd(dx, w_d, gu, "ffn_down_bwd")
        g_wd = _matmul_tn(act, dx, "wd_grad", tm=1408)
        g_wgu = _matmul_tn(dgu, h2, "wgu_grad", tm=1408)
        dx1, d_n2[l] = _rows_bwd([dgu], wgu_t, x1, n2, dx, "ffn_up_bwd")
        dmix = _matmul_nt(dx1, w_o, 0, D_MODEL, F32, "wo_bwd", tn=1024)
        g_wo = _matmul_tn(mix, dx1, "wo_grad")
        dya, dyb, dgab = _mix_bwd(ya, yb, proj, dmix, "mix_bwd")
        g_wpa = _matmul_tn(o_g, dya, "wpa_grad")
        g_wpb = _matmul_tn(o_att, dyb, "wpb_grad")
        dog = _matmul_nt(dya, w_pa, 0, D_MODEL, F32, "wpa_bwd", tn=1024)
        doatt = _matmul_nt(dyb, w_pb, 0, D_MODEL, F32, "wpb_bwd", tn=1024)
        ex = _scatter_exchange(pending) if pending is not None else None
        (dhg, d_lb[l], d_gn[l]), land = _hgrn2_bwd(proj, lb, gn, o_hg, sall, dog, "hgrn2_bwd", ex)
        if pending is not None:
            owned[l + 1] = _sum_slots(land[0], "sum_slots")
        daq, dak, dav, d_sinks[l] = _swa_bwd(proj, sinks, tabs, o_att, doatt, "swa_bwd")
        dakv = jnp.concatenate([dak, dav], axis=1)
        g_win = jnp.concatenate([_matmul_tn(dhg, h, "win_grad_hg"), _matmul_tn(daq, h, "win_grad_aq"),
                                 _matmul_tn(dakv, h, "win_grad_akv"), _matmul_tn(dgab, h, "win_grad_gates")], axis=0)
        dx, d_n1[l] = _rows_bwd([dhg, daq, dakv, dgab], win_t, x0, n1, dx1, "win_bwd")
        pending = (g_win, g_wpa, g_wpb, g_wo, g_wgu, g_wd)
    owned[0] = _sum_slots(_run_exchange(_scatter_exchange(pending), "scatter_grads")[0], "sum_slots")

    d_sink_rows = [jnp.concatenate([d[0, 0, :8], d[1, 0, :8]]) for d in d_sinks]
    small = (jnp.concatenate(d_n1, axis=0), jnp.concatenate(d_lb, axis=0), jnp.concatenate(d_gn, axis=0),
             jnp.concatenate(d_n2, axis=0), d_fn, jnp.stack(d_sink_rows, axis=0))
    return loss, dx, jnp.stack(owned, axis=0), small


def _sum_slots(land, name, tr=480):
    _, rows, d = land.shape

    def body(l_ref, o_ref):
        acc = l_ref[0].astype(F32)
        for k in range(1, N_DEV):
            acc = acc + l_ref[k].astype(F32)
        o_ref[...] = acc

    return pl.pallas_call(
        body, name=name, grid=(rows // tr,),
        in_specs=[pl.BlockSpec((N_DEV, tr, d), lambda i: (0, i, 0))],
        out_specs=pl.BlockSpec((tr, d), lambda i: (i, 0)),
        out_shape=jax.ShapeDtypeStruct((rows, d), F32),
        compiler_params=_cp("parallel"))(land)


def _adamw(w, g, m, v, name):
    shape = w.shape
    c = shape[-1]
    rows = w.size // c
    tr = rows
    for cand in (512, 352, 128):
        if rows % cand == 0:
            tr = cand
            break
    c1 = 1.0 / (1.0 - ADAM_B1 ** ADAM_STEP)
    c2 = 1.0 / (1.0 - ADAM_B2 ** ADAM_STEP)

    def body(w_ref, g_ref, m_ref, v_ref, d_ref, nm_ref, nv_ref):
        gv = g_ref[...]
        nm = ADAM_B1 * m_ref[...] + (1.0 - ADAM_B1) * gv
        nv = ADAM_B2 * v_ref[...] + (1.0 - ADAM_B2) * (gv * gv)
        d_ref[...] = -ADAM_LR * ((nm * c1) / (jnp.sqrt(nv * c2) + ADAM_EPS) + ADAM_WD * w_ref[...])
        nm_ref[...] = nm
        nv_ref[...] = nv

    spec = pl.BlockSpec((tr, c), lambda i: (i, 0))
    outs = pl.pallas_call(
        body, name=name, grid=(rows // tr,), in_specs=[spec] * 4, out_specs=[spec] * 3,
        out_shape=[jax.ShapeDtypeStruct((rows, c), F32)] * 3,
        compiler_params=_cp("parallel"))(*[a.reshape(rows, c) for a in (w, g, m, v)])
    return tuple(o.reshape(shape) for o in outs)


def kernel(x, norm1, w_in, lb_logits, hg_norm, attn_sinks, w_pa, w_pb, w_o, norm2, w_gate, w_up, w_down, final_norm, loss_target, m_norm1, m_w_in, m_lb_logits, m_hg_norm, m_attn_sinks, m_w_pa, m_w_pb, m_w_o, m_norm2, m_w_gate, m_w_up, m_w_down, m_final_norm, v_norm1, v_w_in, v_lb_logits, v_hg_norm, v_attn_sinks, v_w_pa, v_w_pb, v_w_o, v_norm2, v_w_gate, v_w_up, v_w_down, v_final_norm):
    t = x.shape[1]
    shards = [jnp.swapaxes(w_in, 1, 2).astype(BF16), w_pa.astype(BF16), w_pb.astype(BF16), w_o.astype(BF16),
              jnp.swapaxes(w_gate, 1, 2).astype(BF16), jnp.swapaxes(w_up, 1, 2).astype(BF16), w_down.astype(BF16)]
    loss_lanes, grad_x, owned, small = _local_step(
        x.reshape(t, D_MODEL), loss_target.reshape(t, D_MODEL), shards,
        norm1, lb_logits, hg_norm, attn_sinks, norm2, final_norm)

    def rows_of(ti, transpose):
        g = owned[:, SLOT_OFF[ti]:SLOT_OFF[ti] + SHARD_ROWS[ti], :]
        return jnp.swapaxes(g, 1, 2) if transpose else g

    g_big = {"w_in": rows_of(0, True), "w_pa": rows_of(1, False), "w_pb": rows_of(2, False), "w_o": rows_of(3, False),
             "w_gate": rows_of(4, True), "w_up": rows_of(5, True), "w_down": rows_of(6, False)}

    d_n1, d_lb, d_gn, d_n2, d_fn, d_sinks = small
    pad = jnp.zeros((DEPTH, D_MODEL - ATT_Q_HEADS), F32)
    packed = jnp.concatenate([
        d_n1, d_lb, d_gn, d_n2, d_fn, jnp.concatenate([d_sinks, pad], axis=1),
        jnp.concatenate([loss_lanes, jnp.zeros((1, D_MODEL - 128), F32)], axis=1),
        jnp.zeros((SMALL_ROWS - 22, D_MODEL), F32)], axis=0)
    total = _all_reduce_small(packed)
    loss = total[21, 0]
    g_small = {"norm1": total[0:4], "lb_logits": _lb_bwd(lb_logits, total[4:8]), "hg_norm": total[8:12],
               "norm2": total[12:16], "final_norm": total[16], "attn_sinks": total[17:21, :ATT_Q_HEADS]}

    params = {"norm1": (norm1, m_norm1, v_norm1), "w_in": (w_in, m_w_in, v_w_in),
              "lb_logits": (lb_logits, m_lb_logits, v_lb_logits), "hg_norm": (hg_norm, m_hg_norm, v_hg_norm),
              "attn_sinks": (attn_sinks, m_attn_sinks, v_attn_sinks), "w_pa": (w_pa, m_w_pa, v_w_pa),
              "w_pb": (w_pb, m_w_pb, v_w_pb), "w_o": (w_o, m_w_o, v_w_o), "norm2": (norm2, m_norm2, v_norm2),
              "w_gate": (w_gate, m_w_gate, v_w_gate), "w_up": (w_up, m_w_up, v_w_up),
              "w_down": (w_down, m_w_down, v_w_down), "final_norm": (final_norm, m_final_norm, v_final_norm)}
    order = ["norm1", "w_in", "lb_logits", "hg_norm", "attn_sinks", "w_pa", "w_pb", "w_o", "norm2",
             "w_gate", "w_up", "w_down", "final_norm"]
    grads, deltas, new_m, new_v = [], [], [], []
    for name in order:
        w, m, v = params[name]
        g = (g_big[name] if name in g_big else g_small[name]).reshape(w.shape)
        w2 = w.reshape(1, -1) if w.ndim == 1 else w
        d, nm, nv = _adamw(w2, g.reshape(w2.shape), m.reshape(w2.shape), v.reshape(w2.shape), "adamw_" + name)
        grads.append(g)
        deltas.append(d.reshape(w.shape))
        new_m.append(nm.reshape(w.shape))
        new_v.append(nv.reshape(w.shape))
    return (loss, grad_x.reshape(x.shape), *grads, *deltas, *new_m, *new_v)
```

```python
import functools
from typing import Callable, NamedTuple

import jax
import jax.numpy as jnp
from jax import lax
from jax.experimental import pallas as pl
from jax.experimental.pallas import tpu as pltpu

F32, BF16 = jnp.float32, jnp.bfloat16

D_MODEL = 1024
DEPTH = 4
N_DEV = 8
HG_HEADS = 8
HG_DK = 128
HG_CHUNK = 64
HG_BLOCK = 256
HG_EXP_CLAMP = 60.0
ATT_Q_HEADS = 16
ATT_HEAD_DIM = 64
ATT_BLOCK = 128
ROPE_THETA = 500000.0
ROPE_DIM = 16
FFN_HIDDEN = 2816
EPS = 1e-6
MIN_F = 1e-30
ADAM_LR, ADAM_B1, ADAM_B2, ADAM_EPS, ADAM_WD, ADAM_STEP = 0.001, 0.9, 0.999, 1e-08, 0.01, 10

COL_HQ, COL_HF, COL_HI, COL_HG = 0, 1024, 2048, 3072
COL_AQ, COL_AK, COL_AV, COL_GA, COL_GB = 4096, 5120, 5376, 5632, 6656
IN_COLS = 7680

SHARD_ROWS = (960, 128, 128, 128, 352, 352, 352)
SLOT_OFF = (0, 960, 1088, 1216, 1344, 1696, 2048)
SLOT_ROWS = 2400
SMALL_ROWS = 24

VMEM_LIMIT_BYTES = 56 * 1024 * 1024

NN = ((1,), (0,))
NT = ((1,), (1,))
TN = ((0,), (0,))


def _dot(a, b, dims):
    return lax.dot_general(a, b, (dims, ((), ())), preferred_element_type=F32)


def _cp(*sem):
    return pltpu.CompilerParams(dimension_semantics=sem if sem else None, vmem_limit_bytes=VMEM_LIMIT_BYTES)


def _sigmoid(x):
    return 1.0 / (1.0 + jnp.exp(-x))


def _matmul_nt(a, w, row_off, n, out_dtype, name, tm=1024, tn=512):
    t, k = a.shape
    tm = min(tm, t)
    assert n % tn == 0 and row_off % tn == 0 and t % tm == 0

    def body(a_ref, w_ref, o_ref):
        o_ref[...] = _dot(a_ref[...].astype(BF16), w_ref[...], NT).astype(o_ref.dtype)

    return pl.pallas_call(
        body, name=name, grid=(n // tn, t // tm),
        in_specs=[pl.BlockSpec((tm, k), lambda j, i: (i, 0)),
                  pl.BlockSpec((tn, k), lambda j, i: (row_off // tn + j, 0))],
        out_specs=pl.BlockSpec((tm, tn), lambda j, i: (i, j)),
        out_shape=jax.ShapeDtypeStruct((t, n), out_dtype),
        compiler_params=_cp("parallel", "parallel"))(a, w)


def _matmul_nn(a, w, row_off, res, name, tm=512, tk=None):
    t, k = a.shape
    n = w.shape[1]
    tm = min(tm, t)
    tk = tk or k
    nk = k // tk
    assert k % tk == 0 and row_off % tk == 0 and t % tm == 0

    def body(*refs):
        if res is None:
            a_ref, w_ref, o_ref, acc = refs
        else:
            a_ref, w_ref, r_ref, o_ref, acc = refs
        kk = pl.program_id(1)
        part = _dot(a_ref[...].astype(BF16), w_ref[...], NN)

        @pl.when(kk == 0)
        def _():
            acc[...] = part

        @pl.when(kk > 0)
        def _():
            acc[...] += part

        @pl.when(kk == nk - 1)
        def _():
            o_ref[...] = acc[...] if res is None else acc[...] + r_ref[...]

    in_specs = [pl.BlockSpec((tm, tk), lambda i, kk: (i, kk)),
                pl.BlockSpec((tk, n), lambda i, kk: (row_off // tk + kk, 0))]
    args = [a, w]
    if res is not None:
        in_specs.append(pl.BlockSpec((tm, n), lambda i, kk: (i, 0)))
        args.append(res)
    return pl.pallas_call(
        body, name=name, grid=(t // tm, nk), in_specs=in_specs,
        out_specs=pl.BlockSpec((tm, n), lambda i, kk: (i, 0)),
        out_shape=jax.ShapeDtypeStruct((t, n), F32),
        scratch_shapes=[pltpu.VMEM((tm, n), F32)],
        compiler_params=_cp("parallel", "arbitrary"))(*args)


def _matmul_tn(a, b, name, tm=512, tk=1024):
    t, m = a.shape
    n = b.shape[1]
    tk = min(tk, t)
    nk = t // tk
    assert m % tm == 0 and t % tk == 0

    def body(a_ref, b_ref, o_ref, acc):
        kk = pl.program_id(1)
        part = _dot(a_ref[...].astype(BF16), b_ref[...].astype(BF16), TN)

        @pl.when(kk == 0)
        def _():
            acc[...] = part

        @pl.when(kk > 0)
        def _():
            acc[...] += part

        @pl.when(kk == nk - 1)
        def _():
            o_ref[...] = acc[...].astype(BF16)

    return pl.pallas_call(
        body, name=name, grid=(m // tm, nk),
        in_specs=[pl.BlockSpec((tk, tm), lambda i, kk: (kk, i)),
                  pl.BlockSpec((tk, n), lambda i, kk: (kk, 0))],
        out_specs=pl.BlockSpec((tm, n), lambda i, kk: (i, 0)),
        out_shape=jax.ShapeDtypeStruct((m, n), BF16),
        scratch_shapes=[pltpu.VMEM((tm, n), F32)],
        compiler_params=_cp("parallel", "arbitrary"))(a, b)


def _rms(x, g):
    return x * lax.rsqrt(jnp.mean(x * x, axis=-1, keepdims=True) + EPS) * g


def _rms_fwd(x, g, name, tm=512):
    t, d = x.shape
    tm = min(tm, t)

    def body(x_ref, g_ref, o_ref):
        o_ref[...] = _rms(x_ref[...], g_ref[...]).astype(BF16)

    return pl.pallas_call(
        body, name=name, grid=(t // tm,),
        in_specs=[pl.BlockSpec((tm, d), lambda i: (i, 0)), pl.BlockSpec((1, d), lambda i: (0, 0))],
        out_specs=pl.BlockSpec((tm, d), lambda i: (i, 0)),
        out_shape=jax.ShapeDtypeStruct((t, d), BF16),
        compiler_params=_cp("parallel"))(x, g)


def _mix(ya, yb, ga, gb):
    return _sigmoid(ga) * ya + _sigmoid(gb) * yb


def _gate_specs(tm):
    half = D_MODEL // 2
    return [pl.BlockSpec((tm, half), lambda i, c=c: (i, c))
            for c in (COL_GA // half, COL_GA // half + 1, COL_GB // half, COL_GB // half + 1)]


def _mix_bwd(ya, yb, proj, dmix, name, tm=256):
    t, d = ya.shape
    tm = min(tm, t)

    def body(ya_ref, yb_ref, ga0, ga1, gb0, gb1, dm_ref, dya_ref, dyb_ref, dg_ref):
        ga = jnp.concatenate([ga0[...], ga1[...]], axis=1)
        gb = jnp.concatenate([gb0[...], gb1[...]], axis=1)
        _, vjp = jax.vjp(_mix, ya_ref[...].astype(F32), yb_ref[...].astype(F32), ga, gb)
        dya, dyb, dga, dgb = vjp(dm_ref[...])
        dya_ref[...] = dya.astype(BF16)
        dyb_ref[...] = dyb.astype(BF16)
        dg_ref[:, :d] = dga.astype(BF16)
        dg_ref[:, d:] = dgb.astype(BF16)

    row = pl.BlockSpec((tm, d), lambda i: (i, 0))
    wide = pl.BlockSpec((tm, 2 * d), lambda i: (i, 0))
    return pl.pallas_call(
        body, name=name, grid=(t // tm,), in_specs=[row, row] + _gate_specs(tm) + [row],
        out_specs=[row, row, wide],
        out_shape=[jax.ShapeDtypeStruct((t, d), BF16), jax.ShapeDtypeStruct((t, d), BF16),
                   jax.ShapeDtypeStruct((t, 2 * d), BF16)],
        compiler_params=_cp("parallel"))(ya, yb, proj, proj, proj, proj, dmix)


def _swiglu(g, u):
    return g * _sigmoid(g) * u


def _ffn_up_fwd(h2, wgu_t, name, tm=512):
    t, d = h2.shape
    tm = min(tm, t)
    fh = FFN_HIDDEN // 2

    def body(a_ref, w_ref, gu_ref, act_ref):
        r = _dot(a_ref[...], w_ref[...], NT)
        gu_ref[...] = r.astype(BF16)
        act_ref[...] = _swiglu(r[:, :fh], r[:, fh:]).astype(BF16)

    return pl.pallas_call(
        body, name=name, grid=(2, t // tm),
        in_specs=[pl.BlockSpec((tm, d), lambda j, i: (i, 0)), pl.BlockSpec((2 * fh, d), lambda j, i: (j, 0))],
        out_specs=[pl.BlockSpec((tm, 2 * fh), lambda j, i: (i, j)), pl.BlockSpec((tm, fh), lambda j, i: (i, j))],
        out_shape=[jax.ShapeDtypeStruct((t, 4 * fh), BF16), jax.ShapeDtypeStruct((t, 2 * fh), BF16)],
        compiler_params=_cp("parallel", "parallel"))(h2, wgu_t)


def _ffn_down_bwd(dx, w_d, gu, name, tm=512):
    t, d = dx.shape
    tm = min(tm, t)
    fh = FFN_HIDDEN // 2

    def body(a_ref, w_ref, gu_ref, o_ref):
        dact = _dot(a_ref[...].astype(BF16), w_ref[...], NT)
        _, vjp = jax.vjp(_swiglu, gu_ref[:, :fh].astype(F32), gu_ref[:, fh:].astype(F32))
        dg, du = vjp(dact)
        o_ref[:, :fh] = dg.astype(BF16)
        o_ref[:, fh:] = du.astype(BF16)

    wide = pl.BlockSpec((tm, 2 * fh), lambda j, i: (i, j))
    return pl.pallas_call(
        body, name=name, grid=(2, t // tm),
        in_specs=[pl.BlockSpec((tm, d), lambda j, i: (i, 0)), pl.BlockSpec((fh, d), lambda j, i: (j, 0)), wide],
        out_specs=wide,
        out_shape=jax.ShapeDtypeStruct((t, 4 * fh), BF16),
        compiler_params=_cp("parallel", "parallel"))(dx, w_d, gu)


def _rows_bwd(pieces, w, x, g, dres, name, tm=256):
    t, d = x.shape
    tm = min(tm, t)
    widths = [p.shape[1] for p in pieces]
    starts = [sum(widths[:i]) for i in range(len(widths))]
    assert sum(widths) == w.shape[0]
    n_p = len(pieces)

    def body(*refs):
        p_refs, (w_ref, x_ref, g_ref, dres_ref, dx_ref, dg_ref) = refs[:n_p], refs[n_p:]
        dh = _dot(p_refs[0][...], w_ref[pl.ds(starts[0], widths[0]), :], NN)
        for i in range(1, n_p):
            dh = dh + _dot(p_refs[i][...], w_ref[pl.ds(starts[i], widths[i]), :], NN)
        _, vjp = jax.vjp(_rms, x_ref[...], g_ref[...])
        dx, dg = vjp(dh)
        dx_ref[...] = dres_ref[...] + dx

        @pl.when(pl.program_id(0) == 0)
        def _():
            dg_ref[...] = jnp.zeros_like(dg_ref)

        dg_ref[...] += dg

    row = pl.BlockSpec((tm, d), lambda i: (i, 0))
    vec = pl.BlockSpec((1, d), lambda i: (0, 0))
    return pl.pallas_call(
        body, name=name, grid=(t // tm,),
        in_specs=[pl.BlockSpec((tm, k), lambda i: (i, 0)) for k in widths]
        + [pl.BlockSpec(w.shape, lambda i: (0, 0)), row, vec, row],
        out_specs=[row, vec],
        out_shape=[jax.ShapeDtypeStruct((t, d), F32), jax.ShapeDtypeStruct((1, d), F32)],
        compiler_params=_cp("arbitrary"))(*pieces, w, x, g, dres)


def _merge_fwd(o_g, o_att, proj, x, w_pa, w_pb, w_o, name, tm=256):
    t, d = x.shape
    tm = min(tm, t)

    def body(og_ref, oa_ref, ga0, ga1, gb0, gb1, x_ref, wpa_ref, wpb_ref, wo_ref, ya_ref, yb_ref, mix_ref, x1_ref):
        ya = _dot(og_ref[...], wpa_ref[...], NN)
        yb = _dot(oa_ref[...], wpb_ref[...], NN)
        ga = jnp.concatenate([ga0[...], ga1[...]], axis=1)
        gb = jnp.concatenate([gb0[...], gb1[...]], axis=1)
        mix = _mix(ya, yb, ga, gb).astype(BF16)
        ya_ref[...] = ya.astype(BF16)
        yb_ref[...] = yb.astype(BF16)
        mix_ref[...] = mix
        x1_ref[...] = x_ref[...] + _dot(mix, wo_ref[...], NN)

    row = pl.BlockSpec((tm, d), lambda i: (i, 0))
    mat = pl.BlockSpec((d, d), lambda i: (0, 0))
    return pl.pallas_call(
        body, name=name, grid=(t // tm,),
        in_specs=[row, row] + _gate_specs(tm) + [row, mat, mat, mat], out_specs=[row, row, row, row],
        out_shape=[jax.ShapeDtypeStruct((t, d), BF16)] * 3 + [jax.ShapeDtypeStruct((t, d), F32)],
        compiler_params=_cp("parallel"))(o_g, o_att, proj, proj, proj, proj, x, w_pa, w_pb, w_o)


def _loss_head(x, g, target, name, tm=512):
    t, d = x.shape
    tm = min(tm, t)

    def body(x_ref, g_ref, t_ref, dx_ref, dg_ref, loss_ref):
        tgt = t_ref[...]

        def f(xv, gv):
            err = _rms(xv, gv) - tgt
            return 0.5 * jnp.sum(jnp.mean(err * err, axis=-1, keepdims=True))

        loss, vjp = jax.vjp(f, x_ref[...], g_ref[...])
        dx, dg = vjp(jnp.ones((), F32))
        dx_ref[...] = dx

        @pl.when(pl.program_id(0) == 0)
        def _():
            dg_ref[...] = jnp.zeros_like(dg_ref)
            loss_ref[...] = jnp.zeros_like(loss_ref)

        dg_ref[...] += dg
        loss_ref[...] += jnp.full(loss_ref.shape, loss, F32)

    row = pl.BlockSpec((tm, d), lambda i: (i, 0))
    vec = pl.BlockSpec((1, d), lambda i: (0, 0))
    lane = pl.BlockSpec((1, 128), lambda i: (0, 0))
    return pl.pallas_call(
        body, name=name, grid=(t // tm,), in_specs=[row, vec, row], out_specs=[row, vec, lane],
        out_shape=[jax.ShapeDtypeStruct((t, d), F32), jax.ShapeDtypeStruct((1, d), F32),
                   jax.ShapeDtypeStruct((1, 128), F32)],
        compiler_params=_cp("arbitrary"))(x, g, target)


def _lb_rows(l0, l1, l2, l3):
    mx = jnp.maximum(jnp.maximum(l0, l1), jnp.maximum(l2, l3))
    e0, e1, e2, e3 = jnp.exp(l0 - mx), jnp.exp(l1 - mx), jnp.exp(l2 - mx), jnp.exp(l3 - mx)
    s = e0 + e1 + e2 + e3
    p0, p1, p2, p3 = e0 / s, e1 / s, e2 / s, e3 / s
    c1 = p0 + p1
    c2 = c1 + p2
    c3 = c2 + p3
    return p0 - p0, c1 - p0, c2 - p0, c3 - p0


def _lb_fwd(lb_logits):
    def body(l_ref, o_ref):
        rows = _lb_rows(*[l_ref[pl.ds(i, 1), :] for i in range(DEPTH)])
        for i in range(DEPTH):
            o_ref[pl.ds(i, 1), :] = rows[i]

    return pl.pallas_call(body, name="lb_fwd", out_shape=jax.ShapeDtypeStruct(lb_logits.shape, F32))(lb_logits)


def _lb_bwd(lb_logits, dlb):
    def body(l_ref, d_ref, o_ref):
        _, vjp = jax.vjp(_lb_rows, *[l_ref[pl.ds(i, 1), :] for i in range(DEPTH)])
        grads = vjp(tuple(d_ref[pl.ds(i, 1), :] for i in range(DEPTH)))
        for i in range(DEPTH):
            o_ref[pl.ds(i, 1), :] = grads[i]

    return pl.pallas_call(body, name="lb_bwd", out_shape=jax.ShapeDtypeStruct(lb_logits.shape, F32))(lb_logits, dlb)


MESH = pl.DeviceIdType.MESH
ANY = pl.BlockSpec(memory_space=pl.ANY)
N_KINDS = len(SHARD_ROWS)
FFN_HALF = FFN_HIDDEN // 2
KIND_PLACE = ((0, 0), (1, 0), (2, 0), (3, 0), (4, 0), (4, FFN_HALF), (5, 0))
KIND_HALF_SKIP = (0, 0, 0, 0, FFN_HALF, FFN_HALF, 0)
FULL_ROWS = (N_DEV * SHARD_ROWS[0], D_MODEL, D_MODEL, D_MODEL, 2 * N_DEV * SHARD_ROWS[4], N_DEV * SHARD_ROWS[6])


def _kind_rows(ti, dev):
    oi, base = KIND_PLACE[ti]
    start = base + dev * SHARD_ROWS[ti]
    if KIND_HALF_SKIP[ti]:
        start = start + (dev // (N_DEV // 2)) * KIND_HALF_SKIP[ti]
    return oi, pl.ds(start, SHARD_ROWS[ti])


def _position():
    x, y, c = lax.axis_index("x"), lax.axis_index("y"), lax.axis_index("c")
    return x, y, c, 4 * x + 2 * y + c


def _peer(x, y, c, r):
    px = 1 - x if r & 4 else x
    py = 1 - y if r & 2 else y
    pc = 1 - c if r & 1 else c
    return (px, py, pc), 4 * px + 2 * py + pc


class _Exchange(NamedTuple):
    operands: tuple
    out_shape: tuple
    copies: Callable


EXCHANGE_SCRATCH = (pltpu.SemaphoreType.DMA((N_DEV, N_KINDS)), pltpu.SemaphoreType.DMA((N_DEV, N_KINDS)),
                    pltpu.SemaphoreType.DMA((N_KINDS,)))


def _all_pairs(ends, send_sems, recv_sems):
    x, y, c, me = _position()
    out = []
    for r in range(1, N_DEV):
        peer, pid = _peer(x, y, c, r)
        for ti in range(N_KINDS):
            src, dst = ends(ti, me, pid)
            out.append(pltpu.make_async_remote_copy(
                src_ref=src, dst_ref=dst, send_sem=send_sems.at[r, ti], recv_sem=recv_sems.at[r, ti],
                device_id=peer, device_id_type=MESH))
    return out


def _gather_exchange(shards):
    def copies(ins, outs, send_sems, recv_sems, local_sems, arrivals):
        def window(ti, dev):
            oi, rows = _kind_rows(ti, dev)
            return outs[oi].at[rows, :]

        if arrivals:
            return _all_pairs(lambda ti, me, pid: (ins[ti], window(ti, pid)), send_sems, recv_sems)
        _, _, _, me = _position()
        local = [pltpu.make_async_copy(ins[ti], window(ti, me), local_sems.at[ti]) for ti in range(N_KINDS)]
        return local + _all_pairs(lambda ti, me, pid: (ins[ti], window(ti, me)), send_sems, recv_sems)

    return _Exchange(tuple(shards), tuple(jax.ShapeDtypeStruct((rows, D_MODEL), BF16) for rows in FULL_ROWS), copies)


def _scatter_exchange(grads):
    def copies(ins, outs, send_sems, recv_sems, local_sems, arrivals):
        land = outs[0]

        def piece(ti, dev):
            ii, rows = _kind_rows(ti, dev)
            return ins[ii].at[rows, :]

        def slot(ti, dev):
            return land.at[dev, pl.ds(SLOT_OFF[ti], SHARD_ROWS[ti]), :]

        if arrivals:
            return _all_pairs(lambda ti, me, pid: (piece(ti, me), slot(ti, pid)), send_sems, recv_sems)
        _, _, _, me = _position()
        local = [pltpu.make_async_copy(piece(ti, me), slot(ti, me), local_sems.at[ti]) for ti in range(N_KINDS)]
        return local + _all_pairs(lambda ti, me, pid: (piece(ti, pid), slot(ti, me)), send_sems, recv_sems)

    return _Exchange(tuple(grads), (jax.ShapeDtypeStruct((N_DEV, SLOT_ROWS, D_MODEL), BF16),), copies)


def _exchange_start(ex, ins, outs, sems):
    for cp in ex.copies(ins, outs, *sems, False):
        cp.start()


def _exchange_finish(ex, ins, outs, sems):
    for cp in ex.copies(ins, outs, *sems, True):
        cp.wait_recv()
    mine = ex.copies(ins, outs, *sems, False)
    for cp in mine[:N_KINDS]:
        cp.wait()
    for cp in mine[N_KINDS:]:
        cp.wait_send()


def _run_exchange(ex, name):
    n_in, n_out = len(ex.operands), len(ex.out_shape)

    def body(*refs):
        ins, outs, sems = refs[:n_in], refs[n_in:n_in + n_out], refs[n_in + n_out:]
        _exchange_start(ex, ins, outs, sems)
        _exchange_finish(ex, ins, outs, sems)

    return pl.pallas_call(body, name=name, in_specs=[ANY] * n_in, out_specs=[ANY] * n_out,
                          out_shape=list(ex.out_shape), scratch_shapes=list(EXCHANGE_SCRATCH))(*ex.operands)


def _carry(body, ex, n_in, n_out, n_scratch, n_steps):
    if ex is None:
        return body
    e_in, e_out = len(ex.operands), len(ex.out_shape)

    def carrying(*refs):
        own_in, ex_in = refs[:n_in], refs[n_in:n_in + e_in]
        rest = refs[n_in + e_in:]
        own_out, ex_out = rest[:n_out], rest[n_out:n_out + e_out]
        own_scratch, sems = rest[n_out + e_out:n_out + e_out + n_scratch], rest[n_out + e_out + n_scratch:]

        @pl.when(pl.program_id(0) == 0)
        def _():
            _exchange_start(ex, ex_in, ex_out, sems)

        body(*own_in, *own_out, *own_scratch)

        @pl.when(pl.program_id(0) == n_steps - 1)
        def _():
            _exchange_finish(ex, ex_in, ex_out, sems)

    return carrying


def _small_sum_body(p_ref, o_ref, buf, send_sems, recv_sems):
    x, y, c, me = _position()
    buf[me] = p_ref[...]
    sends = []
    for r in range(1, N_DEV):
        peer, _ = _peer(x, y, c, r)
        sends.append(pltpu.make_async_remote_copy(
            src_ref=p_ref, dst_ref=buf.at[me], send_sem=send_sems.at[r], recv_sem=recv_sems.at[r],
            device_id=peer, device_id_type=MESH))
    for cp in sends:
        cp.start()
    for r in range(1, N_DEV):
        peer, pid = _peer(x, y, c, r)
        pltpu.make_async_remote_copy(
            src_ref=p_ref, dst_ref=buf.at[pid], send_sem=send_sems.at[r], recv_sem=recv_sems.at[r],
            device_id=peer, device_id_type=MESH).wait_recv()
    for cp in sends:
        cp.wait_send()
    acc = buf[0]
    for k in range(1, N_DEV):
        acc = acc + buf[k]
    o_ref[...] = acc


def _all_reduce_small(part):
    rows, d = part.shape
    vmem = pl.BlockSpec(memory_space=pltpu.VMEM)
    return pl.pallas_call(
        functools.partial(_small_sum_body), name="all_reduce_small", in_specs=[vmem], out_specs=vmem,
        out_shape=jax.ShapeDtypeStruct((rows, d), F32),
        scratch_shapes=[pltpu.VMEM((N_DEV, rows, d), F32), pltpu.SemaphoreType.DMA((N_DEV,)),
                        pltpu.SemaphoreType.DMA((N_DEV,))],
    )(part)


HG_PAIR = 2 * HG_DK


def _hg_consts():
    c = HG_CHUNK
    r = lax.broadcasted_iota(jnp.int32, (c, c), 0)
    s = lax.broadcasted_iota(jnp.int32, (c, c), 1)
    r2 = lax.broadcasted_iota(jnp.int32, (c, 2 * c), 0)
    s2 = lax.broadcasted_iota(jnp.int32, (c, 2 * c), 1)
    causal2 = jnp.where(s2 >= c, s2 - c, s2) <= r2
    lane_hi = lax.broadcasted_iota(jnp.int32, (c, HG_PAIR), 1) >= HG_DK
    same_head = ((lax.broadcasted_iota(jnp.int32, (HG_PAIR, HG_PAIR), 0) >= HG_DK)
                 == (lax.broadcasted_iota(jnp.int32, (HG_PAIR, HG_PAIR), 1) >= HG_DK))
    return (s <= r).astype(BF16), (s >= r).astype(BF16), causal2, lane_hi, same_head


def _head_rows(x, lane_hi):
    zero = jnp.zeros_like(x)
    return jnp.concatenate([jnp.where(lane_hi, zero, x), jnp.where(lane_hi, x, zero)], axis=0)


def _own_rows(y, lane_hi):
    return jnp.where(lane_hi, y[HG_CHUNK:], y[:HG_CHUNK])


def _split3(x):
    hi = x.astype(BF16)
    r1 = x - hi.astype(F32)
    mid = r1.astype(BF16)
    lo = (r1 - mid.astype(F32)).astype(BF16)
    return jnp.concatenate([hi, mid, lo], axis=1)


def _cumsum_rows(tri, x):
    w = x.shape[1]
    y = _dot(tri, _split3(x), NN)
    return y[:, :w] + y[:, w:2 * w] + y[:, 2 * w:]


def _hg_chunk(zq, zf, lb, tril, b_ref):
    c = HG_CHUNK
    sq = _sigmoid(zq)
    q = zq * sq
    sg = _sigmoid(zf)
    f = lb + (1.0 - lb) * sg
    logf = jnp.log(jnp.maximum(f, MIN_F))
    k = 1.0 - f
    b = _cumsum_rows(tril, logf)
    b_ref[...] = b
    mid = b_ref[pl.ds(c // 2 - 1, 1), :]
    bc = b_ref[pl.ds(c - 1, 1), :]
    em = jnp.exp(jnp.minimum(b - mid, HG_EXP_CLAMP))
    en = jnp.exp(jnp.minimum(mid - b, HG_EXP_CLAMP))
    return sq, q, sg, f, k, b, em, en, bc


def _hg_gate(o, zg, gn):
    return o * lax.rsqrt(jnp.mean(o * o, axis=-1, keepdims=True) + EPS) * gn * (zg * _sigmoid(zg))


def _hgrn2_fwd(proj, lb, gn, name, ex=None):
    t = proj.shape[0]
    bs_tok = min(HG_BLOCK, t)
    n_chunks = bs_tok // HG_CHUNK
    w = HG_HEADS * HG_DK

    def body(hq_ref, hf_ref, hi_ref, hg_ref, lb_ref, gn_ref, o_ref, og_ref, sall_ref, st_ref, b_ref):
        @pl.when(pl.program_id(0) == 0)
        def _():
            st_ref[...] = jnp.zeros_like(st_ref)

        tril, _, causal2, lane_hi, same_head = _hg_consts()

        for ci in range(n_chunks):
            rows = pl.ds(ci * HG_CHUNK, HG_CHUNK)
            for p in range(HG_HEADS // 2):
                cols = slice(p * HG_PAIR, (p + 1) * HG_PAIR)
                v = hi_ref[rows, cols]
                zg = hg_ref[rows, cols]
                _, q, _, _, k, b, em, en, bc = _hg_chunk(hq_ref[rows, cols], hf_ref[rows, cols],
                                                         lb_ref[:, cols], tril, b_ref.at[ci, p])
                st0 = st_ref[p]
                sall_ref[ci, 2 * p] = st0[:HG_DK, :HG_DK]
                sall_ref[ci, 2 * p + 1] = st0[HG_DK:, HG_DK:]
                vb = v.astype(BF16)
                o = _dot((q * jnp.exp(b)).astype(BF16), st0.astype(BF16), NT)
                a = jnp.where(causal2, _dot((q * em).astype(BF16), _head_rows((k * en).astype(BF16), lane_hi), NT), 0.0)
                o = o + _dot(a.astype(BF16), _head_rows(vb, lane_hi), NN)
                kdec = (k * jnp.exp(bc - b)).astype(BF16)
                st_ref[p] = st0 * jnp.exp(bc) + jnp.where(same_head, _dot(vb, kdec, TN), 0.0)
                o_ref[rows, cols] = o
                for hh in range(2):
                    sl = slice(hh * HG_DK, (hh + 1) * HG_DK)
                    hcols = slice(p * HG_PAIR + hh * HG_DK, p * HG_PAIR + (hh + 1) * HG_DK)
                    og_ref[rows, hcols] = _hg_gate(o[:, sl], zg[:, sl], gn_ref[:, hcols]).astype(BF16)

    def col(j):
        return pl.BlockSpec((bs_tok, w), lambda n, j=j: (n, j))

    vec = pl.BlockSpec((1, w), lambda n: (0, 0))
    ex_in, ex_out = (ex.operands, ex.out_shape) if ex else ((), ())
    outs = pl.pallas_call(
        _carry(body, ex, 6, 3, 2, t // bs_tok), name=name, grid=(t // bs_tok,),
        in_specs=[col(COL_HQ // w), col(COL_HF // w), col(COL_HI // w), col(COL_HG // w), vec, vec] + [ANY] * len(ex_in),
        out_specs=[col(0), col(0),
                   pl.BlockSpec((n_chunks, HG_HEADS, HG_DK, HG_DK), lambda n: (n, 0, 0, 0))] + [ANY] * len(ex_out),
        out_shape=[jax.ShapeDtypeStruct((t, w), F32), jax.ShapeDtypeStruct((t, w), BF16),
                   jax.ShapeDtypeStruct((t // HG_CHUNK, HG_HEADS, HG_DK, HG_DK), F32)] + list(ex_out),
        scratch_shapes=[pltpu.VMEM((HG_HEADS // 2, HG_PAIR, HG_PAIR), F32),
                        pltpu.VMEM((n_chunks, HG_HEADS // 2, HG_CHUNK, HG_PAIR), F32)]
        + (list(EXCHANGE_SCRATCH) if ex else []),
        compiler_params=_cp("arbitrary"))(proj, proj, proj, proj, lb, gn, *ex_in)
    return outs[:3], outs[3:]


def _hgrn2_bwd(proj, lb, gn, o_hg, sall, dog, name, ex=None):
    t = proj.shape[0]
    bs_tok = min(HG_BLOCK, t)
    n_chunks = bs_tok // HG_CHUNK
    n_blocks = t // bs_tok
    w = HG_HEADS * HG_DK

    def body(hq_ref, hf_ref, hi_ref, hg_ref, lb_ref, gn_ref, o_ref, sall_ref, dog_ref,
             da_ref, dlb_ref, dgn_ref, dst_ref, b_ref):
        @pl.when(pl.program_id(0) == 0)
        def _():
            dst_ref[...] = jnp.zeros_like(dst_ref)
            dlb_ref[...] = jnp.zeros_like(dlb_ref)
            dgn_ref[...] = jnp.zeros_like(dgn_ref)

        tril, rev_tril, causal2, lane_hi, same_head = _hg_consts()
        zero_block = jnp.zeros((HG_DK, HG_DK), F32)

        for ci in reversed(range(n_chunks)):
            rows = pl.ds(ci * HG_CHUNK, HG_CHUNK)
            for p in range(HG_HEADS // 2):
                cols = slice(p * HG_PAIR, (p + 1) * HG_PAIR)
                zq = hq_ref[rows, cols]
                v = hi_ref[rows, cols]
                zg = hg_ref[rows, cols]
                lbv = lb_ref[:, cols]
                sq, q, sg, f, k, b, em, en, bc = _hg_chunk(zq, hf_ref[rows, cols], lbv, tril, b_ref.at[ci, p])
                st0 = jnp.concatenate([jnp.concatenate([sall_ref[ci, 2 * p], zero_block], axis=1),
                                       jnp.concatenate([zero_block, sall_ref[ci, 2 * p + 1]], axis=1)], axis=0)
                dst1 = dst_ref[p]
                vb = v.astype(BF16)
                eb = jnp.exp(b)
                qg = (q * eb).astype(BF16)
                qt = (q * em).astype(BF16)
                kref = (k * en).astype(BF16)
                ebcb = jnp.exp(bc - b)
                kdec = (k * ebcb).astype(BF16)
                ebc = jnp.exp(bc)
                st1 = st0 * ebc + jnp.where(same_head, _dot(vb, kdec, TN), 0.0)

                dos, dzgs, dgns = [], [], []
                for hh in range(2):
                    sl = slice(hh * HG_DK, (hh + 1) * HG_DK)
                    hcols = slice(p * HG_PAIR + hh * HG_DK, p * HG_PAIR + (hh + 1) * HG_DK)
                    _, gate_vjp = jax.vjp(_hg_gate, o_ref[rows, hcols], zg[:, sl], gn_ref[:, hcols])
                    do_h, dzg_h, dgn_h = gate_vjp(dog_ref[rows, hcols])
                    dos.append(do_h)
                    dzgs.append(dzg_h)
                    dgns.append(dgn_h)
                dob = jnp.concatenate(dos, axis=1).astype(BF16)
                vrows, krows = _head_rows(vb, lane_hi), _head_rows(kref, lane_hi)
                dam = jnp.where(causal2, _dot(dob, vrows, NT), 0.0).astype(BF16)
                a = jnp.where(causal2, _dot(qt, krows, NT), 0.0)
                dk = ebcb * _dot(vb, dst1.astype(BF16), NN) + en * _own_rows(_dot(dam, qt, TN), lane_hi)
                dq = eb * _dot(dob, st0.astype(BF16), NN) + em * _dot(dam, krows, NN)
                dv = _own_rows(_dot(a.astype(BF16), dob, TN), lane_hi) + _dot(kdec, dst1.astype(BF16), NT)
                dst_ref[p] = dst1 * ebc + jnp.where(same_head, _dot(dob, qg, TN), 0.0)

                dbx = jnp.sum(dst1 * st1, axis=0, keepdims=True)
                dlogf = _cumsum_rows(rev_tril, q * dq - k * dk) + dbx
                df = jnp.where(f > MIN_F, dlogf / f, 0.0) - dk
                dzf = df * (1.0 - lbv) * sg * (1.0 - sg)
                dzq = dq * (sq * (1.0 + zq * (1.0 - sq)))
                da_ref[rows, pl.ds(COL_HQ + p * HG_PAIR, HG_PAIR)] = dzq.astype(BF16)
                da_ref[rows, pl.ds(COL_HF + p * HG_PAIR, HG_PAIR)] = dzf.astype(BF16)
                da_ref[rows, pl.ds(COL_HI + p * HG_PAIR, HG_PAIR)] = dv.astype(BF16)
                da_ref[rows, pl.ds(COL_HG + p * HG_PAIR, HG_PAIR)] = jnp.concatenate(dzgs, axis=1).astype(BF16)
                dlb_ref[:, cols] += jnp.sum(df * (1.0 - sg), axis=0, keepdims=True)
                dgn_ref[:, cols] += jnp.concatenate(dgns, axis=1)

    def col(j):
        return pl.BlockSpec((bs_tok, w), lambda n, j=j: (n_blocks - 1 - n, j))

    vec = pl.BlockSpec((1, w), lambda n: (0, 0))
    ex_in, ex_out = (ex.operands, ex.out_shape) if ex else ((), ())
    outs = pl.pallas_call(
        _carry(body, ex, 9, 3, 2, n_blocks), name=name, grid=(n_blocks,),
        in_specs=[col(COL_HQ // w), col(COL_HF // w), col(COL_HI // w), col(COL_HG // w), vec, vec, col(0),
                  pl.BlockSpec((n_chunks, HG_HEADS, HG_DK, HG_DK), lambda n: (n_blocks - 1 - n, 0, 0, 0)),
                  col(0)] + [ANY] * len(ex_in),
        out_specs=[pl.BlockSpec((bs_tok, 4 * w), lambda n: (n_blocks - 1 - n, 0)), vec, vec] + [ANY] * len(ex_out),
        out_shape=[jax.ShapeDtypeStruct((t, 4 * w), BF16), jax.ShapeDtypeStruct((1, w), F32),
                   jax.ShapeDtypeStruct((1, w), F32)] + list(ex_out),
        scratch_shapes=[pltpu.VMEM((HG_HEADS // 2, HG_PAIR, HG_PAIR), F32),
                        pltpu.VMEM((n_chunks, HG_HEADS // 2, HG_CHUNK, HG_PAIR), F32)]
        + (list(EXCHANGE_SCRATCH) if ex else []),
        compiler_params=_cp("arbitrary"))(proj, proj, proj, proj, lb, gn, o_hg, sall, dog, *ex_in)
    return outs[:3], outs[3:]


def _rope_tables(t):
    half = ROPE_DIM // 2
    inv = ROPE_THETA ** (-jnp.arange(half, dtype=F32) * 2.0 / ROPE_DIM)
    ang = jnp.arange(t).astype(F32)[:, None] * inv[None, :]
    cos, sin = jnp.cos(ang), jnp.sin(ang)
    pad = ATT_HEAD_DIM - ROPE_DIM
    c = jnp.concatenate([cos, cos, jnp.ones((t, pad), F32)], axis=1)
    su = jnp.concatenate([-sin, jnp.zeros((t, half + pad), F32)], axis=1)
    sd = jnp.concatenate([jnp.zeros((t, half), F32), sin, jnp.zeros((t, pad), F32)], axis=1)
    return tuple(jnp.concatenate([m, m], axis=1) for m in (c, su, sd))


def _rope(x, tabs):
    c, su, sd = tabs
    n = x.shape[1]
    half = ROPE_DIM // 2
    return x * c + pltpu.roll(x, n - half, 1) * su + pltpu.roll(x, half, 1) * sd


def _rope_t(dy, tabs):
    c, su, sd = tabs
    n = dy.shape[1]
    half = ROPE_DIM // 2
    return dy * c + pltpu.roll(dy * su, half, 1) + pltpu.roll(dy * sd, n - half, 1)


def _swa_specs(n_blocks, clamp):
    blk = ATT_BLOCK

    def cur(n):
        return jnp.minimum(n, n_blocks - 1) if clamp else n

    def prev(n):
        return jnp.maximum(cur(n) - 1, 0)

    q_spec = pl.BlockSpec((blk, 512), lambda m, n: (cur(n), COL_AQ // 512 + m))
    kv = [pl.BlockSpec((blk, 128), lambda m, n, c=c, f=f: (f(n), c + m))
          for c in (COL_AK // 128, COL_AV // 128) for f in (cur, prev)]
    tabs = [pl.BlockSpec((blk, 128), lambda m, n, f=f: (f(n), 0)) for f in (cur, prev) for _ in range(3)]
    return q_spec, kv, tabs, cur, prev


ATT_SCALE = ATT_HEAD_DIM ** -0.5


def _swa_scores(qm, kd, sink, mask):
    s = jnp.where(mask, _dot(qm, kd, NT), -jnp.inf)
    mx = jnp.maximum(jnp.max(s, axis=-1, keepdims=True), sink)
    p = jnp.exp(s - mx)
    es = jnp.exp(sink - mx)
    rinv = 1.0 / (jnp.sum(p, axis=-1, keepdims=True) + es)
    return p * rinv, es * rinv


def _swa_window(kc_ref, kp_ref, vc_ref, vp_ref, tabs_c, tabs_p, n):
    k2 = jnp.concatenate([_rope(kp_ref[...], tabs_p), _rope(kc_ref[...], tabs_c)], axis=0)
    v2 = jnp.concatenate([vp_ref[...], vc_ref[...]], axis=0)
    blk = ATT_BLOCK
    qi = lax.broadcasted_iota(jnp.int32, (blk, 2 * blk), 0)
    kj = lax.broadcasted_iota(jnp.int32, (blk, 2 * blk), 1)
    delta = qi + blk - kj
    mask = (delta >= 0) & (delta < blk) & ((kj >= blk) | (n > 0))
    return k2, v2, mask


def _swa_fwd(proj, sinks, tabs, name):
    t = proj.shape[0]
    n_blocks = t // ATT_BLOCK
    q_spec, kv_specs, tab_specs, _, _ = _swa_specs(n_blocks, clamp=False)

    def body(q_ref, kc_ref, kp_ref, vc_ref, vp_ref, c0, c1, c2, p0, p1, p2, sink_ref, o_ref):
        m, n = pl.program_id(0), pl.program_id(1)
        tabs_c = (c0[...], c1[...], c2[...])
        tabs_p = (p0[...], p1[...], p2[...])
        k2, v2, mask = _swa_window(kc_ref, kp_ref, vc_ref, vp_ref, tabs_c, tabs_p, n)
        k2r, v2r = pltpu.roll(k2, 64, 1), pltpu.roll(v2, 64, 1)
        upper_k = lax.broadcasted_iota(jnp.int32, k2.shape, 1) >= 64
        upper_q = lax.broadcasted_iota(jnp.int32, (ATT_BLOCK, 128), 1) >= 64
        for jj in range(2):
            own = upper_k if jj else ~upper_k
            kd = jnp.where(own, k2, k2r).astype(BF16)
            vd = jnp.where(own, v2, v2r).astype(BF16)
            for pi in range(2):
                cols = slice(256 * jj + 128 * pi, 256 * jj + 128 * pi + 128)
                qp = _rope(q_ref[:, cols], tabs_c) * ATT_SCALE
                outs = []
                for e in range(2):
                    sink = sink_ref[0, 8 * m + 4 * jj + 2 * pi + e]
                    qm = jnp.where(upper_q if e else ~upper_q, qp, 0.0).astype(BF16)
                    pn, _ = _swa_scores(qm, kd, sink, mask)
                    outs.append(_dot(pn.astype(BF16), vd, NN))
                o_ref[:, cols] = jnp.where(upper_q, outs[1], outs[0]).astype(BF16)

    return pl.pallas_call(
        body, name=name, grid=(2, n_blocks),
        in_specs=[q_spec] + kv_specs + tab_specs + [pl.BlockSpec(memory_space=pltpu.SMEM)],
        out_specs=pl.BlockSpec((ATT_BLOCK, 512), lambda m, n: (n, m)),
        out_shape=jax.ShapeDtypeStruct((t, ATT_Q_HEADS * ATT_HEAD_DIM), BF16),
        compiler_params=_cp("parallel", "arbitrary"))(proj, proj, proj, proj, proj, *tabs, *tabs, sinks)


def _swa_bwd(proj, sinks, tabs, o_att, do_att, name):
    t = proj.shape[0]
    n_blocks = t // ATT_BLOCK
    blk = ATT_BLOCK
    q_spec, kv_specs, tab_specs, cur, prev = _swa_specs(n_blocks, clamp=True)

    def body(q_ref, kc_ref, kp_ref, vc_ref, vp_ref, c0, c1, c2, p0, p1, p2, sink_ref, o_ref, do_ref,
             dq_ref, dk_ref, dv_ref, ds_ref, ck_ref, cv_ref):
        m, n = pl.program_id(0), pl.program_id(1)

        @pl.when(n == 0)
        def _():
            ds_ref[...] = jnp.zeros_like(ds_ref)
            ck_ref[...] = jnp.zeros_like(ck_ref)
            cv_ref[...] = jnp.zeros_like(cv_ref)

        @pl.when(n < n_blocks)
        def _():
            tabs_c = (c0[...], c1[...], c2[...])
            tabs_p = (p0[...], p1[...], p2[...])
            k2, v2, mask = _swa_window(kc_ref, kp_ref, vc_ref, vp_ref, tabs_c, tabs_p, n)
            k2r, v2r = pltpu.roll(k2, 64, 1), pltpu.roll(v2, 64, 1)
            upper_k = lax.broadcasted_iota(jnp.int32, k2.shape, 1) >= 64
            upper_q = lax.broadcasted_iota(jnp.int32, (blk, 128), 1) >= 64
            lane = lax.broadcasted_iota(jnp.int32, (8, 128), 1)
            dk2 = jnp.zeros(k2.shape, F32)
            dv2 = jnp.zeros(k2.shape, F32)
            dsv = jnp.zeros((8, 128), F32)
            for jj in range(2):
                own = upper_k if jj else ~upper_k
                kd = jnp.where(own, k2, k2r).astype(BF16)
                vd = jnp.where(own, v2, v2r).astype(BF16)
                dkd = jnp.zeros(k2.shape, F32)
                dvd = jnp.zeros(k2.shape, F32)
                for pi in range(2):
                    cols = slice(256 * jj + 128 * pi, 256 * jj + 128 * pi + 128)
                    qp = _rope(q_ref[:, cols], tabs_c) * ATT_SCALE
                    do_pair = do_ref[:, cols]
                    o_pair = o_ref[:, cols].astype(F32)
                    dqs = []
                    for e in range(2):
                        hl = 4 * jj + 2 * pi + e
                        sink = sink_ref[0, 8 * m + hl]
                        half = upper_q if e else ~upper_q
                        qm = jnp.where(half, qp, 0.0).astype(BF16)
                        pn, ps = _swa_scores(qm, kd, sink, mask)
                        dom = jnp.where(half, do_pair, 0.0)
                        delta = jnp.sum(dom * o_pair, axis=-1, keepdims=True)
                        domb = dom.astype(BF16)
                        dsb = (pn * (_dot(domb, vd, NT) - delta)).astype(BF16)
                        dqs.append(_dot(dsb, kd, NN) * ATT_SCALE)
                        dkd = dkd + _dot(dsb, qm, TN)
                        dvd = dvd + _dot(pn.astype(BF16), domb, TN)
                        dsv = dsv + jnp.where(lane == hl, -jnp.sum(ps * delta), 0.0)
                    dq_ref[:, cols] = _rope_t(jnp.where(upper_q, dqs[1], dqs[0]), tabs_c).astype(BF16)
                dk2 = dk2 + jnp.where(own, dkd + pltpu.roll(dkd, 64, 1), 0.0)
                dv2 = dv2 + jnp.where(own, dvd + pltpu.roll(dvd, 64, 1), 0.0)
            dk_ref[...] = (ck_ref[...] + _rope_t(dk2[:blk], tabs_p)).astype(BF16)
            dv_ref[...] = (cv_ref[...] + dv2[:blk]).astype(BF16)
            ck_ref[...] = _rope_t(dk2[blk:], tabs_c)
            cv_ref[...] = dv2[blk:]
            ds_ref[...] += dsv

        @pl.when(n == n_blocks)
        def _():
            dk_ref[...] = ck_ref[...].astype(BF16)
            dv_ref[...] = cv_ref[...].astype(BF16)

    wide = pl.BlockSpec((blk, 512), lambda m, n: (cur(n), m))
    lagged = pl.BlockSpec((blk, 128), lambda m, n: (jnp.maximum(n - 1, 0), m))
    return pl.pallas_call(
        body, name=name, grid=(2, n_blocks + 1),
        in_specs=[q_spec] + kv_specs + tab_specs + [pl.BlockSpec(memory_space=pltpu.SMEM), wide, wide],
        out_specs=[wide, lagged, lagged, pl.BlockSpec((None, 8, 128), lambda m, n: (m, 0, 0))],
        out_shape=[jax.ShapeDtypeStruct((t, 1024), BF16), jax.ShapeDtypeStruct((t, 256), BF16),
                   jax.ShapeDtypeStruct((t, 256), BF16), jax.ShapeDtypeStruct((2, 8, 128), F32)],
        scratch_shapes=[pltpu.VMEM((blk, 128), F32), pltpu.VMEM((blk, 128), F32)],
        compiler_params=_cp("arbitrary", "arbitrary"))(proj, proj, proj, proj, proj, *tabs, *tabs, sinks, o_att, do_att)


def _local_step(x, target, shards, norm1, lb_logits, hg_norm, attn_sinks, norm2, final_norm):
    t = x.shape[0]
    tabs = _rope_tables(t)
    lb_all = _lb_fwd(lb_logits)
    saved = []
    weights = _run_exchange(_gather_exchange([s[0] for s in shards]), "gather_weights")
    for l in range(DEPTH):
        win_t, w_pa, w_pb, w_o, wgu_t, w_d = weights
        n1, n2 = norm1[l][None, :], norm2[l][None, :]
        lb, gn, sinks = lb_all[l][None, :], hg_norm[l][None, :], attn_sinks[l][None, :]
        h = _rms_fwd(x, n1, "rms1_fwd")
        proj = _matmul_nt(h, win_t, 0, IN_COLS, F32, "proj_fwd", tn=1280)
        ex = _gather_exchange([s[l + 1] for s in shards]) if l + 1 < DEPTH else None
        (o_hg, o_g, sall), next_weights = _hgrn2_fwd(proj, lb, gn, "hgrn2_fwd", ex)
        o_att = _swa_fwd(proj, sinks, tabs, "swa_fwd")
        ya, yb, mix, x1 = _merge_fwd(o_g, o_att, proj, x, w_pa, w_pb, w_o, "merge_fwd")
        h2 = _rms_fwd(x1, n2, "rms2_fwd")
        gu, act = _ffn_up_fwd(h2, wgu_t, "ffn_up_fwd")
        x2 = _matmul_nn(act, w_d, 0, x1, "wd_fwd")
        saved.append((x, h, proj, o_hg, o_g, sall, o_att, ya, yb, mix, x1, h2, gu, act, n1, n2, lb, gn, sinks, weights))
        x, weights = x2, next_weights

    dx, d_fn, loss = _loss_head(x, final_norm[None, :], target, "loss_head")

    owned = [None] * DEPTH
    pending = None
    d_n1, d_n2, d_lb, d_gn, d_sinks = ([None] * DEPTH for _ in range(5))
    for l in reversed(range(DEPTH)):
        x0, h, proj, o_hg, o_g, sall, o_att, ya, yb, mix, x1, h2, gu, act, n1, n2, lb, gn, sinks, weights = saved[l]
        win_t, w_pa, w_pb, w_o, wgu_t, w_d = weights
        dgu = _ffn_down_bwd(dx, w_d, gu, "ffn_down_bwd")
        g_wd = _matmul_tn(act, dx, "wd_grad", tm=1408)
        g_wgu = _matmul_tn(dgu, h2, "wgu_grad", tm=1408)
        dx1, d_n2[l] = _rows_bwd([dgu], wgu_t, x1, n2, dx, "ffn_up_bwd")
        dmix = _matmul_nt(dx1, w_o, 0, D_MODEL, F32, "wo_bwd", tn=1024)
        g_wo = _matmul_tn(mix, dx1, "wo_grad")
        dya, dyb, dgab = _mix_bwd(ya, yb, proj, dmix, "mix_bwd")
        g_wpa = _matmul_tn(o_g, dya, "wpa_grad")
        g_wpb = _matmul_tn(o_att, dyb, "wpb_grad")
        dog = _matmul_nt(dya, w_pa, 0, D_MODEL, F32, "wpa_bwd", tn=1024)
        doatt = _matmul_nt(dyb, w_pb, 0, D_MODEL, F32, "wpb_bwd", tn=1024)
        ex = _scatter_exchange(pending) if pending is not None else None
        (dhg, d_lb[l], d_gn[l]), land = _hgrn2_bwd(proj, lb, gn, o_hg, sall, dog, "hgrn2_bwd", ex)
        if pending is not None:
            owned[l + 1] = _sum_slots(land[0], "sum_slots")
        daq, dak, dav, d_sinks[l] = _swa_bwd(proj, sinks, tabs, o_att, doatt, "swa_bwd")
        dakv = jnp.concatenate([dak, dav], axis=1)
        g_win = jnp.concatenate([_matmul_tn(dhg, h, "win_grad_hg"), _matmul_tn(daq, h, "win_grad_aq"),
                                 _matmul_tn(dakv, h, "win_grad_akv"), _matmul_tn(dgab, h, "win_grad_gates")], axis=0)
        dx, d_n1[l] = _rows_bwd([dhg, daq, dakv, dgab], win_t, x0, n1, dx1, "win_bwd")
        pending = (g_win, g_wpa, g_wpb, g_wo, g_wgu, g_wd)
    owned[0] = _sum_slots(_run_exchange(_scatter_exchange(pending), "scatter_grads")[0], "sum_slots")

    d_sink_rows = [jnp.concatenate([d[0, 0, :8], d[1, 0, :8]]) for d in d_sinks]
    small = (jnp.concatenate(d_n1, axis=0), jnp.concatenate(d_lb, axis=0), jnp.concatenate(d_gn, axis=0),
             jnp.concatenate(d_n2, axis=0), d_fn, jnp.stack(d_sink_rows, axis=0))
    return loss, dx, jnp.stack(owned, axis=0), small


def _sum_slots(land, name, tr=480):
    _, rows, d = land.shape

    def body(l_ref, o_ref):
        acc = l_ref[0].astype(F32)
        for k in range(1, N_DEV):
            acc = acc + l_ref[k].astype(F32)
        o_ref[...] = acc

    return pl.pallas_call(
        body, name=name, grid=(rows // tr,),
        in_specs=[pl.BlockSpec((N_DEV, tr, d), lambda i: (0, i, 0))],
        out_specs=pl.BlockSpec((tr, d), lambda i: (i, 0)),
        out_shape=jax.ShapeDtypeStruct((rows, d), F32),
        compiler_params=_cp("parallel"))(land)


def _adamw(w, g, m, v, name):
    shape = w.shape
    c = shape[-1]
    rows = w.size // c
    tr = rows
    for cand in (512, 352, 128):
        if rows % cand == 0:
            tr = cand
            break
    c1 = 1.0 / (1.0 - ADAM_B1 ** ADAM_STEP)
    c2 = 1.0 / (1.0 - ADAM_B2 ** ADAM_STEP)

    def body(w_ref, g_ref, m_ref, v_ref, d_ref, nm_ref, nv_ref):
        gv = g_ref[...]
        nm = ADAM_B1 * m_ref[...] + (1.0 - ADAM_B1) * gv
        nv = ADAM_B2 * v_ref[...] + (1.0 - ADAM_B2) * (gv * gv)
        d_ref[...] = -ADAM_LR * ((nm * c1) / (jnp.sqrt(nv * c2) + ADAM_EPS) + ADAM_WD * w_ref[...])
        nm_ref[...] = nm
        nv_ref[...] = nv

    spec = pl.BlockSpec((tr, c), lambda i: (i, 0))
    outs = pl.pallas_call(
        body, name=name, grid=(rows // tr,), in_specs=[spec] * 4, out_specs=[spec] * 3,
        out_shape=[jax.ShapeDtypeStruct((rows, c), F32)] * 3,
        compiler_params=_cp("parallel"))(*[a.reshape(rows, c) for a in (w, g, m, v)])
    return tuple(o.reshape(shape) for o in outs)


def kernel(x, norm1, w_in, lb_logits, hg_norm, attn_sinks, w_pa, w_pb, w_o, norm2, w_gate, w_up, w_down, final_norm, loss_target, m_norm1, m_w_in, m_lb_logits, m_hg_norm, m_attn_sinks, m_w_pa, m_w_pb, m_w_o, m_norm2, m_w_gate, m_w_up, m_w_down, m_final_norm, v_norm1, v_w_in, v_lb_logits, v_hg_norm, v_attn_sinks, v_w_pa, v_w_pb, v_w_o, v_norm2, v_w_gate, v_w_up, v_w_down, v_final_norm):
    t = x.shape[1]
    shards = [jnp.swapaxes(w_in, 1, 2).astype(BF16), w_pa.astype(BF16), w_pb.astype(BF16), w_o.astype(BF16),
              jnp.swapaxes(w_gate, 1, 2).astype(BF16), jnp.swapaxes(w_up, 1, 2).astype(BF16), w_down.astype(BF16)]
    loss_lanes, grad_x, owned, small = _local_step(
        x.reshape(t, D_MODEL), loss_target.reshape(t, D_MODEL), shards,
        norm1, lb_logits, hg_norm, attn_sinks, norm2, final_norm)

    def rows_of(ti, transpose):
        g = owned[:, SLOT_OFF[ti]:SLOT_OFF[ti] + SHARD_ROWS[ti], :]
        return jnp.swapaxes(g, 1, 2) if transpose else g

    g_big = {"w_in": rows_of(0, True), "w_pa": rows_of(1, False), "w_pb": rows_of(2, False), "w_o": rows_of(3, False),
             "w_gate": rows_of(4, True), "w_up": rows_of(5, True), "w_down": rows_of(6, False)}

    d_n1, d_lb, d_gn, d_n2, d_fn, d_sinks = small
    pad = jnp.zeros((DEPTH, D_MODEL - ATT_Q_HEADS), F32)
    packed = jnp.concatenate([
        d_n1, d_lb, d_gn, d_n2, d_fn, jnp.concatenate([d_sinks, pad], axis=1),
        jnp.concatenate([loss_lanes, jnp.zeros((1, D_MODEL - 128), F32)], axis=1),
        jnp.zeros((SMALL_ROWS - 22, D_MODEL), F32)], axis=0)
    total = _all_reduce_small(packed)
    loss = total[21, 0]
    g_small = {"norm1": total[0:4], "lb_logits": _lb_bwd(lb_logits, total[4:8]), "hg_norm": total[8:12],
               "norm2": total[12:16], "final_norm": total[16], "attn_sinks": total[17:21, :ATT_Q_HEADS]}

    params = {"norm1": (norm1, m_norm1, v_norm1), "w_in": (w_in, m_w_in, v_w_in),
              "lb_logits": (lb_logits, m_lb_logits, v_lb_logits), "hg_norm": (hg_norm, m_hg_norm, v_hg_norm),
              "attn_sinks": (attn_sinks, m_attn_sinks, v_attn_sinks), "w_pa": (w_pa, m_w_pa, v_w_pa),
              "w_pb": (w_pb, m_w_pb, v_w_pb), "w_o": (w_o, m_w_o, v_w_o), "norm2": (norm2, m_norm2, v_norm2),
              "w_gate": (w_gate, m_w_gate, v_w_gate), "w_up": (w_up, m_w_up, v_w_up),
              "w_down": (w_down, m_w_down, v_w_down), "final_norm": (final_norm, m_final_norm, v_final_norm)}
    order = ["norm1", "w_in", "lb_logits", "hg_norm", "attn_sinks", "w_pa", "w_pb", "w_o", "norm2",
             "w_gate", "w_up", "w_down", "final_norm"]
    grads, deltas, new_m, new_v = [], [], [], []
    for name in order:
        w, m, v = params[name]
        g = (g_big[name] if name in g_big else g_small[name]).reshape(w.shape)
        w2 = w.reshape(1, -1) if w.ndim == 1 else w
        d, nm, nv = _adamw(w2, g.reshape(w2.shape), m.reshape(w2.shape), v.reshape(w2.shape), "adamw_" + name)
        grads.append(g)
        deltas.append(d.reshape(w.shape))
        new_m.append(nm.reshape(w.shape))
        new_v.append(nv.reshape(w.shape))
    return (loss, grad_x.reshape(x.shape), *grads, *deltas, *new_m, *new_v)
```

```python
import functools
from typing import Callable, NamedTuple

import jax
import jax.numpy as jnp
from jax import lax
from jax.experimental import pallas as pl
from jax.experimental.pallas import tpu as pltpu

F32, BF16 = jnp.float32, jnp.bfloat16

D_MODEL = 1024
DEPTH = 4
N_DEV = 8
HG_HEADS = 8
HG_DK = 128
HG_CHUNK = 64
HG_BLOCK = 256
HG_EXP_CLAMP = 60.0
ATT_Q_HEADS = 16
ATT_HEAD_DIM = 64
ATT_BLOCK = 128
ROPE_THETA = 500000.0
ROPE_DIM = 16
FFN_HIDDEN = 2816
EPS = 1e-6
MIN_F = 1e-30
ADAM_LR, ADAM_B1, ADAM_B2, ADAM_EPS, ADAM_WD, ADAM_STEP = 0.001, 0.9, 0.999, 1e-08, 0.01, 10

COL_HQ, COL_HF, COL_HI, COL_HG = 0, 1024, 2048, 3072
COL_AQ, COL_AK, COL_AV, COL_GA, COL_GB = 4096, 5120, 5376, 5632, 6656
IN_COLS = 7680

SHARD_ROWS = (960, 128, 128, 128, 352, 352, 352)
SLOT_OFF = (0, 960, 1088, 1216, 1344, 1696, 2048)
SLOT_ROWS = 2400
SMALL_ROWS = 24

VMEM_LIMIT_BYTES = 56 * 1024 * 1024

NN = ((1,), (0,))
NT = ((1,), (1,))
TN = ((0,), (0,))


def _dot(a, b, dims):
    return lax.dot_general(a, b, (dims, ((), ())), preferred_element_type=F32)


def _cp(*sem):
    return pltpu.CompilerParams(dimension_semantics=sem if sem else None, vmem_limit_bytes=VMEM_LIMIT_BYTES)


def _sigmoid(x):
    return 1.0 / (1.0 + jnp.exp(-x))


def _matmul_nt(a, w, row_off, n, out_dtype, name, tm=1024, tn=512, ex=None):
    t, k = a.shape
    tm = min(tm, t)
    assert n % tn == 0 and row_off % tn == 0 and t % tm == 0
    grid = (n // tn, t // tm)

    def body(a_ref, w_ref, o_ref):
        o_ref[...] = _dot(a_ref[...].astype(BF16), w_ref[...], NT).astype(o_ref.dtype)

    ex_in, ex_in_specs, ex_out, ex_out_specs, ex_scratch = _carried(ex)
    outs = pl.pallas_call(
        _carry(body, ex, 2, 1, 0, grid), name=name, grid=grid,
        in_specs=[pl.BlockSpec((tm, k), lambda j, i: (i, 0)),
                  pl.BlockSpec((tn, k), lambda j, i: (row_off // tn + j, 0))] + ex_in_specs,
        out_specs=[pl.BlockSpec((tm, tn), lambda j, i: (i, j))] + ex_out_specs,
        out_shape=[jax.ShapeDtypeStruct((t, n), out_dtype)] + ex_out,
        scratch_shapes=ex_scratch,
        compiler_params=_cp("arbitrary", "arbitrary") if ex else _cp("parallel", "parallel"))(a, w, *ex_in)
    return (outs[0], outs[1:]) if ex else outs[0]


def _matmul_nn(a, w, row_off, res, name, tm=512, tk=None):
    t, k = a.shape
    n = w.shape[1]
    tm = min(tm, t)
    tk = tk or k
    nk = k // tk
    assert k % tk == 0 and row_off % tk == 0 and t % tm == 0

    def body(*refs):
        if res is None:
            a_ref, w_ref, o_ref, acc = refs
        else:
            a_ref, w_ref, r_ref, o_ref, acc = refs
        kk = pl.program_id(1)
        part = _dot(a_ref[...].astype(BF16), w_ref[...], NN)

        @pl.when(kk == 0)
        def _():
            acc[...] = part

        @pl.when(kk > 0)
        def _():
            acc[...] += part

        @pl.when(kk == nk - 1)
        def _():
            o_ref[...] = acc[...] if res is None else acc[...] + r_ref[...]

    in_specs = [pl.BlockSpec((tm, tk), lambda i, kk: (i, kk)),
                pl.BlockSpec((tk, n), lambda i, kk: (row_off // tk + kk, 0))]
    args = [a, w]
    if res is not None:
        in_specs.append(pl.BlockSpec((tm, n), lambda i, kk: (i, 0)))
        args.append(res)
    return pl.pallas_call(
        body, name=name, grid=(t // tm, nk), in_specs=in_specs,
        out_specs=pl.BlockSpec((tm, n), lambda i, kk: (i, 0)),
        out_shape=jax.ShapeDtypeStruct((t, n), F32),
        scratch_shapes=[pltpu.VMEM((tm, n), F32)],
        compiler_params=_cp("parallel", "arbitrary"))(*args)


def _matmul_tn(a, b, name, tm=512, tk=2048):
    t, m = a.shape
    n = b.shape[1]
    tk = min(tk, t)
    nk = t // tk
    assert m % tm == 0 and t % tk == 0

    def body(a_ref, b_ref, o_ref, acc):
        kk = pl.program_id(1)
        part = _dot(a_ref[...].astype(BF16), b_ref[...].astype(BF16), TN)

        @pl.when(kk == 0)
        def _():
            acc[...] = part

        @pl.when(kk > 0)
        def _():
            acc[...] += part

        @pl.when(kk == nk - 1)
        def _():
            o_ref[...] = acc[...].astype(BF16)

    return pl.pallas_call(
        body, name=name, grid=(m // tm, nk),
        in_specs=[pl.BlockSpec((tk, tm), lambda i, kk: (kk, i)),
                  pl.BlockSpec((tk, n), lambda i, kk: (kk, 0))],
        out_specs=pl.BlockSpec((tm, n), lambda i, kk: (i, 0)),
        out_shape=jax.ShapeDtypeStruct((m, n), BF16),
        scratch_shapes=[pltpu.VMEM((tm, n), F32)],
        compiler_params=_cp("parallel", "arbitrary"))(a, b)


def _rms(x, g):
    return x * lax.rsqrt(jnp.mean(x * x, axis=-1, keepdims=True) + EPS) * g


def _rms_fwd(x, g, name, tm=512):
    t, d = x.shape
    tm = min(tm, t)

    def body(x_ref, g_ref, o_ref):
        o_ref[...] = _rms(x_ref[...], g_ref[...]).astype(BF16)

    return pl.pallas_call(
        body, name=name, grid=(t // tm,),
        in_specs=[pl.BlockSpec((tm, d), lambda i: (i, 0)), pl.BlockSpec((1, d), lambda i: (0, 0))],
        out_specs=pl.BlockSpec((tm, d), lambda i: (i, 0)),
        out_shape=jax.ShapeDtypeStruct((t, d), BF16),
        compiler_params=_cp("parallel"))(x, g)


def _mix(ya, yb, ga, gb):
    return _sigmoid(ga) * ya + _sigmoid(gb) * yb


def _gate_specs(tm):
    half = D_MODEL // 2
    return [pl.BlockSpec((tm, half), lambda i, c=c: (i, c))
            for c in (COL_GA // half, COL_GA // half + 1, COL_GB // half, COL_GB // half + 1)]


def _merge_bwd(dx1, ya, yb, proj, w_pa, w_pb, w_o, name, tm=256):
    t, d = dx1.shape
    tm = min(tm, t)

    def body(dx_ref, ya_ref, yb_ref, ga0, ga1, gb0, gb1, wpa_ref, wpb_ref, wo_ref,
             dya_ref, dyb_ref, dg_ref, dog_ref, doa_ref):
        dmix = _dot(dx_ref[...].astype(BF16), wo_ref[...], NT)
        ga = jnp.concatenate([ga0[...], ga1[...]], axis=1)
        gb = jnp.concatenate([gb0[...], gb1[...]], axis=1)
        _, vjp = jax.vjp(_mix, ya_ref[...].astype(F32), yb_ref[...].astype(F32), ga, gb)
        dya, dyb, dga, dgb = vjp(dmix)
        dya, dyb = dya.astype(BF16), dyb.astype(BF16)
        dya_ref[...] = dya
        dyb_ref[...] = dyb
        dg_ref[:, :d] = dga.astype(BF16)
        dg_ref[:, d:] = dgb.astype(BF16)
        dog_ref[...] = _dot(dya, wpa_ref[...], NT)
        doa_ref[...] = _dot(dyb, wpb_ref[...], NT)

    row = pl.BlockSpec((tm, d), lambda i: (i, 0))
    wide = pl.BlockSpec((tm, 2 * d), lambda i: (i, 0))
    mat = pl.BlockSpec((d, d), lambda i: (0, 0))
    return pl.pallas_call(
        body, name=name, grid=(t // tm,), in_specs=[row, row, row] + _gate_specs(tm) + [mat, mat, mat],
        out_specs=[row, row, wide, row, row],
        out_shape=[jax.ShapeDtypeStruct((t, d), BF16), jax.ShapeDtypeStruct((t, d), BF16),
                   jax.ShapeDtypeStruct((t, 2 * d), BF16), jax.ShapeDtypeStruct((t, d), F32),
                   jax.ShapeDtypeStruct((t, d), F32)],
        compiler_params=_cp("parallel"))(dx1, ya, yb, proj, proj, proj, proj, w_pa, w_pb, w_o)


def _swiglu(g, u):
    return g * _sigmoid(g) * u


def _ffn_up_fwd(h2, wgu_t, name, tm=512):
    t, d = h2.shape
    tm = min(tm, t)
    fh = FFN_HIDDEN // 2

    def body(a_ref, w_ref, gu_ref, act_ref):
        r = _dot(a_ref[...], w_ref[...], NT)
        gu_ref[...] = r.astype(BF16)
        act_ref[...] = _swiglu(r[:, :fh], r[:, fh:]).astype(BF16)

    return pl.pallas_call(
        body, name=name, grid=(2, t // tm),
        in_specs=[pl.BlockSpec((tm, d), lambda j, i: (i, 0)), pl.BlockSpec((2 * fh, d), lambda j, i: (j, 0))],
        out_specs=[pl.BlockSpec((tm, 2 * fh), lambda j, i: (i, j)), pl.BlockSpec((tm, fh), lambda j, i: (i, j))],
        out_shape=[jax.ShapeDtypeStruct((t, 4 * fh), BF16), jax.ShapeDtypeStruct((t, 2 * fh), BF16)],
        compiler_params=_cp("parallel", "parallel"))(h2, wgu_t)


def _ffn_down_bwd(dx, w_d, gu, name, tm=512):
    t, d = dx.shape
    tm = min(tm, t)
    fh = FFN_HIDDEN // 2

    def body(a_ref, w_ref, gu_ref, o_ref):
        dact = _dot(a_ref[...].astype(BF16), w_ref[...], NT)
        _, vjp = jax.vjp(_swiglu, gu_ref[:, :fh].astype(F32), gu_ref[:, fh:].astype(F32))
        dg, du = vjp(dact)
        o_ref[:, :fh] = dg.astype(BF16)
        o_ref[:, fh:] = du.astype(BF16)

    wide = pl.BlockSpec((tm, 2 * fh), lambda j, i: (i, j))
    return pl.pallas_call(
        body, name=name, grid=(2, t // tm),
        in_specs=[pl.BlockSpec((tm, d), lambda j, i: (i, 0)), pl.BlockSpec((fh, d), lambda j, i: (j, 0)), wide],
        out_specs=wide,
        out_shape=jax.ShapeDtypeStruct((t, 4 * fh), BF16),
        compiler_params=_cp("parallel", "parallel"))(dx, w_d, gu)


def _rows_bwd(pieces, w, x, g, dres, name, tm=256):
    t, d = x.shape
    tm = min(tm, t)
    widths = [p.shape[1] for p in pieces]
    starts = [sum(widths[:i]) for i in range(len(widths))]
    assert sum(widths) == w.shape[0]
    n_p = len(pieces)

    def body(*refs):
        p_refs, (w_ref, x_ref, g_ref, dres_ref, dx_ref, dg_ref) = refs[:n_p], refs[n_p:]
        dh = _dot(p_refs[0][...], w_ref[pl.ds(starts[0], widths[0]), :], NN)
        for i in range(1, n_p):
            dh = dh + _dot(p_refs[i][...], w_ref[pl.ds(starts[i], widths[i]), :], NN)
        _, vjp = jax.vjp(_rms, x_ref[...], g_ref[...])
        dx, dg = vjp(dh)
        dx_ref[...] = dres_ref[...] + dx

        @pl.when(pl.program_id(0) == 0)
        def _():
            dg_ref[...] = jnp.zeros_like(dg_ref)

        dg_ref[...] += dg

    row = pl.BlockSpec((tm, d), lambda i: (i, 0))
    vec = pl.BlockSpec((1, d), lambda i: (0, 0))
    return pl.pallas_call(
        body, name=name, grid=(t // tm,),
        in_specs=[pl.BlockSpec((tm, k), lambda i: (i, 0)) for k in widths]
        + [pl.BlockSpec(w.shape, lambda i: (0, 0)), row, vec, row],
        out_specs=[row, vec],
        out_shape=[jax.ShapeDtypeStruct((t, d), F32), jax.ShapeDtypeStruct((1, d), F32)],
        compiler_params=_cp("arbitrary"))(*pieces, w, x, g, dres)


def _merge_fwd(o_g, o_att, proj, x, w_pa, w_pb, w_o, name, tm=256):
    t, d = x.shape
    tm = min(tm, t)

    def body(og_ref, oa_ref, ga0, ga1, gb0, gb1, x_ref, wpa_ref, wpb_ref, wo_ref, ya_ref, yb_ref, mix_ref, x1_ref):
        ya = _dot(og_ref[...], wpa_ref[...], NN)
        yb = _dot(oa_ref[...], wpb_ref[...], NN)
        ga = jnp.concatenate([ga0[...], ga1[...]], axis=1)
        gb = jnp.concatenate([gb0[...], gb1[...]], axis=1)
        mix = _mix(ya, yb, ga, gb).astype(BF16)
        ya_ref[...] = ya.astype(BF16)
        yb_ref[...] = yb.astype(BF16)
        mix_ref[...] = mix
        x1_ref[...] = x_ref[...] + _dot(mix, wo_ref[...], NN)

    row = pl.BlockSpec((tm, d), lambda i: (i, 0))
    mat = pl.BlockSpec((d, d), lambda i: (0, 0))
    return pl.pallas_call(
        body, name=name, grid=(t // tm,),
        in_specs=[row, row] + _gate_specs(tm) + [row, mat, mat, mat], out_specs=[row, row, row, row],
        out_shape=[jax.ShapeDtypeStruct((t, d), BF16)] * 3 + [jax.ShapeDtypeStruct((t, d), F32)],
        compiler_params=_cp("parallel"))(o_g, o_att, proj, proj, proj, proj, x, w_pa, w_pb, w_o)


def _loss_head(x, g, target, name, tm=512):
    t, d = x.shape
    tm = min(tm, t)

    def body(x_ref, g_ref, t_ref, dx_ref, dg_ref, loss_ref):
        tgt = t_ref[...]

        def f(xv, gv):
            err = _rms(xv, gv) - tgt
            return 0.5 * jnp.sum(jnp.mean(err * err, axis=-1, keepdims=True))

        loss, vjp = jax.vjp(f, x_ref[...], g_ref[...])
        dx, dg = vjp(jnp.ones((), F32))
        dx_ref[...] = dx

        @pl.when(pl.program_id(0) == 0)
        def _():
            dg_ref[...] = jnp.zeros_like(dg_ref)
            loss_ref[...] = jnp.zeros_like(loss_ref)

        dg_ref[...] += dg
        loss_ref[...] += jnp.full(loss_ref.shape, loss, F32)

    row = pl.BlockSpec((tm, d), lambda i: (i, 0))
    vec = pl.BlockSpec((1, d), lambda i: (0, 0))
    lane = pl.BlockSpec((1, 128), lambda i: (0, 0))
    return pl.pallas_call(
        body, name=name, grid=(t // tm,), in_specs=[row, vec, row], out_specs=[row, vec, lane],
        out_shape=[jax.ShapeDtypeStruct((t, d), F32), jax.ShapeDtypeStruct((1, d), F32),
                   jax.ShapeDtypeStruct((1, 128), F32)],
        compiler_params=_cp("arbitrary"))(x, g, target)


def _lb_rows(l0, l1, l2, l3):
    mx = jnp.maximum(jnp.maximum(l0, l1), jnp.maximum(l2, l3))
    e0, e1, e2, e3 = jnp.exp(l0 - mx), jnp.exp(l1 - mx), jnp.exp(l2 - mx), jnp.exp(l3 - mx)
    s = e0 + e1 + e2 + e3
    p0, p1, p2, p3 = e0 / s, e1 / s, e2 / s, e3 / s
    c1 = p0 + p1
    c2 = c1 + p2
    c3 = c2 + p3
    return p0 - p0, c1 - p0, c2 - p0, c3 - p0


def _lb_fwd(lb_logits):
    def body(l_ref, o_ref):
        rows = _lb_rows(*[l_ref[pl.ds(i, 1), :] for i in range(DEPTH)])
        for i in range(DEPTH):
            o_ref[pl.ds(i, 1), :] = rows[i]

    return pl.pallas_call(body, name="lb_fwd", out_shape=jax.ShapeDtypeStruct(lb_logits.shape, F32))(lb_logits)


def _lb_bwd(lb_logits, dlb):
    def body(l_ref, d_ref, o_ref):
        _, vjp = jax.vjp(_lb_rows, *[l_ref[pl.ds(i, 1), :] for i in range(DEPTH)])
        grads = vjp(tuple(d_ref[pl.ds(i, 1), :] for i in range(DEPTH)))
        for i in range(DEPTH):
            o_ref[pl.ds(i, 1), :] = grads[i]

    return pl.pallas_call(body, name="lb_bwd", out_shape=jax.ShapeDtypeStruct(lb_logits.shape, F32))(lb_logits, dlb)


MESH = pl.DeviceIdType.MESH
ANY = pl.BlockSpec(memory_space=pl.ANY)
N_KINDS = len(SHARD_ROWS)
FFN_HALF = FFN_HIDDEN // 2
KIND_PLACE = ((0, 0), (1, 0), (2, 0), (3, 0), (4, 0), (4, FFN_HALF), (5, 0))
KIND_HALF_SKIP = (0, 0, 0, 0, FFN_HALF, FFN_HALF, 0)
FULL_ROWS = (N_DEV * SHARD_ROWS[0], D_MODEL, D_MODEL, D_MODEL, 2 * N_DEV * SHARD_ROWS[4], N_DEV * SHARD_ROWS[6])


def _kind_rows(ti, dev):
    oi, base = KIND_PLACE[ti]
    start = base + dev * SHARD_ROWS[ti]
    if KIND_HALF_SKIP[ti]:
        start = start + (dev // (N_DEV // 2)) * KIND_HALF_SKIP[ti]
    return oi, pl.ds(start, SHARD_ROWS[ti])


def _position():
    x, y, c = lax.axis_index("x"), lax.axis_index("y"), lax.axis_index("c")
    return x, y, c, 4 * x + 2 * y + c


def _peer(x, y, c, r):
    px = 1 - x if r & 4 else x
    py = 1 - y if r & 2 else y
    pc = 1 - c if r & 1 else c
    return (px, py, pc), 4 * px + 2 * py + pc


class _Exchange(NamedTuple):
    operands: tuple
    out_shape: tuple
    copies: Callable
    n_local: int


EXCHANGE_SCRATCH = (pltpu.SemaphoreType.DMA((N_DEV, N_KINDS)), pltpu.SemaphoreType.DMA((N_DEV, N_KINDS)),
                    pltpu.SemaphoreType.DMA((N_KINDS,)))
ALL_KINDS = tuple(range(N_KINDS))
KINDS_W_IN = (0,)
KINDS_REST = ALL_KINDS[1:]


def _all_pairs(kinds, ends, send_sems, recv_sems):
    x, y, c, me = _position()
    out = []
    for r in range(1, N_DEV):
        peer, pid = _peer(x, y, c, r)
        for ti in kinds:
            src, dst = ends(ti, me, pid)
            out.append(pltpu.make_async_remote_copy(
                src_ref=src, dst_ref=dst, send_sem=send_sems.at[r, ti], recv_sem=recv_sems.at[r, ti],
                device_id=peer, device_id_type=MESH))
    return out


def _gather_exchange(shards, kinds):
    arrays = sorted({KIND_PLACE[ti][0] for ti in kinds})

    def copies(ins, outs, send_sems, recv_sems, local_sems, arrivals):
        src = dict(zip(kinds, ins))

        def window(ti, dev):
            oi, rows = _kind_rows(ti, dev)
            return outs[arrays.index(oi)].at[rows, :]

        if arrivals:
            return _all_pairs(kinds, lambda ti, me, pid: (src[ti], window(ti, pid)), send_sems, recv_sems)
        _, _, _, me = _position()
        local = [pltpu.make_async_copy(src[ti], window(ti, me), local_sems.at[ti]) for ti in kinds]
        return local + _all_pairs(kinds, lambda ti, me, pid: (src[ti], window(ti, me)), send_sems, recv_sems)

    return _Exchange(tuple(shards[ti] for ti in kinds),
                     tuple(jax.ShapeDtypeStruct((FULL_ROWS[oi], D_MODEL), BF16) for oi in arrays), copies, len(kinds))


def _scatter_exchange(grads, kinds):
    arrays = sorted({KIND_PLACE[ti][0] for ti in kinds})
    offsets, total = {}, 0
    for ti in kinds:
        offsets[ti], total = total, total + SHARD_ROWS[ti]

    def copies(ins, outs, send_sems, recv_sems, local_sems, arrivals):
        land = outs[0]

        def piece(ti, dev):
            ii, rows = _kind_rows(ti, dev)
            return ins[arrays.index(ii)].at[rows, :]

        def slot(ti, dev):
            return land.at[dev, pl.ds(offsets[ti], SHARD_ROWS[ti]), :]

        if arrivals:
            return _all_pairs(kinds, lambda ti, me, pid: (piece(ti, me), slot(ti, pid)), send_sems, recv_sems)
        _, _, _, me = _position()
        local = [pltpu.make_async_copy(piece(ti, me), slot(ti, me), local_sems.at[ti]) for ti in kinds]
        return local + _all_pairs(kinds, lambda ti, me, pid: (piece(ti, pid), slot(ti, me)), send_sems, recv_sems)

    return _Exchange(tuple(grads[oi] for oi in arrays), (jax.ShapeDtypeStruct((N_DEV, total, D_MODEL), BF16),),
                     copies, len(kinds))


def _exchange_start(ex, ins, outs, sems):
    for cp in ex.copies(ins, outs, *sems, False):
        cp.start()


def _exchange_finish(ex, ins, outs, sems):
    for cp in ex.copies(ins, outs, *sems, True):
        cp.wait_recv()
    mine = ex.copies(ins, outs, *sems, False)
    for cp in mine[:ex.n_local]:
        cp.wait()
    for cp in mine[ex.n_local:]:
        cp.wait_send()


def _run_exchange(ex, name):
    n_in, n_out = len(ex.operands), len(ex.out_shape)

    def body(*refs):
        ins, outs, sems = refs[:n_in], refs[n_in:n_in + n_out], refs[n_in + n_out:]
        _exchange_start(ex, ins, outs, sems)
        _exchange_finish(ex, ins, outs, sems)

    return pl.pallas_call(body, name=name, in_specs=[ANY] * n_in, out_specs=[ANY] * n_out,
                          out_shape=list(ex.out_shape), scratch_shapes=list(EXCHANGE_SCRATCH))(*ex.operands)


def _carry(body, ex, n_in, n_out, n_scratch, grid):
    if ex is None:
        return body
    e_in, e_out = len(ex.operands), len(ex.out_shape)

    def at(step):
        hit = pl.program_id(0) == step[0]
        for axis in range(1, len(grid)):
            hit = hit & (pl.program_id(axis) == step[axis])
        return hit

    def carrying(*refs):
        own_in, ex_in = refs[:n_in], refs[n_in:n_in + e_in]
        rest = refs[n_in + e_in:]
        own_out, ex_out = rest[:n_out], rest[n_out:n_out + e_out]
        own_scratch, sems = rest[n_out + e_out:n_out + e_out + n_scratch], rest[n_out + e_out + n_scratch:]

        @pl.when(at([0] * len(grid)))
        def _():
            _exchange_start(ex, ex_in, ex_out, sems)

        body(*own_in, *own_out, *own_scratch)

        @pl.when(at([g - 1 for g in grid]))
        def _():
            _exchange_finish(ex, ex_in, ex_out, sems)

    return carrying


def _carried(ex):
    if ex is None:
        return (), [], [], [], []
    return (ex.operands, [ANY] * len(ex.operands), list(ex.out_shape), [ANY] * len(ex.out_shape),
            list(EXCHANGE_SCRATCH))


def _small_sum_body(p_ref, o_ref, buf, send_sems, recv_sems):
    x, y, c, me = _position()
    buf[me] = p_ref[...]
    sends = []
    for r in range(1, N_DEV):
        peer, _ = _peer(x, y, c, r)
        sends.append(pltpu.make_async_remote_copy(
            src_ref=p_ref, dst_ref=buf.at[me], send_sem=send_sems.at[r], recv_sem=recv_sems.at[r],
            device_id=peer, device_id_type=MESH))
    for cp in sends:
        cp.start()
    for r in range(1, N_DEV):
        peer, pid = _peer(x, y, c, r)
        pltpu.make_async_remote_copy(
            src_ref=p_ref, dst_ref=buf.at[pid], send_sem=send_sems.at[r], recv_sem=recv_sems.at[r],
            device_id=peer, device_id_type=MESH).wait_recv()
    for cp in sends:
        cp.wait_send()
    acc = buf[0]
    for k in range(1, N_DEV):
        acc = acc + buf[k]
    o_ref[...] = acc


def _all_reduce_small(part):
    rows, d = part.shape
    vmem = pl.BlockSpec(memory_space=pltpu.VMEM)
    return pl.pallas_call(
        functools.partial(_small_sum_body), name="all_reduce_small", in_specs=[vmem], out_specs=vmem,
        out_shape=jax.ShapeDtypeStruct((rows, d), F32),
        scratch_shapes=[pltpu.VMEM((N_DEV, rows, d), F32), pltpu.SemaphoreType.DMA((N_DEV,)),
                        pltpu.SemaphoreType.DMA((N_DEV,))],
    )(part)


HG_PAIR = 2 * HG_DK


def _hg_consts():
    c = HG_CHUNK
    r = lax.broadcasted_iota(jnp.int32, (c, c), 0)
    s = lax.broadcasted_iota(jnp.int32, (c, c), 1)
    r2 = lax.broadcasted_iota(jnp.int32, (c, 2 * c), 0)
    s2 = lax.broadcasted_iota(jnp.int32, (c, 2 * c), 1)
    causal2 = jnp.where(s2 >= c, s2 - c, s2) <= r2
    lane_hi = lax.broadcasted_iota(jnp.int32, (c, HG_PAIR), 1) >= HG_DK
    same_head = ((lax.broadcasted_iota(jnp.int32, (HG_PAIR, HG_PAIR), 0) >= HG_DK)
                 == (lax.broadcasted_iota(jnp.int32, (HG_PAIR, HG_PAIR), 1) >= HG_DK))
    return (s <= r).astype(BF16), (s >= r).astype(BF16), causal2, lane_hi, same_head


def _head_rows(x, lane_hi):
    zero = jnp.zeros_like(x)
    return jnp.concatenate([jnp.where(lane_hi, zero, x), jnp.where(lane_hi, x, zero)], axis=0)


def _own_rows(y, lane_hi):
    return jnp.where(lane_hi, y[HG_CHUNK:], y[:HG_CHUNK])


def _split3(x):
    hi = x.astype(BF16)
    r1 = x - hi.astype(F32)
    mid = r1.astype(BF16)
    lo = (r1 - mid.astype(F32)).astype(BF16)
    return jnp.concatenate([hi, mid, lo], axis=1)


def _cumsum_rows(tri, x):
    w = x.shape[1]
    y = _dot(tri, _split3(x), NN)
    return y[:, :w] + y[:, w:2 * w] + y[:, 2 * w:]


def _hg_chunk(zq, zf, lb, tril, b_ref):
    c = HG_CHUNK
    sq = _sigmoid(zq)
    q = zq * sq
    sg = _sigmoid(zf)
    f = lb + (1.0 - lb) * sg
    logf = jnp.log(jnp.maximum(f, MIN_F))
    k = 1.0 - f
    b = _cumsum_rows(tril, logf)
    b_ref[...] = b
    mid = b_ref[pl.ds(c // 2 - 1, 1), :]
    bc = b_ref[pl.ds(c - 1, 1), :]
    em = jnp.exp(jnp.minimum(b - mid, HG_EXP_CLAMP))
    en = jnp.exp(jnp.minimum(mid - b, HG_EXP_CLAMP))
    return sq, q, sg, f, k, b, em, en, bc


def _hg_gate(o, zg, gn):
    return o * lax.rsqrt(jnp.mean(o * o, axis=-1, keepdims=True) + EPS) * gn * (zg * _sigmoid(zg))


def _hgrn2_fwd(proj, lb, gn, name, ex=None):
    t = proj.shape[0]
    bs_tok = min(HG_BLOCK, t)
    n_chunks = bs_tok // HG_CHUNK
    w = HG_HEADS * HG_DK

    def body(hq_ref, hf_ref, hi_ref, hg_ref, lb_ref, gn_ref, o_ref, og_ref, sall_ref, st_ref, b_ref):
        @pl.when(pl.program_id(0) == 0)
        def _():
            st_ref[...] = jnp.zeros_like(st_ref)

        tril, _, causal2, lane_hi, same_head = _hg_consts()

        for ci in range(n_chunks):
            rows = pl.ds(ci * HG_CHUNK, HG_CHUNK)
            for p in range(HG_HEADS // 2):
                cols = slice(p * HG_PAIR, (p + 1) * HG_PAIR)
                v = hi_ref[rows, cols]
                zg = hg_ref[rows, cols]
                _, q, _, _, k, b, em, en, bc = _hg_chunk(hq_ref[rows, cols], hf_ref[rows, cols],
                                                         lb_ref[:, cols], tril, b_ref.at[ci, p])
                st0 = st_ref[p]
                sall_ref[ci, 2 * p] = st0[:HG_DK, :HG_DK]
                sall_ref[ci, 2 * p + 1] = st0[HG_DK:, HG_DK:]
                vb = v.astype(BF16)
                o = _dot((q * jnp.exp(b)).astype(BF16), st0.astype(BF16), NT)
                a = jnp.where(causal2, _dot((q * em).astype(BF16), _head_rows((k * en).astype(BF16), lane_hi), NT), 0.0)
                o = o + _dot(a.astype(BF16), _head_rows(vb, lane_hi), NN)
                kdec = (k * jnp.exp(bc - b)).astype(BF16)
                st_ref[p] = st0 * jnp.exp(bc) + jnp.where(same_head, _dot(vb, kdec, TN), 0.0)
                o_ref[rows, cols] = o
                for hh in range(2):
                    sl = slice(hh * HG_DK, (hh + 1) * HG_DK)
                    hcols = slice(p * HG_PAIR + hh * HG_DK, p * HG_PAIR + (hh + 1) * HG_DK)
                    og_ref[rows, hcols] = _hg_gate(o[:, sl], zg[:, sl], gn_ref[:, hcols]).astype(BF16)

    def col(j):
        return pl.BlockSpec((bs_tok, w), lambda n, j=j: (n, j))

    vec = pl.BlockSpec((1, w), lambda n: (0, 0))
    ex_in, ex_in_specs, ex_out, ex_out_specs, ex_scratch = _carried(ex)
    outs = pl.pallas_call(
        _carry(body, ex, 6, 3, 2, (t // bs_tok,)), name=name, grid=(t // bs_tok,),
        in_specs=[col(COL_HQ // w), col(COL_HF // w), col(COL_HI // w), col(COL_HG // w), vec, vec] + ex_in_specs,
        out_specs=[col(0), col(0),
                   pl.BlockSpec((n_chunks, HG_HEADS, HG_DK, HG_DK), lambda n: (n, 0, 0, 0))] + ex_out_specs,
        out_shape=[jax.ShapeDtypeStruct((t, w), F32), jax.ShapeDtypeStruct((t, w), BF16),
                   jax.ShapeDtypeStruct((t // HG_CHUNK, HG_HEADS, HG_DK, HG_DK), F32)] + ex_out,
        scratch_shapes=[pltpu.VMEM((HG_HEADS // 2, HG_PAIR, HG_PAIR), F32),
                        pltpu.VMEM((n_chunks, HG_HEADS // 2, HG_CHUNK, HG_PAIR), F32)] + ex_scratch,
        compiler_params=_cp("arbitrary"))(proj, proj, proj, proj, lb, gn, *ex_in)
    return outs[:3], outs[3:]


def _hgrn2_bwd(proj, lb, gn, o_hg, sall, dog, name, ex=None):
    t = proj.shape[0]
    bs_tok = min(HG_BLOCK, t)
    n_chunks = bs_tok // HG_CHUNK
    n_blocks = t // bs_tok
    w = HG_HEADS * HG_DK

    def body(hq_ref, hf_ref, hi_ref, hg_ref, lb_ref, gn_ref, o_ref, sall_ref, dog_ref,
             da_ref, dlb_ref, dgn_ref, dst_ref, b_ref):
        @pl.when(pl.program_id(0) == 0)
        def _():
            dst_ref[...] = jnp.zeros_like(dst_ref)
            dlb_ref[...] = jnp.zeros_like(dlb_ref)
            dgn_ref[...] = jnp.zeros_like(dgn_ref)

        tril, rev_tril, causal2, lane_hi, same_head = _hg_consts()
        zero_block = jnp.zeros((HG_DK, HG_DK), F32)

        for ci in reversed(range(n_chunks)):
            rows = pl.ds(ci * HG_CHUNK, HG_CHUNK)
            for p in range(HG_HEADS // 2):
                cols = slice(p * HG_PAIR, (p + 1) * HG_PAIR)
                zq = hq_ref[rows, cols]
                v = hi_ref[rows, cols]
                zg = hg_ref[rows, cols]
                lbv = lb_ref[:, cols]
                sq, q, sg, f, k, b, em, en, bc = _hg_chunk(zq, hf_ref[rows, cols], lbv, tril, b_ref.at[ci, p])
                st0 = jnp.concatenate([jnp.concatenate([sall_ref[ci, 2 * p], zero_block], axis=1),
                                       jnp.concatenate([zero_block, sall_ref[ci, 2 * p + 1]], axis=1)], axis=0)
                dst1 = dst_ref[p]
                vb = v.astype(BF16)
                eb = jnp.exp(b)
                qg = (q * eb).astype(BF16)
                qt = (q * em).astype(BF16)
                kref = (k * en).astype(BF16)
                ebcb = jnp.exp(bc - b)
                kdec = (k * ebcb).astype(BF16)
                ebc = jnp.exp(bc)
                st1 = st0 * ebc + jnp.where(same_head, _dot(vb, kdec, TN), 0.0)

                dos, dzgs, dgns = [], [], []
                for hh in range(2):
                    sl = slice(hh * HG_DK, (hh + 1) * HG_DK)
                    hcols = slice(p * HG_PAIR + hh * HG_DK, p * HG_PAIR + (hh + 1) * HG_DK)
                    _, gate_vjp = jax.vjp(_hg_gate, o_ref[rows, hcols], zg[:, sl], gn_ref[:, hcols])
                    do_h, dzg_h, dgn_h = gate_vjp(dog_ref[rows, hcols])
                    dos.append(do_h)
                    dzgs.append(dzg_h)
                    dgns.append(dgn_h)
                dob = jnp.concatenate(dos, axis=1).astype(BF16)
                vrows, krows = _head_rows(vb, lane_hi), _head_rows(kref, lane_hi)
                dam = jnp.where(causal2, _dot(dob, vrows, NT), 0.0).astype(BF16)
                a = jnp.where(causal2, _dot(qt, krows, NT), 0.0)
                dk = ebcb * _dot(vb, dst1.astype(BF16), NN) + en * _own_rows(_dot(dam, qt, TN), lane_hi)
                dq = eb * _dot(dob, st0.astype(BF16), NN) + em * _dot(dam, krows, NN)
                dv = _own_rows(_dot(a.astype(BF16), dob, TN), lane_hi) + _dot(kdec, dst1.astype(BF16), NT)
                dst_ref[p] = dst1 * ebc + jnp.where(same_head, _dot(dob, qg, TN), 0.0)

                dbx = jnp.sum(dst1 * st1, axis=0, keepdims=True)
                dlogf = _cumsum_rows(rev_tril, q * dq - k * dk) + dbx
                df = jnp.where(f > MIN_F, dlogf / f, 0.0) - dk
                dzf = df * (1.0 - lbv) * sg * (1.0 - sg)
                dzq = dq * (sq * (1.0 + zq * (1.0 - sq)))
                da_ref[rows, pl.ds(COL_HQ + p * HG_PAIR, HG_PAIR)] = dzq.astype(BF16)
                da_ref[rows, pl.ds(COL_HF + p * HG_PAIR, HG_PAIR)] = dzf.astype(BF16)
                da_ref[rows, pl.ds(COL_HI + p * HG_PAIR, HG_PAIR)] = dv.astype(BF16)
                da_ref[rows, pl.ds(COL_HG + p * HG_PAIR, HG_PAIR)] = jnp.concatenate(dzgs, axis=1).astype(BF16)
                dlb_ref[:, cols] += jnp.sum(df * (1.0 - sg), axis=0, keepdims=True)
                dgn_ref[:, cols] += jnp.concatenate(dgns, axis=1)

    def col(j):
        return pl.BlockSpec((bs_tok, w), lambda n, j=j: (n_blocks - 1 - n, j))

    vec = pl.BlockSpec((1, w), lambda n: (0, 0))
    ex_in, ex_in_specs, ex_out, ex_out_specs, ex_scratch = _carried(ex)
    outs = pl.pallas_call(
        _carry(body, ex, 9, 3, 2, (n_blocks,)), name=name, grid=(n_blocks,),
        in_specs=[col(COL_HQ // w), col(COL_HF // w), col(COL_HI // w), col(COL_HG // w), vec, vec, col(0),
                  pl.BlockSpec((n_chunks, HG_HEADS, HG_DK, HG_DK), lambda n: (n_blocks - 1 - n, 0, 0, 0)),
                  col(0)] + ex_in_specs,
        out_specs=[pl.BlockSpec((bs_tok, 4 * w), lambda n: (n_blocks - 1 - n, 0)), vec, vec] + ex_out_specs,
        out_shape=[jax.ShapeDtypeStruct((t, 4 * w), BF16), jax.ShapeDtypeStruct((1, w), F32),
                   jax.ShapeDtypeStruct((1, w), F32)] + ex_out,
        scratch_shapes=[pltpu.VMEM((HG_HEADS // 2, HG_PAIR, HG_PAIR), F32),
                        pltpu.VMEM((n_chunks, HG_HEADS // 2, HG_CHUNK, HG_PAIR), F32)] + ex_scratch,
        compiler_params=_cp("arbitrary"))(proj, proj, proj, proj, lb, gn, o_hg, sall, dog, *ex_in)
    return outs[:3], outs[3:]


def _rope_tables(t):
    half = ROPE_DIM // 2
    inv = ROPE_THETA ** (-jnp.arange(half, dtype=F32) * 2.0 / ROPE_DIM)
    ang = jnp.arange(t).astype(F32)[:, None] * inv[None, :]
    cos, sin = jnp.cos(ang), jnp.sin(ang)
    pad = ATT_HEAD_DIM - ROPE_DIM
    c = jnp.concatenate([cos, cos, jnp.ones((t, pad), F32)], axis=1)
    su = jnp.concatenate([-sin, jnp.zeros((t, half + pad), F32)], axis=1)
    sd = jnp.concatenate([jnp.zeros((t, half), F32), sin, jnp.zeros((t, pad), F32)], axis=1)
    return tuple(jnp.concatenate([m, m], axis=1) for m in (c, su, sd))


def _rope(x, tabs):
    c, su, sd = tabs
    n = x.shape[1]
    half = ROPE_DIM // 2
    return x * c + pltpu.roll(x, n - half, 1) * su + pltpu.roll(x, half, 1) * sd


def _rope_t(dy, tabs):
    c, su, sd = tabs
    n = dy.shape[1]
    half = ROPE_DIM // 2
    return dy * c + pltpu.roll(dy * su, half, 1) + pltpu.roll(dy * sd, n - half, 1)


def _swa_specs(n_blocks, clamp):
    blk = ATT_BLOCK

    def cur(n):
        return jnp.minimum(n, n_blocks - 1) if clamp else n

    def prev(n):
        return jnp.maximum(cur(n) - 1, 0)

    q_spec = pl.BlockSpec((blk, 512), lambda m, n: (cur(n), COL_AQ // 512 + m))
    kv = [pl.BlockSpec((blk, 128), lambda m, n, c=c, f=f: (f(n), c + m))
          for c in (COL_AK // 128, COL_AV // 128) for f in (cur, prev)]
    tabs = [pl.BlockSpec((blk, 128), lambda m, n, f=f: (f(n), 0)) for f in (cur, prev) for _ in range(3)]
    return q_spec, kv, tabs, cur, prev


ATT_SCALE = ATT_HEAD_DIM ** -0.5


def _head_halves(x, upper):
    zero = jnp.zeros_like(x)
    return jnp.concatenate([jnp.where(upper, zero, x), jnp.where(upper, x, zero)], axis=0)


def _swa_scores(scores, sink, mask):
    s = jnp.where(mask, scores, -jnp.inf)
    mx = jnp.maximum(jnp.max(s, axis=-1, keepdims=True), sink)
    p = jnp.exp(s - mx)
    es = jnp.exp(sink - mx)
    rinv = 1.0 / (jnp.sum(p, axis=-1, keepdims=True) + es)
    return p * rinv, es * rinv


def _swa_window(kc_ref, kp_ref, vc_ref, vp_ref, tabs_c, tabs_p, n):
    k2 = jnp.concatenate([_rope(kp_ref[...], tabs_p), _rope(kc_ref[...], tabs_c)], axis=0)
    v2 = jnp.concatenate([vp_ref[...], vc_ref[...]], axis=0)
    blk = ATT_BLOCK
    qi = lax.broadcasted_iota(jnp.int32, (blk, 2 * blk), 0)
    kj = lax.broadcasted_iota(jnp.int32, (blk, 2 * blk), 1)
    delta = qi + blk - kj
    mask = (delta >= 0) & (delta < blk) & ((kj >= blk) | (n > 0))
    return k2, v2, mask


def _swa_fwd(proj, sinks, tabs, name, ex=None):
    t = proj.shape[0]
    n_blocks = t // ATT_BLOCK
    q_spec, kv_specs, tab_specs, _, _ = _swa_specs(n_blocks, clamp=False)

    def body(q_ref, kc_ref, kp_ref, vc_ref, vp_ref, c0, c1, c2, p0, p1, p2, sink_ref, o_ref):
        m, n = pl.program_id(0), pl.program_id(1)
        tabs_c = (c0[...], c1[...], c2[...])
        tabs_p = (p0[...], p1[...], p2[...])
        k2, v2, mask = _swa_window(kc_ref, kp_ref, vc_ref, vp_ref, tabs_c, tabs_p, n)
        k2r, v2r = pltpu.roll(k2, 64, 1), pltpu.roll(v2, 64, 1)
        upper_k = lax.broadcasted_iota(jnp.int32, k2.shape, 1) >= 64
        upper_q = lax.broadcasted_iota(jnp.int32, (ATT_BLOCK, 128), 1) >= 64
        for jj in range(2):
            own = upper_k if jj else ~upper_k
            kd = jnp.where(own, k2, k2r).astype(BF16)
            vd = jnp.where(own, v2, v2r).astype(BF16)
            for pi in range(2):
                cols = slice(256 * jj + 128 * pi, 256 * jj + 128 * pi + 128)
                qp = _rope(q_ref[:, cols], tabs_c) * ATT_SCALE
                outs = []
                for e in range(2):
                    sink = sink_ref[0, 8 * m + 4 * jj + 2 * pi + e]
                    qm = jnp.where(upper_q if e else ~upper_q, qp, 0.0).astype(BF16)
                    pn, _ = _swa_scores(_dot(qm, kd, NT), sink, mask)
                    outs.append(_dot(pn.astype(BF16), vd, NN))
                o_ref[:, cols] = jnp.where(upper_q, outs[1], outs[0]).astype(BF16)

    ex_in, ex_in_specs, ex_out, ex_out_specs, ex_scratch = _carried(ex)
    outs = pl.pallas_call(
        _carry(body, ex, 12, 1, 0, (2, n_blocks)), name=name, grid=(2, n_blocks),
        in_specs=[q_spec] + kv_specs + tab_specs + [pl.BlockSpec(memory_space=pltpu.SMEM)] + ex_in_specs,
        out_specs=[pl.BlockSpec((ATT_BLOCK, 512), lambda m, n: (n, m))] + ex_out_specs,
        out_shape=[jax.ShapeDtypeStruct((t, ATT_Q_HEADS * ATT_HEAD_DIM), BF16)] + ex_out,
        scratch_shapes=ex_scratch,
        compiler_params=_cp("arbitrary", "arbitrary"))(proj, proj, proj, proj, proj, *tabs, *tabs, sinks, *ex_in)
    return outs[0], outs[1:]


def _swa_bwd(proj, sinks, tabs, o_att, do_att, name, ex=None):
    t = proj.shape[0]
    n_blocks = t // ATT_BLOCK
    blk = ATT_BLOCK
    q_spec, kv_specs, tab_specs, cur, prev = _swa_specs(n_blocks, clamp=True)

    def body(q_ref, kc_ref, kp_ref, vc_ref, vp_ref, c0, c1, c2, p0, p1, p2, sink_ref, o_ref, do_ref,
             dq_ref, dk_ref, dv_ref, ds_ref, ck_ref, cv_ref):
        m, n = pl.program_id(0), pl.program_id(1)

        @pl.when(n == 0)
        def _():
            ds_ref[...] = jnp.zeros_like(ds_ref)
            ck_ref[...] = jnp.zeros_like(ck_ref)
            cv_ref[...] = jnp.zeros_like(cv_ref)

        @pl.when(n < n_blocks)
        def _():
            tabs_c = (c0[...], c1[...], c2[...])
            tabs_p = (p0[...], p1[...], p2[...])
            k2, v2, mask = _swa_window(kc_ref, kp_ref, vc_ref, vp_ref, tabs_c, tabs_p, n)
            k2r, v2r = pltpu.roll(k2, 64, 1), pltpu.roll(v2, 64, 1)
            upper_k = lax.broadcasted_iota(jnp.int32, k2.shape, 1) >= 64
            upper_q = lax.broadcasted_iota(jnp.int32, (blk, 128), 1) >= 64
            lane = lax.broadcasted_iota(jnp.int32, (8, 128), 1)
            dk2 = jnp.zeros(k2.shape, F32)
            dv2 = jnp.zeros(k2.shape, F32)
            dsv = jnp.zeros((8, 128), F32)
            nk = 2 * blk
            for jj in range(2):
                own = upper_k if jj else ~upper_k
                kh = _head_halves(jnp.where(own, k2, k2r).astype(BF16), upper_k)
                vh = _head_halves(jnp.where(own, v2, v2r).astype(BF16), upper_k)
                dkd = jnp.zeros(k2.shape, F32)
                dvd = jnp.zeros(k2.shape, F32)
                for pi in range(2):
                    cols = slice(256 * jj + 128 * pi, 256 * jj + 128 * pi + 128)
                    qp = (_rope(q_ref[:, cols], tabs_c) * ATT_SCALE).astype(BF16)
                    do_pair = do_ref[:, cols]
                    o_pair = o_ref[:, cols].astype(F32)
                    dob = do_pair.astype(BF16)
                    s = _dot(qp, kh, NT)
                    dp = _dot(dob, vh, NT)
                    pns, dss = [], []
                    for e in range(2):
                        hl = 4 * jj + 2 * pi + e
                        pn, ps = _swa_scores(s[:, e * nk:(e + 1) * nk], sink_ref[0, 8 * m + hl], mask)
                        delta = jnp.sum(jnp.where(upper_q if e else ~upper_q, do_pair * o_pair, 0.0), axis=-1, keepdims=True)
                        pns.append(pn.astype(BF16))
                        dss.append((pn * (dp[:, e * nk:(e + 1) * nk] - delta)).astype(BF16))
                        dsv = dsv + jnp.where(lane == hl, -jnp.sum(ps * delta), 0.0)
                    dsb = jnp.concatenate(dss, axis=1)
                    dq_ref[:, cols] = _rope_t(_dot(dsb, kh, NN) * ATT_SCALE, tabs_c).astype(BF16)
                    rk = _dot(dsb, qp, TN)
                    rv = _dot(jnp.concatenate(pns, axis=1), dob, TN)
                    dkd = dkd + jnp.where(upper_k, rk[nk:], rk[:nk])
                    dvd = dvd + jnp.where(upper_k, rv[nk:], rv[:nk])
                dk2 = dk2 + jnp.where(own, dkd + pltpu.roll(dkd, 64, 1), 0.0)
                dv2 = dv2 + jnp.where(own, dvd + pltpu.roll(dvd, 64, 1), 0.0)
            dk_ref[...] = (ck_ref[...] + _rope_t(dk2[:blk], tabs_p)).astype(BF16)
            dv_ref[...] = (cv_ref[...] + dv2[:blk]).astype(BF16)
            ck_ref[...] = _rope_t(dk2[blk:], tabs_c)
            cv_ref[...] = dv2[blk:]
            ds_ref[...] += dsv

        @pl.when(n == n_blocks)
        def _():
            dk_ref[...] = ck_ref[...].astype(BF16)
            dv_ref[...] = cv_ref[...].astype(BF16)

    wide = pl.BlockSpec((blk, 512), lambda m, n: (cur(n), m))
    lagged = pl.BlockSpec((blk, 128), lambda m, n: (jnp.maximum(n - 1, 0), m))
    ex_in, ex_in_specs, ex_out, ex_out_specs, ex_scratch = _carried(ex)
    outs = pl.pallas_call(
        _carry(body, ex, 14, 4, 2, (2, n_blocks + 1)), name=name, grid=(2, n_blocks + 1),
        in_specs=[q_spec] + kv_specs + tab_specs + [pl.BlockSpec(memory_space=pltpu.SMEM), wide, wide] + ex_in_specs,
        out_specs=[wide, lagged, lagged, pl.BlockSpec((None, 8, 128), lambda m, n: (m, 0, 0))] + ex_out_specs,
        out_shape=[jax.ShapeDtypeStruct((t, 1024), BF16), jax.ShapeDtypeStruct((t, 256), BF16),
                   jax.ShapeDtypeStruct((t, 256), BF16), jax.ShapeDtypeStruct((2, 8, 128), F32)] + ex_out,
        scratch_shapes=[pltpu.VMEM((blk, 128), F32), pltpu.VMEM((blk, 128), F32)] + ex_scratch,
        compiler_params=_cp("arbitrary", "arbitrary"))(proj, proj, proj, proj, proj, *tabs, *tabs, sinks, o_att, do_att,
                                                       *ex_in)
    return outs[:4], outs[4:]


def _local_step(x, target, shards, norm1, lb_logits, hg_norm, attn_sinks, norm2, final_norm):
    t = x.shape[0]
    tabs = _rope_tables(t)
    lb_all = _lb_fwd(lb_logits)
    saved = []

    def shards_of(l):
        return {ti: shards[ti][l] for ti in ALL_KINDS}

    win_next = _run_exchange(_gather_exchange(shards_of(0), KINDS_W_IN), "gather_w_in")
    rest_next = None
    for l in range(DEPTH):
        n1, n2 = norm1[l][None, :], norm2[l][None, :]
        lb, gn, sinks = lb_all[l][None, :], hg_norm[l][None, :], attn_sinks[l][None, :]
        (win_t,) = win_next
        h = _rms_fwd(x, n1, "rms1_fwd")
        if l == 0:
            proj, rest_next = _matmul_nt(h, win_t, 0, IN_COLS, F32, "proj_fwd", tn=1280,
                                         ex=_gather_exchange(shards_of(0), KINDS_REST))
        else:
            proj = _matmul_nt(h, win_t, 0, IN_COLS, F32, "proj_fwd", tn=1280)
        w_pa, w_pb, w_o, wgu_t, w_d = rest_next
        more = l + 1 < DEPTH
        (o_hg, o_g, sall), rest_next = _hgrn2_fwd(
            proj, lb, gn, "hgrn2_fwd", _gather_exchange(shards_of(l + 1), KINDS_REST) if more else None)
        o_att, win_next = _swa_fwd(
            proj, sinks, tabs, "swa_fwd", _gather_exchange(shards_of(l + 1), KINDS_W_IN) if more else None)
        ya, yb, mix, x1 = _merge_fwd(o_g, o_att, proj, x, w_pa, w_pb, w_o, "merge_fwd")
        h2 = _rms_fwd(x1, n2, "rms2_fwd")
        gu, act = _ffn_up_fwd(h2, wgu_t, "ffn_up_fwd")
        x2 = _matmul_nn(act, w_d, 0, x1, "wd_fwd")
        saved.append((x, h, proj, o_hg, o_g, sall, o_att, ya, yb, mix, x1, h2, gu, act, n1, n2, lb, gn, sinks,
                      (win_t, w_pa, w_pb, w_o, wgu_t, w_d)))
        x = x2

    dx, d_fn, loss = _loss_head(x, final_norm[None, :], target, "loss_head")

    owned = [None] * DEPTH
    pending = None
    d_n1, d_n2, d_lb, d_gn, d_sinks = ([None] * DEPTH for _ in range(5))
    for l in reversed(range(DEPTH)):
        x0, h, proj, o_hg, o_g, sall, o_att, ya, yb, mix, x1, h2, gu, act, n1, n2, lb, gn, sinks, weights = saved[l]
        win_t, w_pa, w_pb, w_o, wgu_t, w_d = weights
        dgu = _ffn_down_bwd(dx, w_d, gu, "ffn_down_bwd")
        g_wd = _matmul_tn(act, dx, "wd_grad", tm=1408)
        g_wgu = _matmul_tn(dgu, h2, "wgu_grad", tm=1408)
        dx1, d_n2[l] = _rows_bwd([dgu], wgu_t, x1, n2, dx, "ffn_up_bwd")
        g_wo = _matmul_tn(mix, dx1, "wo_grad")
        dya, dyb, dgab, dog, doatt = _merge_bwd(dx1, ya, yb, proj, w_pa, w_pb, w_o, "merge_bwd")
        g_wpa = _matmul_tn(o_g, dya, "wpa_grad")
        g_wpb = _matmul_tn(o_att, dyb, "wpb_grad")
        ex = _scatter_exchange(pending, ALL_KINDS) if pending is not None else None
        (dhg, d_lb[l], d_gn[l]), land = _hgrn2_bwd(proj, lb, gn, o_hg, sall, dog, "hgrn2_bwd", ex)
        if pending is not None:
            owned[l + 1] = _sum_slots(land[0], "sum_slots")
        ex = _scatter_exchange((None, g_wpa, g_wpb, g_wo, g_wgu, g_wd), KINDS_REST) if l == 0 else None
        (daq, dak, dav, d_sinks[l]), land_rest = _swa_bwd(proj, sinks, tabs, o_att, doatt, "swa_bwd", ex)
        dakv = jnp.concatenate([dak, dav], axis=1)
        g_win = jnp.concatenate([_matmul_tn(dhg, h, "win_grad_hg"), _matmul_tn(daq, h, "win_grad_aq"),
                                 _matmul_tn(dakv, h, "win_grad_akv"), _matmul_tn(dgab, h, "win_grad_gates")], axis=0)
        dx, d_n1[l] = _rows_bwd([dhg, daq, dakv, dgab], win_t, x0, n1, dx1, "win_bwd")
        pending = (g_win, g_wpa, g_wpb, g_wo, g_wgu, g_wd)
    land_win = _run_exchange(_scatter_exchange(pending, KINDS_W_IN), "scatter_w_in")
    owned[0] = jnp.concatenate([_sum_slots(land_win[0], "sum_slots_w_in"), _sum_slots(land_rest[0], "sum_slots_rest")],
                               axis=0)

    d_sink_rows = [jnp.concatenate([d[0, 0, :8], d[1, 0, :8]]) for d in d_sinks]
    small = (jnp.concatenate(d_n1, axis=0), jnp.concatenate(d_lb, axis=0), jnp.concatenate(d_gn, axis=0),
             jnp.concatenate(d_n2, axis=0), d_fn, jnp.stack(d_sink_rows, axis=0))
    return loss, dx, jnp.stack(owned, axis=0), small


def _sum_slots(land, name, tr=480):
    _, rows, d = land.shape

    def body(l_ref, o_ref):
        acc = l_ref[0].astype(F32)
        for k in range(1, N_DEV):
            acc = acc + l_ref[k].astype(F32)
        o_ref[...] = acc

    return pl.pallas_call(
        body, name=name, grid=(rows // tr,),
        in_specs=[pl.BlockSpec((N_DEV, tr, d), lambda i: (0, i, 0))],
        out_specs=pl.BlockSpec((tr, d), lambda i: (i, 0)),
        out_shape=jax.ShapeDtypeStruct((rows, d), F32),
        compiler_params=_cp("parallel"))(land)


def _adamw(w, g, m, v, name):
    shape = w.shape
    c = shape[-1]
    rows = w.size // c
    tr = rows
    for cand in (512, 352, 128):
        if rows % cand == 0:
            tr = cand
            break
    c1 = 1.0 / (1.0 - ADAM_B1 ** ADAM_STEP)
    c2 = 1.0 / (1.0 - ADAM_B2 ** ADAM_STEP)

    def body(w_ref, g_ref, m_ref, v_ref, d_ref, nm_ref, nv_ref):
        gv = g_ref[...]
        nm = ADAM_B1 * m_ref[...] + (1.0 - ADAM_B1) * gv
        nv = ADAM_B2 * v_ref[...] + (1.0 - ADAM_B2) * (gv * gv)
        d_ref[...] = -ADAM_LR * ((nm * c1) / (jnp.sqrt(nv * c2) + ADAM_EPS) + ADAM_WD * w_ref[...])
        nm_ref[...] = nm
        nv_ref[...] = nv

    spec = pl.BlockSpec((tr, c), lambda i: (i, 0))
    outs = pl.pallas_call(
        body, name=name, grid=(rows // tr,), in_specs=[spec] * 4, out_specs=[spec] * 3,
        out_shape=[jax.ShapeDtypeStruct((rows, c), F32)] * 3,
        compiler_params=_cp("parallel"))(*[a.reshape(rows, c) for a in (w, g, m, v)])
    return tuple(o.reshape(shape) for o in outs)


def kernel(x, norm1, w_in, lb_logits, hg_norm, attn_sinks, w_pa, w_pb, w_o, norm2, w_gate, w_up, w_down, final_norm, loss_target, m_norm1, m_w_in, m_lb_logits, m_hg_norm, m_attn_sinks, m_w_pa, m_w_pb, m_w_o, m_norm2, m_w_gate, m_w_up, m_w_down, m_final_norm, v_norm1, v_w_in, v_lb_logits, v_hg_norm, v_attn_sinks, v_w_pa, v_w_pb, v_w_o, v_norm2, v_w_gate, v_w_up, v_w_down, v_final_norm):
    t = x.shape[1]
    shards = [jnp.swapaxes(w_in, 1, 2).astype(BF16), w_pa.astype(BF16), w_pb.astype(BF16), w_o.astype(BF16),
              jnp.swapaxes(w_gate, 1, 2).astype(BF16), jnp.swapaxes(w_up, 1, 2).astype(BF16), w_down.astype(BF16)]
    loss_lanes, grad_x, owned, small = _local_step(
        x.reshape(t, D_MODEL), loss_target.reshape(t, D_MODEL), shards,
        norm1, lb_logits, hg_norm, attn_sinks, norm2, final_norm)

    def rows_of(ti, transpose):
        g = owned[:, SLOT_OFF[ti]:SLOT_OFF[ti] + SHARD_ROWS[ti], :]
        return jnp.swapaxes(g, 1, 2) if transpose else g

    g_big = {"w_in": rows_of(0, True), "w_pa": rows_of(1, False), "w_pb": rows_of(2, False), "w_o": rows_of(3, False),
             "w_gate": rows_of(4, True), "w_up": rows_of(5, True), "w_down": rows_of(6, False)}

    d_n1, d_lb, d_gn, d_n2, d_fn, d_sinks = small
    pad = jnp.zeros((DEPTH, D_MODEL - ATT_Q_HEADS), F32)
    packed = jnp.concatenate([
        d_n1, d_lb, d_gn, d_n2, d_fn, jnp.concatenate([d_sinks, pad], axis=1),
        jnp.concatenate([loss_lanes, jnp.zeros((1, D_MODEL - 128), F32)], axis=1),
        jnp.zeros((SMALL_ROWS - 22, D_MODEL), F32)], axis=0)
    total = _all_reduce_small(packed)
    loss = total[21, 0]
    g_small = {"norm1": total[0:4], "lb_logits": _lb_bwd(lb_logits, total[4:8]), "hg_norm": total[8:12],
               "norm2": total[12:16], "final_norm": total[16], "attn_sinks": total[17:21, :ATT_Q_HEADS]}

    params = {"norm1": (norm1, m_norm1, v_norm1), "w_in": (w_in, m_w_in, v_w_in),
              "lb_logits": (lb_logits, m_lb_logits, v_lb_logits), "hg_norm": (hg_norm, m_hg_norm, v_hg_norm),
              "attn_sinks": (attn_sinks, m_attn_sinks, v_attn_sinks), "w_pa": (w_pa, m_w_pa, v_w_pa),
              "w_pb": (w_pb, m_w_pb, v_w_pb), "w_o": (w_o, m_w_o, v_w_o), "norm2": (norm2, m_norm2, v_norm2),
              "w_gate": (w_gate, m_w_gate, v_w_gate), "w_up": (w_up, m_w_up, v_w_up),
              "w_down": (w_down, m_w_down, v_w_down), "final_norm": (final_norm, m_final_norm, v_final_norm)}
    order = ["norm1", "w_in", "lb_logits", "hg_norm", "attn_sinks", "w_pa", "w_pb", "w_o", "norm2",
             "w_gate", "w_up", "w_down", "final_norm"]
    grads, deltas, new_m, new_v = [], [], [], []
    for name in order:
        w, m, v = params[name]
        g = (g_big[name] if name in g_big else g_small[name]).reshape(w.shape)
        w2 = w.reshape(1, -1) if w.ndim == 1 else w
        d, nm, nv = _adamw(w2, g.reshape(w2.shape), m.reshape(w2.shape), v.reshape(w2.shape), "adamw_" + name)
        grads.append(g)
        deltas.append(d.reshape(w.shape))
        new_m.append(nm.reshape(w.shape))
        new_v.append(nv.reshape(w.shape))
    return (loss, grad_x.reshape(x.shape), *grads, *deltas, *new_m, *new_v)
```

```python
import functools
from typing import Callable, NamedTuple

import jax
import jax.numpy as jnp
from jax import lax
from jax.experimental import pallas as pl
from jax.experimental.pallas import tpu as pltpu

F32, BF16 = jnp.float32, jnp.bfloat16

D_MODEL = 1024
DEPTH = 4
N_DEV = 8
HG_HEADS = 8
HG_DK = 128
HG_CHUNK = 64
HG_BLOCK = 256
HG_EXP_CLAMP = 60.0
ATT_Q_HEADS = 16
ATT_HEAD_DIM = 64
ATT_BLOCK = 128
ROPE_THETA = 500000.0
ROPE_DIM = 16
FFN_HIDDEN = 2816
EPS = 1e-6
MIN_F = 1e-30
ADAM_LR, ADAM_B1, ADAM_B2, ADAM_EPS, ADAM_WD, ADAM_STEP = 0.001, 0.9, 0.999, 1e-08, 0.01, 10

COL_HQ, COL_HF, COL_HI, COL_HG = 0, 1024, 2048, 3072
COL_AQ, COL_AK, COL_AV, COL_GA, COL_GB = 4096, 5120, 5376, 5632, 6656
IN_COLS = 7680

SHARD_ROWS = (960, 128, 128, 128, 352, 352, 352)
SLOT_OFF = (0, 960, 1088, 1216, 1344, 1696, 2048)
SLOT_ROWS = 2400
SMALL_ROWS = 24

VMEM_LIMIT_BYTES = 56 * 1024 * 1024

NN = ((1,), (0,))
NT = ((1,), (1,))
TN = ((0,), (0,))


def _dot(a, b, dims):
    return lax.dot_general(a, b, (dims, ((), ())), preferred_element_type=F32)


def _cp(*sem):
    return pltpu.CompilerParams(dimension_semantics=sem if sem else None, vmem_limit_bytes=VMEM_LIMIT_BYTES)


def _sigmoid(x):
    return 1.0 / (1.0 + jnp.exp(-x))


def _matmul_nt(a, w, row_off, n, out_dtype, name, tm=1024, tn=512, ex=None):
    t, k = a.shape
    tm = min(tm, t)
    assert n % tn == 0 and row_off % tn == 0 and t % tm == 0
    grid = (n // tn, t // tm)

    def body(a_ref, w_ref, o_ref):
        o_ref[...] = _dot(a_ref[...].astype(BF16), w_ref[...], NT).astype(o_ref.dtype)

    ex_in, ex_in_specs, ex_out, ex_out_specs, ex_scratch = _carried(ex)
    outs = pl.pallas_call(
        _carry(body, ex, 2, 1, 0, grid), name=name, grid=grid,
        in_specs=[pl.BlockSpec((tm, k), lambda j, i: (i, 0)),
                  pl.BlockSpec((tn, k), lambda j, i: (row_off // tn + j, 0))] + ex_in_specs,
        out_specs=[pl.BlockSpec((tm, tn), lambda j, i: (i, j))] + ex_out_specs,
        out_shape=[jax.ShapeDtypeStruct((t, n), out_dtype)] + ex_out,
        scratch_shapes=ex_scratch,
        compiler_params=_cp("arbitrary", "arbitrary") if ex else _cp("parallel", "parallel"))(a, w, *ex_in)
    return (outs[0], outs[1:]) if ex else outs[0]


def _matmul_nn(a, w, row_off, res, name, tm=512, tk=None, gain=None):
    t, k = a.shape
    n = w.shape[1]
    tm = min(tm, t)
    tk = tk or k
    nk = k // tk
    assert k % tk == 0 and row_off % tk == 0 and t % tm == 0

    def body(*refs):
        refs = list(refs)
        a_ref, w_ref = refs[:2]
        r_ref = refs[2] if res is not None else None
        g_ref = refs[2 + (res is not None)] if gain is not None else None
        acc = refs[-1]
        o_ref = refs[-3] if gain is not None else refs[-2]
        kk = pl.program_id(1)
        part = _dot(a_ref[...].astype(BF16), w_ref[...], NN)

        @pl.when(kk == 0)
        def _():
            acc[...] = part

        @pl.when(kk > 0)
        def _():
            acc[...] += part

        @pl.when(kk == nk - 1)
        def _():
            y = acc[...] if res is None else acc[...] + r_ref[...]
            o_ref[...] = y
            if gain is not None:
                refs[-2][...] = _rms(y, g_ref[...]).astype(BF16)

    row = pl.BlockSpec((tm, n), lambda i, kk: (i, 0))
    in_specs = [pl.BlockSpec((tm, tk), lambda i, kk: (i, kk)),
                pl.BlockSpec((tk, n), lambda i, kk: (row_off // tk + kk, 0))]
    args = [a, w]
    if res is not None:
        in_specs.append(row)
        args.append(res)
    if gain is not None:
        in_specs.append(pl.BlockSpec((1, n), lambda i, kk: (0, 0)))
        args.append(gain)
    outs = pl.pallas_call(
        body, name=name, grid=(t // tm, nk), in_specs=in_specs,
        out_specs=[row, row] if gain is not None else [row],
        out_shape=[jax.ShapeDtypeStruct((t, n), F32)] + ([jax.ShapeDtypeStruct((t, n), BF16)] if gain is not None else []),
        scratch_shapes=[pltpu.VMEM((tm, n), F32)],
        compiler_params=_cp("parallel", "arbitrary"))(*args)
    return tuple(outs) if gain is not None else outs[0]


def _matmul_tn(a, b, name, tm=512, tk=2048, rows=None, row_off=0, into=None):
    t, m = a.shape
    n = b.shape[1]
    tk = min(tk, t)
    nk = t // tk
    rows = rows or m
    assert m % tm == 0 and t % tk == 0 and row_off % tm == 0

    def body(*refs):
        a_ref, b_ref, o_ref, acc = refs[0], refs[1], refs[-2], refs[-1]
        kk = pl.program_id(1)
        part = _dot(a_ref[...].astype(BF16), b_ref[...].astype(BF16), TN)

        @pl.when(kk == 0)
        def _():
            acc[...] = part

        @pl.when(kk > 0)
        def _():
            acc[...] += part

        @pl.when(kk == nk - 1)
        def _():
            o_ref[...] = acc[...].astype(BF16)

    return pl.pallas_call(
        body, name=name, grid=(m // tm, nk),
        in_specs=[pl.BlockSpec((tk, tm), lambda i, kk: (kk, i)),
                  pl.BlockSpec((tk, n), lambda i, kk: (kk, 0))] + ([ANY] if into is not None else []),
        out_specs=pl.BlockSpec((tm, n), lambda i, kk: (row_off // tm + i, 0)),
        out_shape=jax.ShapeDtypeStruct((rows, n), BF16),
        scratch_shapes=[pltpu.VMEM((tm, n), F32)],
        input_output_aliases={2: 0} if into is not None else {},
        compiler_params=_cp("parallel", "arbitrary"))(a, b, *([into] if into is not None else []))


def _rms(x, g):
    return x * lax.rsqrt(jnp.mean(x * x, axis=-1, keepdims=True) + EPS) * g


def _rms_fwd(x, g, name, tm=512):
    t, d = x.shape
    tm = min(tm, t)

    def body(x_ref, g_ref, o_ref):
        o_ref[...] = _rms(x_ref[...], g_ref[...]).astype(BF16)

    return pl.pallas_call(
        body, name=name, grid=(t // tm,),
        in_specs=[pl.BlockSpec((tm, d), lambda i: (i, 0)), pl.BlockSpec((1, d), lambda i: (0, 0))],
        out_specs=pl.BlockSpec((tm, d), lambda i: (i, 0)),
        out_shape=jax.ShapeDtypeStruct((t, d), BF16),
        compiler_params=_cp("parallel"))(x, g)


def _mix(ya, yb, ga, gb):
    return _sigmoid(ga) * ya + _sigmoid(gb) * yb


def _gate_specs(tm):
    half = D_MODEL // 2
    return [pl.BlockSpec((tm, half), lambda i, c=c: (i, c))
            for c in (COL_GA // half, COL_GA // half + 1, COL_GB // half, COL_GB // half + 1)]


def _merge_bwd(dx1, ya, yb, proj, w_pa, w_pb, w_o, name, tm=512):
    t, d = dx1.shape
    tm = min(tm, t)

    def body(dx_ref, ya_ref, yb_ref, ga0, ga1, gb0, gb1, wpa_ref, wpb_ref, wo_ref,
             dya_ref, dyb_ref, dg_ref, dog_ref, doa_ref):
        dmix = _dot(dx_ref[...].astype(BF16), wo_ref[...], NT)
        ga = jnp.concatenate([ga0[...], ga1[...]], axis=1)
        gb = jnp.concatenate([gb0[...], gb1[...]], axis=1)
        _, vjp = jax.vjp(_mix, ya_ref[...].astype(F32), yb_ref[...].astype(F32), ga, gb)
        dya, dyb, dga, dgb = vjp(dmix)
        dya, dyb = dya.astype(BF16), dyb.astype(BF16)
        dya_ref[...] = dya
        dyb_ref[...] = dyb
        dg_ref[:, :d] = dga.astype(BF16)
        dg_ref[:, d:] = dgb.astype(BF16)
        dog_ref[...] = _dot(dya, wpa_ref[...], NT)
        doa_ref[...] = _dot(dyb, wpb_ref[...], NT)

    row = pl.BlockSpec((tm, d), lambda i: (i, 0))
    wide = pl.BlockSpec((tm, 2 * d), lambda i: (i, 0))
    mat = pl.BlockSpec((d, d), lambda i: (0, 0))
    return pl.pallas_call(
        body, name=name, grid=(t // tm,), in_specs=[row, row, row] + _gate_specs(tm) + [mat, mat, mat],
        out_specs=[row, row, wide, row, row],
        out_shape=[jax.ShapeDtypeStruct((t, d), BF16), jax.ShapeDtypeStruct((t, d), BF16),
                   jax.ShapeDtypeStruct((t, 2 * d), BF16), jax.ShapeDtypeStruct((t, d), F32),
                   jax.ShapeDtypeStruct((t, d), F32)],
        compiler_params=_cp("parallel"))(dx1, ya, yb, proj, proj, proj, proj, w_pa, w_pb, w_o)


def _swiglu(g, u):
    return g * _sigmoid(g) * u


def _swiglu_bwd(g, u, dact):
    sg = _sigmoid(g)
    gs = g * sg
    return dact * u * (sg + gs * (1.0 - sg)), dact * gs


def _ffn_up_fwd(h2, wgu_t, name, tm=512):
    t, d = h2.shape
    tm = min(tm, t)
    fh = FFN_HIDDEN // 2

    def body(a_ref, w_ref, gu_ref, act_ref):
        r = _dot(a_ref[...], w_ref[...], NT)
        gu_ref[...] = r.astype(BF16)
        act_ref[...] = _swiglu(r[:, :fh], r[:, fh:]).astype(BF16)

    return pl.pallas_call(
        body, name=name, grid=(2, t // tm),
        in_specs=[pl.BlockSpec((tm, d), lambda j, i: (i, 0)), pl.BlockSpec((2 * fh, d), lambda j, i: (j, 0))],
        out_specs=[pl.BlockSpec((tm, 2 * fh), lambda j, i: (i, j)), pl.BlockSpec((tm, fh), lambda j, i: (i, j))],
        out_shape=[jax.ShapeDtypeStruct((t, 4 * fh), BF16), jax.ShapeDtypeStruct((t, 2 * fh), BF16)],
        compiler_params=_cp("parallel", "parallel"))(h2, wgu_t)


def _ffn_down_bwd(dx, w_d, gu, name, tm=512):
    t, d = dx.shape
    tm = min(tm, t)
    fh = FFN_HIDDEN // 2

    def body(a_ref, w_ref, gu_ref, o_ref):
        dact = _dot(a_ref[...].astype(BF16), w_ref[...], NT)
        dg, du = _swiglu_bwd(gu_ref[:, :fh].astype(F32), gu_ref[:, fh:].astype(F32), dact)
        o_ref[:, :fh] = dg.astype(BF16)
        o_ref[:, fh:] = du.astype(BF16)

    wide = pl.BlockSpec((tm, 2 * fh), lambda j, i: (i, j))
    return pl.pallas_call(
        body, name=name, grid=(2, t // tm),
        in_specs=[pl.BlockSpec((tm, d), lambda j, i: (i, 0)), pl.BlockSpec((fh, d), lambda j, i: (j, 0)), wide],
        out_specs=wide,
        out_shape=jax.ShapeDtypeStruct((t, 4 * fh), BF16),
        compiler_params=_cp("parallel", "parallel"))(dx, w_d, gu)


def _rows_bwd(pieces, w, x, g, dres, name, tm=256):
    t, d = x.shape
    tm = min(tm, t)
    widths = [p.shape[1] for p in pieces]
    starts = [sum(widths[:i]) for i in range(len(widths))]
    assert sum(widths) == w.shape[0]
    n_p = len(pieces)

    def body(*refs):
        p_refs, (w_ref, x_ref, g_ref, dres_ref, dx_ref, dg_ref) = refs[:n_p], refs[n_p:]
        dh = _dot(p_refs[0][...], w_ref[pl.ds(starts[0], widths[0]), :], NN)
        for i in range(1, n_p):
            dh = dh + _dot(p_refs[i][...], w_ref[pl.ds(starts[i], widths[i]), :], NN)
        _, vjp = jax.vjp(_rms, x_ref[...], g_ref[...])
        dx, dg = vjp(dh)
        dx_ref[...] = dres_ref[...] + dx

        @pl.when(pl.program_id(0) == 0)
        def _():
            dg_ref[...] = jnp.zeros_like(dg_ref)

        dg_ref[...] += dg

    row = pl.BlockSpec((tm, d), lambda i: (i, 0))
    vec = pl.BlockSpec((1, d), lambda i: (0, 0))
    return pl.pallas_call(
        body, name=name, grid=(t // tm,),
        in_specs=[pl.BlockSpec((tm, k), lambda i: (i, 0)) for k in widths]
        + [pl.BlockSpec(w.shape, lambda i: (0, 0)), row, vec, row],
        out_specs=[row, vec],
        out_shape=[jax.ShapeDtypeStruct((t, d), F32), jax.ShapeDtypeStruct((1, d), F32)],
        compiler_params=_cp("arbitrary"))(*pieces, w, x, g, dres)


def _merge_fwd(o_g, o_att, proj, x, w_pa, w_pb, w_o, gain, name, tm=512):
    t, d = x.shape
    tm = min(tm, t)

    def body(og_ref, oa_ref, ga0, ga1, gb0, gb1, x_ref, wpa_ref, wpb_ref, wo_ref, g_ref,
             ya_ref, yb_ref, mix_ref, h2_ref, x1_ref):
        ya = _dot(og_ref[...], wpa_ref[...], NN)
        yb = _dot(oa_ref[...], wpb_ref[...], NN)
        ga = jnp.concatenate([ga0[...], ga1[...]], axis=1)
        gb = jnp.concatenate([gb0[...], gb1[...]], axis=1)
        mix = _mix(ya, yb, ga, gb).astype(BF16)
        ya_ref[...] = ya.astype(BF16)
        yb_ref[...] = yb.astype(BF16)
        mix_ref[...] = mix
        x1 = x_ref[...] + _dot(mix, wo_ref[...], NN)
        x1_ref[...] = x1
        h2_ref[...] = _rms(x1, g_ref[...]).astype(BF16)

    row = pl.BlockSpec((tm, d), lambda i: (i, 0))
    mat = pl.BlockSpec((d, d), lambda i: (0, 0))
    return pl.pallas_call(
        body, name=name, grid=(t // tm,),
        in_specs=[row, row] + _gate_specs(tm) + [row, mat, mat, mat, pl.BlockSpec((1, d), lambda i: (0, 0))],
        out_specs=[row] * 5,
        out_shape=[jax.ShapeDtypeStruct((t, d), BF16)] * 4 + [jax.ShapeDtypeStruct((t, d), F32)],
        compiler_params=_cp("parallel"))(o_g, o_att, proj, proj, proj, proj, x, w_pa, w_pb, w_o, gain)


def _loss_head(x, g, target, name, tm=512):
    t, d = x.shape
    tm = min(tm, t)

    def body(x_ref, g_ref, t_ref, dx_ref, dg_ref, loss_ref):
        tgt = t_ref[...]

        def f(xv, gv):
            err = _rms(xv, gv) - tgt
            return 0.5 * jnp.sum(jnp.mean(err * err, axis=-1, keepdims=True))

        loss, vjp = jax.vjp(f, x_ref[...], g_ref[...])
        dx, dg = vjp(jnp.ones((), F32))
        dx_ref[...] = dx

        @pl.when(pl.program_id(0) == 0)
        def _():
            dg_ref[...] = jnp.zeros_like(dg_ref)
            loss_ref[...] = jnp.zeros_like(loss_ref)

        dg_ref[...] += dg
        loss_ref[...] += jnp.full(loss_ref.shape, loss, F32)

    row = pl.BlockSpec((tm, d), lambda i: (i, 0))
    vec = pl.BlockSpec((1, d), lambda i: (0, 0))
    lane = pl.BlockSpec((1, 128), lambda i: (0, 0))
    return pl.pallas_call(
        body, name=name, grid=(t // tm,), in_specs=[row, vec, row], out_specs=[row, vec, lane],
        out_shape=[jax.ShapeDtypeStruct((t, d), F32), jax.ShapeDtypeStruct((1, d), F32),
                   jax.ShapeDtypeStruct((1, 128), F32)],
        compiler_params=_cp("arbitrary"))(x, g, target)


def _lb_rows(l0, l1, l2, l3):
    mx = jnp.maximum(jnp.maximum(l0, l1), jnp.maximum(l2, l3))
    e0, e1, e2, e3 = jnp.exp(l0 - mx), jnp.exp(l1 - mx), jnp.exp(l2 - mx), jnp.exp(l3 - mx)
    s = e0 + e1 + e2 + e3
    p0, p1, p2, p3 = e0 / s, e1 / s, e2 / s, e3 / s
    c1 = p0 + p1
    c2 = c1 + p2
    c3 = c2 + p3
    return p0 - p0, c1 - p0, c2 - p0, c3 - p0


def _lb_fwd(lb_logits):
    def body(l_ref, o_ref):
        rows = _lb_rows(*[l_ref[pl.ds(i, 1), :] for i in range(DEPTH)])
        for i in range(DEPTH):
            o_ref[pl.ds(i, 1), :] = rows[i]

    return pl.pallas_call(body, name="lb_fwd", out_shape=jax.ShapeDtypeStruct(lb_logits.shape, F32))(lb_logits)


def _lb_bwd(lb_logits, dlb):
    def body(l_ref, d_ref, o_ref):
        _, vjp = jax.vjp(_lb_rows, *[l_ref[pl.ds(i, 1), :] for i in range(DEPTH)])
        grads = vjp(tuple(d_ref[pl.ds(i, 1), :] for i in range(DEPTH)))
        for i in range(DEPTH):
            o_ref[pl.ds(i, 1), :] = grads[i]

    return pl.pallas_call(body, name="lb_bwd", out_shape=jax.ShapeDtypeStruct(lb_logits.shape, F32))(lb_logits, dlb)


MESH = pl.DeviceIdType.MESH
ANY = pl.BlockSpec(memory_space=pl.ANY)
N_KINDS = len(SHARD_ROWS)
FFN_HALF = FFN_HIDDEN // 2
KIND_PLACE = ((0, 0), (1, 0), (2, 0), (3, 0), (4, 0), (4, FFN_HALF), (5, 0))
KIND_HALF_SKIP = (0, 0, 0, 0, FFN_HALF, FFN_HALF, 0)
FULL_ROWS = (N_DEV * SHARD_ROWS[0], D_MODEL, D_MODEL, D_MODEL, 2 * N_DEV * SHARD_ROWS[4], N_DEV * SHARD_ROWS[6])


def _kind_rows(ti, dev):
    oi, base = KIND_PLACE[ti]
    start = base + dev * SHARD_ROWS[ti]
    if KIND_HALF_SKIP[ti]:
        start = start + (dev // (N_DEV // 2)) * KIND_HALF_SKIP[ti]
    return oi, pl.ds(start, SHARD_ROWS[ti])


def _position():
    x, y, c = lax.axis_index("x"), lax.axis_index("y"), lax.axis_index("c")
    return x, y, c, 4 * x + 2 * y + c


def _peer(x, y, c, r):
    px = 1 - x if r & 4 else x
    py = 1 - y if r & 2 else y
    pc = 1 - c if r & 1 else c
    return (px, py, pc), 4 * px + 2 * py + pc


class _Exchange(NamedTuple):
    operands: tuple
    out_shape: tuple
    copies: Callable
    n_local: int


EXCHANGE_SCRATCH = (pltpu.SemaphoreType.DMA((N_DEV, N_KINDS)), pltpu.SemaphoreType.DMA((N_DEV, N_KINDS)),
                    pltpu.SemaphoreType.DMA((N_KINDS,)))
ALL_KINDS = tuple(range(N_KINDS))
KINDS_W_IN = (0,)
KINDS_REST = ALL_KINDS[1:]


def _all_pairs(kinds, ends, send_sems, recv_sems):
    x, y, c, me = _position()
    out = []
    for r in range(1, N_DEV):
        peer, pid = _peer(x, y, c, r)
        for ti in kinds:
            src, dst = ends(ti, me, pid)
            out.append(pltpu.make_async_remote_copy(
                src_ref=src, dst_ref=dst, send_sem=send_sems.at[r, ti], recv_sem=recv_sems.at[r, ti],
                device_id=peer, device_id_type=MESH))
    return out


def _gather_exchange(shards, kinds):
    arrays = sorted({KIND_PLACE[ti][0] for ti in kinds})

    def copies(ins, outs, send_sems, recv_sems, local_sems, arrivals):
        src = dict(zip(kinds, ins))

        def window(ti, dev):
            oi, rows = _kind_rows(ti, dev)
            return outs[arrays.index(oi)].at[rows, :]

        if arrivals:
            return _all_pairs(kinds, lambda ti, me, pid: (src[ti], window(ti, pid)), send_sems, recv_sems)
        _, _, _, me = _position()
        local = [pltpu.make_async_copy(src[ti], window(ti, me), local_sems.at[ti]) for ti in kinds]
        return local + _all_pairs(kinds, lambda ti, me, pid: (src[ti], window(ti, me)), send_sems, recv_sems)

    return _Exchange(tuple(shards[ti] for ti in kinds),
                     tuple(jax.ShapeDtypeStruct((FULL_ROWS[oi], D_MODEL), BF16) for oi in arrays), copies, len(kinds))


def _scatter_exchange(grads, kinds):
    arrays = sorted({KIND_PLACE[ti][0] for ti in kinds})
    offsets, total = {}, 0
    for ti in kinds:
        offsets[ti], total = total, total + SHARD_ROWS[ti]

    def copies(ins, outs, send_sems, recv_sems, local_sems, arrivals):
        land = outs[0]

        def piece(ti, dev):
            ii, rows = _kind_rows(ti, dev)
            return ins[arrays.index(ii)].at[rows, :]

        def slot(ti, dev):
            return land.at[dev, pl.ds(offsets[ti], SHARD_ROWS[ti]), :]

        if arrivals:
            return _all_pairs(kinds, lambda ti, me, pid: (piece(ti, me), slot(ti, pid)), send_sems, recv_sems)
        _, _, _, me = _position()
        local = [pltpu.make_async_copy(piece(ti, me), slot(ti, me), local_sems.at[ti]) for ti in kinds]
        return local + _all_pairs(kinds, lambda ti, me, pid: (piece(ti, pid), slot(ti, me)), send_sems, recv_sems)

    return _Exchange(tuple(grads[oi] for oi in arrays), (jax.ShapeDtypeStruct((N_DEV, total, D_MODEL), BF16),),
                     copies, len(kinds))


def _exchange_start(ex, ins, outs, sems):
    for cp in ex.copies(ins, outs, *sems, False):
        cp.start()


def _exchange_finish(ex, ins, outs, sems):
    for cp in ex.copies(ins, outs, *sems, True):
        cp.wait_recv()
    mine = ex.copies(ins, outs, *sems, False)
    for cp in mine[:ex.n_local]:
        cp.wait()
    for cp in mine[ex.n_local:]:
        cp.wait_send()


def _run_exchange(ex, name):
    n_in, n_out = len(ex.operands), len(ex.out_shape)

    def body(*refs):
        ins, outs, sems = refs[:n_in], refs[n_in:n_in + n_out], refs[n_in + n_out:]
        _exchange_start(ex, ins, outs, sems)
        _exchange_finish(ex, ins, outs, sems)

    return pl.pallas_call(body, name=name, in_specs=[ANY] * n_in, out_specs=[ANY] * n_out,
                          out_shape=list(ex.out_shape), scratch_shapes=list(EXCHANGE_SCRATCH))(*ex.operands)


def _carry(body, ex, n_in, n_out, n_scratch, grid):
    if ex is None:
        return body
    e_in, e_out = len(ex.operands), len(ex.out_shape)

    def at(step):
        hit = pl.program_id(0) == step[0]
        for axis in range(1, len(grid)):
            hit = hit & (pl.program_id(axis) == step[axis])
        return hit

    def carrying(*refs):
        own_in, ex_in = refs[:n_in], refs[n_in:n_in + e_in]
        rest = refs[n_in + e_in:]
        own_out, ex_out = rest[:n_out], rest[n_out:n_out + e_out]
        own_scratch, sems = rest[n_out + e_out:n_out + e_out + n_scratch], rest[n_out + e_out + n_scratch:]

        @pl.when(at([0] * len(grid)))
        def _():
            _exchange_start(ex, ex_in, ex_out, sems)

        body(*own_in, *own_out, *own_scratch)

        @pl.when(at([g - 1 for g in grid]))
        def _():
            _exchange_finish(ex, ex_in, ex_out, sems)

    return carrying


def _carried(ex):
    if ex is None:
        return (), [], [], [], []
    return (ex.operands, [ANY] * len(ex.operands), list(ex.out_shape), [ANY] * len(ex.out_shape),
            list(EXCHANGE_SCRATCH))


def _small_sum_body(p_ref, o_ref, buf, send_sems, recv_sems):
    x, y, c, me = _position()
    buf[me] = p_ref[...]
    sends = []
    for r in range(1, N_DEV):
        peer, _ = _peer(x, y, c, r)
        sends.append(pltpu.make_async_remote_copy(
            src_ref=p_ref, dst_ref=buf.at[me], send_sem=send_sems.at[r], recv_sem=recv_sems.at[r],
            device_id=peer, device_id_type=MESH))
    for cp in sends:
        cp.start()
    for r in range(1, N_DEV):
        peer, pid = _peer(x, y, c, r)
        pltpu.make_async_remote_copy(
            src_ref=p_ref, dst_ref=buf.at[pid], send_sem=send_sems.at[r], recv_sem=recv_sems.at[r],
            device_id=peer, device_id_type=MESH).wait_recv()
    for cp in sends:
        cp.wait_send()
    acc = buf[0]
    for k in range(1, N_DEV):
        acc = acc + buf[k]
    o_ref[...] = acc


def _all_reduce_small(part):
    rows, d = part.shape
    vmem = pl.BlockSpec(memory_space=pltpu.VMEM)
    return pl.pallas_call(
        functools.partial(_small_sum_body), name="all_reduce_small", in_specs=[vmem], out_specs=vmem,
        out_shape=jax.ShapeDtypeStruct((rows, d), F32),
        scratch_shapes=[pltpu.VMEM((N_DEV, rows, d), F32), pltpu.SemaphoreType.DMA((N_DEV,)),
                        pltpu.SemaphoreType.DMA((N_DEV,))],
    )(part)


HG_PAIR = 2 * HG_DK


def _hg_consts():
    c = HG_CHUNK
    r = lax.broadcasted_iota(jnp.int32, (c, c), 0)
    s = lax.broadcasted_iota(jnp.int32, (c, c), 1)
    r2 = lax.broadcasted_iota(jnp.int32, (c, 2 * c), 0)
    s2 = lax.broadcasted_iota(jnp.int32, (c, 2 * c), 1)
    causal2 = jnp.where(s2 >= c, s2 - c, s2) <= r2
    lane_hi = lax.broadcasted_iota(jnp.int32, (c, HG_PAIR), 1) >= HG_DK
    same_head = ((lax.broadcasted_iota(jnp.int32, (HG_PAIR, HG_PAIR), 0) >= HG_DK)
                 == (lax.broadcasted_iota(jnp.int32, (HG_PAIR, HG_PAIR), 1) >= HG_DK))
    return (s <= r).astype(BF16), (s >= r).astype(BF16), causal2, lane_hi, same_head


def _head_rows(x, lane_hi):
    zero = jnp.zeros_like(x)
    return jnp.concatenate([jnp.where(lane_hi, zero, x), jnp.where(lane_hi, x, zero)], axis=0)


def _own_rows(y, lane_hi):
    return jnp.where(lane_hi, y[HG_CHUNK:], y[:HG_CHUNK])


def _split3(x):
    hi = x.astype(BF16)
    r1 = x - hi.astype(F32)
    mid = r1.astype(BF16)
    lo = (r1 - mid.astype(F32)).astype(BF16)
    return jnp.concatenate([hi, mid, lo], axis=1)


def _cumsum_rows(tri, x):
    w = x.shape[1]
    y = _dot(tri, _split3(x), NN)
    return y[:, :w] + y[:, w:2 * w] + y[:, 2 * w:]


def _hg_chunk(zq, zf, lb, tril, b_ref):
    c = HG_CHUNK
    sq = _sigmoid(zq)
    q = zq * sq
    sg = _sigmoid(zf)
    f = lb + (1.0 - lb) * sg
    logf = jnp.log(jnp.maximum(f, MIN_F))
    k = 1.0 - f
    b = _cumsum_rows(tril, logf)
    b_ref[...] = b
    mid = b_ref[pl.ds(c // 2 - 1, 1), :]
    bc = b_ref[pl.ds(c - 1, 1), :]
    em = jnp.exp(jnp.minimum(b - mid, HG_EXP_CLAMP))
    en = jnp.exp(jnp.minimum(mid - b, HG_EXP_CLAMP))
    return sq, q, sg, f, k, b, em, en, bc


def _hg_gate(o, zg, gn):
    return o * lax.rsqrt(jnp.mean(o * o, axis=-1, keepdims=True) + EPS) * gn * (zg * _sigmoid(zg))


def _hgrn2_fwd(proj, lb, gn, name, ex=None):
    t = proj.shape[0]
    bs_tok = min(HG_BLOCK, t)
    n_chunks = bs_tok // HG_CHUNK
    w = HG_HEADS * HG_DK

    def body(hq_ref, hf_ref, hi_ref, hg_ref, lb_ref, gn_ref, o_ref, og_ref, sall_ref, st_ref, b_ref):
        @pl.when(pl.program_id(0) == 0)
        def _():
            st_ref[...] = jnp.zeros_like(st_ref)

        tril, _, causal2, lane_hi, same_head = _hg_consts()

        for ci in range(n_chunks):
            rows = pl.ds(ci * HG_CHUNK, HG_CHUNK)
            for p in range(HG_HEADS // 2):
                cols = slice(p * HG_PAIR, (p + 1) * HG_PAIR)
                v = hi_ref[rows, cols]
                zg = hg_ref[rows, cols]
                _, q, _, _, k, b, em, en, bc = _hg_chunk(hq_ref[rows, cols], hf_ref[rows, cols],
                                                         lb_ref[:, cols], tril, b_ref.at[ci, p])
                st0 = st_ref[p]
                sall_ref[ci, 2 * p] = st0[:HG_DK, :HG_DK]
                sall_ref[ci, 2 * p + 1] = st0[HG_DK:, HG_DK:]
                vb = v.astype(BF16)
                o = _dot((q * jnp.exp(b)).astype(BF16), st0.astype(BF16), NT)
                a = jnp.where(causal2, _dot((q * em).astype(BF16), _head_rows((k * en).astype(BF16), lane_hi), NT), 0.0)
                o = o + _dot(a.astype(BF16), _head_rows(vb, lane_hi), NN)
                kdec = (k * jnp.exp(bc - b)).astype(BF16)
                st_ref[p] = st0 * jnp.exp(bc) + jnp.where(same_head, _dot(vb, kdec, TN), 0.0)
                o_ref[rows, cols] = o
                for hh in range(2):
                    sl = slice(hh * HG_DK, (hh + 1) * HG_DK)
                    hcols = slice(p * HG_PAIR + hh * HG_DK, p * HG_PAIR + (hh + 1) * HG_DK)
                    og_ref[rows, hcols] = _hg_gate(o[:, sl], zg[:, sl], gn_ref[:, hcols]).astype(BF16)

    def col(j):
        return pl.BlockSpec((bs_tok, w), lambda n, j=j: (n, j))

    vec = pl.BlockSpec((1, w), lambda n: (0, 0))
    ex_in, ex_in_specs, ex_out, ex_out_specs, ex_scratch = _carried(ex)
    outs = pl.pallas_call(
        _carry(body, ex, 6, 3, 2, (t // bs_tok,)), name=name, grid=(t // bs_tok,),
        in_specs=[col(COL_HQ // w), col(COL_HF // w), col(COL_HI // w), col(COL_HG // w), vec, vec] + ex_in_specs,
        out_specs=[col(0), col(0),
                   pl.BlockSpec((n_chunks, HG_HEADS, HG_DK, HG_DK), lambda n: (n, 0, 0, 0))] + ex_out_specs,
        out_shape=[jax.ShapeDtypeStruct((t, w), F32), jax.ShapeDtypeStruct((t, w), BF16),
                   jax.ShapeDtypeStruct((t // HG_CHUNK, HG_HEADS, HG_DK, HG_DK), F32)] + ex_out,
        scratch_shapes=[pltpu.VMEM((HG_HEADS // 2, HG_PAIR, HG_PAIR), F32),
                        pltpu.VMEM((n_chunks, HG_HEADS // 2, HG_CHUNK, HG_PAIR), F32)] + ex_scratch,
        compiler_params=_cp("arbitrary"))(proj, proj, proj, proj, lb, gn, *ex_in)
    return outs[:3], outs[3:]


def _hgrn2_bwd(proj, lb, gn, o_hg, sall, dog, name, ex=None):
    t = proj.shape[0]
    bs_tok = min(HG_BLOCK, t)
    n_chunks = bs_tok // HG_CHUNK
    n_blocks = t // bs_tok
    w = HG_HEADS * HG_DK

    def body(hq_ref, hf_ref, hi_ref, hg_ref, lb_ref, gn_ref, o_ref, sall_ref, dog_ref,
             da_ref, dlb_ref, dgn_ref, dst_ref, b_ref):
        @pl.when(pl.program_id(0) == 0)
        def _():
            dst_ref[...] = jnp.zeros_like(dst_ref)
            dlb_ref[...] = jnp.zeros_like(dlb_ref)
            dgn_ref[...] = jnp.zeros_like(dgn_ref)

        tril, rev_tril, causal2, lane_hi, same_head = _hg_consts()
        zero_block = jnp.zeros((HG_DK, HG_DK), F32)

        for ci in reversed(range(n_chunks)):
            rows = pl.ds(ci * HG_CHUNK, HG_CHUNK)
            for p in range(HG_HEADS // 2):
                cols = slice(p * HG_PAIR, (p + 1) * HG_PAIR)
                zq = hq_ref[rows, cols]
                v = hi_ref[rows, cols]
                zg = hg_ref[rows, cols]
                lbv = lb_ref[:, cols]
                sq, q, sg, f, k, b, em, en, bc = _hg_chunk(zq, hf_ref[rows, cols], lbv, tril, b_ref.at[ci, p])
                st0 = jnp.concatenate([jnp.concatenate([sall_ref[ci, 2 * p], zero_block], axis=1),
                                       jnp.concatenate([zero_block, sall_ref[ci, 2 * p + 1]], axis=1)], axis=0)
                dst1 = dst_ref[p]
                vb = v.astype(BF16)
                eb = jnp.exp(b)
                qg = (q * eb).astype(BF16)
                qt = (q * em).astype(BF16)
                kref = (k * en).astype(BF16)
                ebcb = jnp.exp(bc - b)
                kdec = (k * ebcb).astype(BF16)
                ebc = jnp.exp(bc)
                st1 = st0 * ebc + jnp.where(same_head, _dot(vb, kdec, TN), 0.0)

                dos, dzgs, dgns = [], [], []
                for hh in range(2):
                    sl = slice(hh * HG_DK, (hh + 1) * HG_DK)
                    hcols = slice(p * HG_PAIR + hh * HG_DK, p * HG_PAIR + (hh + 1) * HG_DK)
                    _, gate_vjp = jax.vjp(_hg_gate, o_ref[rows, hcols], zg[:, sl], gn_ref[:, hcols])
                    do_h, dzg_h, dgn_h = gate_vjp(dog_ref[rows, hcols])
                    dos.append(do_h)
                    dzgs.append(dzg_h)
                    dgns.append(dgn_h)
                dob = jnp.concatenate(dos, axis=1).astype(BF16)
                vrows, krows = _head_rows(vb, lane_hi), _head_rows(kref, lane_hi)
                dam = jnp.where(causal2, _dot(dob, vrows, NT), 0.0).astype(BF16)
                a = jnp.where(causal2, _dot(qt, krows, NT), 0.0)
                dk = ebcb * _dot(vb, dst1.astype(BF16), NN) + en * _own_rows(_dot(dam, qt, TN), lane_hi)
                dq = eb * _dot(dob, st0.astype(BF16), NN) + em * _dot(dam, krows, NN)
                dv = _own_rows(_dot(a.astype(BF16), dob, TN), lane_hi) + _dot(kdec, dst1.astype(BF16), NT)
                dst_ref[p] = dst1 * ebc + jnp.where(same_head, _dot(dob, qg, TN), 0.0)

                dbx = jnp.sum(dst1 * st1, axis=0, keepdims=True)
                dlogf = _cumsum_rows(rev_tril, q * dq - k * dk) + dbx
                df = jnp.where(f > MIN_F, dlogf / f, 0.0) - dk
                dzf = df * (1.0 - lbv) * sg * (1.0 - sg)
                dzq = dq * (sq * (1.0 + zq * (1.0 - sq)))
                da_ref[rows, pl.ds(COL_HQ + p * HG_PAIR, HG_PAIR)] = dzq.astype(BF16)
                da_ref[rows, pl.ds(COL_HF + p * HG_PAIR, HG_PAIR)] = dzf.astype(BF16)
                da_ref[rows, pl.ds(COL_HI + p * HG_PAIR, HG_PAIR)] = dv.astype(BF16)
                da_ref[rows, pl.ds(COL_HG + p * HG_PAIR, HG_PAIR)] = jnp.concatenate(dzgs, axis=1).astype(BF16)
                dlb_ref[:, cols] += jnp.sum(df * (1.0 - sg), axis=0, keepdims=True)
                dgn_ref[:, cols] += jnp.concatenate(dgns, axis=1)

    def col(j):
        return pl.BlockSpec((bs_tok, w), lambda n, j=j: (n_blocks - 1 - n, j))

    vec = pl.BlockSpec((1, w), lambda n: (0, 0))
    ex_in, ex_in_specs, ex_out, ex_out_specs, ex_scratch = _carried(ex)
    outs = pl.pallas_call(
        _carry(body, ex, 9, 3, 2, (n_blocks,)), name=name, grid=(n_blocks,),
        in_specs=[col(COL_HQ // w), col(COL_HF // w), col(COL_HI // w), col(COL_HG // w), vec, vec, col(0),
                  pl.BlockSpec((n_chunks, HG_HEADS, HG_DK, HG_DK), lambda n: (n_blocks - 1 - n, 0, 0, 0)),
                  col(0)] + ex_in_specs,
        out_specs=[pl.BlockSpec((bs_tok, 4 * w), lambda n: (n_blocks - 1 - n, 0)), vec, vec] + ex_out_specs,
        out_shape=[jax.ShapeDtypeStruct((t, 4 * w), BF16), jax.ShapeDtypeStruct((1, w), F32),
                   jax.ShapeDtypeStruct((1, w), F32)] + ex_out,
        scratch_shapes=[pltpu.VMEM((HG_HEADS // 2, HG_PAIR, HG_PAIR), F32),
                        pltpu.VMEM((n_chunks, HG_HEADS // 2, HG_CHUNK, HG_PAIR), F32)] + ex_scratch,
        compiler_params=_cp("arbitrary"))(proj, proj, proj, proj, lb, gn, o_hg, sall, dog, *ex_in)
    return outs[:3], outs[3:]


def _rope_tables(t):
    half = ROPE_DIM // 2
    inv = ROPE_THETA ** (-jnp.arange(half, dtype=F32) * 2.0 / ROPE_DIM)
    ang = jnp.arange(t).astype(F32)[:, None] * inv[None, :]
    cos, sin = jnp.cos(ang), jnp.sin(ang)
    pad = ATT_HEAD_DIM - ROPE_DIM
    c = jnp.concatenate([cos, cos, jnp.ones((t, pad), F32)], axis=1)
    su = jnp.concatenate([-sin, jnp.zeros((t, half + pad), F32)], axis=1)
    sd = jnp.concatenate([jnp.zeros((t, half), F32), sin, jnp.zeros((t, pad), F32)], axis=1)
    return tuple(jnp.concatenate([m, m], axis=1) for m in (c, su, sd))


def _rope(x, tabs):
    c, su, sd = tabs
    n = x.shape[1]
    half = ROPE_DIM // 2
    return x * c + pltpu.roll(x, n - half, 1) * su + pltpu.roll(x, half, 1) * sd


def _rope_t(dy, tabs):
    c, su, sd = tabs
    n = dy.shape[1]
    half = ROPE_DIM // 2
    return dy * c + pltpu.roll(dy * su, half, 1) + pltpu.roll(dy * sd, n - half, 1)


def _swa_specs(n_blocks, clamp):
    blk = ATT_BLOCK

    def cur(n):
        return jnp.minimum(n, n_blocks - 1) if clamp else n

    def prev(n):
        return jnp.maximum(cur(n) - 1, 0)

    q_spec = pl.BlockSpec((blk, 512), lambda m, n: (cur(n), COL_AQ // 512 + m))
    kv = [pl.BlockSpec((blk, 128), lambda m, n, c=c, f=f: (f(n), c + m))
          for c in (COL_AK // 128, COL_AV // 128) for f in (cur, prev)]
    tabs = [pl.BlockSpec((blk, 128), lambda m, n, f=f: (f(n), 0)) for f in (cur, prev) for _ in range(3)]
    return q_spec, kv, tabs, cur, prev


ATT_SCALE = ATT_HEAD_DIM ** -0.5


def _head_halves(x, upper):
    zero = jnp.zeros_like(x)
    return jnp.concatenate([jnp.where(upper, zero, x), jnp.where(upper, x, zero)], axis=0)


def _swa_scores(scores, sink, mask):
    s = jnp.where(mask, scores, -jnp.inf)
    mx = jnp.maximum(jnp.max(s, axis=-1, keepdims=True), sink)
    p = jnp.exp(s - mx)
    es = jnp.exp(sink - mx)
    rinv = 1.0 / (jnp.sum(p, axis=-1, keepdims=True) + es)
    return p * rinv, es * rinv


def _swa_window(kc_ref, kp_ref, vc_ref, vp_ref, tabs_c, tabs_p, n):
    k2 = jnp.concatenate([_rope(kp_ref[...], tabs_p), _rope(kc_ref[...], tabs_c)], axis=0)
    v2 = jnp.concatenate([vp_ref[...], vc_ref[...]], axis=0)
    blk = ATT_BLOCK
    qi = lax.broadcasted_iota(jnp.int32, (blk, 2 * blk), 0)
    kj = lax.broadcasted_iota(jnp.int32, (blk, 2 * blk), 1)
    delta = qi + blk - kj
    mask = (delta >= 0) & (delta < blk) & ((kj >= blk) | (n > 0))
    return k2, v2, mask


def _swa_fwd(proj, sinks, tabs, name, ex=None):
    t = proj.shape[0]
    n_blocks = t // ATT_BLOCK
    q_spec, kv_specs, tab_specs, _, _ = _swa_specs(n_blocks, clamp=False)

    def body(q_ref, kc_ref, kp_ref, vc_ref, vp_ref, c0, c1, c2, p0, p1, p2, sink_ref, o_ref):
        m, n = pl.program_id(0), pl.program_id(1)
        tabs_c = (c0[...], c1[...], c2[...])
        tabs_p = (p0[...], p1[...], p2[...])
        k2, v2, mask = _swa_window(kc_ref, kp_ref, vc_ref, vp_ref, tabs_c, tabs_p, n)
        k2r, v2r = pltpu.roll(k2, 64, 1), pltpu.roll(v2, 64, 1)
        upper_k = lax.broadcasted_iota(jnp.int32, k2.shape, 1) >= 64
        upper_q = lax.broadcasted_iota(jnp.int32, (ATT_BLOCK, 128), 1) >= 64
        for jj in range(2):
            own = upper_k if jj else ~upper_k
            kd = jnp.where(own, k2, k2r).astype(BF16)
            vd = jnp.where(own, v2, v2r).astype(BF16)
            for pi in range(2):
                cols = slice(256 * jj + 128 * pi, 256 * jj + 128 * pi + 128)
                qp = _rope(q_ref[:, cols], tabs_c) * ATT_SCALE
                outs = []
                for e in range(2):
                    sink = sink_ref[0, 8 * m + 4 * jj + 2 * pi + e]
                    qm = jnp.where(upper_q if e else ~upper_q, qp, 0.0).astype(BF16)
                    pn, _ = _swa_scores(_dot(qm, kd, NT), sink, mask)
                    outs.append(_dot(pn.astype(BF16), vd, NN))
                o_ref[:, cols] = jnp.where(upper_q, outs[1], outs[0]).astype(BF16)

    ex_in, ex_in_specs, ex_out, ex_out_specs, ex_scratch = _carried(ex)
    outs = pl.pallas_call(
        _carry(body, ex, 12, 1, 0, (2, n_blocks)), name=name, grid=(2, n_blocks),
        in_specs=[q_spec] + kv_specs + tab_specs + [pl.BlockSpec(memory_space=pltpu.SMEM)] + ex_in_specs,
        out_specs=[pl.BlockSpec((ATT_BLOCK, 512), lambda m, n: (n, m))] + ex_out_specs,
        out_shape=[jax.ShapeDtypeStruct((t, ATT_Q_HEADS * ATT_HEAD_DIM), BF16)] + ex_out,
        scratch_shapes=ex_scratch,
        compiler_params=_cp("arbitrary", "arbitrary"))(proj, proj, proj, proj, proj, *tabs, *tabs, sinks, *ex_in)
    return outs[0], outs[1:]


def _swa_bwd(proj, sinks, tabs, o_att, do_att, name, ex=None):
    t = proj.shape[0]
    n_blocks = t // ATT_BLOCK
    blk = ATT_BLOCK
    q_spec, kv_specs, tab_specs, cur, prev = _swa_specs(n_blocks, clamp=True)

    def body(q_ref, kc_ref, kp_ref, vc_ref, vp_ref, c0, c1, c2, p0, p1, p2, sink_ref, o_ref, do_ref,
             dq_ref, dk_ref, dv_ref, ds_ref, ck_ref, cv_ref):
        m, n = pl.program_id(0), pl.program_id(1)

        @pl.when(n == 0)
        def _():
            ds_ref[...] = jnp.zeros_like(ds_ref)
            ck_ref[...] = jnp.zeros_like(ck_ref)
            cv_ref[...] = jnp.zeros_like(cv_ref)

        @pl.when(n < n_blocks)
        def _():
            tabs_c = (c0[...], c1[...], c2[...])
            tabs_p = (p0[...], p1[...], p2[...])
            k2, v2, mask = _swa_window(kc_ref, kp_ref, vc_ref, vp_ref, tabs_c, tabs_p, n)
            k2r, v2r = pltpu.roll(k2, 64, 1), pltpu.roll(v2, 64, 1)
            upper_k = lax.broadcasted_iota(jnp.int32, k2.shape, 1) >= 64
            upper_q = lax.broadcasted_iota(jnp.int32, (blk, 128), 1) >= 64
            lane = lax.broadcasted_iota(jnp.int32, (8, 128), 1)
            dk2 = jnp.zeros(k2.shape, F32)
            dv2 = jnp.zeros(k2.shape, F32)
            dsv = jnp.zeros((8, 128), F32)
            nk = 2 * blk
            for jj in range(2):
                own = upper_k if jj else ~upper_k
                kh = _head_halves(jnp.where(own, k2, k2r).astype(BF16), upper_k)
                vh = _head_halves(jnp.where(own, v2, v2r).astype(BF16), upper_k)
                dkd = jnp.zeros(k2.shape, F32)
                dvd = jnp.zeros(k2.shape, F32)
                for pi in range(2):
                    cols = slice(256 * jj + 128 * pi, 256 * jj + 128 * pi + 128)
                    qp = (_rope(q_ref[:, cols], tabs_c) * ATT_SCALE).astype(BF16)
                    do_pair = do_ref[:, cols]
                    o_pair = o_ref[:, cols].astype(F32)
                    dob = do_pair.astype(BF16)
                    s = _dot(qp, kh, NT)
                    dp = _dot(dob, vh, NT)
                    pns, dss = [], []
                    for e in range(2):
                        hl = 4 * jj + 2 * pi + e
                        pn, ps = _swa_scores(s[:, e * nk:(e + 1) * nk], sink_ref[0, 8 * m + hl], mask)
                        delta = jnp.sum(jnp.where(upper_q if e else ~upper_q, do_pair * o_pair, 0.0), axis=-1, keepdims=True)
                        pns.append(pn.astype(BF16))
                        dss.append((pn * (dp[:, e * nk:(e + 1) * nk] - delta)).astype(BF16))
                        dsv = dsv + jnp.where(lane == hl, -jnp.sum(ps * delta), 0.0)
                    dsb = jnp.concatenate(dss, axis=1)
                    dq_ref[:, cols] = _rope_t(_dot(dsb, kh, NN) * ATT_SCALE, tabs_c).astype(BF16)
                    rk = _dot(dsb, qp, TN)
                    rv = _dot(jnp.concatenate(pns, axis=1), dob, TN)
                    dkd = dkd + jnp.where(upper_k, rk[nk:], rk[:nk])
                    dvd = dvd + jnp.where(upper_k, rv[nk:], rv[:nk])
                dk2 = dk2 + jnp.where(own, dkd + pltpu.roll(dkd, 64, 1), 0.0)
                dv2 = dv2 + jnp.where(own, dvd + pltpu.roll(dvd, 64, 1), 0.0)
            dk_ref[...] = (ck_ref[...] + _rope_t(dk2[:blk], tabs_p)).astype(BF16)
            dv_ref[...] = (cv_ref[...] + dv2[:blk]).astype(BF16)
            ck_ref[...] = _rope_t(dk2[blk:], tabs_c)
            cv_ref[...] = dv2[blk:]
            ds_ref[...] += dsv

        @pl.when(n == n_blocks)
        def _():
            dk_ref[...] = ck_ref[...].astype(BF16)
            dv_ref[...] = cv_ref[...].astype(BF16)

    wide = pl.BlockSpec((blk, 512), lambda m, n: (cur(n), m))
    lagged = pl.BlockSpec((blk, 128), lambda m, n: (jnp.maximum(n - 1, 0), m))
    ex_in, ex_in_specs, ex_out, ex_out_specs, ex_scratch = _carried(ex)
    outs = pl.pallas_call(
        _carry(body, ex, 14, 4, 2, (2, n_blocks + 1)), name=name, grid=(2, n_blocks + 1),
        in_specs=[q_spec] + kv_specs + tab_specs + [pl.BlockSpec(memory_space=pltpu.SMEM), wide, wide] + ex_in_specs,
        out_specs=[wide, lagged, lagged, pl.BlockSpec((None, 8, 128), lambda m, n: (m, 0, 0))] + ex_out_specs,
        out_shape=[jax.ShapeDtypeStruct((t, 1024), BF16), jax.ShapeDtypeStruct((t, 256), BF16),
                   jax.ShapeDtypeStruct((t, 256), BF16), jax.ShapeDtypeStruct((2, 8, 128), F32)] + ex_out,
        scratch_shapes=[pltpu.VMEM((blk, 128), F32), pltpu.VMEM((blk, 128), F32)] + ex_scratch,
        compiler_params=_cp("arbitrary", "arbitrary"))(proj, proj, proj, proj, proj, *tabs, *tabs, sinks, o_att, do_att,
                                                       *ex_in)
    return outs[:4], outs[4:]


def _local_step(x, target, shards, norm1, lb_logits, hg_norm, attn_sinks, norm2, final_norm):
    t = x.shape[0]
    tabs = _rope_tables(t)
    lb_all = _lb_fwd(lb_logits)
    saved = []

    def shards_of(l):
        return {ti: shards[ti][l] for ti in ALL_KINDS}

    win_next = _run_exchange(_gather_exchange(shards_of(0), KINDS_W_IN), "gather_w_in")
    rest_next = None
    for l in range(DEPTH):
        n1, n2 = norm1[l][None, :], norm2[l][None, :]
        lb, gn, sinks = lb_all[l][None, :], hg_norm[l][None, :], attn_sinks[l][None, :]
        (win_t,) = win_next
        if l == 0:
            h = _rms_fwd(x, n1, "rms1_fwd")
        if l == 0:
            proj, rest_next = _matmul_nt(h, win_t, 0, IN_COLS, F32, "proj_fwd", tn=1280,
                                         ex=_gather_exchange(shards_of(0), KINDS_REST))
        else:
            proj = _matmul_nt(h, win_t, 0, IN_COLS, F32, "proj_fwd", tn=1280)
        w_pa, w_pb, w_o, wgu_t, w_d = rest_next
        more = l + 1 < DEPTH
        (o_hg, o_g, sall), rest_next = _hgrn2_fwd(
            proj, lb, gn, "hgrn2_fwd", _gather_exchange(shards_of(l + 1), KINDS_REST) if more else None)
        o_att, win_next = _swa_fwd(
            proj, sinks, tabs, "swa_fwd", _gather_exchange(shards_of(l + 1), KINDS_W_IN) if more else None)
        ya, yb, mix, h2, x1 = _merge_fwd(o_g, o_att, proj, x, w_pa, w_pb, w_o, n2, "merge_fwd")
        gu, act = _ffn_up_fwd(h2, wgu_t, "ffn_up_fwd")
        if more:
            x2, h_next = _matmul_nn(act, w_d, 0, x1, "wd_fwd", gain=norm1[l + 1][None, :])
        else:
            x2, h_next = _matmul_nn(act, w_d, 0, x1, "wd_fwd_last"), None
        saved.append((x, h, proj, o_hg, o_g, sall, o_att, ya, yb, mix, x1, h2, gu, act, n1, n2, lb, gn, sinks,
                      (win_t, w_pa, w_pb, w_o, wgu_t, w_d)))
        x, h = x2, h_next

    dx, d_fn, loss = _loss_head(x, final_norm[None, :], target, "loss_head")

    owned = [None] * DEPTH
    pending = None
    d_n1, d_n2, d_lb, d_gn, d_sinks = ([None] * DEPTH for _ in range(5))
    for l in reversed(range(DEPTH)):
        x0, h, proj, o_hg, o_g, sall, o_att, ya, yb, mix, x1, h2, gu, act, n1, n2, lb, gn, sinks, weights = saved[l]
        win_t, w_pa, w_pb, w_o, wgu_t, w_d = weights
        dgu = _ffn_down_bwd(dx, w_d, gu, "ffn_down_bwd")
        g_wd = _matmul_tn(act, dx, "wd_grad", tm=1408)
        g_wgu = _matmul_tn(dgu, h2, "wgu_grad", tm=1408)
        dx1, d_n2[l] = _rows_bwd([dgu], wgu_t, x1, n2, dx, "ffn_up_bwd", tm=512)
        g_wo = _matmul_tn(mix, dx1, "wo_grad")
        dya, dyb, dgab, dog, doatt = _merge_bwd(dx1, ya, yb, proj, w_pa, w_pb, w_o, "merge_bwd")
        g_wpa = _matmul_tn(o_g, dya, "wpa_grad")
        g_wpb = _matmul_tn(o_att, dyb, "wpb_grad")
        ex = _scatter_exchange(pending, ALL_KINDS) if pending is not None else None
        (dhg, d_lb[l], d_gn[l]), land = _hgrn2_bwd(proj, lb, gn, o_hg, sall, dog, "hgrn2_bwd", ex)
        if pending is not None:
            owned[l + 1] = _sum_slots(land[0], "sum_slots")
        ex = _scatter_exchange((None, g_wpa, g_wpb, g_wo, g_wgu, g_wd), KINDS_REST) if l == 0 else None
        (daq, dak, dav, d_sinks[l]), land_rest = _swa_bwd(proj, sinks, tabs, o_att, doatt, "swa_bwd", ex)
        g_win = None
        for piece, off, tm, tag in ((dhg, COL_HQ, 512, "hg"), (daq, COL_AQ, 512, "aq"), (dak, COL_AK, 256, "ak"),
                                    (dav, COL_AV, 256, "av"), (dgab, COL_GA, 512, "gates")):
            g_win = _matmul_tn(piece, h, "win_grad_" + tag, tm=tm, rows=IN_COLS, row_off=off, into=g_win)
        dx, d_n1[l] = _rows_bwd([dhg, daq, dak, dav, dgab], win_t, x0, n1, dx1, "win_bwd")
        pending = (g_win, g_wpa, g_wpb, g_wo, g_wgu, g_wd)
    land_win = _run_exchange(_scatter_exchange(pending, KINDS_W_IN), "scatter_w_in")
    owned[0] = jnp.concatenate([_sum_slots(land_win[0], "sum_slots_w_in"), _sum_slots(land_rest[0], "sum_slots_rest")],
                               axis=0)

    d_sink_rows = [jnp.concatenate([d[0, 0, :8], d[1, 0, :8]]) for d in d_sinks]
    small = (jnp.concatenate(d_n1, axis=0), jnp.concatenate(d_lb, axis=0), jnp.concatenate(d_gn, axis=0),
             jnp.concatenate(d_n2, axis=0), d_fn, jnp.stack(d_sink_rows, axis=0))
    return loss, dx, jnp.stack(owned, axis=0), small


def _sum_slots(land, name, tr=480):
    _, rows, d = land.shape

    def body(l_ref, o_ref):
        acc = l_ref[0].astype(F32)
        for k in range(1, N_DEV):
            acc = acc + l_ref[k].astype(F32)
        o_ref[...] = acc

    return pl.pallas_call(
        body, name=name, grid=(rows // tr,),
        in_specs=[pl.BlockSpec((N_DEV, tr, d), lambda i: (0, i, 0))],
        out_specs=pl.BlockSpec((tr, d), lambda i: (i, 0)),
        out_shape=jax.ShapeDtypeStruct((rows, d), F32),
        compiler_params=_cp("parallel"))(land)


def _adamw(w, g, m, v, name):
    shape = w.shape
    c = shape[-1]
    rows = w.size // c
    tr = rows
    for cand in (512, 352, 128):
        if rows % cand == 0:
            tr = cand
            break
    c1 = 1.0 / (1.0 - ADAM_B1 ** ADAM_STEP)
    c2 = 1.0 / (1.0 - ADAM_B2 ** ADAM_STEP)

    def body(w_ref, g_ref, m_ref, v_ref, d_ref, nm_ref, nv_ref):
        gv = g_ref[...]
        nm = ADAM_B1 * m_ref[...] + (1.0 - ADAM_B1) * gv
        nv = ADAM_B2 * v_ref[...] + (1.0 - ADAM_B2) * (gv * gv)
        d_ref[...] = -ADAM_LR * ((nm * c1) / (jnp.sqrt(nv * c2) + ADAM_EPS) + ADAM_WD * w_ref[...])
        nm_ref[...] = nm
        nv_ref[...] = nv

    spec = pl.BlockSpec((tr, c), lambda i: (i, 0))
    outs = pl.pallas_call(
        body, name=name, grid=(rows // tr,), in_specs=[spec] * 4, out_specs=[spec] * 3,
        out_shape=[jax.ShapeDtypeStruct((rows, c), F32)] * 3,
        compiler_params=_cp("parallel"))(*[a.reshape(rows, c) for a in (w, g, m, v)])
    return tuple(o.reshape(shape) for o in outs)


def kernel(x, norm1, w_in, lb_logits, hg_norm, attn_sinks, w_pa, w_pb, w_o, norm2, w_gate, w_up, w_down, final_norm, loss_target, m_norm1, m_w_in, m_lb_logits, m_hg_norm, m_attn_sinks, m_w_pa, m_w_pb, m_w_o, m_norm2, m_w_gate, m_w_up, m_w_down, m_final_norm, v_norm1, v_w_in, v_lb_logits, v_hg_norm, v_attn_sinks, v_w_pa, v_w_pb, v_w_o, v_norm2, v_w_gate, v_w_up, v_w_down, v_final_norm):
    t = x.shape[1]
    shards = [jnp.swapaxes(w_in, 1, 2).astype(BF16), w_pa.astype(BF16), w_pb.astype(BF16), w_o.astype(BF16),
              jnp.swapaxes(w_gate, 1, 2).astype(BF16), jnp.swapaxes(w_up, 1, 2).astype(BF16), w_down.astype(BF16)]
    loss_lanes, grad_x, owned, small = _local_step(
        x.reshape(t, D_MODEL), loss_target.reshape(t, D_MODEL), shards,
        norm1, lb_logits, hg_norm, attn_sinks, norm2, final_norm)

    def rows_of(ti, transpose):
        g = owned[:, SLOT_OFF[ti]:SLOT_OFF[ti] + SHARD_ROWS[ti], :]
        return jnp.swapaxes(g, 1, 2) if transpose else g

    g_big = {"w_in": rows_of(0, True), "w_pa": rows_of(1, False), "w_pb": rows_of(2, False), "w_o": rows_of(3, False),
             "w_gate": rows_of(4, True), "w_up": rows_of(5, True), "w_down": rows_of(6, False)}

    d_n1, d_lb, d_gn, d_n2, d_fn, d_sinks = small
    pad = jnp.zeros((DEPTH, D_MODEL - ATT_Q_HEADS), F32)
    packed = jnp.concatenate([
        d_n1, d_lb, d_gn, d_n2, d_fn, jnp.concatenate([d_sinks, pad], axis=1),
        jnp.concatenate([loss_lanes, jnp.zeros((1, D_MODEL - 128), F32)], axis=1),
        jnp.zeros((SMALL_ROWS - 22, D_MODEL), F32)], axis=0)
    total = _all_reduce_small(packed)
    loss = total[21, 0]
    g_small = {"norm1": total[0:4], "lb_logits": _lb_bwd(lb_logits, total[4:8]), "hg_norm": total[8:12],
               "norm2": total[12:16], "final_norm": total[16], "attn_sinks": total[17:21, :ATT_Q_HEADS]}

    params = {"norm1": (norm1, m_norm1, v_norm1), "w_in": (w_in, m_w_in, v_w_in),
              "lb_logits": (lb_logits, m_lb_logits, v_lb_logits), "hg_norm": (hg_norm, m_hg_norm, v_hg_norm),
              "attn_sinks": (attn_sinks, m_attn_sinks, v_attn_sinks), "w_pa": (w_pa, m_w_pa, v_w_pa),
              "w_pb": (w_pb, m_w_pb, v_w_pb), "w_o": (w_o, m_w_o, v_w_o), "norm2": (norm2, m_norm2, v_norm2),
              "w_gate": (w_gate, m_w_gate, v_w_gate), "w_up": (w_up, m_w_up, v_w_up),
              "w_down": (w_down, m_w_down, v_w_down), "final_norm": (final_norm, m_final_norm, v_final_norm)}
    order = ["norm1", "w_in", "lb_logits", "hg_norm", "attn_sinks", "w_pa", "w_pb", "w_o", "norm2",
             "w_gate", "w_up", "w_down", "final_norm"]
    grads, deltas, new_m, new_v = [], [], [], []
    for name in order:
        w, m, v = params[name]
        g = (g_big[name] if name in g_big else g_small[name]).reshape(w.shape)
        w2 = w.reshape(1, -1) if w.ndim == 1 else w
        d, nm, nv = _adamw(w2, g.reshape(w2.shape), m.reshape(w2.shape), v.reshape(w2.shape), "adamw_" + name)
        grads.append(g)
        deltas.append(d.reshape(w.shape))
        new_m.append(nm.reshape(w.shape))
        new_v.append(nv.reshape(w.shape))
    return (loss, grad_x.reshape(x.shape), *grads, *deltas, *new_m, *new_v)
```

```python
import functools
from typing import Callable, NamedTuple

import jax
import jax.numpy as jnp
from jax import lax
from jax.experimental import pallas as pl
from jax.experimental.pallas import tpu as pltpu

F32, BF16 = jnp.float32, jnp.bfloat16

D_MODEL = 1024
DEPTH = 4
N_DEV = 8
HG_HEADS = 8
HG_DK = 128
HG_CHUNK = 64
HG_BLOCK = 256
HG_EXP_CLAMP = 60.0
ATT_Q_HEADS = 16
ATT_HEAD_DIM = 64
ATT_BLOCK = 128
ROPE_THETA = 500000.0
ROPE_DIM = 16
FFN_HIDDEN = 2816
EPS = 1e-6
MIN_F = 1e-30
ADAM_LR, ADAM_B1, ADAM_B2, ADAM_EPS, ADAM_WD, ADAM_STEP = 0.001, 0.9, 0.999, 1e-08, 0.01, 10

COL_HQ, COL_HF, COL_HI, COL_HG = 0, 1024, 2048, 3072
COL_AQ, COL_AK, COL_AV, COL_GA, COL_GB = 4096, 5120, 5376, 5632, 6656
IN_COLS = 7680

SHARD_ROWS = (960, 128, 128, 128, 352, 352, 352)
SLOT_OFF = (0, 960, 1088, 1216, 1344, 1696, 2048)
SLOT_ROWS = 2400
SMALL_ROWS = 24

VMEM_LIMIT_BYTES = 56 * 1024 * 1024

NN = ((1,), (0,))
NT = ((1,), (1,))
TN = ((0,), (0,))


def _dot(a, b, dims):
    return lax.dot_general(a, b, (dims, ((), ())), preferred_element_type=F32)


def _cp(*sem):
    return pltpu.CompilerParams(dimension_semantics=sem if sem else None, vmem_limit_bytes=VMEM_LIMIT_BYTES)


def _sigmoid(x):
    return 0.5 * jnp.tanh(0.5 * x) + 0.5


def _matmul_nt(a, w, row_off, n, out_dtype, name, tm=1024, tn=512, ex=None):
    t, k = a.shape
    tm = min(tm, t)
    assert n % tn == 0 and row_off % tn == 0 and t % tm == 0
    grid = (n // tn, t // tm)

    def body(a_ref, w_ref, o_ref):
        o_ref[...] = _dot(a_ref[...].astype(BF16), w_ref[...], NT).astype(o_ref.dtype)

    ex_in, ex_in_specs, ex_out, ex_out_specs, ex_scratch = _carried(ex)
    outs = pl.pallas_call(
        _carry(body, ex, 2, 1, 0, grid), name=name, grid=grid,
        in_specs=[pl.BlockSpec((tm, k), lambda j, i: (i, 0)),
                  pl.BlockSpec((tn, k), lambda j, i: (row_off // tn + j, 0))] + ex_in_specs,
        out_specs=[pl.BlockSpec((tm, tn), lambda j, i: (i, j))] + ex_out_specs,
        out_shape=[jax.ShapeDtypeStruct((t, n), out_dtype)] + ex_out,
        scratch_shapes=ex_scratch,
        compiler_params=_cp("arbitrary", "arbitrary") if ex else _cp("parallel", "parallel"))(a, w, *ex_in)
    return (outs[0], outs[1:]) if ex else outs[0]


def _matmul_nn(a, w, row_off, res, name, tm=512, tk=None, gain=None):
    t, k = a.shape
    n = w.shape[1]
    tm = min(tm, t)
    tk = tk or k
    nk = k // tk
    assert k % tk == 0 and row_off % tk == 0 and t % tm == 0

    def body(*refs):
        refs = list(refs)
        a_ref, w_ref = refs[:2]
        r_ref = refs[2] if res is not None else None
        g_ref = refs[2 + (res is not None)] if gain is not None else None
        acc = refs[-1]
        o_ref = refs[-3] if gain is not None else refs[-2]
        kk = pl.program_id(1)
        part = _dot(a_ref[...].astype(BF16), w_ref[...], NN)

        @pl.when(kk == 0)
        def _():
            acc[...] = part

        @pl.when(kk > 0)
        def _():
            acc[...] += part

        @pl.when(kk == nk - 1)
        def _():
            y = acc[...] if res is None else acc[...] + r_ref[...]
            o_ref[...] = y
            if gain is not None:
                refs[-2][...] = _rms(y, g_ref[...]).astype(BF16)

    row = pl.BlockSpec((tm, n), lambda i, kk: (i, 0))
    in_specs = [pl.BlockSpec((tm, tk), lambda i, kk: (i, kk)),
                pl.BlockSpec((tk, n), lambda i, kk: (row_off // tk + kk, 0))]
    args = [a, w]
    if res is not None:
        in_specs.append(row)
        args.append(res)
    if gain is not None:
        in_specs.append(pl.BlockSpec((1, n), lambda i, kk: (0, 0)))
        args.append(gain)
    outs = pl.pallas_call(
        body, name=name, grid=(t // tm, nk), in_specs=in_specs,
        out_specs=[row, row] if gain is not None else [row],
        out_shape=[jax.ShapeDtypeStruct((t, n), F32)] + ([jax.ShapeDtypeStruct((t, n), BF16)] if gain is not None else []),
        scratch_shapes=[pltpu.VMEM((tm, n), F32)],
        compiler_params=_cp("parallel", "arbitrary"))(*args)
    return tuple(outs) if gain is not None else outs[0]


def _matmul_tn(a, b, name, tm=512, tk=2048, rows=None, row_off=0, into=None):
    t, m = a.shape
    n = b.shape[1]
    tk = min(tk, t)
    nk = t // tk
    rows = rows or m
    assert m % tm == 0 and t % tk == 0 and row_off % tm == 0

    def body(*refs):
        a_ref, b_ref, o_ref, acc = refs[0], refs[1], refs[-2], refs[-1]
        kk = pl.program_id(1)
        part = _dot(a_ref[...].astype(BF16), b_ref[...].astype(BF16), TN)

        @pl.when(kk == 0)
        def _():
            acc[...] = part

        @pl.when(kk > 0)
        def _():
            acc[...] += part

        @pl.when(kk == nk - 1)
        def _():
            o_ref[...] = acc[...].astype(BF16)

    return pl.pallas_call(
        body, name=name, grid=(m // tm, nk),
        in_specs=[pl.BlockSpec((tk, tm), lambda i, kk: (kk, i)),
                  pl.BlockSpec((tk, n), lambda i, kk: (kk, 0))] + ([ANY] if into is not None else []),
        out_specs=pl.BlockSpec((tm, n), lambda i, kk: (row_off // tm + i, 0)),
        out_shape=jax.ShapeDtypeStruct((rows, n), BF16),
        scratch_shapes=[pltpu.VMEM((tm, n), F32)],
        input_output_aliases={2: 0} if into is not None else {},
        compiler_params=_cp("parallel", "arbitrary"))(a, b, *([into] if into is not None else []))


def _rms(x, g):
    return x * lax.rsqrt(jnp.mean(x * x, axis=-1, keepdims=True) + EPS) * g


def _rms_fwd(x, g, name, tm=512):
    t, d = x.shape
    tm = min(tm, t)

    def body(x_ref, g_ref, o_ref):
        o_ref[...] = _rms(x_ref[...], g_ref[...]).astype(BF16)

    return pl.pallas_call(
        body, name=name, grid=(t // tm,),
        in_specs=[pl.BlockSpec((tm, d), lambda i: (i, 0)), pl.BlockSpec((1, d), lambda i: (0, 0))],
        out_specs=pl.BlockSpec((tm, d), lambda i: (i, 0)),
        out_shape=jax.ShapeDtypeStruct((t, d), BF16),
        compiler_params=_cp("parallel"))(x, g)


def _mix(ya, yb, ga, gb):
    return _sigmoid(ga) * ya + _sigmoid(gb) * yb


def _gate_specs(tm):
    half = D_MODEL // 2
    return [pl.BlockSpec((tm, half), lambda i, c=c: (i, c))
            for c in (COL_GA // half, COL_GA // half + 1, COL_GB // half, COL_GB // half + 1)]


def _merge_bwd(dx1, ya, yb, proj, w_pa, w_pb, w_o, name, tm=512):
    t, d = dx1.shape
    tm = min(tm, t)

    def body(dx_ref, ya_ref, yb_ref, ga0, ga1, gb0, gb1, wpa_ref, wpb_ref, wo_ref,
             dya_ref, dyb_ref, dg_ref, dog_ref, doa_ref):
        dmix = _dot(dx_ref[...].astype(BF16), wo_ref[...], NT)
        ga = jnp.concatenate([ga0[...], ga1[...]], axis=1)
        gb = jnp.concatenate([gb0[...], gb1[...]], axis=1)
        _, vjp = jax.vjp(_mix, ya_ref[...].astype(F32), yb_ref[...].astype(F32), ga, gb)
        dya, dyb, dga, dgb = vjp(dmix)
        dya, dyb = dya.astype(BF16), dyb.astype(BF16)
        dya_ref[...] = dya
        dyb_ref[...] = dyb
        dg_ref[:, :d] = dga.astype(BF16)
        dg_ref[:, d:] = dgb.astype(BF16)
        dog_ref[...] = _dot(dya, wpa_ref[...], NT)
        doa_ref[...] = _dot(dyb, wpb_ref[...], NT)

    row = pl.BlockSpec((tm, d), lambda i: (i, 0))
    wide = pl.BlockSpec((tm, 2 * d), lambda i: (i, 0))
    mat = pl.BlockSpec((d, d), lambda i: (0, 0))
    return pl.pallas_call(
        body, name=name, grid=(t // tm,), in_specs=[row, row, row] + _gate_specs(tm) + [mat, mat, mat],
        out_specs=[row, row, wide, row, row],
        out_shape=[jax.ShapeDtypeStruct((t, d), BF16), jax.ShapeDtypeStruct((t, d), BF16),
                   jax.ShapeDtypeStruct((t, 2 * d), BF16), jax.ShapeDtypeStruct((t, d), F32),
                   jax.ShapeDtypeStruct((t, d), F32)],
        compiler_params=_cp("parallel"))(dx1, ya, yb, proj, proj, proj, proj, w_pa, w_pb, w_o)


def _swiglu(g, u):
    return g * _sigmoid(g) * u


def _swiglu_bwd(g, u, dact):
    sg = _sigmoid(g)
    gs = g * sg
    return dact * u * (sg + gs * (1.0 - sg)), dact * gs


def _ffn_up_fwd(h2, wgu_t, name, tm=512):
    t, d = h2.shape
    tm = min(tm, t)
    fh = FFN_HIDDEN // 2

    def body(a_ref, w_ref, gu_ref, act_ref):
        r = _dot(a_ref[...], w_ref[...], NT)
        gu_ref[...] = r.astype(BF16)
        act_ref[...] = _swiglu(r[:, :fh], r[:, fh:]).astype(BF16)

    return pl.pallas_call(
        body, name=name, grid=(2, t // tm),
        in_specs=[pl.BlockSpec((tm, d), lambda j, i: (i, 0)), pl.BlockSpec((2 * fh, d), lambda j, i: (j, 0))],
        out_specs=[pl.BlockSpec((tm, 2 * fh), lambda j, i: (i, j)), pl.BlockSpec((tm, fh), lambda j, i: (i, j))],
        out_shape=[jax.ShapeDtypeStruct((t, 4 * fh), BF16), jax.ShapeDtypeStruct((t, 2 * fh), BF16)],
        compiler_params=_cp("parallel", "parallel"))(h2, wgu_t)


def _ffn_down_bwd(dx, w_d, gu, name, tm=512):
    t, d = dx.shape
    tm = min(tm, t)
    fh = FFN_HIDDEN // 2

    def body(a_ref, w_ref, gu_ref, o_ref):
        dact = _dot(a_ref[...].astype(BF16), w_ref[...], NT)
        dg, du = _swiglu_bwd(gu_ref[:, :fh].astype(F32), gu_ref[:, fh:].astype(F32), dact)
        o_ref[:, :fh] = dg.astype(BF16)
        o_ref[:, fh:] = du.astype(BF16)

    wide = pl.BlockSpec((tm, 2 * fh), lambda j, i: (i, j))
    return pl.pallas_call(
        body, name=name, grid=(2, t // tm),
        in_specs=[pl.BlockSpec((tm, d), lambda j, i: (i, 0)), pl.BlockSpec((fh, d), lambda j, i: (j, 0)), wide],
        out_specs=wide,
        out_shape=jax.ShapeDtypeStruct((t, 4 * fh), BF16),
        compiler_params=_cp("parallel", "parallel"))(dx, w_d, gu)


def _rows_bwd(pieces, w, x, g, dres, name, tm=256, ex=None):
    t, d = x.shape
    tm = min(tm, t)
    widths = [p.shape[1] for p in pieces]
    starts = [sum(widths[:i]) for i in range(len(widths))]
    assert sum(widths) == w.shape[0]
    n_p = len(pieces)

    def body(*refs):
        p_refs, (w_ref, x_ref, g_ref, dres_ref, dx_ref, dg_ref) = refs[:n_p], refs[n_p:]
        dh = _dot(p_refs[0][...], w_ref[pl.ds(starts[0], widths[0]), :], NN)
        for i in range(1, n_p):
            dh = dh + _dot(p_refs[i][...], w_ref[pl.ds(starts[i], widths[i]), :], NN)
        _, vjp = jax.vjp(_rms, x_ref[...], g_ref[...])
        dx, dg = vjp(dh)
        dx_ref[...] = dres_ref[...] + dx

        @pl.when(pl.program_id(0) == 0)
        def _():
            dg_ref[...] = jnp.zeros_like(dg_ref)

        dg_ref[...] += dg

    row = pl.BlockSpec((tm, d), lambda i: (i, 0))
    vec = pl.BlockSpec((1, d), lambda i: (0, 0))
    ex_in, ex_in_specs, ex_out, ex_out_specs, ex_scratch = _carried(ex)
    outs = pl.pallas_call(
        _carry(body, ex, n_p + 4, 2, 0, (t // tm,)), name=name, grid=(t // tm,),
        in_specs=[pl.BlockSpec((tm, k), lambda i: (i, 0)) for k in widths]
        + [pl.BlockSpec(w.shape, lambda i: (0, 0)), row, vec, row] + ex_in_specs,
        out_specs=[row, vec] + ex_out_specs,
        out_shape=[jax.ShapeDtypeStruct((t, d), F32), jax.ShapeDtypeStruct((1, d), F32)] + ex_out,
        scratch_shapes=ex_scratch,
        compiler_params=_cp("arbitrary"))(*pieces, w, x, g, dres, *ex_in)
    return (outs[0], outs[1], outs[2:]) if ex else (outs[0], outs[1])


def _merge_fwd(o_g, o_att, proj, x, w_pa, w_pb, w_o, gain, name, tm=512):
    t, d = x.shape
    tm = min(tm, t)

    def body(og_ref, oa_ref, ga0, ga1, gb0, gb1, x_ref, wpa_ref, wpb_ref, wo_ref, g_ref,
             ya_ref, yb_ref, mix_ref, h2_ref, x1_ref):
        ya = _dot(og_ref[...], wpa_ref[...], NN)
        yb = _dot(oa_ref[...], wpb_ref[...], NN)
        ga = jnp.concatenate([ga0[...], ga1[...]], axis=1)
        gb = jnp.concatenate([gb0[...], gb1[...]], axis=1)
        mix = _mix(ya, yb, ga, gb).astype(BF16)
        ya_ref[...] = ya.astype(BF16)
        yb_ref[...] = yb.astype(BF16)
        mix_ref[...] = mix
        x1 = x_ref[...] + _dot(mix, wo_ref[...], NN)
        x1_ref[...] = x1
        h2_ref[...] = _rms(x1, g_ref[...]).astype(BF16)

    row = pl.BlockSpec((tm, d), lambda i: (i, 0))
    mat = pl.BlockSpec((d, d), lambda i: (0, 0))
    return pl.pallas_call(
        body, name=name, grid=(t // tm,),
        in_specs=[row, row] + _gate_specs(tm) + [row, mat, mat, mat, pl.BlockSpec((1, d), lambda i: (0, 0))],
        out_specs=[row] * 5,
        out_shape=[jax.ShapeDtypeStruct((t, d), BF16)] * 4 + [jax.ShapeDtypeStruct((t, d), F32)],
        compiler_params=_cp("parallel"))(o_g, o_att, proj, proj, proj, proj, x, w_pa, w_pb, w_o, gain)


def _loss_head(x, g, target, name, tm=512):
    t, d = x.shape
    tm = min(tm, t)

    def body(x_ref, g_ref, t_ref, dx_ref, dg_ref, loss_ref):
        tgt = t_ref[...]

        def f(xv, gv):
            err = _rms(xv, gv) - tgt
            return 0.5 * jnp.sum(jnp.mean(err * err, axis=-1, keepdims=True))

        loss, vjp = jax.vjp(f, x_ref[...], g_ref[...])
        dx, dg = vjp(jnp.ones((), F32))
        dx_ref[...] = dx

        @pl.when(pl.program_id(0) == 0)
        def _():
            dg_ref[...] = jnp.zeros_like(dg_ref)
            loss_ref[...] = jnp.zeros_like(loss_ref)

        dg_ref[...] += dg
        loss_ref[...] += jnp.full(loss_ref.shape, loss, F32)

    row = pl.BlockSpec((tm, d), lambda i: (i, 0))
    vec = pl.BlockSpec((1, d), lambda i: (0, 0))
    lane = pl.BlockSpec((1, 128), lambda i: (0, 0))
    return pl.pallas_call(
        body, name=name, grid=(t // tm,), in_specs=[row, vec, row], out_specs=[row, vec, lane],
        out_shape=[jax.ShapeDtypeStruct((t, d), F32), jax.ShapeDtypeStruct((1, d), F32),
                   jax.ShapeDtypeStruct((1, 128), F32)],
        compiler_params=_cp("arbitrary"))(x, g, target)


def _lb_rows(l0, l1, l2, l3):
    mx = jnp.maximum(jnp.maximum(l0, l1), jnp.maximum(l2, l3))
    e0, e1, e2, e3 = jnp.exp(l0 - mx), jnp.exp(l1 - mx), jnp.exp(l2 - mx), jnp.exp(l3 - mx)
    s = e0 + e1 + e2 + e3
    p0, p1, p2, p3 = e0 / s, e1 / s, e2 / s, e3 / s
    c1 = p0 + p1
    c2 = c1 + p2
    c3 = c2 + p3
    return p0 - p0, c1 - p0, c2 - p0, c3 - p0


def _lb_fwd(lb_logits):
    def body(l_ref, o_ref):
        rows = _lb_rows(*[l_ref[pl.ds(i, 1), :] for i in range(DEPTH)])
        for i in range(DEPTH):
            o_ref[pl.ds(i, 1), :] = rows[i]

    return pl.pallas_call(body, name="lb_fwd", out_shape=jax.ShapeDtypeStruct(lb_logits.shape, F32))(lb_logits)


def _lb_bwd(lb_logits, dlb):
    def body(l_ref, d_ref, o_ref):
        _, vjp = jax.vjp(_lb_rows, *[l_ref[pl.ds(i, 1), :] for i in range(DEPTH)])
        grads = vjp(tuple(d_ref[pl.ds(i, 1), :] for i in range(DEPTH)))
        for i in range(DEPTH):
            o_ref[pl.ds(i, 1), :] = grads[i]

    return pl.pallas_call(body, name="lb_bwd", out_shape=jax.ShapeDtypeStruct(lb_logits.shape, F32))(lb_logits, dlb)


MESH = pl.DeviceIdType.MESH
ANY = pl.BlockSpec(memory_space=pl.ANY)
N_KINDS = len(SHARD_ROWS)
FFN_HALF = FFN_HIDDEN // 2
KIND_PLACE = ((0, 0), (1, 0), (2, 0), (3, 0), (4, 0), (4, FFN_HALF), (5, 0))
KIND_HALF_SKIP = (0, 0, 0, 0, FFN_HALF, FFN_HALF, 0)
FULL_ROWS = (N_DEV * SHARD_ROWS[0], D_MODEL, D_MODEL, D_MODEL, 2 * N_DEV * SHARD_ROWS[4], N_DEV * SHARD_ROWS[6])


def _kind_rows(ti, dev):
    oi, base = KIND_PLACE[ti]
    start = base + dev * SHARD_ROWS[ti]
    if KIND_HALF_SKIP[ti]:
        start = start + (dev // (N_DEV // 2)) * KIND_HALF_SKIP[ti]
    return oi, pl.ds(start, SHARD_ROWS[ti])


def _position():
    x, y, c = lax.axis_index("x"), lax.axis_index("y"), lax.axis_index("c")
    return x, y, c, 4 * x + 2 * y + c


def _peer(x, y, c, r):
    px = 1 - x if r & 4 else x
    py = 1 - y if r & 2 else y
    pc = 1 - c if r & 1 else c
    return (px, py, pc), 4 * px + 2 * py + pc


class _Exchange(NamedTuple):
    operands: tuple
    out_shape: tuple
    copies: Callable
    n_local: int


EXCHANGE_SCRATCH = (pltpu.SemaphoreType.DMA((N_DEV, N_KINDS)), pltpu.SemaphoreType.DMA((N_DEV, N_KINDS)),
                    pltpu.SemaphoreType.DMA((N_KINDS,)))
ALL_KINDS = tuple(range(N_KINDS))
KINDS_W_IN = (0,)
KINDS_REST = ALL_KINDS[1:]


def _all_pairs(kinds, ends, send_sems, recv_sems):
    x, y, c, me = _position()
    out = []
    for r in range(1, N_DEV):
        peer, pid = _peer(x, y, c, r)
        for ti in kinds:
            src, dst = ends(ti, me, pid)
            out.append(pltpu.make_async_remote_copy(
                src_ref=src, dst_ref=dst, send_sem=send_sems.at[r, ti], recv_sem=recv_sems.at[r, ti],
                device_id=peer, device_id_type=MESH))
    return out


def _gather_exchange(shards, kinds):
    arrays = sorted({KIND_PLACE[ti][0] for ti in kinds})

    def copies(ins, outs, send_sems, recv_sems, local_sems, arrivals):
        src = dict(zip(kinds, ins))

        def window(ti, dev):
            oi, rows = _kind_rows(ti, dev)
            return outs[arrays.index(oi)].at[rows, :]

        if arrivals:
            return _all_pairs(kinds, lambda ti, me, pid: (src[ti], window(ti, pid)), send_sems, recv_sems)
        _, _, _, me = _position()
        local = [pltpu.make_async_copy(src[ti], window(ti, me), local_sems.at[ti]) for ti in kinds]
        return local + _all_pairs(kinds, lambda ti, me, pid: (src[ti], window(ti, me)), send_sems, recv_sems)

    return _Exchange(tuple(shards[ti] for ti in kinds),
                     tuple(jax.ShapeDtypeStruct((FULL_ROWS[oi], D_MODEL), BF16) for oi in arrays), copies, len(kinds))


def _scatter_exchange(grads, kinds):
    arrays = sorted({KIND_PLACE[ti][0] for ti in kinds})
    offsets, total = {}, 0
    for ti in kinds:
        offsets[ti], total = total, total + SHARD_ROWS[ti]

    def copies(ins, outs, send_sems, recv_sems, local_sems, arrivals):
        land = outs[0]

        def piece(ti, dev):
            ii, rows = _kind_rows(ti, dev)
            return ins[arrays.index(ii)].at[rows, :]

        def slot(ti, dev):
            return land.at[dev, pl.ds(offsets[ti], SHARD_ROWS[ti]), :]

        if arrivals:
            return _all_pairs(kinds, lambda ti, me, pid: (piece(ti, me), slot(ti, pid)), send_sems, recv_sems)
        _, _, _, me = _position()
        local = [pltpu.make_async_copy(piece(ti, me), slot(ti, me), local_sems.at[ti]) for ti in kinds]
        return local + _all_pairs(kinds, lambda ti, me, pid: (piece(ti, pid), slot(ti, me)), send_sems, recv_sems)

    return _Exchange(tuple(grads[oi] for oi in arrays), (jax.ShapeDtypeStruct((N_DEV, total, D_MODEL), BF16),),
                     copies, len(kinds))


def _exchange_start(ex, ins, outs, sems):
    for cp in ex.copies(ins, outs, *sems, False):
        cp.start()


def _exchange_finish(ex, ins, outs, sems):
    for cp in ex.copies(ins, outs, *sems, True):
        cp.wait_recv()
    mine = ex.copies(ins, outs, *sems, False)
    for cp in mine[:ex.n_local]:
        cp.wait()
    for cp in mine[ex.n_local:]:
        cp.wait_send()


def _run_exchange(ex, name):
    n_in, n_out = len(ex.operands), len(ex.out_shape)

    def body(*refs):
        ins, outs, sems = refs[:n_in], refs[n_in:n_in + n_out], refs[n_in + n_out:]
        _exchange_start(ex, ins, outs, sems)
        _exchange_finish(ex, ins, outs, sems)

    return pl.pallas_call(body, name=name, in_specs=[ANY] * n_in, out_specs=[ANY] * n_out,
                          out_shape=list(ex.out_shape), scratch_shapes=list(EXCHANGE_SCRATCH))(*ex.operands)


def _carry(body, ex, n_in, n_out, n_scratch, grid):
    if ex is None:
        return body
    e_in, e_out = len(ex.operands), len(ex.out_shape)

    def at(step):
        hit = pl.program_id(0) == step[0]
        for axis in range(1, len(grid)):
            hit = hit & (pl.program_id(axis) == step[axis])
        return hit

    def carrying(*refs):
        own_in, ex_in = refs[:n_in], refs[n_in:n_in + e_in]
        rest = refs[n_in + e_in:]
        own_out, ex_out = rest[:n_out], rest[n_out:n_out + e_out]
        own_scratch, sems = rest[n_out + e_out:n_out + e_out + n_scratch], rest[n_out + e_out + n_scratch:]

        @pl.when(at([0] * len(grid)))
        def _():
            _exchange_start(ex, ex_in, ex_out, sems)

        body(*own_in, *own_out, *own_scratch)

        @pl.when(at([g - 1 for g in grid]))
        def _():
            _exchange_finish(ex, ex_in, ex_out, sems)

    return carrying


def _carried(ex):
    if ex is None:
        return (), [], [], [], []
    return (ex.operands, [ANY] * len(ex.operands), list(ex.out_shape), [ANY] * len(ex.out_shape),
            list(EXCHANGE_SCRATCH))


def _small_sum_body(p_ref, o_ref, buf, send_sems, recv_sems):
    x, y, c, me = _position()
    buf[me] = p_ref[...]
    sends = []
    for r in range(1, N_DEV):
        peer, _ = _peer(x, y, c, r)
        sends.append(pltpu.make_async_remote_copy(
            src_ref=p_ref, dst_ref=buf.at[me], send_sem=send_sems.at[r], recv_sem=recv_sems.at[r],
            device_id=peer, device_id_type=MESH))
    for cp in sends:
        cp.start()
    for r in range(1, N_DEV):
        peer, pid = _peer(x, y, c, r)
        pltpu.make_async_remote_copy(
            src_ref=p_ref, dst_ref=buf.at[pid], send_sem=send_sems.at[r], recv_sem=recv_sems.at[r],
            device_id=peer, device_id_type=MESH).wait_recv()
    for cp in sends:
        cp.wait_send()
    acc = buf[0]
    for k in range(1, N_DEV):
        acc = acc + buf[k]
    o_ref[...] = acc


def _all_reduce_small(part):
    rows, d = part.shape
    vmem = pl.BlockSpec(memory_space=pltpu.VMEM)
    return pl.pallas_call(
        functools.partial(_small_sum_body), name="all_reduce_small", in_specs=[vmem], out_specs=vmem,
        out_shape=jax.ShapeDtypeStruct((rows, d), F32),
        scratch_shapes=[pltpu.VMEM((N_DEV, rows, d), F32), pltpu.SemaphoreType.DMA((N_DEV,)),
                        pltpu.SemaphoreType.DMA((N_DEV,))],
    )(part)


HG_PAIR = 2 * HG_DK


def _hg_consts():
    c = HG_CHUNK
    r = lax.broadcasted_iota(jnp.int32, (c, c), 0)
    s = lax.broadcasted_iota(jnp.int32, (c, c), 1)
    r2 = lax.broadcasted_iota(jnp.int32, (c, 2 * c), 0)
    s2 = lax.broadcasted_iota(jnp.int32, (c, 2 * c), 1)
    causal2 = jnp.where(s2 >= c, s2 - c, s2) <= r2
    lane_hi = lax.broadcasted_iota(jnp.int32, (c, HG_PAIR), 1) >= HG_DK
    same_head = ((lax.broadcasted_iota(jnp.int32, (HG_PAIR, HG_PAIR), 0) >= HG_DK)
                 == (lax.broadcasted_iota(jnp.int32, (HG_PAIR, HG_PAIR), 1) >= HG_DK))
    return (s <= r).astype(BF16), (s >= r).astype(BF16), causal2, lane_hi, same_head


def _head_rows(x, lane_hi):
    zero = jnp.zeros_like(x)
    return jnp.concatenate([jnp.where(lane_hi, zero, x), jnp.where(lane_hi, x, zero)], axis=0)


def _own_rows(y, lane_hi):
    return jnp.where(lane_hi, y[HG_CHUNK:], y[:HG_CHUNK])


def _split3(x):
    hi = x.astype(BF16)
    r1 = x - hi.astype(F32)
    mid = r1.astype(BF16)
    lo = (r1 - mid.astype(F32)).astype(BF16)
    return jnp.concatenate([hi, mid, lo], axis=1)


def _cumsum_rows(tri, x):
    w = x.shape[1]
    y = _dot(tri, _split3(x), NN)
    return y[:, :w] + y[:, w:2 * w] + y[:, 2 * w:]


def _hg_chunk(zq, zf, lb, tril, b_ref):
    c = HG_CHUNK
    sq = _sigmoid(zq)
    q = zq * sq
    sg = _sigmoid(zf)
    f = lb + (1.0 - lb) * sg
    logf = jnp.log(jnp.maximum(f, MIN_F))
    k = 1.0 - f
    b = _cumsum_rows(tril, logf)
    b_ref[...] = b
    mid = b_ref[pl.ds(c // 2 - 1, 1), :]
    bc = b_ref[pl.ds(c - 1, 1), :]
    em = jnp.exp(jnp.minimum(b - mid, HG_EXP_CLAMP))
    en = jnp.exp(jnp.minimum(mid - b, HG_EXP_CLAMP))
    return sq, q, sg, f, k, b, em, en, bc


def _hg_gate(o, zg, gn):
    return o * lax.rsqrt(jnp.mean(o * o, axis=-1, keepdims=True) + EPS) * gn * (zg * _sigmoid(zg))


def _hgrn2_fwd(proj, lb, gn, name, ex=None):
    t = proj.shape[0]
    bs_tok = min(HG_BLOCK, t)
    n_chunks = bs_tok // HG_CHUNK
    w = HG_HEADS * HG_DK

    def body(hq_ref, hf_ref, hi_ref, hg_ref, lb_ref, gn_ref, o_ref, og_ref, sall_ref, st_ref, b_ref):
        @pl.when(pl.program_id(0) == 0)
        def _():
            st_ref[...] = jnp.zeros_like(st_ref)

        tril, _, causal2, lane_hi, same_head = _hg_consts()

        for ci in range(n_chunks):
            rows = pl.ds(ci * HG_CHUNK, HG_CHUNK)
            for p in range(HG_HEADS // 2):
                cols = slice(p * HG_PAIR, (p + 1) * HG_PAIR)
                v = hi_ref[rows, cols]
                zg = hg_ref[rows, cols]
                _, q, _, _, k, b, em, en, bc = _hg_chunk(hq_ref[rows, cols], hf_ref[rows, cols],
                                                         lb_ref[:, cols], tril, b_ref.at[ci, p])
                st0 = st_ref[p]
                sall_ref[ci, 2 * p] = st0[:HG_DK, :HG_DK]
                sall_ref[ci, 2 * p + 1] = st0[HG_DK:, HG_DK:]
                vb = v.astype(BF16)
                o = _dot((q * jnp.exp(b)).astype(BF16), st0.astype(BF16), NT)
                a = jnp.where(causal2, _dot((q * em).astype(BF16), _head_rows((k * en).astype(BF16), lane_hi), NT), 0.0)
                o = o + _dot(a.astype(BF16), _head_rows(vb, lane_hi), NN)
                kdec = (k * jnp.exp(bc - b)).astype(BF16)
                st_ref[p] = st0 * jnp.exp(bc) + jnp.where(same_head, _dot(vb, kdec, TN), 0.0)
                o_ref[rows, cols] = o
                for hh in range(2):
                    sl = slice(hh * HG_DK, (hh + 1) * HG_DK)
                    hcols = slice(p * HG_PAIR + hh * HG_DK, p * HG_PAIR + (hh + 1) * HG_DK)
                    og_ref[rows, hcols] = _hg_gate(o[:, sl], zg[:, sl], gn_ref[:, hcols]).astype(BF16)

    def col(j):
        return pl.BlockSpec((bs_tok, w), lambda n, j=j: (n, j))

    vec = pl.BlockSpec((1, w), lambda n: (0, 0))
    ex_in, ex_in_specs, ex_out, ex_out_specs, ex_scratch = _carried(ex)
    outs = pl.pallas_call(
        _carry(body, ex, 6, 3, 2, (t // bs_tok,)), name=name, grid=(t // bs_tok,),
        in_specs=[col(COL_HQ // w), col(COL_HF // w), col(COL_HI // w), col(COL_HG // w), vec, vec] + ex_in_specs,
        out_specs=[col(0), col(0),
                   pl.BlockSpec((n_chunks, HG_HEADS, HG_DK, HG_DK), lambda n: (n, 0, 0, 0))] + ex_out_specs,
        out_shape=[jax.ShapeDtypeStruct((t, w), F32), jax.ShapeDtypeStruct((t, w), BF16),
                   jax.ShapeDtypeStruct((t // HG_CHUNK, HG_HEADS, HG_DK, HG_DK), F32)] + ex_out,
        scratch_shapes=[pltpu.VMEM((HG_HEADS // 2, HG_PAIR, HG_PAIR), F32),
                        pltpu.VMEM((n_chunks, HG_HEADS // 2, HG_CHUNK, HG_PAIR), F32)] + ex_scratch,
        compiler_params=_cp("arbitrary"))(proj, proj, proj, proj, lb, gn, *ex_in)
    return outs[:3], outs[3:]


def _hgrn2_bwd(proj, lb, gn, o_hg, sall, dog, name, ex=None):
    t = proj.shape[0]
    bs_tok = min(HG_BLOCK, t)
    n_chunks = bs_tok // HG_CHUNK
    n_blocks = t // bs_tok
    w = HG_HEADS * HG_DK

    def body(hq_ref, hf_ref, hi_ref, hg_ref, lb_ref, gn_ref, o_ref, sall_ref, snext_ref, dog_ref,
             da_ref, dlb_ref, dgn_ref, dst_ref, b_ref):
        @pl.when(pl.program_id(0) == 0)
        def _():
            dst_ref[...] = jnp.zeros_like(dst_ref)
            dlb_ref[...] = jnp.zeros_like(dlb_ref)
            dgn_ref[...] = jnp.zeros_like(dgn_ref)

        tril, rev_tril, causal2, lane_hi, same_head = _hg_consts()
        zero_block = jnp.zeros((HG_DK, HG_DK), F32)

        for ci in reversed(range(n_chunks)):
            rows = pl.ds(ci * HG_CHUNK, HG_CHUNK)
            for p in range(HG_HEADS // 2):
                cols = slice(p * HG_PAIR, (p + 1) * HG_PAIR)
                zq = hq_ref[rows, cols]
                v = hi_ref[rows, cols]
                zg = hg_ref[rows, cols]
                lbv = lb_ref[:, cols]
                sq, q, sg, f, k, b, em, en, bc = _hg_chunk(zq, hf_ref[rows, cols], lbv, tril, b_ref.at[ci, p])
                st0 = jnp.concatenate([jnp.concatenate([sall_ref[ci, 2 * p], zero_block], axis=1),
                                       jnp.concatenate([zero_block, sall_ref[ci, 2 * p + 1]], axis=1)], axis=0)
                dst1 = dst_ref[p]
                vb = v.astype(BF16)
                eb = jnp.exp(b)
                qg = (q * eb).astype(BF16)
                qt = (q * em).astype(BF16)
                kref = (k * en).astype(BF16)
                ebcb = jnp.exp(bc - b)
                kdec = (k * ebcb).astype(BF16)
                ebc = jnp.exp(bc)
                dos, dzgs, dgns = [], [], []
                for hh in range(2):
                    sl = slice(hh * HG_DK, (hh + 1) * HG_DK)
                    hcols = slice(p * HG_PAIR + hh * HG_DK, p * HG_PAIR + (hh + 1) * HG_DK)
                    _, gate_vjp = jax.vjp(_hg_gate, o_ref[rows, hcols], zg[:, sl], gn_ref[:, hcols])
                    do_h, dzg_h, dgn_h = gate_vjp(dog_ref[rows, hcols])
                    dos.append(do_h)
                    dzgs.append(dzg_h)
                    dgns.append(dgn_h)
                dob = jnp.concatenate(dos, axis=1).astype(BF16)
                vrows, krows = _head_rows(vb, lane_hi), _head_rows(kref, lane_hi)
                dam = jnp.where(causal2, _dot(dob, vrows, NT), 0.0).astype(BF16)
                a = jnp.where(causal2, _dot(qt, krows, NT), 0.0)
                dk = ebcb * _dot(vb, dst1.astype(BF16), NN) + en * _own_rows(_dot(dam, qt, TN), lane_hi)
                dq = eb * _dot(dob, st0.astype(BF16), NN) + em * _dot(dam, krows, NN)
                dv = _own_rows(_dot(a.astype(BF16), dob, TN), lane_hi) + _dot(kdec, dst1.astype(BF16), NT)
                dst_ref[p] = dst1 * ebc + jnp.where(same_head, _dot(dob, qg, TN), 0.0)

                after = [sall_ref[ci + 1, 2 * p + hh] if ci + 1 < n_chunks else snext_ref[0, 2 * p + hh] for hh in range(2)]
                dbx = jnp.concatenate(
                    [jnp.sum(dst1[hh * HG_DK:(hh + 1) * HG_DK, hh * HG_DK:(hh + 1) * HG_DK] * after[hh], axis=0, keepdims=True)
                     for hh in range(2)], axis=1)
                dlogf = _cumsum_rows(rev_tril, q * dq - k * dk) + dbx
                df = jnp.where(f > MIN_F, dlogf / f, 0.0) - dk
                dzf = df * (1.0 - lbv) * sg * (1.0 - sg)
                dzq = dq * (sq * (1.0 + zq * (1.0 - sq)))
                da_ref[rows, pl.ds(COL_HQ + p * HG_PAIR, HG_PAIR)] = dzq.astype(BF16)
                da_ref[rows, pl.ds(COL_HF + p * HG_PAIR, HG_PAIR)] = dzf.astype(BF16)
                da_ref[rows, pl.ds(COL_HI + p * HG_PAIR, HG_PAIR)] = dv.astype(BF16)
                da_ref[rows, pl.ds(COL_HG + p * HG_PAIR, HG_PAIR)] = jnp.concatenate(dzgs, axis=1).astype(BF16)
                dlb_ref[:, cols] += jnp.sum(df * (1.0 - sg), axis=0, keepdims=True)
                dgn_ref[:, cols] += jnp.concatenate(dgns, axis=1)

    def col(j):
        return pl.BlockSpec((bs_tok, w), lambda n, j=j: (n_blocks - 1 - n, j))

    vec = pl.BlockSpec((1, w), lambda n: (0, 0))
    ex_in, ex_in_specs, ex_out, ex_out_specs, ex_scratch = _carried(ex)
    outs = pl.pallas_call(
        _carry(body, ex, 10, 3, 2, (n_blocks,)), name=name, grid=(n_blocks,),
        in_specs=[col(COL_HQ // w), col(COL_HF // w), col(COL_HI // w), col(COL_HG // w), vec, vec, col(0),
                  pl.BlockSpec((n_chunks, HG_HEADS, HG_DK, HG_DK), lambda n: (n_blocks - 1 - n, 0, 0, 0)),
                  pl.BlockSpec((1, HG_HEADS, HG_DK, HG_DK),
                               lambda n: (jnp.minimum((n_blocks - n) * n_chunks, t // HG_CHUNK - 1), 0, 0, 0)),
                  col(0)] + ex_in_specs,
        out_specs=[pl.BlockSpec((bs_tok, 4 * w), lambda n: (n_blocks - 1 - n, 0)), vec, vec] + ex_out_specs,
        out_shape=[jax.ShapeDtypeStruct((t, 4 * w), BF16), jax.ShapeDtypeStruct((1, w), F32),
                   jax.ShapeDtypeStruct((1, w), F32)] + ex_out,
        scratch_shapes=[pltpu.VMEM((HG_HEADS // 2, HG_PAIR, HG_PAIR), F32),
                        pltpu.VMEM((n_chunks, HG_HEADS // 2, HG_CHUNK, HG_PAIR), F32)] + ex_scratch,
        compiler_params=_cp("arbitrary"))(proj, proj, proj, proj, lb, gn, o_hg, sall, sall, dog, *ex_in)
    return outs[:3], outs[3:]


def _rope_tables(t):
    half = ROPE_DIM // 2
    inv = ROPE_THETA ** (-jnp.arange(half, dtype=F32) * 2.0 / ROPE_DIM)
    d = jnp.arange(2 * ATT_HEAD_DIM) % ATT_HEAD_DIM
    ang = jnp.arange(t).astype(F32)[:, None] * inv[d % half][None, :]
    cos, sin = jnp.cos(ang), jnp.sin(ang)
    c = jnp.where(d < ROPE_DIM, cos, 1.0)
    su = jnp.where(d < half, -sin, 0.0)
    sd = jnp.where((d >= half) & (d < ROPE_DIM), sin, 0.0)
    return c, su, sd


def _rope(x, tabs):
    c, su, sd = tabs
    n = x.shape[1]
    half = ROPE_DIM // 2
    return x * c + pltpu.roll(x, n - half, 1) * su + pltpu.roll(x, half, 1) * sd


def _rope_t(dy, tabs):
    c, su, sd = tabs
    n = dy.shape[1]
    half = ROPE_DIM // 2
    return dy * c + pltpu.roll(dy * su, half, 1) + pltpu.roll(dy * sd, n - half, 1)


def _swa_specs(n_blocks, clamp):
    blk = ATT_BLOCK

    def cur(n):
        return jnp.minimum(n, n_blocks - 1) if clamp else n

    def prev(n):
        return jnp.maximum(cur(n) - 1, 0)

    q_spec = pl.BlockSpec((blk, 512), lambda m, n: (cur(n), COL_AQ // 512 + m))
    kv = [pl.BlockSpec((blk, 128), lambda m, n, c=c, f=f: (f(n), c + m))
          for c in (COL_AK // 128, COL_AV // 128) for f in (cur, prev)]
    tabs = [pl.BlockSpec((blk, 128), lambda m, n, f=f: (f(n), 0)) for f in (cur, prev) for _ in range(3)]
    return q_spec, kv, tabs, cur, prev


ATT_SCALE = ATT_HEAD_DIM ** -0.5


def _head_halves(x, upper):
    zero = jnp.zeros_like(x)
    return jnp.concatenate([jnp.where(upper, zero, x), jnp.where(upper, x, zero)], axis=0)


def _swa_scores(scores, sink, mask):
    s = jnp.where(mask, scores, -jnp.inf)
    mx = jnp.maximum(jnp.max(s, axis=-1, keepdims=True), sink)
    p = jnp.exp(s - mx)
    es = jnp.exp(sink - mx)
    rinv = 1.0 / (jnp.sum(p, axis=-1, keepdims=True) + es)
    return p * rinv, es * rinv


def _swa_window(kc_ref, kp_ref, vc_ref, vp_ref, tabs_c, tabs_p, n):
    k2 = jnp.concatenate([_rope(kp_ref[...], tabs_p), _rope(kc_ref[...], tabs_c)], axis=0)
    v2 = jnp.concatenate([vp_ref[...], vc_ref[...]], axis=0)
    blk = ATT_BLOCK
    qi = lax.broadcasted_iota(jnp.int32, (blk, 2 * blk), 0)
    kj = lax.broadcasted_iota(jnp.int32, (blk, 2 * blk), 1)
    delta = qi + blk - kj
    mask = (delta >= 0) & (delta < blk) & ((kj >= blk) | (n > 0))
    return k2, v2, mask


def _swa_fwd(proj, sinks, tabs, name, ex=None):
    t = proj.shape[0]
    n_blocks = t // ATT_BLOCK
    q_spec, kv_specs, tab_specs, _, _ = _swa_specs(n_blocks, clamp=False)

    def body(q_ref, kc_ref, kp_ref, vc_ref, vp_ref, c0, c1, c2, p0, p1, p2, sink_ref, o_ref):
        m, n = pl.program_id(0), pl.program_id(1)
        tabs_c = (c0[...], c1[...], c2[...])
        tabs_p = (p0[...], p1[...], p2[...])
        k2, v2, mask = _swa_window(kc_ref, kp_ref, vc_ref, vp_ref, tabs_c, tabs_p, n)
        k2r, v2r = pltpu.roll(k2, 64, 1), pltpu.roll(v2, 64, 1)
        upper_k = lax.broadcasted_iota(jnp.int32, k2.shape, 1) >= 64
        upper_q = lax.broadcasted_iota(jnp.int32, (ATT_BLOCK, 128), 1) >= 64
        for jj in range(2):
            own = upper_k if jj else ~upper_k
            kd = jnp.where(own, k2, k2r).astype(BF16)
            vd = jnp.where(own, v2, v2r).astype(BF16)
            for pi in range(2):
                cols = slice(256 * jj + 128 * pi, 256 * jj + 128 * pi + 128)
                qp = _rope(q_ref[:, cols], tabs_c) * ATT_SCALE
                outs = []
                for e in range(2):
                    sink = sink_ref[0, 8 * m + 4 * jj + 2 * pi + e]
                    qm = jnp.where(upper_q if e else ~upper_q, qp, 0.0).astype(BF16)
                    pn, _ = _swa_scores(_dot(qm, kd, NT), sink, mask)
                    outs.append(_dot(pn.astype(BF16), vd, NN))
                o_ref[:, cols] = jnp.where(upper_q, outs[1], outs[0]).astype(BF16)

    ex_in, ex_in_specs, ex_out, ex_out_specs, ex_scratch = _carried(ex)
    outs = pl.pallas_call(
        _carry(body, ex, 12, 1, 0, (2, n_blocks)), name=name, grid=(2, n_blocks),
        in_specs=[q_spec] + kv_specs + tab_specs + [pl.BlockSpec(memory_space=pltpu.SMEM)] + ex_in_specs,
        out_specs=[pl.BlockSpec((ATT_BLOCK, 512), lambda m, n: (n, m))] + ex_out_specs,
        out_shape=[jax.ShapeDtypeStruct((t, ATT_Q_HEADS * ATT_HEAD_DIM), BF16)] + ex_out,
        scratch_shapes=ex_scratch,
        compiler_params=_cp("arbitrary", "arbitrary"))(proj, proj, proj, proj, proj, *tabs, *tabs, sinks, *ex_in)
    return outs[0], outs[1:]


def _swa_bwd(proj, sinks, tabs, o_att, do_att, name, ex=None):
    t = proj.shape[0]
    n_blocks = t // ATT_BLOCK
    blk = ATT_BLOCK
    q_spec, kv_specs, tab_specs, cur, prev = _swa_specs(n_blocks, clamp=True)

    def body(q_ref, kc_ref, kp_ref, vc_ref, vp_ref, c0, c1, c2, p0, p1, p2, sink_ref, o_ref, do_ref,
             dq_ref, dk_ref, dv_ref, ds_ref, ck_ref, cv_ref):
        m, n = pl.program_id(0), pl.program_id(1)

        @pl.when(n == 0)
        def _():
            ds_ref[...] = jnp.zeros_like(ds_ref)
            ck_ref[...] = jnp.zeros_like(ck_ref)
            cv_ref[...] = jnp.zeros_like(cv_ref)

        @pl.when(n < n_blocks)
        def _():
            tabs_c = (c0[...], c1[...], c2[...])
            tabs_p = (p0[...], p1[...], p2[...])
            k2, v2, mask = _swa_window(kc_ref, kp_ref, vc_ref, vp_ref, tabs_c, tabs_p, n)
            k2r, v2r = pltpu.roll(k2, 64, 1), pltpu.roll(v2, 64, 1)
            upper_k = lax.broadcasted_iota(jnp.int32, k2.shape, 1) >= 64
            upper_q = lax.broadcasted_iota(jnp.int32, (blk, 128), 1) >= 64
            lane = lax.broadcasted_iota(jnp.int32, (8, 128), 1)
            dk2 = jnp.zeros(k2.shape, F32)
            dv2 = jnp.zeros(k2.shape, F32)
            dsv = jnp.zeros((8, 128), F32)
            nk = 2 * blk
            for jj in range(2):
                own = upper_k if jj else ~upper_k
                kh = _head_halves(jnp.where(own, k2, k2r).astype(BF16), upper_k)
                vh = _head_halves(jnp.where(own, v2, v2r).astype(BF16), upper_k)
                dkd = jnp.zeros(k2.shape, F32)
                dvd = jnp.zeros(k2.shape, F32)
                for pi in range(2):
                    cols = slice(256 * jj + 128 * pi, 256 * jj + 128 * pi + 128)
                    qp = (_rope(q_ref[:, cols], tabs_c) * ATT_SCALE).astype(BF16)
                    do_pair = do_ref[:, cols]
                    o_pair = o_ref[:, cols].astype(F32)
                    dob = do_pair.astype(BF16)
                    s = _dot(qp, kh, NT)
                    dp = _dot(dob, vh, NT)
                    pns, dss = [], []
                    for e in range(2):
                        hl = 4 * jj + 2 * pi + e
                        pn, ps = _swa_scores(s[:, e * nk:(e + 1) * nk], sink_ref[0, 8 * m + hl], mask)
                        delta = jnp.sum(jnp.where(upper_q if e else ~upper_q, do_pair * o_pair, 0.0), axis=-1, keepdims=True)
                        pns.append(pn.astype(BF16))
                        dss.append((pn * (dp[:, e * nk:(e + 1) * nk] - delta)).astype(BF16))
                        dsv = dsv + jnp.where(lane == hl, -jnp.sum(ps * delta), 0.0)
                    dsb = jnp.concatenate(dss, axis=1)
                    dq_ref[:, cols] = _rope_t(_dot(dsb, kh, NN) * ATT_SCALE, tabs_c).astype(BF16)
                    rk = _dot(dsb, qp, TN)
                    rv = _dot(jnp.concatenate(pns, axis=1), dob, TN)
                    dkd = dkd + jnp.where(upper_k, rk[nk:], rk[:nk])
                    dvd = dvd + jnp.where(upper_k, rv[nk:], rv[:nk])
                dk2 = dk2 + jnp.where(own, dkd + pltpu.roll(dkd, 64, 1), 0.0)
                dv2 = dv2 + jnp.where(own, dvd + pltpu.roll(dvd, 64, 1), 0.0)
            dk_ref[...] = (ck_ref[...] + _rope_t(dk2[:blk], tabs_p)).astype(BF16)
            dv_ref[...] = (cv_ref[...] + dv2[:blk]).astype(BF16)
            ck_ref[...] = _rope_t(dk2[blk:], tabs_c)
            cv_ref[...] = dv2[blk:]
            ds_ref[...] += dsv

        @pl.when(n == n_blocks)
        def _():
            dk_ref[...] = ck_ref[...].astype(BF16)
            dv_ref[...] = cv_ref[...].astype(BF16)

    wide = pl.BlockSpec((blk, 512), lambda m, n: (cur(n), m))
    lagged = pl.BlockSpec((blk, 128), lambda m, n: (jnp.maximum(n - 1, 0), m))
    ex_in, ex_in_specs, ex_out, ex_out_specs, ex_scratch = _carried(ex)
    outs = pl.pallas_call(
        _carry(body, ex, 14, 4, 2, (2, n_blocks + 1)), name=name, grid=(2, n_blocks + 1),
        in_specs=[q_spec] + kv_specs + tab_specs + [pl.BlockSpec(memory_space=pltpu.SMEM), wide, wide] + ex_in_specs,
        out_specs=[wide, lagged, lagged, pl.BlockSpec((None, 8, 128), lambda m, n: (m, 0, 0))] + ex_out_specs,
        out_shape=[jax.ShapeDtypeStruct((t, 1024), BF16), jax.ShapeDtypeStruct((t, 256), BF16),
                   jax.ShapeDtypeStruct((t, 256), BF16), jax.ShapeDtypeStruct((2, 8, 128), F32)] + ex_out,
        scratch_shapes=[pltpu.VMEM((blk, 128), F32), pltpu.VMEM((blk, 128), F32)] + ex_scratch,
        compiler_params=_cp("arbitrary", "arbitrary"))(proj, proj, proj, proj, proj, *tabs, *tabs, sinks, o_att, do_att,
                                                       *ex_in)
    return outs[:4], outs[4:]


def _local_step(x, target, shards, norm1, lb_logits, hg_norm, attn_sinks, norm2, final_norm):
    t = x.shape[0]
    tabs = _rope_tables(t)
    lb_all = _lb_fwd(lb_logits)
    saved = []

    def shards_of(l):
        return {ti: shards[ti][l] for ti in ALL_KINDS}

    win_next = _run_exchange(_gather_exchange(shards_of(0), KINDS_W_IN), "gather_w_in")
    rest_next = None
    for l in range(DEPTH):
        n1, n2 = norm1[l][None, :], norm2[l][None, :]
        lb, gn, sinks = lb_all[l][None, :], hg_norm[l][None, :], attn_sinks[l][None, :]
        (win_t,) = win_next
        if l == 0:
            h = _rms_fwd(x, n1, "rms1_fwd")
        if l == 0:
            proj, rest_next = _matmul_nt(h, win_t, 0, IN_COLS, F32, "proj_fwd", tn=1280,
                                         ex=_gather_exchange(shards_of(0), KINDS_REST))
        else:
            proj = _matmul_nt(h, win_t, 0, IN_COLS, F32, "proj_fwd", tn=1280)
        w_pa, w_pb, w_o, wgu_t, w_d = rest_next
        more = l + 1 < DEPTH
        (o_hg, o_g, sall), rest_next = _hgrn2_fwd(
            proj, lb, gn, "hgrn2_fwd", _gather_exchange(shards_of(l + 1), KINDS_REST) if more else None)
        o_att, win_next = _swa_fwd(
            proj, sinks, tabs, "swa_fwd", _gather_exchange(shards_of(l + 1), KINDS_W_IN) if more else None)
        ya, yb, mix, h2, x1 = _merge_fwd(o_g, o_att, proj, x, w_pa, w_pb, w_o, n2, "merge_fwd")
        gu, act = _ffn_up_fwd(h2, wgu_t, "ffn_up_fwd")
        if more:
            x2, h_next = _matmul_nn(act, w_d, 0, x1, "wd_fwd", gain=norm1[l + 1][None, :])
        else:
            x2, h_next = _matmul_nn(act, w_d, 0, x1, "wd_fwd_last"), None
        saved.append((x, h, proj, o_hg, o_g, sall, o_att, ya, yb, mix, x1, h2, gu, act, n1, n2, lb, gn, sinks,
                      (win_t, w_pa, w_pb, w_o, wgu_t, w_d)))
        x, h = x2, h_next

    dx, d_fn, loss = _loss_head(x, final_norm[None, :], target, "loss_head")

    owned = [None] * DEPTH
    pending = None
    d_n1, d_n2, d_lb, d_gn, d_sinks = ([None] * DEPTH for _ in range(5))
    for l in reversed(range(DEPTH)):
        x0, h, proj, o_hg, o_g, sall, o_att, ya, yb, mix, x1, h2, gu, act, n1, n2, lb, gn, sinks, weights = saved[l]
        win_t, w_pa, w_pb, w_o, wgu_t, w_d = weights
        dgu = _ffn_down_bwd(dx, w_d, gu, "ffn_down_bwd")
        g_wd = _matmul_tn(act, dx, "wd_grad", tm=1408)
        g_wgu = _matmul_tn(dgu, h2, "wgu_grad", tm=1408)
        dx1, d_n2[l] = _rows_bwd([dgu], wgu_t, x1, n2, dx, "ffn_up_bwd", tm=512)
        g_wo = _matmul_tn(mix, dx1, "wo_grad")
        dya, dyb, dgab, dog, doatt = _merge_bwd(dx1, ya, yb, proj, w_pa, w_pb, w_o, "merge_bwd")
        g_wpa = _matmul_tn(o_g, dya, "wpa_grad")
        g_wpb = _matmul_tn(o_att, dyb, "wpb_grad")
        ex = _scatter_exchange(pending, ALL_KINDS) if pending is not None else None
        (dhg, d_lb[l], d_gn[l]), land = _hgrn2_bwd(proj, lb, gn, o_hg, sall, dog, "hgrn2_bwd", ex)
        if pending is not None:
            owned[l + 1] = _sum_slots(land[0], "sum_slots")
        ex = _scatter_exchange((None, g_wpa, g_wpb, g_wo, g_wgu, g_wd), KINDS_REST) if l == 0 else None
        (daq, dak, dav, d_sinks[l]), land_rest = _swa_bwd(proj, sinks, tabs, o_att, doatt, "swa_bwd", ex)
        g_win = None
        for piece, off, tm, tag in ((dhg, COL_HQ, 512, "hg"), (daq, COL_AQ, 512, "aq"), (dak, COL_AK, 256, "ak"),
                                    (dav, COL_AV, 256, "av"), (dgab, COL_GA, 512, "gates")):
            g_win = _matmul_tn(piece, h, "win_grad_" + tag, tm=tm, rows=IN_COLS, row_off=off, into=g_win)
        if l > 0:
            dx, d_n1[l] = _rows_bwd([dhg, daq, dak, dav, dgab], win_t, x0, n1, dx1, "win_bwd")
        else:
            dx, d_n1[l], land_win = _rows_bwd([dhg, daq, dak, dav, dgab], win_t, x0, n1, dx1, "win_bwd",
                                              ex=_scatter_exchange((g_win,), KINDS_W_IN))
        pending = (g_win, g_wpa, g_wpb, g_wo, g_wgu, g_wd)
    owned[0] = jnp.concatenate([_sum_slots(land_win[0], "sum_slots_w_in"), _sum_slots(land_rest[0], "sum_slots_rest")],
                               axis=0)

    d_sink_rows = [jnp.concatenate([d[0, 0, :8], d[1, 0, :8]]) for d in d_sinks]
    small = (jnp.concatenate(d_n1, axis=0), jnp.concatenate(d_lb, axis=0), jnp.concatenate(d_gn, axis=0),
             jnp.concatenate(d_n2, axis=0), d_fn, jnp.stack(d_sink_rows, axis=0))
    return loss, dx, jnp.stack(owned, axis=0), small


def _sum_slots(land, name, tr=480):
    _, rows, d = land.shape

    def body(l_ref, o_ref):
        acc = l_ref[0].astype(F32)
        for k in range(1, N_DEV):
            acc = acc + l_ref[k].astype(F32)
        o_ref[...] = acc

    return pl.pallas_call(
        body, name=name, grid=(rows // tr,),
        in_specs=[pl.BlockSpec((N_DEV, tr, d), lambda i: (0, i, 0))],
        out_specs=pl.BlockSpec((tr, d), lambda i: (i, 0)),
        out_shape=jax.ShapeDtypeStruct((rows, d), F32),
        compiler_params=_cp("parallel"))(land)


def _adamw(w, g, m, v, name):
    shape = w.shape
    c = shape[-1]
    rows = w.size // c
    tr = rows
    for cand in (512, 352, 128):
        if rows % cand == 0:
            tr = cand
            break
    c1 = 1.0 / (1.0 - ADAM_B1 ** ADAM_STEP)
    c2 = 1.0 / (1.0 - ADAM_B2 ** ADAM_STEP)

    def body(w_ref, g_ref, m_ref, v_ref, d_ref, nm_ref, nv_ref):
        gv = g_ref[...]
        nm = ADAM_B1 * m_ref[...] + (1.0 - ADAM_B1) * gv
        nv = ADAM_B2 * v_ref[...] + (1.0 - ADAM_B2) * (gv * gv)
        d_ref[...] = -ADAM_LR * ((nm * c1) / (jnp.sqrt(nv * c2) + ADAM_EPS) + ADAM_WD * w_ref[...])
        nm_ref[...] = nm
        nv_ref[...] = nv

    spec = pl.BlockSpec((tr, c), lambda i: (i, 0))
    outs = pl.pallas_call(
        body, name=name, grid=(rows // tr,), in_specs=[spec] * 4, out_specs=[spec] * 3,
        out_shape=[jax.ShapeDtypeStruct((rows, c), F32)] * 3,
        compiler_params=_cp("parallel"))(*[a.reshape(rows, c) for a in (w, g, m, v)])
    return tuple(o.reshape(shape) for o in outs)


def kernel(x, norm1, w_in, lb_logits, hg_norm, attn_sinks, w_pa, w_pb, w_o, norm2, w_gate, w_up, w_down, final_norm, loss_target, m_norm1, m_w_in, m_lb_logits, m_hg_norm, m_attn_sinks, m_w_pa, m_w_pb, m_w_o, m_norm2, m_w_gate, m_w_up, m_w_down, m_final_norm, v_norm1, v_w_in, v_lb_logits, v_hg_norm, v_attn_sinks, v_w_pa, v_w_pb, v_w_o, v_norm2, v_w_gate, v_w_up, v_w_down, v_final_norm):
    t = x.shape[1]
    shards = [jnp.swapaxes(w_in, 1, 2).astype(BF16), w_pa.astype(BF16), w_pb.astype(BF16), w_o.astype(BF16),
              jnp.swapaxes(w_gate, 1, 2).astype(BF16), jnp.swapaxes(w_up, 1, 2).astype(BF16), w_down.astype(BF16)]
    loss_lanes, grad_x, owned, small = _local_step(
        x.reshape(t, D_MODEL), loss_target.reshape(t, D_MODEL), shards,
        norm1, lb_logits, hg_norm, attn_sinks, norm2, final_norm)

    def rows_of(ti, transpose):
        g = owned[:, SLOT_OFF[ti]:SLOT_OFF[ti] + SHARD_ROWS[ti], :]
        return jnp.swapaxes(g, 1, 2) if transpose else g

    g_big = {"w_in": rows_of(0, True), "w_pa": rows_of(1, False), "w_pb": rows_of(2, False), "w_o": rows_of(3, False),
             "w_gate": rows_of(4, True), "w_up": rows_of(5, True), "w_down": rows_of(6, False)}

    d_n1, d_lb, d_gn, d_n2, d_fn, d_sinks = small
    pad = jnp.zeros((DEPTH, D_MODEL - ATT_Q_HEADS), F32)
    packed = jnp.concatenate([
        d_n1, d_lb, d_gn, d_n2, d_fn, jnp.concatenate([d_sinks, pad], axis=1),
        jnp.concatenate([loss_lanes, jnp.zeros((1, D_MODEL - 128), F32)], axis=1),
        jnp.zeros((SMALL_ROWS - 22, D_MODEL), F32)], axis=0)
    total = _all_reduce_small(packed)
    loss = total[21, 0]
    g_small = {"norm1": total[0:4], "lb_logits": _lb_bwd(lb_logits, total[4:8]), "hg_norm": total[8:12],
               "norm2": total[12:16], "final_norm": total[16], "attn_sinks": total[17:21, :ATT_Q_HEADS]}

    params = {"norm1": (norm1, m_norm1, v_norm1), "w_in": (w_in, m_w_in, v_w_in),
              "lb_logits": (lb_logits, m_lb_logits, v_lb_logits), "hg_norm": (hg_norm, m_hg_norm, v_hg_norm),
              "attn_sinks": (attn_sinks, m_attn_sinks, v_attn_sinks), "w_pa": (w_pa, m_w_pa, v_w_pa),
              "w_pb": (w_pb, m_w_pb, v_w_pb), "w_o": (w_o, m_w_o, v_w_o), "norm2": (norm2, m_norm2, v_norm2),
              "w_gate": (w_gate, m_w_gate, v_w_gate), "w_up": (w_up, m_w_up, v_w_up),
              "w_down": (w_down, m_w_down, v_w_down), "final_norm": (final_norm, m_final_norm, v_final_norm)}
    order = ["norm1", "w_in", "lb_logits", "hg_norm", "attn_sinks", "w_pa", "w_pb", "w_o", "norm2",
             "w_gate", "w_up", "w_down", "final_norm"]
    grads, deltas, new_m, new_v = [], [], [], []
    for name in order:
        w, m, v = params[name]
        g = (g_big[name] if name in g_big else g_small[name]).reshape(w.shape)
        w2 = w.reshape(1, -1) if w.ndim == 1 else w
        d, nm, nv = _adamw(w2, g.reshape(w2.shape), m.reshape(w2.shape), v.reshape(w2.shape), "adamw_" + name)
        grads.append(g)
        deltas.append(d.reshape(w.shape))
        new_m.append(nm.reshape(w.shape))
        new_v.append(nv.reshape(w.shape))
    return (loss, grad_x.reshape(x.shape), *grads, *deltas, *new_m, *new_v)
```

```python
import functools
from typing import Callable, NamedTuple

import jax
import jax.numpy as jnp
from jax import lax
from jax.experimental import pallas as pl
from jax.experimental.pallas import tpu as pltpu

F32, BF16 = jnp.float32, jnp.bfloat16

D_MODEL = 1024
DEPTH = 4
N_DEV = 8
HG_HEADS = 8
HG_DK = 128
HG_CHUNK = 64
HG_BLOCK = 256
HG_EXP_CLAMP = 60.0
ATT_Q_HEADS = 16
ATT_HEAD_DIM = 64
ATT_BLOCK = 128
ROPE_THETA = 500000.0
ROPE_DIM = 16
FFN_HIDDEN = 2816
EPS = 1e-6
MIN_F = 1e-30
ADAM_LR, ADAM_B1, ADAM_B2, ADAM_EPS, ADAM_WD, ADAM_STEP = 0.001, 0.9, 0.999, 1e-08, 0.01, 10

COL_HQ, COL_HF, COL_HI, COL_HG = 0, 1024, 2048, 3072
COL_AQ, COL_AK, COL_AV, COL_GA, COL_GB = 4096, 5120, 5376, 5632, 6656
IN_COLS = 7680

SHARD_ROWS = (960, 128, 128, 128, 352, 352, 352)
SLOT_OFF = (0, 960, 1088, 1216, 1344, 1696, 2048)
SLOT_ROWS = 2400
SMALL_ROWS = 24

VMEM_LIMIT_BYTES = 56 * 1024 * 1024

NN = ((1,), (0,))
NT = ((1,), (1,))
TN = ((0,), (0,))


def _dot(a, b, dims):
    return lax.dot_general(a, b, (dims, ((), ())), preferred_element_type=F32)


def _cp(*sem):
    return pltpu.CompilerParams(dimension_semantics=sem if sem else None, vmem_limit_bytes=VMEM_LIMIT_BYTES)


def _sigmoid(x):
    return 0.5 * jnp.tanh(0.5 * x) + 0.5


def _matmul_nt(a, w, row_off, n, out_dtype, name, tm=1024, tn=512, ex=None):
    t, k = a.shape
    tm = min(tm, t)
    assert n % tn == 0 and row_off % tn == 0 and t % tm == 0
    grid = (n // tn, t // tm)

    def body(a_ref, w_ref, o_ref):
        o_ref[...] = _dot(a_ref[...].astype(BF16), w_ref[...], NT).astype(o_ref.dtype)

    ex_in, ex_in_specs, ex_out, ex_out_specs, ex_scratch = _carried(ex)
    outs = pl.pallas_call(
        _carry(body, ex, 2, 1, 0, grid), name=name, grid=grid,
        in_specs=[pl.BlockSpec((tm, k), lambda j, i: (i, 0)),
                  pl.BlockSpec((tn, k), lambda j, i: (row_off // tn + j, 0))] + ex_in_specs,
        out_specs=[pl.BlockSpec((tm, tn), lambda j, i: (i, j))] + ex_out_specs,
        out_shape=[jax.ShapeDtypeStruct((t, n), out_dtype)] + ex_out,
        scratch_shapes=ex_scratch,
        compiler_params=_cp("arbitrary", "arbitrary") if ex else _cp("parallel", "parallel"))(a, w, *ex_in)
    return (outs[0], outs[1:]) if ex else outs[0]


def _matmul_nn(a, w, row_off, res, name, tm=512, tk=None, gain=None):
    t, k = a.shape
    n = w.shape[1]
    tm = min(tm, t)
    tk = tk or k
    nk = k // tk
    assert k % tk == 0 and row_off % tk == 0 and t % tm == 0

    def body(*refs):
        refs = list(refs)
        a_ref, w_ref = refs[:2]
        r_ref = refs[2] if res is not None else None
        g_ref = refs[2 + (res is not None)] if gain is not None else None
        acc = refs[-1]
        o_ref = refs[-3] if gain is not None else refs[-2]
        kk = pl.program_id(1)
        part = _dot(a_ref[...].astype(BF16), w_ref[...], NN)

        @pl.when(kk == 0)
        def _():
            acc[...] = part

        @pl.when(kk > 0)
        def _():
            acc[...] += part

        @pl.when(kk == nk - 1)
        def _():
            y = acc[...] if res is None else acc[...] + r_ref[...]
            o_ref[...] = y
            if gain is not None:
                refs[-2][...] = _rms(y, g_ref[...]).astype(BF16)

    row = pl.BlockSpec((tm, n), lambda i, kk: (i, 0))
    in_specs = [pl.BlockSpec((tm, tk), lambda i, kk: (i, kk)),
                pl.BlockSpec((tk, n), lambda i, kk: (row_off // tk + kk, 0))]
    args = [a, w]
    if res is not None:
        in_specs.append(row)
        args.append(res)
    if gain is not None:
        in_specs.append(pl.BlockSpec((1, n), lambda i, kk: (0, 0)))
        args.append(gain)
    outs = pl.pallas_call(
        body, name=name, grid=(t // tm, nk), in_specs=in_specs,
        out_specs=[row, row] if gain is not None else [row],
        out_shape=[jax.ShapeDtypeStruct((t, n), F32)] + ([jax.ShapeDtypeStruct((t, n), BF16)] if gain is not None else []),
        scratch_shapes=[pltpu.VMEM((tm, n), F32)],
        compiler_params=_cp("parallel", "arbitrary"))(*args)
    return tuple(outs) if gain is not None else outs[0]


def _matmul_tn(a, b, name, tm=512, tk=2048, rows=None, row_off=0, into=None):
    t, m = a.shape
    n = b.shape[1]
    tk = min(tk, t)
    nk = t // tk
    rows = rows or m
    assert m % tm == 0 and t % tk == 0 and row_off % tm == 0

    def body(*refs):
        a_ref, b_ref, o_ref, acc = refs[0], refs[1], refs[-2], refs[-1]
        kk = pl.program_id(1)
        part = _dot(a_ref[...].astype(BF16), b_ref[...].astype(BF16), TN)

        @pl.when(kk == 0)
        def _():
            acc[...] = part

        @pl.when(kk > 0)
        def _():
            acc[...] += part

        @pl.when(kk == nk - 1)
        def _():
            o_ref[...] = acc[...].astype(BF16)

    return pl.pallas_call(
        body, name=name, grid=(m // tm, nk),
        in_specs=[pl.BlockSpec((tk, tm), lambda i, kk: (kk, i)),
                  pl.BlockSpec((tk, n), lambda i, kk: (kk, 0))] + ([ANY] if into is not None else []),
        out_specs=pl.BlockSpec((tm, n), lambda i, kk: (row_off // tm + i, 0)),
        out_shape=jax.ShapeDtypeStruct((rows, n), BF16),
        scratch_shapes=[pltpu.VMEM((tm, n), F32)],
        input_output_aliases={2: 0} if into is not None else {},
        compiler_params=_cp("parallel", "arbitrary"))(a, b, *([into] if into is not None else []))


def _rms(x, g):
    return x * lax.rsqrt(jnp.mean(x * x, axis=-1, keepdims=True) + EPS) * g


def _rms_fwd(x, g, name, tm=512):
    t, d = x.shape
    tm = min(tm, t)

    def body(x_ref, g_ref, o_ref):
        o_ref[...] = _rms(x_ref[...], g_ref[...]).astype(BF16)

    return pl.pallas_call(
        body, name=name, grid=(t // tm,),
        in_specs=[pl.BlockSpec((tm, d), lambda i: (i, 0)), pl.BlockSpec((1, d), lambda i: (0, 0))],
        out_specs=pl.BlockSpec((tm, d), lambda i: (i, 0)),
        out_shape=jax.ShapeDtypeStruct((t, d), BF16),
        compiler_params=_cp("parallel"))(x, g)


def _mix(ya, yb, ga, gb):
    return _sigmoid(ga) * ya + _sigmoid(gb) * yb


def _gate_specs(tm):
    half = D_MODEL // 2
    return [pl.BlockSpec((tm, half), lambda i, c=c: (i, c))
            for c in (COL_GA // half, COL_GA // half + 1, COL_GB // half, COL_GB // half + 1)]


def _merge_bwd(dx1, ya, yb, proj, w_pa, w_pb, w_o, name, tm=512):
    t, d = dx1.shape
    tm = min(tm, t)

    def body(dx_ref, ya_ref, yb_ref, ga0, ga1, gb0, gb1, wpa_ref, wpb_ref, wo_ref,
             dya_ref, dyb_ref, dg_ref, dog_ref, doa_ref):
        dmix = _dot(dx_ref[...].astype(BF16), wo_ref[...], NT)
        ga = jnp.concatenate([ga0[...], ga1[...]], axis=1)
        gb = jnp.concatenate([gb0[...], gb1[...]], axis=1)
        _, vjp = jax.vjp(_mix, ya_ref[...].astype(F32), yb_ref[...].astype(F32), ga, gb)
        dya, dyb, dga, dgb = vjp(dmix)
        dya, dyb = dya.astype(BF16), dyb.astype(BF16)
        dya_ref[...] = dya
        dyb_ref[...] = dyb
        dg_ref[:, :d] = dga.astype(BF16)
        dg_ref[:, d:] = dgb.astype(BF16)
        dog_ref[...] = _dot(dya, wpa_ref[...], NT)
        doa_ref[...] = _dot(dyb, wpb_ref[...], NT)

    row = pl.BlockSpec((tm, d), lambda i: (i, 0))
    wide = pl.BlockSpec((tm, 2 * d), lambda i: (i, 0))
    mat = pl.BlockSpec((d, d), lambda i: (0, 0))
    return pl.pallas_call(
        body, name=name, grid=(t // tm,), in_specs=[row, row, row] + _gate_specs(tm) + [mat, mat, mat],
        out_specs=[row, row, wide, row, row],
        out_shape=[jax.ShapeDtypeStruct((t, d), BF16), jax.ShapeDtypeStruct((t, d), BF16),
                   jax.ShapeDtypeStruct((t, 2 * d), BF16), jax.ShapeDtypeStruct((t, d), F32),
                   jax.ShapeDtypeStruct((t, d), F32)],
        compiler_params=_cp("parallel"))(dx1, ya, yb, proj, proj, proj, proj, w_pa, w_pb, w_o)


def _swiglu(g, u):
    return g * _sigmoid(g) * u


def _swiglu_bwd(g, u, dact):
    sg = _sigmoid(g)
    gs = g * sg
    return dact * u * (sg + gs * (1.0 - sg)), dact * gs


def _ffn_up_fwd(h2, wgu_t, name, tm=512):
    t, d = h2.shape
    tm = min(tm, t)
    fh = FFN_HIDDEN // 2

    def body(a_ref, w_ref, gu_ref, act_ref):
        r = _dot(a_ref[...], w_ref[...], NT)
        gu_ref[...] = r.astype(BF16)
        act_ref[...] = _swiglu(r[:, :fh], r[:, fh:]).astype(BF16)

    return pl.pallas_call(
        body, name=name, grid=(2, t // tm),
        in_specs=[pl.BlockSpec((tm, d), lambda j, i: (i, 0)), pl.BlockSpec((2 * fh, d), lambda j, i: (j, 0))],
        out_specs=[pl.BlockSpec((tm, 2 * fh), lambda j, i: (i, j)), pl.BlockSpec((tm, fh), lambda j, i: (i, j))],
        out_shape=[jax.ShapeDtypeStruct((t, 4 * fh), BF16), jax.ShapeDtypeStruct((t, 2 * fh), BF16)],
        compiler_params=_cp("parallel", "parallel"))(h2, wgu_t)


def _ffn_down_bwd(dx, w_d, gu, name, tm=512):
    t, d = dx.shape
    tm = min(tm, t)
    fh = FFN_HIDDEN // 2

    def body(a_ref, w_ref, gu_ref, o_ref):
        dact = _dot(a_ref[...].astype(BF16), w_ref[...], NT)
        dg, du = _swiglu_bwd(gu_ref[:, :fh].astype(F32), gu_ref[:, fh:].astype(F32), dact)
        o_ref[:, :fh] = dg.astype(BF16)
        o_ref[:, fh:] = du.astype(BF16)

    wide = pl.BlockSpec((tm, 2 * fh), lambda j, i: (i, j))
    return pl.pallas_call(
        body, name=name, grid=(2, t // tm),
        in_specs=[pl.BlockSpec((tm, d), lambda j, i: (i, 0)), pl.BlockSpec((fh, d), lambda j, i: (j, 0)), wide],
        out_specs=wide,
        out_shape=jax.ShapeDtypeStruct((t, 4 * fh), BF16),
        compiler_params=_cp("parallel", "parallel"))(dx, w_d, gu)


def _rows_bwd(pieces, w, x, g, dres, name, tm=256, ex=None):
    t, d = x.shape
    tm = min(tm, t)
    widths = [p.shape[1] for p in pieces]
    starts = [sum(widths[:i]) for i in range(len(widths))]
    assert sum(widths) == w.shape[0]
    n_p = len(pieces)

    def body(*refs):
        p_refs, (w_ref, x_ref, g_ref, dres_ref, dx_ref, dg_ref) = refs[:n_p], refs[n_p:]
        dh = _dot(p_refs[0][...], w_ref[pl.ds(starts[0], widths[0]), :], NN)
        for i in range(1, n_p):
            dh = dh + _dot(p_refs[i][...], w_ref[pl.ds(starts[i], widths[i]), :], NN)
        _, vjp = jax.vjp(_rms, x_ref[...], g_ref[...])
        dx, dg = vjp(dh)
        dx_ref[...] = dres_ref[...] + dx

        @pl.when(pl.program_id(0) == 0)
        def _():
            dg_ref[...] = jnp.zeros_like(dg_ref)

        dg_ref[...] += dg

    row = pl.BlockSpec((tm, d), lambda i: (i, 0))
    vec = pl.BlockSpec((1, d), lambda i: (0, 0))
    ex_in, ex_in_specs, ex_out, ex_out_specs, ex_scratch = _carried(ex)
    outs = pl.pallas_call(
        _carry(body, ex, n_p + 4, 2, 0, (t // tm,)), name=name, grid=(t // tm,),
        in_specs=[pl.BlockSpec((tm, k), lambda i: (i, 0)) for k in widths]
        + [pl.BlockSpec(w.shape, lambda i: (0, 0)), row, vec, row] + ex_in_specs,
        out_specs=[row, vec] + ex_out_specs,
        out_shape=[jax.ShapeDtypeStruct((t, d), F32), jax.ShapeDtypeStruct((1, d), F32)] + ex_out,
        scratch_shapes=ex_scratch,
        compiler_params=_cp("arbitrary"))(*pieces, w, x, g, dres, *ex_in)
    return (outs[0], outs[1], outs[2:]) if ex else (outs[0], outs[1])


def _merge_fwd(o_g, o_att, proj, x, w_pa, w_pb, w_o, gain, name, tm=512):
    t, d = x.shape
    tm = min(tm, t)

    def body(og_ref, oa_ref, ga0, ga1, gb0, gb1, x_ref, wpa_ref, wpb_ref, wo_ref, g_ref,
             ya_ref, yb_ref, mix_ref, h2_ref, x1_ref):
        ya = _dot(og_ref[...], wpa_ref[...], NN)
        yb = _dot(oa_ref[...], wpb_ref[...], NN)
        ga = jnp.concatenate([ga0[...], ga1[...]], axis=1)
        gb = jnp.concatenate([gb0[...], gb1[...]], axis=1)
        mix = _mix(ya, yb, ga, gb).astype(BF16)
        ya_ref[...] = ya.astype(BF16)
        yb_ref[...] = yb.astype(BF16)
        mix_ref[...] = mix
        x1 = x_ref[...] + _dot(mix, wo_ref[...], NN)
        x1_ref[...] = x1
        h2_ref[...] = _rms(x1, g_ref[...]).astype(BF16)

    row = pl.BlockSpec((tm, d), lambda i: (i, 0))
    mat = pl.BlockSpec((d, d), lambda i: (0, 0))
    return pl.pallas_call(
        body, name=name, grid=(t // tm,),
        in_specs=[row, row] + _gate_specs(tm) + [row, mat, mat, mat, pl.BlockSpec((1, d), lambda i: (0, 0))],
        out_specs=[row] * 5,
        out_shape=[jax.ShapeDtypeStruct((t, d), BF16)] * 4 + [jax.ShapeDtypeStruct((t, d), F32)],
        compiler_params=_cp("parallel"))(o_g, o_att, proj, proj, proj, proj, x, w_pa, w_pb, w_o, gain)


def _loss_head(x, g, target, name, tm=512):
    t, d = x.shape
    tm = min(tm, t)

    def body(x_ref, g_ref, t_ref, dx_ref, dg_ref, loss_ref):
        tgt = t_ref[...]

        def f(xv, gv):
            err = _rms(xv, gv) - tgt
            return 0.5 * jnp.sum(jnp.mean(err * err, axis=-1, keepdims=True))

        loss, vjp = jax.vjp(f, x_ref[...], g_ref[...])
        dx, dg = vjp(jnp.ones((), F32))
        dx_ref[...] = dx

        @pl.when(pl.program_id(0) == 0)
        def _():
            dg_ref[...] = jnp.zeros_like(dg_ref)
            loss_ref[...] = jnp.zeros_like(loss_ref)

        dg_ref[...] += dg
        loss_ref[...] += jnp.full(loss_ref.shape, loss, F32)

    row = pl.BlockSpec((tm, d), lambda i: (i, 0))
    vec = pl.BlockSpec((1, d), lambda i: (0, 0))
    lane = pl.BlockSpec((1, 128), lambda i: (0, 0))
    return pl.pallas_call(
        body, name=name, grid=(t // tm,), in_specs=[row, vec, row], out_specs=[row, vec, lane],
        out_shape=[jax.ShapeDtypeStruct((t, d), F32), jax.ShapeDtypeStruct((1, d), F32),
                   jax.ShapeDtypeStruct((1, 128), F32)],
        compiler_params=_cp("arbitrary"))(x, g, target)


def _lb_rows(l0, l1, l2, l3):
    mx = jnp.maximum(jnp.maximum(l0, l1), jnp.maximum(l2, l3))
    e0, e1, e2, e3 = jnp.exp(l0 - mx), jnp.exp(l1 - mx), jnp.exp(l2 - mx), jnp.exp(l3 - mx)
    s = e0 + e1 + e2 + e3
    p0, p1, p2, p3 = e0 / s, e1 / s, e2 / s, e3 / s
    c1 = p0 + p1
    c2 = c1 + p2
    c3 = c2 + p3
    return p0 - p0, c1 - p0, c2 - p0, c3 - p0


def _lb_fwd(lb_logits):
    def body(l_ref, o_ref):
        rows = _lb_rows(*[l_ref[pl.ds(i, 1), :] for i in range(DEPTH)])
        for i in range(DEPTH):
            o_ref[pl.ds(i, 1), :] = rows[i]

    return pl.pallas_call(body, name="lb_fwd", out_shape=jax.ShapeDtypeStruct(lb_logits.shape, F32))(lb_logits)


def _lb_bwd(lb_logits, dlb):
    def body(l_ref, d_ref, o_ref):
        _, vjp = jax.vjp(_lb_rows, *[l_ref[pl.ds(i, 1), :] for i in range(DEPTH)])
        grads = vjp(tuple(d_ref[pl.ds(i, 1), :] for i in range(DEPTH)))
        for i in range(DEPTH):
            o_ref[pl.ds(i, 1), :] = grads[i]

    return pl.pallas_call(body, name="lb_bwd", out_shape=jax.ShapeDtypeStruct(lb_logits.shape, F32))(lb_logits, dlb)


MESH = pl.DeviceIdType.MESH
ANY = pl.BlockSpec(memory_space=pl.ANY)
N_KINDS = len(SHARD_ROWS)
FFN_HALF = FFN_HIDDEN // 2
KIND_PLACE = ((0, 0), (1, 0), (2, 0), (3, 0), (4, 0), (4, FFN_HALF), (5, 0))
KIND_HALF_SKIP = (0, 0, 0, 0, FFN_HALF, FFN_HALF, 0)
FULL_ROWS = (N_DEV * SHARD_ROWS[0], D_MODEL, D_MODEL, D_MODEL, 2 * N_DEV * SHARD_ROWS[4], N_DEV * SHARD_ROWS[6])


def _kind_rows(ti, dev):
    oi, base = KIND_PLACE[ti]
    start = base + dev * SHARD_ROWS[ti]
    if KIND_HALF_SKIP[ti]:
        start = start + (dev // (N_DEV // 2)) * KIND_HALF_SKIP[ti]
    return oi, pl.ds(start, SHARD_ROWS[ti])


def _position():
    x, y, c = lax.axis_index("x"), lax.axis_index("y"), lax.axis_index("c")
    return x, y, c, 4 * x + 2 * y + c


def _peer(x, y, c, r):
    px = 1 - x if r & 4 else x
    py = 1 - y if r & 2 else y
    pc = 1 - c if r & 1 else c
    return (px, py, pc), 4 * px + 2 * py + pc


class _Exchange(NamedTuple):
    operands: tuple
    out_shape: tuple
    copies: Callable
    n_local: int


EXCHANGE_SCRATCH = (pltpu.SemaphoreType.DMA((N_DEV, N_KINDS)), pltpu.SemaphoreType.DMA((N_DEV, N_KINDS)),
                    pltpu.SemaphoreType.DMA((N_KINDS,)))
ALL_KINDS = tuple(range(N_KINDS))
KINDS_W_IN = (0,)
KINDS_REST = ALL_KINDS[1:]


def _all_pairs(kinds, ends, send_sems, recv_sems):
    x, y, c, me = _position()
    out = []
    for r in range(1, N_DEV):
        peer, pid = _peer(x, y, c, r)
        for ti in kinds:
            src, dst = ends(ti, me, pid)
            out.append(pltpu.make_async_remote_copy(
                src_ref=src, dst_ref=dst, send_sem=send_sems.at[r, ti], recv_sem=recv_sems.at[r, ti],
                device_id=peer, device_id_type=MESH))
    return out


def _gather_exchange(shards, kinds):
    arrays = sorted({KIND_PLACE[ti][0] for ti in kinds})

    def copies(ins, outs, send_sems, recv_sems, local_sems, arrivals):
        src = dict(zip(kinds, ins))

        def window(ti, dev):
            oi, rows = _kind_rows(ti, dev)
            return outs[arrays.index(oi)].at[rows, :]

        if arrivals:
            return _all_pairs(kinds, lambda ti, me, pid: (src[ti], window(ti, pid)), send_sems, recv_sems)
        _, _, _, me = _position()
        local = [pltpu.make_async_copy(src[ti], window(ti, me), local_sems.at[ti]) for ti in kinds]
        return local + _all_pairs(kinds, lambda ti, me, pid: (src[ti], window(ti, me)), send_sems, recv_sems)

    return _Exchange(tuple(shards[ti] for ti in kinds),
                     tuple(jax.ShapeDtypeStruct((FULL_ROWS[oi], D_MODEL), BF16) for oi in arrays), copies, len(kinds))


def _scatter_exchange(grads, kinds):
    arrays = sorted({KIND_PLACE[ti][0] for ti in kinds})
    offsets, total = {}, 0
    for ti in kinds:
        offsets[ti], total = total, total + SHARD_ROWS[ti]

    def copies(ins, outs, send_sems, recv_sems, local_sems, arrivals):
        land = outs[0]

        def piece(ti, dev):
            ii, rows = _kind_rows(ti, dev)
            return ins[arrays.index(ii)].at[rows, :]

        def slot(ti, dev):
            return land.at[dev, pl.ds(offsets[ti], SHARD_ROWS[ti]), :]

        if arrivals:
            return _all_pairs(kinds, lambda ti, me, pid: (piece(ti, me), slot(ti, pid)), send_sems, recv_sems)
        _, _, _, me = _position()
        local = [pltpu.make_async_copy(piece(ti, me), slot(ti, me), local_sems.at[ti]) for ti in kinds]
        return local + _all_pairs(kinds, lambda ti, me, pid: (piece(ti, pid), slot(ti, me)), send_sems, recv_sems)

    return _Exchange(tuple(grads[oi] for oi in arrays), (jax.ShapeDtypeStruct((N_DEV, total, D_MODEL), BF16),),
                     copies, len(kinds))


def _exchange_start(ex, ins, outs, sems):
    for cp in ex.copies(ins, outs, *sems, False):
        cp.start()


def _exchange_finish(ex, ins, outs, sems):
    for cp in ex.copies(ins, outs, *sems, True):
        cp.wait_recv()
    mine = ex.copies(ins, outs, *sems, False)
    for cp in mine[:ex.n_local]:
        cp.wait()
    for cp in mine[ex.n_local:]:
        cp.wait_send()


def _run_exchange(ex, name):
    n_in, n_out = len(ex.operands), len(ex.out_shape)

    def body(*refs):
        ins, outs, sems = refs[:n_in], refs[n_in:n_in + n_out], refs[n_in + n_out:]
        _exchange_start(ex, ins, outs, sems)
        _exchange_finish(ex, ins, outs, sems)

    return pl.pallas_call(body, name=name, in_specs=[ANY] * n_in, out_specs=[ANY] * n_out,
                          out_shape=list(ex.out_shape), scratch_shapes=list(EXCHANGE_SCRATCH))(*ex.operands)


def _carry(body, ex, n_in, n_out, n_scratch, grid):
    if ex is None:
        return body
    e_in, e_out = len(ex.operands), len(ex.out_shape)

    def at(step):
        hit = pl.program_id(0) == step[0]
        for axis in range(1, len(grid)):
            hit = hit & (pl.program_id(axis) == step[axis])
        return hit

    def carrying(*refs):
        own_in, ex_in = refs[:n_in], refs[n_in:n_in + e_in]
        rest = refs[n_in + e_in:]
        own_out, ex_out = rest[:n_out], rest[n_out:n_out + e_out]
        own_scratch, sems = rest[n_out + e_out:n_out + e_out + n_scratch], rest[n_out + e_out + n_scratch:]

        @pl.when(at([0] * len(grid)))
        def _():
            _exchange_start(ex, ex_in, ex_out, sems)

        body(*own_in, *own_out, *own_scratch)

        @pl.when(at([g - 1 for g in grid]))
        def _():
            _exchange_finish(ex, ex_in, ex_out, sems)

    return carrying


def _carried(ex):
    if ex is None:
        return (), [], [], [], []
    return (ex.operands, [ANY] * len(ex.operands), list(ex.out_shape), [ANY] * len(ex.out_shape),
            list(EXCHANGE_SCRATCH))


def _small_sum_body(p_ref, o_ref, buf, send_sems, recv_sems):
    x, y, c, me = _position()
    buf[me] = p_ref[...]
    sends = []
    for r in range(1, N_DEV):
        peer, _ = _peer(x, y, c, r)
        sends.append(pltpu.make_async_remote_copy(
            src_ref=p_ref, dst_ref=buf.at[me], send_sem=send_sems.at[r], recv_sem=recv_sems.at[r],
            device_id=peer, device_id_type=MESH))
    for cp in sends:
        cp.start()
    for r in range(1, N_DEV):
        peer, pid = _peer(x, y, c, r)
        pltpu.make_async_remote_copy(
            src_ref=p_ref, dst_ref=buf.at[pid], send_sem=send_sems.at[r], recv_sem=recv_sems.at[r],
            device_id=peer, device_id_type=MESH).wait_recv()
    for cp in sends:
        cp.wait_send()
    acc = buf[0]
    for k in range(1, N_DEV):
        acc = acc + buf[k]
    o_ref[...] = acc


def _all_reduce_small(part):
    rows, d = part.shape
    vmem = pl.BlockSpec(memory_space=pltpu.VMEM)
    return pl.pallas_call(
        functools.partial(_small_sum_body), name="all_reduce_small", in_specs=[vmem], out_specs=vmem,
        out_shape=jax.ShapeDtypeStruct((rows, d), F32),
        scratch_shapes=[pltpu.VMEM((N_DEV, rows, d), F32), pltpu.SemaphoreType.DMA((N_DEV,)),
                        pltpu.SemaphoreType.DMA((N_DEV,))],
    )(part)


HG_PAIR = 2 * HG_DK


def _hg_consts():
    c = HG_CHUNK
    r = lax.broadcasted_iota(jnp.int32, (c, c), 0)
    s = lax.broadcasted_iota(jnp.int32, (c, c), 1)
    r2 = lax.broadcasted_iota(jnp.int32, (c, 2 * c), 0)
    s2 = lax.broadcasted_iota(jnp.int32, (c, 2 * c), 1)
    causal2 = jnp.where(s2 >= c, s2 - c, s2) <= r2
    lane_hi = lax.broadcasted_iota(jnp.int32, (c, HG_PAIR), 1) >= HG_DK
    same_head = ((lax.broadcasted_iota(jnp.int32, (HG_PAIR, HG_PAIR), 0) >= HG_DK)
                 == (lax.broadcasted_iota(jnp.int32, (HG_PAIR, HG_PAIR), 1) >= HG_DK))
    return (s <= r).astype(BF16), (s >= r).astype(BF16), causal2, lane_hi, same_head


def _head_rows(x, lane_hi):
    zero = jnp.zeros_like(x)
    return jnp.concatenate([jnp.where(lane_hi, zero, x), jnp.where(lane_hi, x, zero)], axis=0)


def _own_rows(y, lane_hi):
    return jnp.where(lane_hi, y[HG_CHUNK:], y[:HG_CHUNK])


def _split3(x):
    hi = x.astype(BF16)
    r1 = x - hi.astype(F32)
    mid = r1.astype(BF16)
    lo = (r1 - mid.astype(F32)).astype(BF16)
    return jnp.concatenate([hi, mid, lo], axis=1)


def _cumsum_rows(tri, x):
    w = x.shape[1]
    y = _dot(tri, _split3(x), NN)
    return y[:, :w] + y[:, w:2 * w] + y[:, 2 * w:]


def _hg_gates(zq, zf, lb):
    sq = _sigmoid(zq)
    sg = _sigmoid(zf)
    f = lb + (1.0 - lb) * sg
    return sq, zq * sq, sg, f, jnp.log(jnp.maximum(f, MIN_F)), 1.0 - f


def _hg_decays(ball_ref, ci, cols):
    c = HG_CHUNK
    b = ball_ref[pl.ds(ci * c, c), cols]
    mid = ball_ref[pl.ds(ci * c + c // 2 - 1, 1), cols]
    bc = ball_ref[pl.ds(ci * c + c - 1, 1), cols]
    return b, jnp.exp(jnp.minimum(b - mid, HG_EXP_CLAMP)), jnp.exp(jnp.minimum(mid - b, HG_EXP_CLAMP)), bc


def _hg_gate(o, zg, gn):
    return o * lax.rsqrt(jnp.mean(o * o, axis=-1, keepdims=True) + EPS) * gn * (zg * _sigmoid(zg))


def _hgrn2_fwd(proj, lb, gn, name, ex=None):
    t = proj.shape[0]
    bs_tok = min(HG_BLOCK, t)
    n_chunks = bs_tok // HG_CHUNK
    w = HG_HEADS * HG_DK

    def body(hq_ref, hf_ref, hi_ref, hg_ref, lb_ref, gn_ref, o_ref, og_ref, sall_ref, ball_ref, aall_ref, st_ref):
        @pl.when(pl.program_id(0) == 0)
        def _():
            st_ref[...] = jnp.zeros_like(st_ref)

        tril, _, causal2, lane_hi, same_head = _hg_consts()

        for ci in range(n_chunks):
            rows = pl.ds(ci * HG_CHUNK, HG_CHUNK)
            for p in range(HG_HEADS // 2):
                cols = slice(p * HG_PAIR, (p + 1) * HG_PAIR)
                v = hi_ref[rows, cols]
                zg = hg_ref[rows, cols]
                _, q, _, _, logf, k = _hg_gates(hq_ref[rows, cols], hf_ref[rows, cols], lb_ref[:, cols])
                ball_ref[rows, cols] = _cumsum_rows(tril, logf)
                b, em, en, bc = _hg_decays(ball_ref, ci, cols)
                st0 = st_ref[p]
                sall_ref[ci, 2 * p] = st0[:HG_DK, :HG_DK]
                sall_ref[ci, 2 * p + 1] = st0[HG_DK:, HG_DK:]
                vb = v.astype(BF16)
                o = _dot((q * jnp.exp(b)).astype(BF16), st0.astype(BF16), NT)
                a = jnp.where(causal2, _dot((q * em).astype(BF16), _head_rows((k * en).astype(BF16), lane_hi), NT),
                              0.0).astype(BF16)
                aall_ref[ci, p] = a
                o = o + _dot(a, _head_rows(vb, lane_hi), NN)
                kdec = (k * jnp.exp(bc - b)).astype(BF16)
                st_ref[p] = st0 * jnp.exp(bc) + jnp.where(same_head, _dot(vb, kdec, TN), 0.0)
                o_ref[rows, cols] = o
                for hh in range(2):
                    sl = slice(hh * HG_DK, (hh + 1) * HG_DK)
                    hcols = slice(p * HG_PAIR + hh * HG_DK, p * HG_PAIR + (hh + 1) * HG_DK)
                    og_ref[rows, hcols] = _hg_gate(o[:, sl], zg[:, sl], gn_ref[:, hcols]).astype(BF16)

    def col(j):
        return pl.BlockSpec((bs_tok, w), lambda n, j=j: (n, j))

    vec = pl.BlockSpec((1, w), lambda n: (0, 0))
    ex_in, ex_in_specs, ex_out, ex_out_specs, ex_scratch = _carried(ex)
    outs = pl.pallas_call(
        _carry(body, ex, 6, 5, 1, (t // bs_tok,)), name=name, grid=(t // bs_tok,),
        in_specs=[col(COL_HQ // w), col(COL_HF // w), col(COL_HI // w), col(COL_HG // w), vec, vec] + ex_in_specs,
        out_specs=[col(0), col(0),
                   pl.BlockSpec((n_chunks, HG_HEADS, HG_DK, HG_DK), lambda n: (n, 0, 0, 0)), col(0),
                   pl.BlockSpec((n_chunks, HG_HEADS // 2, HG_CHUNK, 2 * HG_CHUNK), lambda n: (n, 0, 0, 0))] + ex_out_specs,
        out_shape=[jax.ShapeDtypeStruct((t, w), F32), jax.ShapeDtypeStruct((t, w), BF16),
                   jax.ShapeDtypeStruct((t // HG_CHUNK, HG_HEADS, HG_DK, HG_DK), F32), jax.ShapeDtypeStruct((t, w), F32),
                   jax.ShapeDtypeStruct((t // HG_CHUNK, HG_HEADS // 2, HG_CHUNK, 2 * HG_CHUNK), BF16)] + ex_out,
        scratch_shapes=[pltpu.VMEM((HG_HEADS // 2, HG_PAIR, HG_PAIR), F32)] + ex_scratch,
        compiler_params=_cp("arbitrary"))(proj, proj, proj, proj, lb, gn, *ex_in)
    return outs[:5], outs[5:]


def _hgrn2_bwd(proj, lb, gn, o_hg, sall, ball, aall, dog, name, ex=None):
    t = proj.shape[0]
    bs_tok = min(HG_BLOCK, t)
    n_chunks = bs_tok // HG_CHUNK
    n_blocks = t // bs_tok
    w = HG_HEADS * HG_DK

    def body(hq_ref, hf_ref, hi_ref, hg_ref, lb_ref, gn_ref, o_ref, sall_ref, snext_ref, ball_ref, aall_ref, dog_ref,
             da_ref, dlb_ref, dgn_ref, dst_ref):
        @pl.when(pl.program_id(0) == 0)
        def _():
            dst_ref[...] = jnp.zeros_like(dst_ref)
            dlb_ref[...] = jnp.zeros_like(dlb_ref)
            dgn_ref[...] = jnp.zeros_like(dgn_ref)

        _, rev_tril, causal2, lane_hi, same_head = _hg_consts()
        zero_block = jnp.zeros((HG_DK, HG_DK), F32)

        for ci in reversed(range(n_chunks)):
            rows = pl.ds(ci * HG_CHUNK, HG_CHUNK)
            for p in range(HG_HEADS // 2):
                cols = slice(p * HG_PAIR, (p + 1) * HG_PAIR)
                zq = hq_ref[rows, cols]
                v = hi_ref[rows, cols]
                zg = hg_ref[rows, cols]
                lbv = lb_ref[:, cols]
                sq, q, sg, f, _, k = _hg_gates(zq, hf_ref[rows, cols], lbv)
                b, em, en, bc = _hg_decays(ball_ref, ci, cols)
                st0 = jnp.concatenate([jnp.concatenate([sall_ref[ci, 2 * p], zero_block], axis=1),
                                       jnp.concatenate([zero_block, sall_ref[ci, 2 * p + 1]], axis=1)], axis=0)
                dst1 = dst_ref[p]
                vb = v.astype(BF16)
                eb = jnp.exp(b)
                qg = (q * eb).astype(BF16)
                qt = (q * em).astype(BF16)
                kref = (k * en).astype(BF16)
                ebcb = jnp.exp(bc - b)
                kdec = (k * ebcb).astype(BF16)
                ebc = jnp.exp(bc)
                dos, dzgs, dgns = [], [], []
                for hh in range(2):
                    sl = slice(hh * HG_DK, (hh + 1) * HG_DK)
                    hcols = slice(p * HG_PAIR + hh * HG_DK, p * HG_PAIR + (hh + 1) * HG_DK)
                    _, gate_vjp = jax.vjp(_hg_gate, o_ref[rows, hcols], zg[:, sl], gn_ref[:, hcols])
                    do_h, dzg_h, dgn_h = gate_vjp(dog_ref[rows, hcols])
                    dos.append(do_h)
                    dzgs.append(dzg_h)
                    dgns.append(dgn_h)
                dob = jnp.concatenate(dos, axis=1).astype(BF16)
                vrows, krows = _head_rows(vb, lane_hi), _head_rows(kref, lane_hi)
                dam = jnp.where(causal2, _dot(dob, vrows, NT), 0.0).astype(BF16)
                dk = ebcb * _dot(vb, dst1.astype(BF16), NN) + en * _own_rows(_dot(dam, qt, TN), lane_hi)
                dq = eb * _dot(dob, st0.astype(BF16), NN) + em * _dot(dam, krows, NN)
                dv = _own_rows(_dot(aall_ref[ci, p], dob, TN), lane_hi) + _dot(kdec, dst1.astype(BF16), NT)
                dst_ref[p] = dst1 * ebc + jnp.where(same_head, _dot(dob, qg, TN), 0.0)

                after = [sall_ref[ci + 1, 2 * p + hh] if ci + 1 < n_chunks else snext_ref[0, 2 * p + hh] for hh in range(2)]
                dbx = jnp.concatenate(
                    [jnp.sum(dst1[hh * HG_DK:(hh + 1) * HG_DK, hh * HG_DK:(hh + 1) * HG_DK] * after[hh], axis=0, keepdims=True)
                     for hh in range(2)], axis=1)
                dlogf = _cumsum_rows(rev_tril, q * dq - k * dk) + dbx
                df = jnp.where(f > MIN_F, dlogf / f, 0.0) - dk
                dzf = df * (1.0 - lbv) * sg * (1.0 - sg)
                dzq = dq * (sq * (1.0 + zq * (1.0 - sq)))
                da_ref[rows, pl.ds(COL_HQ + p * HG_PAIR, HG_PAIR)] = dzq.astype(BF16)
                da_ref[rows, pl.ds(COL_HF + p * HG_PAIR, HG_PAIR)] = dzf.astype(BF16)
                da_ref[rows, pl.ds(COL_HI + p * HG_PAIR, HG_PAIR)] = dv.astype(BF16)
                da_ref[rows, pl.ds(COL_HG + p * HG_PAIR, HG_PAIR)] = jnp.concatenate(dzgs, axis=1).astype(BF16)
                dlb_ref[:, cols] += jnp.sum(df * (1.0 - sg), axis=0, keepdims=True)
                dgn_ref[:, cols] += jnp.concatenate(dgns, axis=1)

    def col(j):
        return pl.BlockSpec((bs_tok, w), lambda n, j=j: (n_blocks - 1 - n, j))

    vec = pl.BlockSpec((1, w), lambda n: (0, 0))
    ex_in, ex_in_specs, ex_out, ex_out_specs, ex_scratch = _carried(ex)
    outs = pl.pallas_call(
        _carry(body, ex, 12, 3, 1, (n_blocks,)), name=name, grid=(n_blocks,),
        in_specs=[col(COL_HQ // w), col(COL_HF // w), col(COL_HI // w), col(COL_HG // w), vec, vec, col(0),
                  pl.BlockSpec((n_chunks, HG_HEADS, HG_DK, HG_DK), lambda n: (n_blocks - 1 - n, 0, 0, 0)),
                  pl.BlockSpec((1, HG_HEADS, HG_DK, HG_DK),
                               lambda n: (jnp.minimum((n_blocks - n) * n_chunks, t // HG_CHUNK - 1), 0, 0, 0)),
                  col(0),
                  pl.BlockSpec((n_chunks, HG_HEADS // 2, HG_CHUNK, 2 * HG_CHUNK), lambda n: (n_blocks - 1 - n, 0, 0, 0)),
                  col(0)] + ex_in_specs,
        out_specs=[pl.BlockSpec((bs_tok, 4 * w), lambda n: (n_blocks - 1 - n, 0)), vec, vec] + ex_out_specs,
        out_shape=[jax.ShapeDtypeStruct((t, 4 * w), BF16), jax.ShapeDtypeStruct((1, w), F32),
                   jax.ShapeDtypeStruct((1, w), F32)] + ex_out,
        scratch_shapes=[pltpu.VMEM((HG_HEADS // 2, HG_PAIR, HG_PAIR), F32)] + ex_scratch,
        compiler_params=_cp("arbitrary"))(proj, proj, proj, proj, lb, gn, o_hg, sall, sall, ball, aall, dog, *ex_in)
    return outs[:3], outs[3:]


def _rope_tables(t):
    half = ROPE_DIM // 2
    inv = ROPE_THETA ** (-jnp.arange(half, dtype=F32) * 2.0 / ROPE_DIM)
    d = jnp.arange(2 * ATT_HEAD_DIM) % ATT_HEAD_DIM
    ang = jnp.arange(t).astype(F32)[:, None] * inv[d % half][None, :]
    cos, sin = jnp.cos(ang), jnp.sin(ang)
    c = jnp.where(d < ROPE_DIM, cos, 1.0)
    su = jnp.where(d < half, -sin, 0.0)
    sd = jnp.where((d >= half) & (d < ROPE_DIM), sin, 0.0)
    return c, su, sd


def _rope(x, tabs):
    c, su, sd = tabs
    n = x.shape[1]
    half = ROPE_DIM // 2
    return x * c + pltpu.roll(x, n - half, 1) * su + pltpu.roll(x, half, 1) * sd


def _rope_t(dy, tabs):
    c, su, sd = tabs
    n = dy.shape[1]
    half = ROPE_DIM // 2
    return dy * c + pltpu.roll(dy * su, half, 1) + pltpu.roll(dy * sd, n - half, 1)


def _swa_specs(n_blocks, clamp):
    blk = ATT_BLOCK

    def cur(n):
        return jnp.minimum(n, n_blocks - 1) if clamp else n

    def prev(n):
        return jnp.maximum(cur(n) - 1, 0)

    q_spec = pl.BlockSpec((blk, 512), lambda m, n: (cur(n), COL_AQ // 512 + m))
    kv = [pl.BlockSpec((blk, 128), lambda m, n, c=c, f=f: (f(n), c + m))
          for c in (COL_AK // 128, COL_AV // 128) for f in (cur, prev)]
    tabs = [pl.BlockSpec((blk, 128), lambda m, n, f=f: (f(n), 0)) for f in (cur, prev) for _ in range(3)]
    return q_spec, kv, tabs, cur, prev


ATT_SCALE = ATT_HEAD_DIM ** -0.5


def _head_halves(x, upper):
    zero = jnp.zeros_like(x)
    return jnp.concatenate([jnp.where(upper, zero, x), jnp.where(upper, x, zero)], axis=0)


def _swa_scores(scores, sink, mask):
    s = jnp.where(mask, scores, -jnp.inf)
    mx = jnp.maximum(jnp.max(s, axis=-1, keepdims=True), sink)
    p = jnp.exp(s - mx)
    es = jnp.exp(sink - mx)
    rinv = 1.0 / (jnp.sum(p, axis=-1, keepdims=True) + es)
    return p * rinv, es * rinv


def _swa_window(kc_ref, kp_ref, vc_ref, vp_ref, tabs_c, tabs_p, n):
    k2 = jnp.concatenate([_rope(kp_ref[...], tabs_p), _rope(kc_ref[...], tabs_c)], axis=0)
    v2 = jnp.concatenate([vp_ref[...], vc_ref[...]], axis=0)
    blk = ATT_BLOCK
    qi = lax.broadcasted_iota(jnp.int32, (blk, 2 * blk), 0)
    kj = lax.broadcasted_iota(jnp.int32, (blk, 2 * blk), 1)
    delta = qi + blk - kj
    mask = (delta >= 0) & (delta < blk) & ((kj >= blk) | (n > 0))
    return k2, v2, mask


def _swa_fwd(proj, sinks, tabs, name, ex=None):
    t = proj.shape[0]
    n_blocks = t // ATT_BLOCK
    q_spec, kv_specs, tab_specs, _, _ = _swa_specs(n_blocks, clamp=False)

    def body(q_ref, kc_ref, kp_ref, vc_ref, vp_ref, c0, c1, c2, p0, p1, p2, sink_ref, o_ref):
        m, n = pl.program_id(0), pl.program_id(1)
        tabs_c = (c0[...], c1[...], c2[...])
        tabs_p = (p0[...], p1[...], p2[...])
        k2, v2, mask = _swa_window(kc_ref, kp_ref, vc_ref, vp_ref, tabs_c, tabs_p, n)
        k2r, v2r = pltpu.roll(k2, 64, 1), pltpu.roll(v2, 64, 1)
        upper_k = lax.broadcasted_iota(jnp.int32, k2.shape, 1) >= 64
        upper_q = lax.broadcasted_iota(jnp.int32, (ATT_BLOCK, 128), 1) >= 64
        for jj in range(2):
            own = upper_k if jj else ~upper_k
            kd = jnp.where(own, k2, k2r).astype(BF16)
            vd = jnp.where(own, v2, v2r).astype(BF16)
            for pi in range(2):
                cols = slice(256 * jj + 128 * pi, 256 * jj + 128 * pi + 128)
                qp = _rope(q_ref[:, cols], tabs_c) * ATT_SCALE
                outs = []
                for e in range(2):
                    sink = sink_ref[0, 8 * m + 4 * jj + 2 * pi + e]
                    qm = jnp.where(upper_q if e else ~upper_q, qp, 0.0).astype(BF16)
                    pn, _ = _swa_scores(_dot(qm, kd, NT), sink, mask)
                    outs.append(_dot(pn.astype(BF16), vd, NN))
                o_ref[:, cols] = jnp.where(upper_q, outs[1], outs[0]).astype(BF16)

    ex_in, ex_in_specs, ex_out, ex_out_specs, ex_scratch = _carried(ex)
    outs = pl.pallas_call(
        _carry(body, ex, 12, 1, 0, (2, n_blocks)), name=name, grid=(2, n_blocks),
        in_specs=[q_spec] + kv_specs + tab_specs + [pl.BlockSpec(memory_space=pltpu.SMEM)] + ex_in_specs,
        out_specs=[pl.BlockSpec((ATT_BLOCK, 512), lambda m, n: (n, m))] + ex_out_specs,
        out_shape=[jax.ShapeDtypeStruct((t, ATT_Q_HEADS * ATT_HEAD_DIM), BF16)] + ex_out,
        scratch_shapes=ex_scratch,
        compiler_params=_cp("arbitrary", "arbitrary"))(proj, proj, proj, proj, proj, *tabs, *tabs, sinks, *ex_in)
    return outs[0], outs[1:]


def _swa_bwd(proj, sinks, tabs, o_att, do_att, name, ex=None):
    t = proj.shape[0]
    n_blocks = t // ATT_BLOCK
    blk = ATT_BLOCK
    q_spec, kv_specs, tab_specs, cur, prev = _swa_specs(n_blocks, clamp=True)

    def body(q_ref, kc_ref, kp_ref, vc_ref, vp_ref, c0, c1, c2, p0, p1, p2, sink_ref, o_ref, do_ref,
             dq_ref, dk_ref, dv_ref, ds_ref, ck_ref, cv_ref):
        m, n = pl.program_id(0), pl.program_id(1)

        @pl.when(n == 0)
        def _():
            ds_ref[...] = jnp.zeros_like(ds_ref)
            ck_ref[...] = jnp.zeros_like(ck_ref)
            cv_ref[...] = jnp.zeros_like(cv_ref)

        @pl.when(n < n_blocks)
        def _():
            tabs_c = (c0[...], c1[...], c2[...])
            tabs_p = (p0[...], p1[...], p2[...])
            k2, v2, mask = _swa_window(kc_ref, kp_ref, vc_ref, vp_ref, tabs_c, tabs_p, n)
            k2r, v2r = pltpu.roll(k2, 64, 1), pltpu.roll(v2, 64, 1)
            upper_k = lax.broadcasted_iota(jnp.int32, k2.shape, 1) >= 64
            upper_q = lax.broadcasted_iota(jnp.int32, (blk, 128), 1) >= 64
            lane = lax.broadcasted_iota(jnp.int32, (8, 128), 1)
            dk2 = jnp.zeros(k2.shape, F32)
            dv2 = jnp.zeros(k2.shape, F32)
            dsv = jnp.zeros((8, 128), F32)
            nk = 2 * blk
            for jj in range(2):
                own = upper_k if jj else ~upper_k
                kh = _head_halves(jnp.where(own, k2, k2r).astype(BF16), upper_k)
                vh = _head_halves(jnp.where(own, v2, v2r).astype(BF16), upper_k)
                dkd = jnp.zeros(k2.shape, F32)
                dvd = jnp.zeros(k2.shape, F32)
                for pi in range(2):
                    cols = slice(256 * jj + 128 * pi, 256 * jj + 128 * pi + 128)
                    qp = (_rope(q_ref[:, cols], tabs_c) * ATT_SCALE).astype(BF16)
                    do_pair = do_ref[:, cols]
                    o_pair = o_ref[:, cols].astype(F32)
                    dob = do_pair.astype(BF16)
                    s = _dot(qp, kh, NT)
                    dp = _dot(dob, vh, NT)
                    pns, dss = [], []
                    for e in range(2):
                        hl = 4 * jj + 2 * pi + e
                        pn, ps = _swa_scores(s[:, e * nk:(e + 1) * nk], sink_ref[0, 8 * m + hl], mask)
                        delta = jnp.sum(jnp.where(upper_q if e else ~upper_q, do_pair * o_pair, 0.0), axis=-1, keepdims=True)
                        pns.append(pn.astype(BF16))
                        dss.append((pn * (dp[:, e * nk:(e + 1) * nk] - delta)).astype(BF16))
                        dsv = dsv + jnp.where(lane == hl, -jnp.sum(ps * delta), 0.0)
                    dsb = jnp.concatenate(dss, axis=1)
                    dq_ref[:, cols] = _rope_t(_dot(dsb, kh, NN) * ATT_SCALE, tabs_c).astype(BF16)
                    rk = _dot(dsb, qp, TN)
                    rv = _dot(jnp.concatenate(pns, axis=1), dob, TN)
                    dkd = dkd + jnp.where(upper_k, rk[nk:], rk[:nk])
                    dvd = dvd + jnp.where(upper_k, rv[nk:], rv[:nk])
                dk2 = dk2 + jnp.where(own, dkd + pltpu.roll(dkd, 64, 1), 0.0)
                dv2 = dv2 + jnp.where(own, dvd + pltpu.roll(dvd, 64, 1), 0.0)
            dk_ref[...] = (ck_ref[...] + _rope_t(dk2[:blk], tabs_p)).astype(BF16)
            dv_ref[...] = (cv_ref[...] + dv2[:blk]).astype(BF16)
            ck_ref[...] = _rope_t(dk2[blk:], tabs_c)
            cv_ref[...] = dv2[blk:]
            ds_ref[...] += dsv

        @pl.when(n == n_blocks)
        def _():
            dk_ref[...] = ck_ref[...].astype(BF16)
            dv_ref[...] = cv_ref[...].astype(BF16)

    wide = pl.BlockSpec((blk, 512), lambda m, n: (cur(n), m))
    lagged = pl.BlockSpec((blk, 128), lambda m, n: (jnp.maximum(n - 1, 0), m))
    ex_in, ex_in_specs, ex_out, ex_out_specs, ex_scratch = _carried(ex)
    outs = pl.pallas_call(
        _carry(body, ex, 14, 4, 2, (2, n_blocks + 1)), name=name, grid=(2, n_blocks + 1),
        in_specs=[q_spec] + kv_specs + tab_specs + [pl.BlockSpec(memory_space=pltpu.SMEM), wide, wide] + ex_in_specs,
        out_specs=[wide, lagged, lagged, pl.BlockSpec((None, 8, 128), lambda m, n: (m, 0, 0))] + ex_out_specs,
        out_shape=[jax.ShapeDtypeStruct((t, 1024), BF16), jax.ShapeDtypeStruct((t, 256), BF16),
                   jax.ShapeDtypeStruct((t, 256), BF16), jax.ShapeDtypeStruct((2, 8, 128), F32)] + ex_out,
        scratch_shapes=[pltpu.VMEM((blk, 128), F32), pltpu.VMEM((blk, 128), F32)] + ex_scratch,
        compiler_params=_cp("arbitrary", "arbitrary"))(proj, proj, proj, proj, proj, *tabs, *tabs, sinks, o_att, do_att,
                                                       *ex_in)
    return outs[:4], outs[4:]


def _local_step(x, target, shards, norm1, lb_logits, hg_norm, attn_sinks, norm2, final_norm):
    t = x.shape[0]
    tabs = _rope_tables(t)
    lb_all = _lb_fwd(lb_logits)
    saved = []

    def shards_of(l):
        return {ti: shards[ti][l] for ti in ALL_KINDS}

    win_next = _run_exchange(_gather_exchange(shards_of(0), KINDS_W_IN), "gather_w_in")
    rest_next = None
    for l in range(DEPTH):
        n1, n2 = norm1[l][None, :], norm2[l][None, :]
        lb, gn, sinks = lb_all[l][None, :], hg_norm[l][None, :], attn_sinks[l][None, :]
        (win_t,) = win_next
        if l == 0:
            h = _rms_fwd(x, n1, "rms1_fwd")
        if l == 0:
            proj, rest_next = _matmul_nt(h, win_t, 0, IN_COLS, F32, "proj_fwd", tn=1280,
                                         ex=_gather_exchange(shards_of(0), KINDS_REST))
        else:
            proj = _matmul_nt(h, win_t, 0, IN_COLS, F32, "proj_fwd", tn=1280)
        w_pa, w_pb, w_o, wgu_t, w_d = rest_next
        more = l + 1 < DEPTH
        (o_hg, o_g, sall, ball, aall), rest_next = _hgrn2_fwd(
            proj, lb, gn, "hgrn2_fwd", _gather_exchange(shards_of(l + 1), KINDS_REST) if more else None)
        o_att, win_next = _swa_fwd(
            proj, sinks, tabs, "swa_fwd", _gather_exchange(shards_of(l + 1), KINDS_W_IN) if more else None)
        ya, yb, mix, h2, x1 = _merge_fwd(o_g, o_att, proj, x, w_pa, w_pb, w_o, n2, "merge_fwd")
        gu, act = _ffn_up_fwd(h2, wgu_t, "ffn_up_fwd")
        if more:
            x2, h_next = _matmul_nn(act, w_d, 0, x1, "wd_fwd", gain=norm1[l + 1][None, :])
        else:
            x2, h_next = _matmul_nn(act, w_d, 0, x1, "wd_fwd_last"), None
        saved.append((x, h, proj, o_hg, o_g, (sall, ball, aall), o_att, ya, yb, mix, x1, h2, gu, act, n1, n2, lb, gn, sinks,
                      (win_t, w_pa, w_pb, w_o, wgu_t, w_d)))
        x, h = x2, h_next

    dx, d_fn, loss = _loss_head(x, final_norm[None, :], target, "loss_head")

    owned = [None] * DEPTH
    pending = None
    d_n1, d_n2, d_lb, d_gn, d_sinks = ([None] * DEPTH for _ in range(5))
    for l in reversed(range(DEPTH)):
        x0, h, proj, o_hg, o_g, hg_saved, o_att, ya, yb, mix, x1, h2, gu, act, n1, n2, lb, gn, sinks, weights = saved[l]
        win_t, w_pa, w_pb, w_o, wgu_t, w_d = weights
        dgu = _ffn_down_bwd(dx, w_d, gu, "ffn_down_bwd")
        g_wd = _matmul_tn(act, dx, "wd_grad", tm=1408)
        g_wgu = _matmul_tn(dgu, h2, "wgu_grad", tm=1408)
        late = pending is not None
        dx1, d_n2[l], *land_w = _rows_bwd([dgu], wgu_t, x1, n2, dx, "ffn_up_bwd", tm=512,
                                          ex=_scatter_exchange(pending, KINDS_W_IN) if late else None)
        g_wo = _matmul_tn(mix, dx1, "wo_grad")
        dya, dyb, dgab, dog, doatt = _merge_bwd(dx1, ya, yb, proj, w_pa, w_pb, w_o, "merge_bwd")
        g_wpa = _matmul_tn(o_g, dya, "wpa_grad")
        g_wpb = _matmul_tn(o_att, dyb, "wpb_grad")
        (dhg, d_lb[l], d_gn[l]), land_r = _hgrn2_bwd(proj, lb, gn, o_hg, *hg_saved, dog, "hgrn2_bwd",
                                                     _scatter_exchange(pending, KINDS_REST) if late else None)
        if late:
            owned[l + 1] = jnp.concatenate([_sum_slots(land_w[0][0], "sum_slots_w_in"),
                                            _sum_slots(land_r[0], "sum_slots_rest")], axis=0)
        ex = _scatter_exchange((None, g_wpa, g_wpb, g_wo, g_wgu, g_wd), KINDS_REST) if l == 0 else None
        (daq, dak, dav, d_sinks[l]), land_rest = _swa_bwd(proj, sinks, tabs, o_att, doatt, "swa_bwd", ex)
        g_win = None
        for piece, off, tm, tag in ((dhg, COL_HQ, 512, "hg"), (daq, COL_AQ, 512, "aq"), (dak, COL_AK, 256, "ak"),
                                    (dav, COL_AV, 256, "av"), (dgab, COL_GA, 512, "gates")):
            g_win = _matmul_tn(piece, h, "win_grad_" + tag, tm=tm, rows=IN_COLS, row_off=off, into=g_win)
        if l > 0:
            dx, d_n1[l] = _rows_bwd([dhg, daq, dak, dav, dgab], win_t, x0, n1, dx1, "win_bwd")
        else:
            dx, d_n1[l], land_win = _rows_bwd([dhg, daq, dak, dav, dgab], win_t, x0, n1, dx1, "win_bwd",
                                              ex=_scatter_exchange((g_win,), KINDS_W_IN))
        pending = (g_win, g_wpa, g_wpb, g_wo, g_wgu, g_wd)
    owned[0] = jnp.concatenate([_sum_slots(land_win[0], "sum_slots_w_in"), _sum_slots(land_rest[0], "sum_slots_rest")],
                               axis=0)

    d_sink_rows = [jnp.concatenate([d[0, 0, :8], d[1, 0, :8]]) for d in d_sinks]
    small = (jnp.concatenate(d_n1, axis=0), jnp.concatenate(d_lb, axis=0), jnp.concatenate(d_gn, axis=0),
             jnp.concatenate(d_n2, axis=0), d_fn, jnp.stack(d_sink_rows, axis=0))
    return loss, dx, jnp.stack(owned, axis=0), small


def _sum_slots(land, name, tr=480):
    _, rows, d = land.shape

    def body(l_ref, o_ref):
        acc = l_ref[0].astype(F32)
        for k in range(1, N_DEV):
            acc = acc + l_ref[k].astype(F32)
        o_ref[...] = acc

    return pl.pallas_call(
        body, name=name, grid=(rows // tr,),
        in_specs=[pl.BlockSpec((N_DEV, tr, d), lambda i: (0, i, 0))],
        out_specs=pl.BlockSpec((tr, d), lambda i: (i, 0)),
        out_shape=jax.ShapeDtypeStruct((rows, d), F32),
        compiler_params=_cp("parallel"))(land)


def _adamw(w, g, m, v, name):
    shape = w.shape
    c = shape[-1]
    rows = w.size // c
    tr = rows
    for cand in (512, 352, 128):
        if rows % cand == 0:
            tr = cand
            break
    c1 = 1.0 / (1.0 - ADAM_B1 ** ADAM_STEP)
    c2 = 1.0 / (1.0 - ADAM_B2 ** ADAM_STEP)

    def body(w_ref, g_ref, m_ref, v_ref, d_ref, nm_ref, nv_ref):
        gv = g_ref[...]
        nm = ADAM_B1 * m_ref[...] + (1.0 - ADAM_B1) * gv
        nv = ADAM_B2 * v_ref[...] + (1.0 - ADAM_B2) * (gv * gv)
        d_ref[...] = -ADAM_LR * ((nm * c1) / (jnp.sqrt(nv * c2) + ADAM_EPS) + ADAM_WD * w_ref[...])
        nm_ref[...] = nm
        nv_ref[...] = nv

    spec = pl.BlockSpec((tr, c), lambda i: (i, 0))
    outs = pl.pallas_call(
        body, name=name, grid=(rows // tr,), in_specs=[spec] * 4, out_specs=[spec] * 3,
        out_shape=[jax.ShapeDtypeStruct((rows, c), F32)] * 3,
        compiler_params=_cp("parallel"))(*[a.reshape(rows, c) for a in (w, g, m, v)])
    return tuple(o.reshape(shape) for o in outs)


def kernel(x, norm1, w_in, lb_logits, hg_norm, attn_sinks, w_pa, w_pb, w_o, norm2, w_gate, w_up, w_down, final_norm, loss_target, m_norm1, m_w_in, m_lb_logits, m_hg_norm, m_attn_sinks, m_w_pa, m_w_pb, m_w_o, m_norm2, m_w_gate, m_w_up, m_w_down, m_final_norm, v_norm1, v_w_in, v_lb_logits, v_hg_norm, v_attn_sinks, v_w_pa, v_w_pb, v_w_o, v_norm2, v_w_gate, v_w_up, v_w_down, v_final_norm):
    t = x.shape[1]
    shards = [jnp.swapaxes(w_in, 1, 2).astype(BF16), w_pa.astype(BF16), w_pb.astype(BF16), w_o.astype(BF16),
              jnp.swapaxes(w_gate, 1, 2).astype(BF16), jnp.swapaxes(w_up, 1, 2).astype(BF16), w_down.astype(BF16)]
    loss_lanes, grad_x, owned, small = _local_step(
        x.reshape(t, D_MODEL), loss_target.reshape(t, D_MODEL), shards,
        norm1, lb_logits, hg_norm, attn_sinks, norm2, final_norm)

    def rows_of(ti, transpose):
        g = owned[:, SLOT_OFF[ti]:SLOT_OFF[ti] + SHARD_ROWS[ti], :]
        return jnp.swapaxes(g, 1, 2) if transpose else g

    g_big = {"w_in": rows_of(0, True), "w_pa": rows_of(1, False), "w_pb": rows_of(2, False), "w_o": rows_of(3, False),
             "w_gate": rows_of(4, True), "w_up": rows_of(5, True), "w_down": rows_of(6, False)}

    d_n1, d_lb, d_gn, d_n2, d_fn, d_sinks = small
    pad = jnp.zeros((DEPTH, D_MODEL - ATT_Q_HEADS), F32)
    packed = jnp.concatenate([
        d_n1, d_lb, d_gn, d_n2, d_fn, jnp.concatenate([d_sinks, pad], axis=1),
        jnp.concatenate([loss_lanes, jnp.zeros((1, D_MODEL - 128), F32)], axis=1),
        jnp.zeros((SMALL_ROWS - 22, D_MODEL), F32)], axis=0)
    total = _all_reduce_small(packed)
    loss = total[21, 0]
    g_small = {"norm1": total[0:4], "lb_logits": _lb_bwd(lb_logits, total[4:8]), "hg_norm": total[8:12],
               "norm2": total[12:16], "final_norm": total[16], "attn_sinks": total[17:21, :ATT_Q_HEADS]}

    params = {"norm1": (norm1, m_norm1, v_norm1), "w_in": (w_in, m_w_in, v_w_in),
              "lb_logits": (lb_logits, m_lb_logits, v_lb_logits), "hg_norm": (hg_norm, m_hg_norm, v_hg_norm),
              "attn_sinks": (attn_sinks, m_attn_sinks, v_attn_sinks), "w_pa": (w_pa, m_w_pa, v_w_pa),
              "w_pb": (w_pb, m_w_pb, v_w_pb), "w_o": (w_o, m_w_o, v_w_o), "norm2": (norm2, m_norm2, v_norm2),
              "w_gate": (w_gate, m_w_gate, v_w_gate), "w_up": (w_up, m_w_up, v_w_up),
              "w_down": (w_down, m_w_down, v_w_down), "final_norm": (final_norm, m_final_norm, v_final_norm)}
    order = ["norm1", "w_in", "lb_logits", "hg_norm", "attn_sinks", "w_pa", "w_pb", "w_o", "norm2",
             "w_gate", "w_up", "w_down", "final_norm"]
    grads, deltas, new_m, new_v = [], [], [], []
    for name in order:
        w, m, v = params[name]
        g = (g_big[name] if name in g_big else g_small[name]).reshape(w.shape)
        w2 = w.reshape(1, -1) if w.ndim == 1 else w
        d, nm, nv = _adamw(w2, g.reshape(w2.shape), m.reshape(w2.shape), v.reshape(w2.shape), "adamw_" + name)
        grads.append(g)
        deltas.append(d.reshape(w.shape))
        new_m.append(nm.reshape(w.shape))
        new_v.append(nv.reshape(w.shape))
    return (loss, grad_x.reshape(x.shape), *grads, *deltas, *new_m, *new_v)
```

```python
import functools
from typing import Callable, NamedTuple

import jax
import jax.numpy as jnp
from jax import lax
from jax.experimental import pallas as pl
from jax.experimental.pallas import tpu as pltpu

F32, BF16 = jnp.float32, jnp.bfloat16

D_MODEL = 1024
DEPTH = 4
N_DEV = 8
HG_HEADS = 8
HG_DK = 128
HG_CHUNK = 64
HG_BLOCK = 256
HG_EXP_CLAMP = 60.0
ATT_Q_HEADS = 16
ATT_HEAD_DIM = 64
ATT_BLOCK = 128
ROPE_THETA = 500000.0
ROPE_DIM = 16
FFN_HIDDEN = 2816
EPS = 1e-6
MIN_F = 1e-30
ADAM_LR, ADAM_B1, ADAM_B2, ADAM_EPS, ADAM_WD, ADAM_STEP = 0.001, 0.9, 0.999, 1e-08, 0.01, 10

COL_HQ, COL_HF, COL_HI, COL_HG = 0, 1024, 2048, 3072
COL_AQ, COL_AK, COL_AV, COL_GA, COL_GB = 4096, 5120, 5376, 5632, 6656
IN_COLS = 7680

SHARD_ROWS = (960, 128, 128, 128, 352, 352, 352)
SLOT_OFF = (0, 960, 1088, 1216, 1344, 1696, 2048)
SLOT_ROWS = 2400
SMALL_ROWS = 24

VMEM_LIMIT_BYTES = 56 * 1024 * 1024

NN = ((1,), (0,))
NT = ((1,), (1,))
TN = ((0,), (0,))


def _dot(a, b, dims):
    return lax.dot_general(a, b, (dims, ((), ())), preferred_element_type=F32)


def _cp(*sem):
    return pltpu.CompilerParams(dimension_semantics=sem if sem else None, vmem_limit_bytes=VMEM_LIMIT_BYTES)


def _sigmoid(x):
    return 0.5 * jnp.tanh(0.5 * x) + 0.5


def _matmul_nt(a, w, row_off, n, out_dtype, name, tm=1024, tn=512, ex=None):
    t, k = a.shape
    tm = min(tm, t)
    assert n % tn == 0 and row_off % tn == 0 and t % tm == 0
    grid = (n // tn, t // tm)

    def body(a_ref, w_ref, o_ref):
        o_ref[...] = _dot(a_ref[...].astype(BF16), w_ref[...], NT).astype(o_ref.dtype)

    ex_in, ex_in_specs, ex_out, ex_out_specs, ex_scratch = _carried(ex)
    outs = pl.pallas_call(
        _carry(body, ex, 2, 1, 0, grid), name=name, grid=grid,
        in_specs=[pl.BlockSpec((tm, k), lambda j, i: (i, 0)),
                  pl.BlockSpec((tn, k), lambda j, i: (row_off // tn + j, 0))] + ex_in_specs,
        out_specs=[pl.BlockSpec((tm, tn), lambda j, i: (i, j))] + ex_out_specs,
        out_shape=[jax.ShapeDtypeStruct((t, n), out_dtype)] + ex_out,
        scratch_shapes=ex_scratch,
        compiler_params=_cp("arbitrary", "arbitrary") if ex else _cp("parallel", "parallel"))(a, w, *ex_in)
    return (outs[0], outs[1:]) if ex else outs[0]


def _matmul_nn(a, w, row_off, res, name, tm=512, tk=None, gain=None):
    t, k = a.shape
    n = w.shape[1]
    tm = min(tm, t)
    tk = tk or k
    nk = k // tk
    assert k % tk == 0 and row_off % tk == 0 and t % tm == 0

    def body(*refs):
        refs = list(refs)
        a_ref, w_ref = refs[:2]
        r_ref = refs[2] if res is not None else None
        g_ref = refs[2 + (res is not None)] if gain is not None else None
        acc = refs[-1]
        o_ref = refs[-3] if gain is not None else refs[-2]
        kk = pl.program_id(1)
        part = _dot(a_ref[...].astype(BF16), w_ref[...], NN)

        @pl.when(kk == 0)
        def _():
            acc[...] = part

        @pl.when(kk > 0)
        def _():
            acc[...] += part

        @pl.when(kk == nk - 1)
        def _():
            y = acc[...] if res is None else acc[...] + r_ref[...]
            o_ref[...] = y
            if gain is not None:
                refs[-2][...] = _rms(y, g_ref[...]).astype(BF16)

    row = pl.BlockSpec((tm, n), lambda i, kk: (i, 0))
    in_specs = [pl.BlockSpec((tm, tk), lambda i, kk: (i, kk)),
                pl.BlockSpec((tk, n), lambda i, kk: (row_off // tk + kk, 0))]
    args = [a, w]
    if res is not None:
        in_specs.append(row)
        args.append(res)
    if gain is not None:
        in_specs.append(pl.BlockSpec((1, n), lambda i, kk: (0, 0)))
        args.append(gain)
    outs = pl.pallas_call(
        body, name=name, grid=(t // tm, nk), in_specs=in_specs,
        out_specs=[row, row] if gain is not None else [row],
        out_shape=[jax.ShapeDtypeStruct((t, n), F32)] + ([jax.ShapeDtypeStruct((t, n), BF16)] if gain is not None else []),
        scratch_shapes=[pltpu.VMEM((tm, n), F32)],
        compiler_params=_cp("parallel", "arbitrary"))(*args)
    return tuple(outs) if gain is not None else outs[0]


def _matmul_tn(a, b, name, tm=512, tk=2048, rows=None, row_off=0, into=None):
    t, m = a.shape
    n = b.shape[1]
    tk = min(tk, t)
    nk = t // tk
    rows = rows or m
    assert m % tm == 0 and t % tk == 0 and row_off % tm == 0

    def body(*refs):
        a_ref, b_ref, o_ref, acc = refs[0], refs[1], refs[-2], refs[-1]
        kk = pl.program_id(1)
        part = _dot(a_ref[...].astype(BF16), b_ref[...].astype(BF16), TN)

        @pl.when(kk == 0)
        def _():
            acc[...] = part

        @pl.when(kk > 0)
        def _():
            acc[...] += part

        @pl.when(kk == nk - 1)
        def _():
            o_ref[...] = acc[...].astype(BF16)

    return pl.pallas_call(
        body, name=name, grid=(m // tm, nk),
        in_specs=[pl.BlockSpec((tk, tm), lambda i, kk: (kk, i)),
                  pl.BlockSpec((tk, n), lambda i, kk: (kk, 0))] + ([ANY] if into is not None else []),
        out_specs=pl.BlockSpec((tm, n), lambda i, kk: (row_off // tm + i, 0)),
        out_shape=jax.ShapeDtypeStruct((rows, n), BF16),
        scratch_shapes=[pltpu.VMEM((tm, n), F32)],
        input_output_aliases={2: 0} if into is not None else {},
        compiler_params=_cp("parallel", "arbitrary"))(a, b, *([into] if into is not None else []))


def _rms(x, g):
    return x * lax.rsqrt(jnp.mean(x * x, axis=-1, keepdims=True) + EPS) * g


def _rms_fwd(x, g, name, ex, tm=512):
    t, d = x.shape
    tm = min(tm, t)

    def body(x_ref, g_ref, o_ref):
        o_ref[...] = _rms(x_ref[...], g_ref[...]).astype(BF16)

    ex_in, ex_in_specs, ex_out, ex_out_specs, ex_scratch = _carried(ex)
    outs = pl.pallas_call(
        _carry(body, ex, 2, 1, 0, (t // tm,)), name=name, grid=(t // tm,),
        in_specs=[pl.BlockSpec((tm, d), lambda i: (i, 0)), pl.BlockSpec((1, d), lambda i: (0, 0))] + ex_in_specs,
        out_specs=[pl.BlockSpec((tm, d), lambda i: (i, 0))] + ex_out_specs,
        out_shape=[jax.ShapeDtypeStruct((t, d), BF16)] + ex_out,
        scratch_shapes=ex_scratch,
        compiler_params=_cp("arbitrary"))(x, g, *ex_in)
    return outs[0], outs[1:]


def _mix(ya, yb, ga, gb):
    return _sigmoid(ga) * ya + _sigmoid(gb) * yb


def _gate_specs(tm):
    half = D_MODEL // 2
    return [pl.BlockSpec((tm, half), lambda i, c=c: (i, c))
            for c in (COL_GA // half, COL_GA // half + 1, COL_GB // half, COL_GB // half + 1)]


def _merge_bwd(dx1, ya, yb, proj, w_pa, w_pb, w_o, name, tm=512):
    t, d = dx1.shape
    tm = min(tm, t)

    def body(dx_ref, ya_ref, yb_ref, ga0, ga1, gb0, gb1, wpa_ref, wpb_ref, wo_ref,
             dya_ref, dyb_ref, dg_ref, dog_ref, doa_ref):
        dmix = _dot(dx_ref[...].astype(BF16), wo_ref[...], NT)
        ga = jnp.concatenate([ga0[...], ga1[...]], axis=1)
        gb = jnp.concatenate([gb0[...], gb1[...]], axis=1)
        _, vjp = jax.vjp(_mix, ya_ref[...].astype(F32), yb_ref[...].astype(F32), ga, gb)
        dya, dyb, dga, dgb = vjp(dmix)
        dya, dyb = dya.astype(BF16), dyb.astype(BF16)
        dya_ref[...] = dya
        dyb_ref[...] = dyb
        dg_ref[:, :d] = dga.astype(BF16)
        dg_ref[:, d:] = dgb.astype(BF16)
        dog_ref[...] = _dot(dya, wpa_ref[...], NT)
        doa_ref[...] = _dot(dyb, wpb_ref[...], NT)

    row = pl.BlockSpec((tm, d), lambda i: (i, 0))
    wide = pl.BlockSpec((tm, 2 * d), lambda i: (i, 0))
    mat = pl.BlockSpec((d, d), lambda i: (0, 0))
    return pl.pallas_call(
        body, name=name, grid=(t // tm,), in_specs=[row, row, row] + _gate_specs(tm) + [mat, mat, mat],
        out_specs=[row, row, wide, row, row],
        out_shape=[jax.ShapeDtypeStruct((t, d), BF16), jax.ShapeDtypeStruct((t, d), BF16),
                   jax.ShapeDtypeStruct((t, 2 * d), BF16), jax.ShapeDtypeStruct((t, d), F32),
                   jax.ShapeDtypeStruct((t, d), F32)],
        compiler_params=_cp("parallel"))(dx1, ya, yb, proj, proj, proj, proj, w_pa, w_pb, w_o)


def _swiglu(g, u):
    return g * _sigmoid(g) * u


def _swiglu_bwd(g, u, dact):
    sg = _sigmoid(g)
    gs = g * sg
    return dact * u * (sg + gs * (1.0 - sg)), dact * gs


def _ffn_up_fwd(h2, wgu_t, name, tm=512):
    t, d = h2.shape
    tm = min(tm, t)
    fh = FFN_HIDDEN // 2

    def body(a_ref, w_ref, gu_ref, act_ref):
        r = _dot(a_ref[...], w_ref[...], NT)
        gu_ref[...] = r.astype(BF16)
        act_ref[...] = _swiglu(r[:, :fh], r[:, fh:]).astype(BF16)

    return pl.pallas_call(
        body, name=name, grid=(2, t // tm),
        in_specs=[pl.BlockSpec((tm, d), lambda j, i: (i, 0)), pl.BlockSpec((2 * fh, d), lambda j, i: (j, 0))],
        out_specs=[pl.BlockSpec((tm, 2 * fh), lambda j, i: (i, j)), pl.BlockSpec((tm, fh), lambda j, i: (i, j))],
        out_shape=[jax.ShapeDtypeStruct((t, 4 * fh), BF16), jax.ShapeDtypeStruct((t, 2 * fh), BF16)],
        compiler_params=_cp("parallel", "parallel"))(h2, wgu_t)


def _ffn_down_bwd(dx, w_d, gu, name, tm=512):
    t, d = dx.shape
    tm = min(tm, t)
    fh = FFN_HIDDEN // 2

    def body(a_ref, w_ref, gu_ref, o_ref):
        dact = _dot(a_ref[...].astype(BF16), w_ref[...], NT)
        dg, du = _swiglu_bwd(gu_ref[:, :fh].astype(F32), gu_ref[:, fh:].astype(F32), dact)
        o_ref[:, :fh] = dg.astype(BF16)
        o_ref[:, fh:] = du.astype(BF16)

    wide = pl.BlockSpec((tm, 2 * fh), lambda j, i: (i, j))
    return pl.pallas_call(
        body, name=name, grid=(2, t // tm),
        in_specs=[pl.BlockSpec((tm, d), lambda j, i: (i, 0)), pl.BlockSpec((fh, d), lambda j, i: (j, 0)), wide],
        out_specs=wide,
        out_shape=jax.ShapeDtypeStruct((t, 4 * fh), BF16),
        compiler_params=_cp("parallel", "parallel"))(dx, w_d, gu)


def _rows_bwd(pieces, w, x, g, dres, name, tm=256, ex=None):
    t, d = x.shape
    tm = min(tm, t)
    widths = [p.shape[1] for p in pieces]
    starts = [sum(widths[:i]) for i in range(len(widths))]
    assert sum(widths) == w.shape[0]
    n_p = len(pieces)

    def body(*refs):
        p_refs, (w_ref, x_ref, g_ref, dres_ref, dx_ref, dg_ref) = refs[:n_p], refs[n_p:]
        dh = _dot(p_refs[0][...], w_ref[pl.ds(starts[0], widths[0]), :], NN)
        for i in range(1, n_p):
            dh = dh + _dot(p_refs[i][...], w_ref[pl.ds(starts[i], widths[i]), :], NN)
        _, vjp = jax.vjp(_rms, x_ref[...], g_ref[...])
        dx, dg = vjp(dh)
        dx_ref[...] = dres_ref[...] + dx

        @pl.when(pl.program_id(0) == 0)
        def _():
            dg_ref[...] = jnp.zeros_like(dg_ref)

        dg_ref[...] += dg

    row = pl.BlockSpec((tm, d), lambda i: (i, 0))
    vec = pl.BlockSpec((1, d), lambda i: (0, 0))
    ex_in, ex_in_specs, ex_out, ex_out_specs, ex_scratch = _carried(ex)
    outs = pl.pallas_call(
        _carry(body, ex, n_p + 4, 2, 0, (t // tm,)), name=name, grid=(t // tm,),
        in_specs=[pl.BlockSpec((tm, k), lambda i: (i, 0)) for k in widths]
        + [pl.BlockSpec(w.shape, lambda i: (0, 0)), row, vec, row] + ex_in_specs,
        out_specs=[row, vec] + ex_out_specs,
        out_shape=[jax.ShapeDtypeStruct((t, d), F32), jax.ShapeDtypeStruct((1, d), F32)] + ex_out,
        scratch_shapes=ex_scratch,
        compiler_params=_cp("arbitrary"))(*pieces, w, x, g, dres, *ex_in)
    return (outs[0], outs[1], outs[2:]) if ex else (outs[0], outs[1])


def _merge_fwd(o_g, o_att, proj, x, w_pa, w_pb, w_o, gain, name, tm=512):
    t, d = x.shape
    tm = min(tm, t)

    def body(og_ref, oa_ref, ga0, ga1, gb0, gb1, x_ref, wpa_ref, wpb_ref, wo_ref, g_ref,
             ya_ref, yb_ref, mix_ref, h2_ref, x1_ref):
        ya = _dot(og_ref[...], wpa_ref[...], NN)
        yb = _dot(oa_ref[...], wpb_ref[...], NN)
        ga = jnp.concatenate([ga0[...], ga1[...]], axis=1)
        gb = jnp.concatenate([gb0[...], gb1[...]], axis=1)
        mix = _mix(ya, yb, ga, gb).astype(BF16)
        ya_ref[...] = ya.astype(BF16)
        yb_ref[...] = yb.astype(BF16)
        mix_ref[...] = mix
        x1 = x_ref[...] + _dot(mix, wo_ref[...], NN)
        x1_ref[...] = x1
        h2_ref[...] = _rms(x1, g_ref[...]).astype(BF16)

    row = pl.BlockSpec((tm, d), lambda i: (i, 0))
    mat = pl.BlockSpec((d, d), lambda i: (0, 0))
    return pl.pallas_call(
        body, name=name, grid=(t // tm,),
        in_specs=[row, row] + _gate_specs(tm) + [row, mat, mat, mat, pl.BlockSpec((1, d), lambda i: (0, 0))],
        out_specs=[row] * 5,
        out_shape=[jax.ShapeDtypeStruct((t, d), BF16)] * 4 + [jax.ShapeDtypeStruct((t, d), F32)],
        compiler_params=_cp("parallel"))(o_g, o_att, proj, proj, proj, proj, x, w_pa, w_pb, w_o, gain)


def _loss_head(x, g, target, name, tm=512):
    t, d = x.shape
    tm = min(tm, t)

    def body(x_ref, g_ref, t_ref, dx_ref, dg_ref, loss_ref):
        tgt = t_ref[...]

        def f(xv, gv):
            err = _rms(xv, gv) - tgt
            return 0.5 * jnp.sum(jnp.mean(err * err, axis=-1, keepdims=True))

        loss, vjp = jax.vjp(f, x_ref[...], g_ref[...])
        dx, dg = vjp(jnp.ones((), F32))
        dx_ref[...] = dx

        @pl.when(pl.program_id(0) == 0)
        def _():
            dg_ref[...] = jnp.zeros_like(dg_ref)
            loss_ref[...] = jnp.zeros_like(loss_ref)

        dg_ref[...] += dg
        loss_ref[...] += jnp.full(loss_ref.shape, loss, F32)

    row = pl.BlockSpec((tm, d), lambda i: (i, 0))
    vec = pl.BlockSpec((1, d), lambda i: (0, 0))
    lane = pl.BlockSpec((1, 128), lambda i: (0, 0))
    return pl.pallas_call(
        body, name=name, grid=(t // tm,), in_specs=[row, vec, row], out_specs=[row, vec, lane],
        out_shape=[jax.ShapeDtypeStruct((t, d), F32), jax.ShapeDtypeStruct((1, d), F32),
                   jax.ShapeDtypeStruct((1, 128), F32)],
        compiler_params=_cp("arbitrary"))(x, g, target)


def _lb_rows(l0, l1, l2, l3):
    mx = jnp.maximum(jnp.maximum(l0, l1), jnp.maximum(l2, l3))
    e0, e1, e2, e3 = jnp.exp(l0 - mx), jnp.exp(l1 - mx), jnp.exp(l2 - mx), jnp.exp(l3 - mx)
    s = e0 + e1 + e2 + e3
    p0, p1, p2, p3 = e0 / s, e1 / s, e2 / s, e3 / s
    c1 = p0 + p1
    c2 = c1 + p2
    c3 = c2 + p3
    return p0 - p0, c1 - p0, c2 - p0, c3 - p0


def _lb_fwd(lb_logits):
    def body(l_ref, o_ref):
        rows = _lb_rows(*[l_ref[pl.ds(i, 1), :] for i in range(DEPTH)])
        for i in range(DEPTH):
            o_ref[pl.ds(i, 1), :] = rows[i]

    return pl.pallas_call(body, name="lb_fwd", out_shape=jax.ShapeDtypeStruct(lb_logits.shape, F32))(lb_logits)


def _lb_bwd(lb_logits, dlb):
    def body(l_ref, d_ref, o_ref):
        _, vjp = jax.vjp(_lb_rows, *[l_ref[pl.ds(i, 1), :] for i in range(DEPTH)])
        grads = vjp(tuple(d_ref[pl.ds(i, 1), :] for i in range(DEPTH)))
        for i in range(DEPTH):
            o_ref[pl.ds(i, 1), :] = grads[i]

    return pl.pallas_call(body, name="lb_bwd", out_shape=jax.ShapeDtypeStruct(lb_logits.shape, F32))(lb_logits, dlb)


MESH = pl.DeviceIdType.MESH
ANY = pl.BlockSpec(memory_space=pl.ANY)
N_KINDS = len(SHARD_ROWS)
FFN_HALF = FFN_HIDDEN // 2
KIND_PLACE = ((0, 0), (1, 0), (2, 0), (3, 0), (4, 0), (4, FFN_HALF), (5, 0))
KIND_HALF_SKIP = (0, 0, 0, 0, FFN_HALF, FFN_HALF, 0)
FULL_ROWS = (N_DEV * SHARD_ROWS[0], D_MODEL, D_MODEL, D_MODEL, 2 * N_DEV * SHARD_ROWS[4], N_DEV * SHARD_ROWS[6])


def _kind_rows(ti, dev):
    oi, base = KIND_PLACE[ti]
    start = base + dev * SHARD_ROWS[ti]
    if KIND_HALF_SKIP[ti]:
        start = start + (dev // (N_DEV // 2)) * KIND_HALF_SKIP[ti]
    return oi, pl.ds(start, SHARD_ROWS[ti])


def _position():
    x, y, c = lax.axis_index("x"), lax.axis_index("y"), lax.axis_index("c")
    return x, y, c, 4 * x + 2 * y + c


def _peer(x, y, c, r):
    px = 1 - x if r & 4 else x
    py = 1 - y if r & 2 else y
    pc = 1 - c if r & 1 else c
    return (px, py, pc), 4 * px + 2 * py + pc


class _Exchange(NamedTuple):
    operands: tuple
    out_shape: tuple
    copies: Callable
    n_local: int


EXCHANGE_SCRATCH = (pltpu.SemaphoreType.DMA((N_DEV, N_KINDS)), pltpu.SemaphoreType.DMA((N_DEV, N_KINDS)),
                    pltpu.SemaphoreType.DMA((N_KINDS,)))
ALL_KINDS = tuple(range(N_KINDS))
KINDS_W_IN = (0,)
KINDS_REST = ALL_KINDS[1:]


def _all_pairs(kinds, ends, send_sems, recv_sems):
    x, y, c, me = _position()
    out = []
    for r in range(1, N_DEV):
        peer, pid = _peer(x, y, c, r)
        for ti in kinds:
            src, dst = ends(ti, me, pid)
            out.append(pltpu.make_async_remote_copy(
                src_ref=src, dst_ref=dst, send_sem=send_sems.at[r, ti], recv_sem=recv_sems.at[r, ti],
                device_id=peer, device_id_type=MESH))
    return out


def _gather_exchange(shards, kinds):
    arrays = sorted({KIND_PLACE[ti][0] for ti in kinds})

    def copies(ins, outs, send_sems, recv_sems, local_sems, arrivals):
        src = dict(zip(kinds, ins))

        def window(ti, dev):
            oi, rows = _kind_rows(ti, dev)
            return outs[arrays.index(oi)].at[rows, :]

        if arrivals:
            return _all_pairs(kinds, lambda ti, me, pid: (src[ti], window(ti, pid)), send_sems, recv_sems)
        _, _, _, me = _position()
        local = [pltpu.make_async_copy(src[ti], window(ti, me), local_sems.at[ti]) for ti in kinds]
        return local + _all_pairs(kinds, lambda ti, me, pid: (src[ti], window(ti, me)), send_sems, recv_sems)

    return _Exchange(tuple(shards[ti] for ti in kinds),
                     tuple(jax.ShapeDtypeStruct((FULL_ROWS[oi], D_MODEL), BF16) for oi in arrays), copies, len(kinds))


def _scatter_exchange(grads, kinds):
    arrays = sorted({KIND_PLACE[ti][0] for ti in kinds})
    offsets, total = {}, 0
    for ti in kinds:
        offsets[ti], total = total, total + SHARD_ROWS[ti]

    def copies(ins, outs, send_sems, recv_sems, local_sems, arrivals):
        land = outs[0]

        def piece(ti, dev):
            ii, rows = _kind_rows(ti, dev)
            return ins[arrays.index(ii)].at[rows, :]

        def slot(ti, dev):
            return land.at[dev, pl.ds(offsets[ti], SHARD_ROWS[ti]), :]

        if arrivals:
            return _all_pairs(kinds, lambda ti, me, pid: (piece(ti, me), slot(ti, pid)), send_sems, recv_sems)
        _, _, _, me = _position()
        local = [pltpu.make_async_copy(piece(ti, me), slot(ti, me), local_sems.at[ti]) for ti in kinds]
        return local + _all_pairs(kinds, lambda ti, me, pid: (piece(ti, pid), slot(ti, me)), send_sems, recv_sems)

    return _Exchange(tuple(grads[oi] for oi in arrays), (jax.ShapeDtypeStruct((N_DEV, total, D_MODEL), BF16),),
                     copies, len(kinds))


def _exchange_start(ex, ins, outs, sems):
    for cp in ex.copies(ins, outs, *sems, False):
        cp.start()


def _exchange_finish(ex, ins, outs, sems):
    for cp in ex.copies(ins, outs, *sems, True):
        cp.wait_recv()
    mine = ex.copies(ins, outs, *sems, False)
    for cp in mine[:ex.n_local]:
        cp.wait()
    for cp in mine[ex.n_local:]:
        cp.wait_send()


def _run_exchange(ex, name):
    n_in, n_out = len(ex.operands), len(ex.out_shape)

    def body(*refs):
        ins, outs, sems = refs[:n_in], refs[n_in:n_in + n_out], refs[n_in + n_out:]
        _exchange_start(ex, ins, outs, sems)
        _exchange_finish(ex, ins, outs, sems)

    return pl.pallas_call(body, name=name, in_specs=[ANY] * n_in, out_specs=[ANY] * n_out,
                          out_shape=list(ex.out_shape), scratch_shapes=list(EXCHANGE_SCRATCH))(*ex.operands)


def _carry(body, ex, n_in, n_out, n_scratch, grid):
    if ex is None:
        return body
    e_in, e_out = len(ex.operands), len(ex.out_shape)

    def at(step):
        hit = pl.program_id(0) == step[0]
        for axis in range(1, len(grid)):
            hit = hit & (pl.program_id(axis) == step[axis])
        return hit

    def carrying(*refs):
        own_in, ex_in = refs[:n_in], refs[n_in:n_in + e_in]
        rest = refs[n_in + e_in:]
        own_out, ex_out = rest[:n_out], rest[n_out:n_out + e_out]
        own_scratch, sems = rest[n_out + e_out:n_out + e_out + n_scratch], rest[n_out + e_out + n_scratch:]

        @pl.when(at([0] * len(grid)))
        def _():
            _exchange_start(ex, ex_in, ex_out, sems)

        body(*own_in, *own_out, *own_scratch)

        @pl.when(at([g - 1 for g in grid]))
        def _():
            _exchange_finish(ex, ex_in, ex_out, sems)

    return carrying


def _carried(ex):
    if ex is None:
        return (), [], [], [], []
    return (ex.operands, [ANY] * len(ex.operands), list(ex.out_shape), [ANY] * len(ex.out_shape),
            list(EXCHANGE_SCRATCH))


def _small_sum_body(p_ref, o_ref, buf, send_sems, recv_sems):
    x, y, c, me = _position()
    buf[me] = p_ref[...]
    sends = []
    for r in range(1, N_DEV):
        peer, _ = _peer(x, y, c, r)
        sends.append(pltpu.make_async_remote_copy(
            src_ref=p_ref, dst_ref=buf.at[me], send_sem=send_sems.at[r], recv_sem=recv_sems.at[r],
            device_id=peer, device_id_type=MESH))
    for cp in sends:
        cp.start()
    for r in range(1, N_DEV):
        peer, pid = _peer(x, y, c, r)
        pltpu.make_async_remote_copy(
            src_ref=p_ref, dst_ref=buf.at[pid], send_sem=send_sems.at[r], recv_sem=recv_sems.at[r],
            device_id=peer, device_id_type=MESH).wait_recv()
    for cp in sends:
        cp.wait_send()
    acc = buf[0]
    for k in range(1, N_DEV):
        acc = acc + buf[k]
    o_ref[...] = acc


def _all_reduce_small(part):
    rows, d = part.shape
    vmem = pl.BlockSpec(memory_space=pltpu.VMEM)
    return pl.pallas_call(
        functools.partial(_small_sum_body), name="all_reduce_small", in_specs=[vmem], out_specs=vmem,
        out_shape=jax.ShapeDtypeStruct((rows, d), F32),
        scratch_shapes=[pltpu.VMEM((N_DEV, rows, d), F32), pltpu.SemaphoreType.DMA((N_DEV,)),
                        pltpu.SemaphoreType.DMA((N_DEV,))],
    )(part)


HG_PAIR = 2 * HG_DK


def _hg_consts():
    c = HG_CHUNK
    r = lax.broadcasted_iota(jnp.int32, (c, c), 0)
    s = lax.broadcasted_iota(jnp.int32, (c, c), 1)
    r2 = lax.broadcasted_iota(jnp.int32, (c, 2 * c), 0)
    s2 = lax.broadcasted_iota(jnp.int32, (c, 2 * c), 1)
    causal2 = jnp.where(s2 >= c, s2 - c, s2) <= r2
    lane_hi = lax.broadcasted_iota(jnp.int32, (c, HG_PAIR), 1) >= HG_DK
    same_head = ((lax.broadcasted_iota(jnp.int32, (HG_PAIR, HG_PAIR), 0) >= HG_DK)
                 == (lax.broadcasted_iota(jnp.int32, (HG_PAIR, HG_PAIR), 1) >= HG_DK))
    return (s <= r).astype(BF16), (s >= r).astype(BF16), causal2, lane_hi, same_head


def _head_rows(x, lane_hi):
    zero = jnp.zeros_like(x)
    return jnp.concatenate([jnp.where(lane_hi, zero, x), jnp.where(lane_hi, x, zero)], axis=0)


def _own_rows(y, lane_hi):
    return jnp.where(lane_hi, y[HG_CHUNK:], y[:HG_CHUNK])


def _split3(x):
    hi = x.astype(BF16)
    r1 = x - hi.astype(F32)
    mid = r1.astype(BF16)
    lo = (r1 - mid.astype(F32)).astype(BF16)
    return jnp.concatenate([hi, mid, lo], axis=1)


def _cumsum_rows(tri, x):
    w = x.shape[1]
    y = _dot(tri, _split3(x), NN)
    return y[:, :w] + y[:, w:2 * w] + y[:, 2 * w:]


def _hg_gates(zq, zf, lb):
    sq = _sigmoid(zq)
    sg = _sigmoid(zf)
    f = lb + (1.0 - lb) * sg
    return sq, zq * sq, sg, f, jnp.log(jnp.maximum(f, MIN_F)), 1.0 - f


def _hg_decays(ball_ref, ci, cols):
    c = HG_CHUNK
    b = ball_ref[pl.ds(ci * c, c), cols]
    mid = ball_ref[pl.ds(ci * c + c // 2 - 1, 1), cols]
    bc = ball_ref[pl.ds(ci * c + c - 1, 1), cols]
    return b, jnp.exp(jnp.minimum(b - mid, HG_EXP_CLAMP)), jnp.exp(jnp.minimum(mid - b, HG_EXP_CLAMP)), bc


def _hg_gate(o, zg, gn):
    return o * lax.rsqrt(jnp.mean(o * o, axis=-1, keepdims=True) + EPS) * gn * (zg * _sigmoid(zg))


def _hgrn2_fwd(proj, lb, gn, name, ex=None):
    t = proj.shape[0]
    bs_tok = min(HG_BLOCK, t)
    n_chunks = bs_tok // HG_CHUNK
    w = HG_HEADS * HG_DK

    def body(hq_ref, hf_ref, hi_ref, hg_ref, lb_ref, gn_ref, o_ref, og_ref, sall_ref, ball_ref, aall_ref, st_ref):
        @pl.when(pl.program_id(0) == 0)
        def _():
            st_ref[...] = jnp.zeros_like(st_ref)

        tril, _, causal2, lane_hi, same_head = _hg_consts()

        for ci in range(n_chunks):
            rows = pl.ds(ci * HG_CHUNK, HG_CHUNK)
            for p in range(HG_HEADS // 2):
                cols = slice(p * HG_PAIR, (p + 1) * HG_PAIR)
                v = hi_ref[rows, cols]
                zg = hg_ref[rows, cols]
                _, q, _, _, logf, k = _hg_gates(hq_ref[rows, cols], hf_ref[rows, cols], lb_ref[:, cols])
                ball_ref[rows, cols] = _cumsum_rows(tril, logf)
                b, em, en, bc = _hg_decays(ball_ref, ci, cols)
                st0 = st_ref[p]
                sall_ref[ci, 2 * p] = st0[:HG_DK, :HG_DK]
                sall_ref[ci, 2 * p + 1] = st0[HG_DK:, HG_DK:]
                vb = v.astype(BF16)
                o = _dot((q * jnp.exp(b)).astype(BF16), st0.astype(BF16), NT)
                a = jnp.where(causal2, _dot((q * em).astype(BF16), _head_rows((k * en).astype(BF16), lane_hi), NT),
                              0.0).astype(BF16)
                aall_ref[ci, p] = a
                o = o + _dot(a, _head_rows(vb, lane_hi), NN)
                kdec = (k * jnp.exp(bc - b)).astype(BF16)
                st_ref[p] = st0 * jnp.exp(bc) + jnp.where(same_head, _dot(vb, kdec, TN), 0.0)
                o_ref[rows, cols] = o
                for hh in range(2):
                    sl = slice(hh * HG_DK, (hh + 1) * HG_DK)
                    hcols = slice(p * HG_PAIR + hh * HG_DK, p * HG_PAIR + (hh + 1) * HG_DK)
                    og_ref[rows, hcols] = _hg_gate(o[:, sl], zg[:, sl], gn_ref[:, hcols]).astype(BF16)

    def col(j):
        return pl.BlockSpec((bs_tok, w), lambda n, j=j: (n, j))

    vec = pl.BlockSpec((1, w), lambda n: (0, 0))
    ex_in, ex_in_specs, ex_out, ex_out_specs, ex_scratch = _carried(ex)
    outs = pl.pallas_call(
        _carry(body, ex, 6, 5, 1, (t // bs_tok,)), name=name, grid=(t // bs_tok,),
        in_specs=[col(COL_HQ // w), col(COL_HF // w), col(COL_HI // w), col(COL_HG // w), vec, vec] + ex_in_specs,
        out_specs=[col(0), col(0),
                   pl.BlockSpec((n_chunks, HG_HEADS, HG_DK, HG_DK), lambda n: (n, 0, 0, 0)), col(0),
                   pl.BlockSpec((n_chunks, HG_HEADS // 2, HG_CHUNK, 2 * HG_CHUNK), lambda n: (n, 0, 0, 0))] + ex_out_specs,
        out_shape=[jax.ShapeDtypeStruct((t, w), F32), jax.ShapeDtypeStruct((t, w), BF16),
                   jax.ShapeDtypeStruct((t // HG_CHUNK, HG_HEADS, HG_DK, HG_DK), F32), jax.ShapeDtypeStruct((t, w), F32),
                   jax.ShapeDtypeStruct((t // HG_CHUNK, HG_HEADS // 2, HG_CHUNK, 2 * HG_CHUNK), BF16)] + ex_out,
        scratch_shapes=[pltpu.VMEM((HG_HEADS // 2, HG_PAIR, HG_PAIR), F32)] + ex_scratch,
        compiler_params=_cp("arbitrary"))(proj, proj, proj, proj, lb, gn, *ex_in)
    return outs[:5], outs[5:]


def _hgrn2_bwd(proj, lb, gn, o_hg, sall, ball, aall, dog, name, ex=None):
    t = proj.shape[0]
    bs_tok = min(HG_BLOCK, t)
    n_chunks = bs_tok // HG_CHUNK
    n_blocks = t // bs_tok
    w = HG_HEADS * HG_DK

    def body(hq_ref, hf_ref, hi_ref, hg_ref, lb_ref, gn_ref, o_ref, sall_ref, snext_ref, ball_ref, aall_ref, dog_ref,
             da_ref, dlb_ref, dgn_ref, dst_ref):
        @pl.when(pl.program_id(0) == 0)
        def _():
            dst_ref[...] = jnp.zeros_like(dst_ref)
            dlb_ref[...] = jnp.zeros_like(dlb_ref)
            dgn_ref[...] = jnp.zeros_like(dgn_ref)

        _, rev_tril, causal2, lane_hi, same_head = _hg_consts()
        zero_block = jnp.zeros((HG_DK, HG_DK), F32)

        for ci in reversed(range(n_chunks)):
            rows = pl.ds(ci * HG_CHUNK, HG_CHUNK)
            for p in range(HG_HEADS // 2):
                cols = slice(p * HG_PAIR, (p + 1) * HG_PAIR)
                zq = hq_ref[rows, cols]
                v = hi_ref[rows, cols]
                zg = hg_ref[rows, cols]
                lbv = lb_ref[:, cols]
                sq, q, sg, f, _, k = _hg_gates(zq, hf_ref[rows, cols], lbv)
                b, em, en, bc = _hg_decays(ball_ref, ci, cols)
                st0 = jnp.concatenate([jnp.concatenate([sall_ref[ci, 2 * p], zero_block], axis=1),
                                       jnp.concatenate([zero_block, sall_ref[ci, 2 * p + 1]], axis=1)], axis=0)
                dst1 = dst_ref[p]
                vb = v.astype(BF16)
                eb = jnp.exp(b)
                qg = (q * eb).astype(BF16)
                qt = (q * em).astype(BF16)
                kref = (k * en).astype(BF16)
                ebcb = jnp.exp(bc - b)
                kdec = (k * ebcb).astype(BF16)
                ebc = jnp.exp(bc)
                dos, dzgs, dgns = [], [], []
                for hh in range(2):
                    sl = slice(hh * HG_DK, (hh + 1) * HG_DK)
                    hcols = slice(p * HG_PAIR + hh * HG_DK, p * HG_PAIR + (hh + 1) * HG_DK)
                    _, gate_vjp = jax.vjp(_hg_gate, o_ref[rows, hcols], zg[:, sl], gn_ref[:, hcols])
                    do_h, dzg_h, dgn_h = gate_vjp(dog_ref[rows, hcols])
                    dos.append(do_h)
                    dzgs.append(dzg_h)
                    dgns.append(dgn_h)
                dob = jnp.concatenate(dos, axis=1).astype(BF16)
                vrows, krows = _head_rows(vb, lane_hi), _head_rows(kref, lane_hi)
                dam = jnp.where(causal2, _dot(dob, vrows, NT), 0.0).astype(BF16)
                dk = ebcb * _dot(vb, dst1.astype(BF16), NN) + en * _own_rows(_dot(dam, qt, TN), lane_hi)
                dq = eb * _dot(dob, st0.astype(BF16), NN) + em * _dot(dam, krows, NN)
                dv = _own_rows(_dot(aall_ref[ci, p], dob, TN), lane_hi) + _dot(kdec, dst1.astype(BF16), NT)
                dst_ref[p] = dst1 * ebc + jnp.where(same_head, _dot(dob, qg, TN), 0.0)

                after = [sall_ref[ci + 1, 2 * p + hh] if ci + 1 < n_chunks else snext_ref[0, 2 * p + hh] for hh in range(2)]
                dbx = jnp.concatenate(
                    [jnp.sum(dst1[hh * HG_DK:(hh + 1) * HG_DK, hh * HG_DK:(hh + 1) * HG_DK] * after[hh], axis=0, keepdims=True)
                     for hh in range(2)], axis=1)
                dlogf = _cumsum_rows(rev_tril, q * dq - k * dk) + dbx
                df = jnp.where(f > MIN_F, dlogf / f, 0.0) - dk
                dzf = df * (1.0 - lbv) * sg * (1.0 - sg)
                dzq = dq * (sq * (1.0 + zq * (1.0 - sq)))
                da_ref[rows, pl.ds(COL_HQ + p * HG_PAIR, HG_PAIR)] = dzq.astype(BF16)
                da_ref[rows, pl.ds(COL_HF + p * HG_PAIR, HG_PAIR)] = dzf.astype(BF16)
                da_ref[rows, pl.ds(COL_HI + p * HG_PAIR, HG_PAIR)] = dv.astype(BF16)
                da_ref[rows, pl.ds(COL_HG + p * HG_PAIR, HG_PAIR)] = jnp.concatenate(dzgs, axis=1).astype(BF16)
                dlb_ref[:, cols] += jnp.sum(df * (1.0 - sg), axis=0, keepdims=True)
                dgn_ref[:, cols] += jnp.concatenate(dgns, axis=1)

    def col(j):
        return pl.BlockSpec((bs_tok, w), lambda n, j=j: (n_blocks - 1 - n, j))

    vec = pl.BlockSpec((1, w), lambda n: (0, 0))
    ex_in, ex_in_specs, ex_out, ex_out_specs, ex_scratch = _carried(ex)
    outs = pl.pallas_call(
        _carry(body, ex, 12, 3, 1, (n_blocks,)), name=name, grid=(n_blocks,),
        in_specs=[col(COL_HQ // w), col(COL_HF // w), col(COL_HI // w), col(COL_HG // w), vec, vec, col(0),
                  pl.BlockSpec((n_chunks, HG_HEADS, HG_DK, HG_DK), lambda n: (n_blocks - 1 - n, 0, 0, 0)),
                  pl.BlockSpec((1, HG_HEADS, HG_DK, HG_DK),
                               lambda n: (jnp.minimum((n_blocks - n) * n_chunks, t // HG_CHUNK - 1), 0, 0, 0)),
                  col(0),
                  pl.BlockSpec((n_chunks, HG_HEADS // 2, HG_CHUNK, 2 * HG_CHUNK), lambda n: (n_blocks - 1 - n, 0, 0, 0)),
                  col(0)] + ex_in_specs,
        out_specs=[pl.BlockSpec((bs_tok, 4 * w), lambda n: (n_blocks - 1 - n, 0)), vec, vec] + ex_out_specs,
        out_shape=[jax.ShapeDtypeStruct((t, 4 * w), BF16), jax.ShapeDtypeStruct((1, w), F32),
                   jax.ShapeDtypeStruct((1, w), F32)] + ex_out,
        scratch_shapes=[pltpu.VMEM((HG_HEADS // 2, HG_PAIR, HG_PAIR), F32)] + ex_scratch,
        compiler_params=_cp("arbitrary"))(proj, proj, proj, proj, lb, gn, o_hg, sall, sall, ball, aall, dog, *ex_in)
    return outs[:3], outs[3:]


def _rope_tables(t):
    half = ROPE_DIM // 2
    inv = ROPE_THETA ** (-jnp.arange(half, dtype=F32) * 2.0 / ROPE_DIM)
    d = jnp.arange(2 * ATT_HEAD_DIM) % ATT_HEAD_DIM
    ang = jnp.arange(t).astype(F32)[:, None] * inv[d % half][None, :]
    cos, sin = jnp.cos(ang), jnp.sin(ang)
    c = jnp.where(d < ROPE_DIM, cos, 1.0)
    su = jnp.where(d < half, -sin, 0.0)
    sd = jnp.where((d >= half) & (d < ROPE_DIM), sin, 0.0)
    return c, su, sd


def _rope(x, tabs):
    c, su, sd = tabs
    n = x.shape[1]
    half = ROPE_DIM // 2
    return x * c + pltpu.roll(x, n - half, 1) * su + pltpu.roll(x, half, 1) * sd


def _rope_t(dy, tabs):
    c, su, sd = tabs
    n = dy.shape[1]
    half = ROPE_DIM // 2
    return dy * c + pltpu.roll(dy * su, half, 1) + pltpu.roll(dy * sd, n - half, 1)


def _swa_specs(n_blocks, clamp):
    blk = ATT_BLOCK

    def cur(n):
        return jnp.minimum(n, n_blocks - 1) if clamp else n

    def prev(n):
        return jnp.maximum(cur(n) - 1, 0)

    q_spec = pl.BlockSpec((blk, 512), lambda m, n: (cur(n), COL_AQ // 512 + m))
    kv = [pl.BlockSpec((blk, 128), lambda m, n, c=c, f=f: (f(n), c + m))
          for c in (COL_AK // 128, COL_AV // 128) for f in (cur, prev)]
    tabs = [pl.BlockSpec((blk, 128), lambda m, n, f=f: (f(n), 0)) for f in (cur, prev) for _ in range(3)]
    return q_spec, kv, tabs, cur, prev


ATT_SCALE = ATT_HEAD_DIM ** -0.5


def _head_halves(x, upper):
    zero = jnp.zeros_like(x)
    return jnp.concatenate([jnp.where(upper, zero, x), jnp.where(upper, x, zero)], axis=0)


def _swa_scores(scores, sink, mask):
    s = jnp.where(mask, scores, -jnp.inf)
    mx = jnp.maximum(jnp.max(s, axis=-1, keepdims=True), sink)
    p = jnp.exp(s - mx)
    es = jnp.exp(sink - mx)
    rinv = 1.0 / (jnp.sum(p, axis=-1, keepdims=True) + es)
    return p * rinv, es * rinv


def _swa_window(kc_ref, kp_ref, vc_ref, vp_ref, tabs_c, tabs_p, n):
    k2 = jnp.concatenate([_rope(kp_ref[...], tabs_p), _rope(kc_ref[...], tabs_c)], axis=0)
    v2 = jnp.concatenate([vp_ref[...], vc_ref[...]], axis=0)
    blk = ATT_BLOCK
    qi = lax.broadcasted_iota(jnp.int32, (blk, 2 * blk), 0)
    kj = lax.broadcasted_iota(jnp.int32, (blk, 2 * blk), 1)
    delta = qi + blk - kj
    mask = (delta >= 0) & (delta < blk) & ((kj >= blk) | (n > 0))
    return k2, v2, mask


def _swa_fwd(proj, sinks, tabs, name, ex=None):
    t = proj.shape[0]
    n_blocks = t // ATT_BLOCK
    q_spec, kv_specs, tab_specs, _, _ = _swa_specs(n_blocks, clamp=False)

    def body(q_ref, kc_ref, kp_ref, vc_ref, vp_ref, c0, c1, c2, p0, p1, p2, sink_ref, o_ref):
        m, n = pl.program_id(0), pl.program_id(1)
        tabs_c = (c0[...], c1[...], c2[...])
        tabs_p = (p0[...], p1[...], p2[...])
        k2, v2, mask = _swa_window(kc_ref, kp_ref, vc_ref, vp_ref, tabs_c, tabs_p, n)
        k2r, v2r = pltpu.roll(k2, 64, 1), pltpu.roll(v2, 64, 1)
        upper_k = lax.broadcasted_iota(jnp.int32, k2.shape, 1) >= 64
        upper_q = lax.broadcasted_iota(jnp.int32, (ATT_BLOCK, 128), 1) >= 64
        for jj in range(2):
            own = upper_k if jj else ~upper_k
            kd = jnp.where(own, k2, k2r).astype(BF16)
            vd = jnp.where(own, v2, v2r).astype(BF16)
            for pi in range(2):
                cols = slice(256 * jj + 128 * pi, 256 * jj + 128 * pi + 128)
                qp = _rope(q_ref[:, cols], tabs_c) * ATT_SCALE
                outs = []
                for e in range(2):
                    sink = sink_ref[0, 8 * m + 4 * jj + 2 * pi + e]
                    qm = jnp.where(upper_q if e else ~upper_q, qp, 0.0).astype(BF16)
                    pn, _ = _swa_scores(_dot(qm, kd, NT), sink, mask)
                    outs.append(_dot(pn.astype(BF16), vd, NN))
                o_ref[:, cols] = jnp.where(upper_q, outs[1], outs[0]).astype(BF16)

    ex_in, ex_in_specs, ex_out, ex_out_specs, ex_scratch = _carried(ex)
    outs = pl.pallas_call(
        _carry(body, ex, 12, 1, 0, (2, n_blocks)), name=name, grid=(2, n_blocks),
        in_specs=[q_spec] + kv_specs + tab_specs + [pl.BlockSpec(memory_space=pltpu.SMEM)] + ex_in_specs,
        out_specs=[pl.BlockSpec((ATT_BLOCK, 512), lambda m, n: (n, m))] + ex_out_specs,
        out_shape=[jax.ShapeDtypeStruct((t, ATT_Q_HEADS * ATT_HEAD_DIM), BF16)] + ex_out,
        scratch_shapes=ex_scratch,
        compiler_params=_cp("arbitrary", "arbitrary"))(proj, proj, proj, proj, proj, *tabs, *tabs, sinks, *ex_in)
    return outs[0], outs[1:]


def _swa_bwd(proj, sinks, tabs, o_att, do_att, name, ex=None):
    t = proj.shape[0]
    n_blocks = t // ATT_BLOCK
    blk = ATT_BLOCK
    q_spec, kv_specs, tab_specs, cur, prev = _swa_specs(n_blocks, clamp=True)

    def body(q_ref, kc_ref, kp_ref, vc_ref, vp_ref, c0, c1, c2, p0, p1, p2, sink_ref, o_ref, do_ref,
             dq_ref, dk_ref, dv_ref, ds_ref, ck_ref, cv_ref):
        m, n = pl.program_id(0), pl.program_id(1)

        @pl.when(n == 0)
        def _():
            ds_ref[...] = jnp.zeros_like(ds_ref)
            ck_ref[...] = jnp.zeros_like(ck_ref)
            cv_ref[...] = jnp.zeros_like(cv_ref)

        @pl.when(n < n_blocks)
        def _():
            tabs_c = (c0[...], c1[...], c2[...])
            tabs_p = (p0[...], p1[...], p2[...])
            k2, v2, mask = _swa_window(kc_ref, kp_ref, vc_ref, vp_ref, tabs_c, tabs_p, n)
            k2r, v2r = pltpu.roll(k2, 64, 1), pltpu.roll(v2, 64, 1)
            upper_k = lax.broadcasted_iota(jnp.int32, k2.shape, 1) >= 64
            upper_q = lax.broadcasted_iota(jnp.int32, (blk, 128), 1) >= 64
            lane = lax.broadcasted_iota(jnp.int32, (8, 128), 1)
            dk2 = jnp.zeros(k2.shape, F32)
            dv2 = jnp.zeros(k2.shape, F32)
            dsv = jnp.zeros((8, 128), F32)
            nk = 2 * blk
            for jj in range(2):
                own = upper_k if jj else ~upper_k
                kh = _head_halves(jnp.where(own, k2, k2r).astype(BF16), upper_k)
                vh = _head_halves(jnp.where(own, v2, v2r).astype(BF16), upper_k)
                dkd = jnp.zeros(k2.shape, F32)
                dvd = jnp.zeros(k2.shape, F32)
                for pi in range(2):
                    cols = slice(256 * jj + 128 * pi, 256 * jj + 128 * pi + 128)
                    qp = (_rope(q_ref[:, cols], tabs_c) * ATT_SCALE).astype(BF16)
                    do_pair = do_ref[:, cols]
                    o_pair = o_ref[:, cols].astype(F32)
                    dob = do_pair.astype(BF16)
                    s = _dot(qp, kh, NT)
                    dp = _dot(dob, vh, NT)
                    pns, dss = [], []
                    for e in range(2):
                        hl = 4 * jj + 2 * pi + e
                        pn, ps = _swa_scores(s[:, e * nk:(e + 1) * nk], sink_ref[0, 8 * m + hl], mask)
                        delta = jnp.sum(jnp.where(upper_q if e else ~upper_q, do_pair * o_pair, 0.0), axis=-1, keepdims=True)
                        pns.append(pn.astype(BF16))
                        dss.append((pn * (dp[:, e * nk:(e + 1) * nk] - delta)).astype(BF16))
                        dsv = dsv + jnp.where(lane == hl, -jnp.sum(ps * delta), 0.0)
                    dsb = jnp.concatenate(dss, axis=1)
                    dq_ref[:, cols] = _rope_t(_dot(dsb, kh, NN) * ATT_SCALE, tabs_c).astype(BF16)
                    rk = _dot(dsb, qp, TN)
                    rv = _dot(jnp.concatenate(pns, axis=1), dob, TN)
                    dkd = dkd + jnp.where(upper_k, rk[nk:], rk[:nk])
                    dvd = dvd + jnp.where(upper_k, rv[nk:], rv[:nk])
                dk2 = dk2 + jnp.where(own, dkd + pltpu.roll(dkd, 64, 1), 0.0)
                dv2 = dv2 + jnp.where(own, dvd + pltpu.roll(dvd, 64, 1), 0.0)
            dk_ref[...] = (ck_ref[...] + _rope_t(dk2[:blk], tabs_p)).astype(BF16)
            dv_ref[...] = (cv_ref[...] + dv2[:blk]).astype(BF16)
            ck_ref[...] = _rope_t(dk2[blk:], tabs_c)
            cv_ref[...] = dv2[blk:]
            ds_ref[...] += dsv

        @pl.when(n == n_blocks)
        def _():
            dk_ref[...] = ck_ref[...].astype(BF16)
            dv_ref[...] = cv_ref[...].astype(BF16)

    wide = pl.BlockSpec((blk, 512), lambda m, n: (cur(n), m))
    lagged = pl.BlockSpec((blk, 128), lambda m, n: (jnp.maximum(n - 1, 0), m))
    ex_in, ex_in_specs, ex_out, ex_out_specs, ex_scratch = _carried(ex)
    outs = pl.pallas_call(
        _carry(body, ex, 14, 4, 2, (2, n_blocks + 1)), name=name, grid=(2, n_blocks + 1),
        in_specs=[q_spec] + kv_specs + tab_specs + [pl.BlockSpec(memory_space=pltpu.SMEM), wide, wide] + ex_in_specs,
        out_specs=[wide, lagged, lagged, pl.BlockSpec((None, 8, 128), lambda m, n: (m, 0, 0))] + ex_out_specs,
        out_shape=[jax.ShapeDtypeStruct((t, 1024), BF16), jax.ShapeDtypeStruct((t, 256), BF16),
                   jax.ShapeDtypeStruct((t, 256), BF16), jax.ShapeDtypeStruct((2, 8, 128), F32)] + ex_out,
        scratch_shapes=[pltpu.VMEM((blk, 128), F32), pltpu.VMEM((blk, 128), F32)] + ex_scratch,
        compiler_params=_cp("arbitrary", "arbitrary"))(proj, proj, proj, proj, proj, *tabs, *tabs, sinks, o_att, do_att,
                                                       *ex_in)
    return outs[:4], outs[4:]


def _local_step(x, target, shards, norm1, lb_logits, hg_norm, attn_sinks, norm2, final_norm):
    t = x.shape[0]
    tabs = _rope_tables(t)
    lb_all = _lb_fwd(lb_logits)
    saved = []

    def shards_of(l):
        return {ti: shards[ti][l] for ti in ALL_KINDS}

    win_next = rest_next = None
    for l in range(DEPTH):
        n1, n2 = norm1[l][None, :], norm2[l][None, :]
        lb, gn, sinks = lb_all[l][None, :], hg_norm[l][None, :], attn_sinks[l][None, :]
        if l == 0:
            h, win_next = _rms_fwd(x, n1, "rms1_fwd", _gather_exchange(shards_of(0), KINDS_W_IN))
        (win_t,) = win_next
        if l == 0:
            proj, rest_next = _matmul_nt(h, win_t, 0, IN_COLS, F32, "proj_fwd", tn=1280,
                                         ex=_gather_exchange(shards_of(0), KINDS_REST))
        else:
            proj = _matmul_nt(h, win_t, 0, IN_COLS, F32, "proj_fwd", tn=1280)
        w_pa, w_pb, w_o, wgu_t, w_d = rest_next
        more = l + 1 < DEPTH
        (o_hg, o_g, sall, ball, aall), rest_next = _hgrn2_fwd(
            proj, lb, gn, "hgrn2_fwd", _gather_exchange(shards_of(l + 1), KINDS_REST) if more else None)
        o_att, win_next = _swa_fwd(
            proj, sinks, tabs, "swa_fwd", _gather_exchange(shards_of(l + 1), KINDS_W_IN) if more else None)
        ya, yb, mix, h2, x1 = _merge_fwd(o_g, o_att, proj, x, w_pa, w_pb, w_o, n2, "merge_fwd")
        gu, act = _ffn_up_fwd(h2, wgu_t, "ffn_up_fwd")
        if more:
            x2, h_next = _matmul_nn(act, w_d, 0, x1, "wd_fwd", gain=norm1[l + 1][None, :])
        else:
            x2, h_next = _matmul_nn(act, w_d, 0, x1, "wd_fwd_last"), None
        saved.append((x, h, proj, o_hg, o_g, (sall, ball, aall), o_att, ya, yb, mix, x1, h2, gu, act, n1, n2, lb, gn, sinks,
                      (win_t, w_pa, w_pb, w_o, wgu_t, w_d)))
        x, h = x2, h_next

    dx, d_fn, loss = _loss_head(x, final_norm[None, :], target, "loss_head")

    owned = [None] * DEPTH
    pending = None
    d_n1, d_n2, d_lb, d_gn, d_sinks = ([None] * DEPTH for _ in range(5))
    for l in reversed(range(DEPTH)):
        x0, h, proj, o_hg, o_g, hg_saved, o_att, ya, yb, mix, x1, h2, gu, act, n1, n2, lb, gn, sinks, weights = saved[l]
        win_t, w_pa, w_pb, w_o, wgu_t, w_d = weights
        dgu = _ffn_down_bwd(dx, w_d, gu, "ffn_down_bwd")
        g_wd = _matmul_tn(act, dx, "wd_grad", tm=1408)
        g_wgu = _matmul_tn(dgu, h2, "wgu_grad", tm=1408)
        late = pending is not None
        dx1, d_n2[l], *land_w = _rows_bwd([dgu], wgu_t, x1, n2, dx, "ffn_up_bwd", tm=512,
                                          ex=_scatter_exchange(pending, KINDS_W_IN) if late else None)
        g_wo = _matmul_tn(mix, dx1, "wo_grad")
        dya, dyb, dgab, dog, doatt = _merge_bwd(dx1, ya, yb, proj, w_pa, w_pb, w_o, "merge_bwd")
        g_wpa = _matmul_tn(o_g, dya, "wpa_grad")
        g_wpb = _matmul_tn(o_att, dyb, "wpb_grad")
        (dhg, d_lb[l], d_gn[l]), land_r = _hgrn2_bwd(proj, lb, gn, o_hg, *hg_saved, dog, "hgrn2_bwd",
                                                     _scatter_exchange(pending, KINDS_REST) if late else None)
        if late:
            owned[l + 1] = jnp.concatenate([_sum_slots(land_w[0][0], "sum_slots_w_in"),
                                            _sum_slots(land_r[0], "sum_slots_rest")], axis=0)
        ex = _scatter_exchange((None, g_wpa, g_wpb, g_wo, g_wgu, g_wd), KINDS_REST) if l == 0 else None
        (daq, dak, dav, d_sinks[l]), land_rest = _swa_bwd(proj, sinks, tabs, o_att, doatt, "swa_bwd", ex)
        g_win = None
        for piece, off, tm, tag in ((dhg, COL_HQ, 512, "hg"), (daq, COL_AQ, 512, "aq"), (dak, COL_AK, 256, "ak"),
                                    (dav, COL_AV, 256, "av"), (dgab, COL_GA, 512, "gates")):
            g_win = _matmul_tn(piece, h, "win_grad_" + tag, tm=tm, rows=IN_COLS, row_off=off, into=g_win)
        if l > 0:
            dx, d_n1[l] = _rows_bwd([dhg, daq, dak, dav, dgab], win_t, x0, n1, dx1, "win_bwd")
        else:
            dx, d_n1[l], land_win = _rows_bwd([dhg, daq, dak, dav, dgab], win_t, x0, n1, dx1, "win_bwd",
                                              ex=_scatter_exchange((g_win,), KINDS_W_IN))
        pending = (g_win, g_wpa, g_wpb, g_wo, g_wgu, g_wd)
    owned[0] = jnp.concatenate([_sum_slots(land_win[0], "sum_slots_w_in"), _sum_slots(land_rest[0], "sum_slots_rest")],
                               axis=0)

    d_sink_rows = [jnp.concatenate([d[0, 0, :8], d[1, 0, :8]]) for d in d_sinks]
    small = (jnp.concatenate(d_n1, axis=0), jnp.concatenate(d_lb, axis=0), jnp.concatenate(d_gn, axis=0),
             jnp.concatenate(d_n2, axis=0), d_fn, jnp.stack(d_sink_rows, axis=0))
    return loss, dx, jnp.stack(owned, axis=0), small


def _sum_slots(land, name, tr=480):
    _, rows, d = land.shape

    def body(l_ref, o_ref):
        acc = l_ref[0].astype(F32)
        for k in range(1, N_DEV):
            acc = acc + l_ref[k].astype(F32)
        o_ref[...] = acc

    return pl.pallas_call(
        body, name=name, grid=(rows // tr,),
        in_specs=[pl.BlockSpec((N_DEV, tr, d), lambda i: (0, i, 0))],
        out_specs=pl.BlockSpec((tr, d), lambda i: (i, 0)),
        out_shape=jax.ShapeDtypeStruct((rows, d), F32),
        compiler_params=_cp("parallel"))(land)


def _adamw(w, g, m, v, name):
    shape = w.shape
    rows, c = shape[-2], shape[-1]
    tr = rows
    for cand in (512, 352, 128):
        if rows % cand == 0:
            tr = cand
            break
    c1 = 1.0 / (1.0 - ADAM_B1 ** ADAM_STEP)
    c2 = 1.0 / (1.0 - ADAM_B2 ** ADAM_STEP)

    def body(w_ref, g_ref, m_ref, v_ref, d_ref, nm_ref, nv_ref):
        gv = g_ref[...]
        nm = ADAM_B1 * m_ref[...] + (1.0 - ADAM_B1) * gv
        nv = ADAM_B2 * v_ref[...] + (1.0 - ADAM_B2) * (gv * gv)
        d_ref[...] = -ADAM_LR * ((nm * c1) / (jnp.sqrt(nv * c2) + ADAM_EPS) + ADAM_WD * w_ref[...])
        nm_ref[...] = nm
        nv_ref[...] = nv

    if len(shape) == 3:
        grid = (shape[0], rows // tr)
        spec = pl.BlockSpec((None, tr, c), lambda l, i: (l, i, 0))
        sem = ("parallel", "parallel")
    else:
        grid = (rows // tr,)
        spec = pl.BlockSpec((tr, c), lambda i: (i, 0))
        sem = ("parallel",)
    return tuple(pl.pallas_call(
        body, name=name, grid=grid, in_specs=[spec] * 4, out_specs=[spec] * 3,
        out_shape=[jax.ShapeDtypeStruct(shape, F32)] * 3, compiler_params=_cp(*sem))(w, g, m, v))


def kernel(x, norm1, w_in, lb_logits, hg_norm, attn_sinks, w_pa, w_pb, w_o, norm2, w_gate, w_up, w_down, final_norm, loss_target, m_norm1, m_w_in, m_lb_logits, m_hg_norm, m_attn_sinks, m_w_pa, m_w_pb, m_w_o, m_norm2, m_w_gate, m_w_up, m_w_down, m_final_norm, v_norm1, v_w_in, v_lb_logits, v_hg_norm, v_attn_sinks, v_w_pa, v_w_pb, v_w_o, v_norm2, v_w_gate, v_w_up, v_w_down, v_final_norm):
    t = x.shape[1]
    shards = [jnp.swapaxes(w_in, 1, 2).astype(BF16), w_pa.astype(BF16), w_pb.astype(BF16), w_o.astype(BF16),
              jnp.swapaxes(w_gate, 1, 2).astype(BF16), jnp.swapaxes(w_up, 1, 2).astype(BF16), w_down.astype(BF16)]
    loss_lanes, grad_x, owned, small = _local_step(
        x.reshape(t, D_MODEL), loss_target.reshape(t, D_MODEL), shards,
        norm1, lb_logits, hg_norm, attn_sinks, norm2, final_norm)

    def rows_of(ti, transpose):
        g = owned[:, SLOT_OFF[ti]:SLOT_OFF[ti] + SHARD_ROWS[ti], :]
        return jnp.swapaxes(g, 1, 2) if transpose else g

    g_big = {"w_in": rows_of(0, True), "w_pa": rows_of(1, False), "w_pb": rows_of(2, False), "w_o": rows_of(3, False),
             "w_gate": rows_of(4, True), "w_up": rows_of(5, True), "w_down": rows_of(6, False)}

    d_n1, d_lb, d_gn, d_n2, d_fn, d_sinks = small
    pad = jnp.zeros((DEPTH, D_MODEL - ATT_Q_HEADS), F32)
    packed = jnp.concatenate([
        d_n1, d_lb, d_gn, d_n2, d_fn, jnp.concatenate([d_sinks, pad], axis=1),
        jnp.concatenate([loss_lanes, jnp.zeros((1, D_MODEL - 128), F32)], axis=1),
        jnp.zeros((SMALL_ROWS - 22, D_MODEL), F32)], axis=0)
    total = _all_reduce_small(packed)
    loss = total[21, 0]
    g_small = {"norm1": total[0:4], "lb_logits": _lb_bwd(lb_logits, total[4:8]), "hg_norm": total[8:12],
               "norm2": total[12:16], "final_norm": total[16], "attn_sinks": total[17:21, :ATT_Q_HEADS]}

    params = {"norm1": (norm1, m_norm1, v_norm1), "w_in": (w_in, m_w_in, v_w_in),
              "lb_logits": (lb_logits, m_lb_logits, v_lb_logits), "hg_norm": (hg_norm, m_hg_norm, v_hg_norm),
              "attn_sinks": (attn_sinks, m_attn_sinks, v_attn_sinks), "w_pa": (w_pa, m_w_pa, v_w_pa),
              "w_pb": (w_pb, m_w_pb, v_w_pb), "w_o": (w_o, m_w_o, v_w_o), "norm2": (norm2, m_norm2, v_norm2),
              "w_gate": (w_gate, m_w_gate, v_w_gate), "w_up": (w_up, m_w_up, v_w_up),
              "w_down": (w_down, m_w_down, v_w_down), "final_norm": (final_norm, m_final_norm, v_final_norm)}
    order = ["norm1", "w_in", "lb_logits", "hg_norm", "attn_sinks", "w_pa", "w_pb", "w_o", "norm2",
             "w_gate", "w_up", "w_down", "final_norm"]
    grads, deltas, new_m, new_v = [], [], [], []
    for name in order:
        w, m, v = params[name]
        g = (g_big[name] if name in g_big else g_small[name]).reshape(w.shape)
        w2 = w.reshape(1, -1) if w.ndim == 1 else w
        d, nm, nv = _adamw(w2, g.reshape(w2.shape), m.reshape(w2.shape), v.reshape(w2.shape), "adamw_" + name)
        grads.append(g)
        deltas.append(d.reshape(w.shape))
        new_m.append(nm.reshape(w.shape))
        new_v.append(nv.reshape(w.shape))
    return (loss, grad_x.reshape(x.shape), *grads, *deltas, *new_m, *new_v)
```

```python
import functools
from typing import Callable, NamedTuple

import jax
import jax.numpy as jnp
from jax import lax
from jax.experimental import pallas as pl
from jax.experimental.pallas import tpu as pltpu

F32, BF16 = jnp.float32, jnp.bfloat16

D_MODEL = 1024
DEPTH = 4
N_DEV = 8
HG_HEADS = 8
HG_DK = 128
HG_CHUNK = 64
HG_BLOCK = 256
HG_EXP_CLAMP = 60.0
ATT_Q_HEADS = 16
ATT_HEAD_DIM = 64
ATT_BLOCK = 128
ROPE_THETA = 500000.0
ROPE_DIM = 16
FFN_HIDDEN = 2816
EPS = 1e-6
MIN_F = 1e-30
ADAM_LR, ADAM_B1, ADAM_B2, ADAM_EPS, ADAM_WD, ADAM_STEP = 0.001, 0.9, 0.999, 1e-08, 0.01, 10

COL_HQ, COL_HF, COL_HI, COL_HG = 0, 1024, 2048, 3072
COL_AQ, COL_AK, COL_AV, COL_GA, COL_GB = 4096, 5120, 5376, 5632, 6656
IN_COLS = 7680

SHARD_ROWS = (960, 128, 128, 128, 352, 352, 352)
SLOT_OFF = (0, 960, 1088, 1216, 1344, 1696, 2048)
SLOT_ROWS = 2400
SMALL_ROWS = 24

VMEM_LIMIT_BYTES = 56 * 1024 * 1024

NN = ((1,), (0,))
NT = ((1,), (1,))
TN = ((0,), (0,))


def _dot(a, b, dims):
    return lax.dot_general(a, b, (dims, ((), ())), preferred_element_type=F32)


def _cp(*sem):
    return pltpu.CompilerParams(dimension_semantics=sem if sem else None, vmem_limit_bytes=VMEM_LIMIT_BYTES)


def _sigmoid(x):
    return 0.5 * jnp.tanh(0.5 * x) + 0.5


def _matmul_nt(a, w, row_off, n, out_dtype, name, tm=1024, tn=512, ex=None):
    t, k = a.shape
    tm = min(tm, t)
    assert n % tn == 0 and row_off % tn == 0 and t % tm == 0
    grid = (n // tn, t // tm)

    def body(a_ref, w_ref, o_ref):
        o_ref[...] = _dot(a_ref[...].astype(BF16), w_ref[...], NT).astype(o_ref.dtype)

    ex_in, ex_in_specs, ex_out, ex_out_specs, ex_scratch = _carried(ex)
    outs = pl.pallas_call(
        _carry(body, ex, 2, 1, 0, grid), name=name, grid=grid,
        in_specs=[pl.BlockSpec((tm, k), lambda j, i: (i, 0)),
                  pl.BlockSpec((tn, k), lambda j, i: (row_off // tn + j, 0))] + ex_in_specs,
        out_specs=[pl.BlockSpec((tm, tn), lambda j, i: (i, j))] + ex_out_specs,
        out_shape=[jax.ShapeDtypeStruct((t, n), out_dtype)] + ex_out,
        scratch_shapes=ex_scratch,
        compiler_params=_cp("arbitrary", "arbitrary") if ex else _cp("parallel", "parallel"))(a, w, *ex_in)
    return (outs[0], outs[1:]) if ex else outs[0]


def _matmul_nn(a, w, row_off, res, name, tm=512, tk=None, gain=None):
    t, k = a.shape
    n = w.shape[1]
    tm = min(tm, t)
    tk = tk or k
    nk = k // tk
    assert k % tk == 0 and row_off % tk == 0 and t % tm == 0

    def body(*refs):
        refs = list(refs)
        a_ref, w_ref = refs[:2]
        r_ref = refs[2] if res is not None else None
        g_ref = refs[2 + (res is not None)] if gain is not None else None
        acc = refs[-1]
        o_ref = refs[-3] if gain is not None else refs[-2]
        kk = pl.program_id(1)
        part = _dot(a_ref[...].astype(BF16), w_ref[...], NN)

        @pl.when(kk == 0)
        def _():
            acc[...] = part

        @pl.when(kk > 0)
        def _():
            acc[...] += part

        @pl.when(kk == nk - 1)
        def _():
            y = acc[...] if res is None else acc[...] + r_ref[...]
            o_ref[...] = y
            if gain is not None:
                refs[-2][...] = _rms(y, g_ref[...]).astype(BF16)

    row = pl.BlockSpec((tm, n), lambda i, kk: (i, 0))
    in_specs = [pl.BlockSpec((tm, tk), lambda i, kk: (i, kk)),
                pl.BlockSpec((tk, n), lambda i, kk: (row_off // tk + kk, 0))]
    args = [a, w]
    if res is not None:
        in_specs.append(row)
        args.append(res)
    if gain is not None:
        in_specs.append(pl.BlockSpec((1, n), lambda i, kk: (0, 0)))
        args.append(gain)
    outs = pl.pallas_call(
        body, name=name, grid=(t // tm, nk), in_specs=in_specs,
        out_specs=[row, row] if gain is not None else [row],
        out_shape=[jax.ShapeDtypeStruct((t, n), F32)] + ([jax.ShapeDtypeStruct((t, n), BF16)] if gain is not None else []),
        scratch_shapes=[pltpu.VMEM((tm, n), F32)],
        compiler_params=_cp("parallel", "arbitrary"))(*args)
    return tuple(outs) if gain is not None else outs[0]


def _matmul_tn(a, b, name, tm=512, tk=2048, rows=None, row_off=0, into=None):
    t, m = a.shape
    n = b.shape[1]
    tk = min(tk, t)
    nk = t // tk
    rows = rows or m
    assert m % tm == 0 and t % tk == 0 and row_off % tm == 0

    def body(*refs):
        a_ref, b_ref, o_ref, acc = refs[0], refs[1], refs[-2], refs[-1]
        kk = pl.program_id(1)
        part = _dot(a_ref[...].astype(BF16), b_ref[...].astype(BF16), TN)

        @pl.when(kk == 0)
        def _():
            acc[...] = part

        @pl.when(kk > 0)
        def _():
            acc[...] += part

        @pl.when(kk == nk - 1)
        def _():
            o_ref[...] = acc[...].astype(BF16)

    return pl.pallas_call(
        body, name=name, grid=(m // tm, nk),
        in_specs=[pl.BlockSpec((tk, tm), lambda i, kk: (kk, i)),
                  pl.BlockSpec((tk, n), lambda i, kk: (kk, 0))] + ([ANY] if into is not None else []),
        out_specs=pl.BlockSpec((tm, n), lambda i, kk: (row_off // tm + i, 0)),
        out_shape=jax.ShapeDtypeStruct((rows, n), BF16),
        scratch_shapes=[pltpu.VMEM((tm, n), F32)],
        input_output_aliases={2: 0} if into is not None else {},
        compiler_params=_cp("parallel", "arbitrary"))(a, b, *([into] if into is not None else []))


def _rms(x, g):
    return x * lax.rsqrt(jnp.mean(x * x, axis=-1, keepdims=True) + EPS) * g


def _rms_fwd(x, g, name, tm=512):
    t, d = x.shape
    tm = min(tm, t)

    def body(x_ref, g_ref, o_ref):
        o_ref[...] = _rms(x_ref[...], g_ref[...]).astype(BF16)

    return pl.pallas_call(
        body, name=name, grid=(t // tm,),
        in_specs=[pl.BlockSpec((tm, d), lambda i: (i, 0)), pl.BlockSpec((1, d), lambda i: (0, 0))],
        out_specs=pl.BlockSpec((tm, d), lambda i: (i, 0)),
        out_shape=jax.ShapeDtypeStruct((t, d), BF16),
        compiler_params=_cp("parallel"))(x, g)


def _mix(ya, yb, ga, gb):
    return _sigmoid(ga) * ya + _sigmoid(gb) * yb


def _gate_specs(tm):
    half = D_MODEL // 2
    return [pl.BlockSpec((tm, half), lambda i, c=c: (i, c))
            for c in (COL_GA // half, COL_GA // half + 1, COL_GB // half, COL_GB // half + 1)]


def _merge_bwd(dx1, ya, yb, proj, w_pa, w_pb, w_o, name, tm=512):
    t, d = dx1.shape
    tm = min(tm, t)

    def body(dx_ref, ya_ref, yb_ref, ga0, ga1, gb0, gb1, wpa_ref, wpb_ref, wo_ref,
             dya_ref, dyb_ref, dg_ref, dog_ref, doa_ref):
        dmix = _dot(dx_ref[...].astype(BF16), wo_ref[...], NT)
        ga = jnp.concatenate([ga0[...], ga1[...]], axis=1)
        gb = jnp.concatenate([gb0[...], gb1[...]], axis=1)
        _, vjp = jax.vjp(_mix, ya_ref[...].astype(F32), yb_ref[...].astype(F32), ga, gb)
        dya, dyb, dga, dgb = vjp(dmix)
        dya, dyb = dya.astype(BF16), dyb.astype(BF16)
        dya_ref[...] = dya
        dyb_ref[...] = dyb
        dg_ref[:, :d] = dga.astype(BF16)
        dg_ref[:, d:] = dgb.astype(BF16)
        dog_ref[...] = _dot(dya, wpa_ref[...], NT)
        doa_ref[...] = _dot(dyb, wpb_ref[...], NT)

    row = pl.BlockSpec((tm, d), lambda i: (i, 0))
    wide = pl.BlockSpec((tm, 2 * d), lambda i: (i, 0))
    mat = pl.BlockSpec((d, d), lambda i: (0, 0))
    return pl.pallas_call(
        body, name=name, grid=(t // tm,), in_specs=[row, row, row] + _gate_specs(tm) + [mat, mat, mat],
        out_specs=[row, row, wide, row, row],
        out_shape=[jax.ShapeDtypeStruct((t, d), BF16), jax.ShapeDtypeStruct((t, d), BF16),
                   jax.ShapeDtypeStruct((t, 2 * d), BF16), jax.ShapeDtypeStruct((t, d), F32),
                   jax.ShapeDtypeStruct((t, d), F32)],
        compiler_params=_cp("parallel"))(dx1, ya, yb, proj, proj, proj, proj, w_pa, w_pb, w_o)


def _swiglu(g, u):
    return g * _sigmoid(g) * u


def _swiglu_bwd(g, u, dact):
    sg = _sigmoid(g)
    gs = g * sg
    return dact * u * (sg + gs * (1.0 - sg)), dact * gs


def _ffn_up_fwd(h2, wgu_t, name, tm=512):
    t, d = h2.shape
    tm = min(tm, t)
    fh = FFN_HIDDEN // 2

    def body(a_ref, w_ref, gu_ref, act_ref):
        r = _dot(a_ref[...], w_ref[...], NT)
        gu_ref[...] = r.astype(BF16)
        act_ref[...] = _swiglu(r[:, :fh], r[:, fh:]).astype(BF16)

    return pl.pallas_call(
        body, name=name, grid=(2, t // tm),
        in_specs=[pl.BlockSpec((tm, d), lambda j, i: (i, 0)), pl.BlockSpec((2 * fh, d), lambda j, i: (j, 0))],
        out_specs=[pl.BlockSpec((tm, 2 * fh), lambda j, i: (i, j)), pl.BlockSpec((tm, fh), lambda j, i: (i, j))],
        out_shape=[jax.ShapeDtypeStruct((t, 4 * fh), BF16), jax.ShapeDtypeStruct((t, 2 * fh), BF16)],
        compiler_params=_cp("parallel", "parallel"))(h2, wgu_t)


def _ffn_down_bwd(dx, w_d, gu, name, tm=512):
    t, d = dx.shape
    tm = min(tm, t)
    fh = FFN_HIDDEN // 2

    def body(a_ref, w_ref, gu_ref, o_ref):
        dact = _dot(a_ref[...].astype(BF16), w_ref[...], NT)
        dg, du = _swiglu_bwd(gu_ref[:, :fh].astype(F32), gu_ref[:, fh:].astype(F32), dact)
        o_ref[:, :fh] = dg.astype(BF16)
        o_ref[:, fh:] = du.astype(BF16)

    wide = pl.BlockSpec((tm, 2 * fh), lambda j, i: (i, j))
    return pl.pallas_call(
        body, name=name, grid=(2, t // tm),
        in_specs=[pl.BlockSpec((tm, d), lambda j, i: (i, 0)), pl.BlockSpec((fh, d), lambda j, i: (j, 0)), wide],
        out_specs=wide,
        out_shape=jax.ShapeDtypeStruct((t, 4 * fh), BF16),
        compiler_params=_cp("parallel", "parallel"))(dx, w_d, gu)


def _rows_bwd(pieces, w, x, g, dres, name, tm=256, ex=None):
    t, d = x.shape
    tm = min(tm, t)
    widths = [p.shape[1] for p in pieces]
    starts = [sum(widths[:i]) for i in range(len(widths))]
    assert sum(widths) == w.shape[0]
    n_p = len(pieces)

    def body(*refs):
        p_refs, (w_ref, x_ref, g_ref, dres_ref, dx_ref, dg_ref) = refs[:n_p], refs[n_p:]
        dh = _dot(p_refs[0][...], w_ref[pl.ds(starts[0], widths[0]), :], NN)
        for i in range(1, n_p):
            dh = dh + _dot(p_refs[i][...], w_ref[pl.ds(starts[i], widths[i]), :], NN)
        _, vjp = jax.vjp(_rms, x_ref[...], g_ref[...])
        dx, dg = vjp(dh)
        dx_ref[...] = dres_ref[...] + dx

        @pl.when(pl.program_id(0) == 0)
        def _():
            dg_ref[...] = jnp.zeros_like(dg_ref)

        dg_ref[...] += dg

    row = pl.BlockSpec((tm, d), lambda i: (i, 0))
    vec = pl.BlockSpec((1, d), lambda i: (0, 0))
    ex_in, ex_in_specs, ex_out, ex_out_specs, ex_scratch = _carried(ex)
    outs = pl.pallas_call(
        _carry(body, ex, n_p + 4, 2, 0, (t // tm,)), name=name, grid=(t // tm,),
        in_specs=[pl.BlockSpec((tm, k), lambda i: (i, 0)) for k in widths]
        + [pl.BlockSpec(w.shape, lambda i: (0, 0)), row, vec, row] + ex_in_specs,
        out_specs=[row, vec] + ex_out_specs,
        out_shape=[jax.ShapeDtypeStruct((t, d), F32), jax.ShapeDtypeStruct((1, d), F32)] + ex_out,
        scratch_shapes=ex_scratch,
        compiler_params=_cp("arbitrary"))(*pieces, w, x, g, dres, *ex_in)
    return (outs[0], outs[1], outs[2:]) if ex else (outs[0], outs[1])


def _merge_fwd(o_g, o_att, proj, x, w_pa, w_pb, w_o, gain, name, tm=512):
    t, d = x.shape
    tm = min(tm, t)

    def body(og_ref, oa_ref, ga0, ga1, gb0, gb1, x_ref, wpa_ref, wpb_ref, wo_ref, g_ref,
             ya_ref, yb_ref, mix_ref, h2_ref, x1_ref):
        ya = _dot(og_ref[...], wpa_ref[...], NN)
        yb = _dot(oa_ref[...], wpb_ref[...], NN)
        ga = jnp.concatenate([ga0[...], ga1[...]], axis=1)
        gb = jnp.concatenate([gb0[...], gb1[...]], axis=1)
        mix = _mix(ya, yb, ga, gb).astype(BF16)
        ya_ref[...] = ya.astype(BF16)
        yb_ref[...] = yb.astype(BF16)
        mix_ref[...] = mix
        x1 = x_ref[...] + _dot(mix, wo_ref[...], NN)
        x1_ref[...] = x1
        h2_ref[...] = _rms(x1, g_ref[...]).astype(BF16)

    row = pl.BlockSpec((tm, d), lambda i: (i, 0))
    mat = pl.BlockSpec((d, d), lambda i: (0, 0))
    return pl.pallas_call(
        body, name=name, grid=(t // tm,),
        in_specs=[row, row] + _gate_specs(tm) + [row, mat, mat, mat, pl.BlockSpec((1, d), lambda i: (0, 0))],
        out_specs=[row] * 5,
        out_shape=[jax.ShapeDtypeStruct((t, d), BF16)] * 4 + [jax.ShapeDtypeStruct((t, d), F32)],
        compiler_params=_cp("parallel"))(o_g, o_att, proj, proj, proj, proj, x, w_pa, w_pb, w_o, gain)


def _loss_head(x, g, target, name, tm=512):
    t, d = x.shape
    tm = min(tm, t)

    def body(x_ref, g_ref, t_ref, dx_ref, dg_ref, loss_ref):
        tgt = t_ref[...]

        def f(xv, gv):
            err = _rms(xv, gv) - tgt
            return 0.5 * jnp.sum(jnp.mean(err * err, axis=-1, keepdims=True))

        loss, vjp = jax.vjp(f, x_ref[...], g_ref[...])
        dx, dg = vjp(jnp.ones((), F32))
        dx_ref[...] = dx

        @pl.when(pl.program_id(0) == 0)
        def _():
            dg_ref[...] = jnp.zeros_like(dg_ref)
            loss_ref[...] = jnp.zeros_like(loss_ref)

        dg_ref[...] += dg
        loss_ref[...] += jnp.full(loss_ref.shape, loss, F32)

    row = pl.BlockSpec((tm, d), lambda i: (i, 0))
    vec = pl.BlockSpec((1, d), lambda i: (0, 0))
    lane = pl.BlockSpec((1, 128), lambda i: (0, 0))
    return pl.pallas_call(
        body, name=name, grid=(t // tm,), in_specs=[row, vec, row], out_specs=[row, vec, lane],
        out_shape=[jax.ShapeDtypeStruct((t, d), F32), jax.ShapeDtypeStruct((1, d), F32),
                   jax.ShapeDtypeStruct((1, 128), F32)],
        compiler_params=_cp("arbitrary"))(x, g, target)


def _lb_rows(l0, l1, l2, l3):
    mx = jnp.maximum(jnp.maximum(l0, l1), jnp.maximum(l2, l3))
    e0, e1, e2, e3 = jnp.exp(l0 - mx), jnp.exp(l1 - mx), jnp.exp(l2 - mx), jnp.exp(l3 - mx)
    s = e0 + e1 + e2 + e3
    p0, p1, p2, p3 = e0 / s, e1 / s, e2 / s, e3 / s
    c1 = p0 + p1
    c2 = c1 + p2
    c3 = c2 + p3
    return p0 - p0, c1 - p0, c2 - p0, c3 - p0


def _lb_fwd(lb_logits):
    def body(l_ref, o_ref):
        rows = _lb_rows(*[l_ref[pl.ds(i, 1), :] for i in range(DEPTH)])
        for i in range(DEPTH):
            o_ref[pl.ds(i, 1), :] = rows[i]

    return pl.pallas_call(body, name="lb_fwd", out_shape=jax.ShapeDtypeStruct(lb_logits.shape, F32))(lb_logits)


def _lb_bwd(lb_logits, dlb):
    def body(l_ref, d_ref, o_ref):
        _, vjp = jax.vjp(_lb_rows, *[l_ref[pl.ds(i, 1), :] for i in range(DEPTH)])
        grads = vjp(tuple(d_ref[pl.ds(i, 1), :] for i in range(DEPTH)))
        for i in range(DEPTH):
            o_ref[pl.ds(i, 1), :] = grads[i]

    return pl.pallas_call(body, name="lb_bwd", out_shape=jax.ShapeDtypeStruct(lb_logits.shape, F32))(lb_logits, dlb)


MESH = pl.DeviceIdType.MESH
ANY = pl.BlockSpec(memory_space=pl.ANY)
N_KINDS = len(SHARD_ROWS)
FFN_HALF = FFN_HIDDEN // 2
KIND_PLACE = ((0, 0), (1, 0), (2, 0), (3, 0), (4, 0), (4, FFN_HALF), (5, 0))
KIND_HALF_SKIP = (0, 0, 0, 0, FFN_HALF, FFN_HALF, 0)
FULL_ROWS = (N_DEV * SHARD_ROWS[0], D_MODEL, D_MODEL, D_MODEL, 2 * N_DEV * SHARD_ROWS[4], N_DEV * SHARD_ROWS[6])


def _kind_rows(ti, dev):
    oi, base = KIND_PLACE[ti]
    start = base + dev * SHARD_ROWS[ti]
    if KIND_HALF_SKIP[ti]:
        start = start + (dev // (N_DEV // 2)) * KIND_HALF_SKIP[ti]
    return oi, pl.ds(start, SHARD_ROWS[ti])


def _position():
    x, y, c = lax.axis_index("x"), lax.axis_index("y"), lax.axis_index("c")
    return x, y, c, 4 * x + 2 * y + c


def _peer(x, y, c, r):
    px = 1 - x if r & 4 else x
    py = 1 - y if r & 2 else y
    pc = 1 - c if r & 1 else c
    return (px, py, pc), 4 * px + 2 * py + pc


class _Exchange(NamedTuple):
    operands: tuple
    out_shape: tuple
    copies: Callable
    n_local: int


EXCHANGE_SCRATCH = (pltpu.SemaphoreType.DMA((N_DEV, N_KINDS)), pltpu.SemaphoreType.DMA((N_DEV, N_KINDS)),
                    pltpu.SemaphoreType.DMA((N_KINDS,)))
ALL_KINDS = tuple(range(N_KINDS))
KINDS_W_IN = (0,)
KINDS_REST = ALL_KINDS[1:]


def _all_pairs(kinds, ends, send_sems, recv_sems):
    x, y, c, me = _position()
    out = []
    for r in range(1, N_DEV):
        peer, pid = _peer(x, y, c, r)
        for ti in kinds:
            src, dst = ends(ti, me, pid)
            out.append(pltpu.make_async_remote_copy(
                src_ref=src, dst_ref=dst, send_sem=send_sems.at[r, ti], recv_sem=recv_sems.at[r, ti],
                device_id=peer, device_id_type=MESH))
    return out


def _gather_exchange(shards, kinds):
    arrays = sorted({KIND_PLACE[ti][0] for ti in kinds})

    def copies(ins, outs, send_sems, recv_sems, local_sems, arrivals):
        src = dict(zip(kinds, ins))

        def window(ti, dev):
            oi, rows = _kind_rows(ti, dev)
            return outs[arrays.index(oi)].at[rows, :]

        if arrivals:
            return _all_pairs(kinds, lambda ti, me, pid: (src[ti], window(ti, pid)), send_sems, recv_sems)
        _, _, _, me = _position()
        local = [pltpu.make_async_copy(src[ti], window(ti, me), local_sems.at[ti]) for ti in kinds]
        return local + _all_pairs(kinds, lambda ti, me, pid: (src[ti], window(ti, me)), send_sems, recv_sems)

    return _Exchange(tuple(shards[ti] for ti in kinds),
                     tuple(jax.ShapeDtypeStruct((FULL_ROWS[oi], D_MODEL), BF16) for oi in arrays), copies, len(kinds))


def _scatter_exchange(grads, kinds):
    arrays = sorted({KIND_PLACE[ti][0] for ti in kinds})
    offsets, total = {}, 0
    for ti in kinds:
        offsets[ti], total = total, total + SHARD_ROWS[ti]

    def copies(ins, outs, send_sems, recv_sems, local_sems, arrivals):
        land = outs[0]

        def piece(ti, dev):
            ii, rows = _kind_rows(ti, dev)
            return ins[arrays.index(ii)].at[rows, :]

        def slot(ti, dev):
            return land.at[dev, pl.ds(offsets[ti], SHARD_ROWS[ti]), :]

        if arrivals:
            return _all_pairs(kinds, lambda ti, me, pid: (piece(ti, me), slot(ti, pid)), send_sems, recv_sems)
        _, _, _, me = _position()
        local = [pltpu.make_async_copy(piece(ti, me), slot(ti, me), local_sems.at[ti]) for ti in kinds]
        return local + _all_pairs(kinds, lambda ti, me, pid: (piece(ti, pid), slot(ti, me)), send_sems, recv_sems)

    return _Exchange(tuple(grads[oi] for oi in arrays), (jax.ShapeDtypeStruct((N_DEV, total, D_MODEL), BF16),),
                     copies, len(kinds))


def _exchange_start(ex, ins, outs, sems):
    for cp in ex.copies(ins, outs, *sems, False):
        cp.start()


def _exchange_finish(ex, ins, outs, sems):
    for cp in ex.copies(ins, outs, *sems, True):
        cp.wait_recv()
    mine = ex.copies(ins, outs, *sems, False)
    for cp in mine[:ex.n_local]:
        cp.wait()
    for cp in mine[ex.n_local:]:
        cp.wait_send()


def _run_exchange(ex, name):
    n_in, n_out = len(ex.operands), len(ex.out_shape)

    def body(*refs):
        ins, outs, sems = refs[:n_in], refs[n_in:n_in + n_out], refs[n_in + n_out:]
        _exchange_start(ex, ins, outs, sems)
        _exchange_finish(ex, ins, outs, sems)

    return pl.pallas_call(body, name=name, in_specs=[ANY] * n_in, out_specs=[ANY] * n_out,
                          out_shape=list(ex.out_shape), scratch_shapes=list(EXCHANGE_SCRATCH))(*ex.operands)


def _carry(body, ex, n_in, n_out, n_scratch, grid):
    if ex is None:
        return body
    e_in, e_out = len(ex.operands), len(ex.out_shape)

    def at(step):
        hit = pl.program_id(0) == step[0]
        for axis in range(1, len(grid)):
            hit = hit & (pl.program_id(axis) == step[axis])
        return hit

    def carrying(*refs):
        own_in, ex_in = refs[:n_in], refs[n_in:n_in + e_in]
        rest = refs[n_in + e_in:]
        own_out, ex_out = rest[:n_out], rest[n_out:n_out + e_out]
        own_scratch, sems = rest[n_out + e_out:n_out + e_out + n_scratch], rest[n_out + e_out + n_scratch:]

        @pl.when(at([0] * len(grid)))
        def _():
            _exchange_start(ex, ex_in, ex_out, sems)

        body(*own_in, *own_out, *own_scratch)

        @pl.when(at([g - 1 for g in grid]))
        def _():
            _exchange_finish(ex, ex_in, ex_out, sems)

    return carrying


def _carried(ex):
    if ex is None:
        return (), [], [], [], []
    return (ex.operands, [ANY] * len(ex.operands), list(ex.out_shape), [ANY] * len(ex.out_shape),
            list(EXCHANGE_SCRATCH))


def _small_sum_body(p_ref, o_ref, buf, send_sems, recv_sems):
    x, y, c, me = _position()
    buf[me] = p_ref[...]
    sends = []
    for r in range(1, N_DEV):
        peer, _ = _peer(x, y, c, r)
        sends.append(pltpu.make_async_remote_copy(
            src_ref=p_ref, dst_ref=buf.at[me], send_sem=send_sems.at[r], recv_sem=recv_sems.at[r],
            device_id=peer, device_id_type=MESH))
    for cp in sends:
        cp.start()
    for r in range(1, N_DEV):
        peer, pid = _peer(x, y, c, r)
        pltpu.make_async_remote_copy(
            src_ref=p_ref, dst_ref=buf.at[pid], send_sem=send_sems.at[r], recv_sem=recv_sems.at[r],
            device_id=peer, device_id_type=MESH).wait_recv()
    for cp in sends:
        cp.wait_send()
    acc = buf[0]
    for k in range(1, N_DEV):
        acc = acc + buf[k]
    o_ref[...] = acc


def _all_reduce_small(part):
    rows, d = part.shape
    vmem = pl.BlockSpec(memory_space=pltpu.VMEM)
    return pl.pallas_call(
        functools.partial(_small_sum_body), name="all_reduce_small", in_specs=[vmem], out_specs=vmem,
        out_shape=jax.ShapeDtypeStruct((rows, d), F32),
        scratch_shapes=[pltpu.VMEM((N_DEV, rows, d), F32), pltpu.SemaphoreType.DMA((N_DEV,)),
                        pltpu.SemaphoreType.DMA((N_DEV,))],
    )(part)


HG_PAIR = 2 * HG_DK


def _hg_consts():
    c = HG_CHUNK
    r = lax.broadcasted_iota(jnp.int32, (c, c), 0)
    s = lax.broadcasted_iota(jnp.int32, (c, c), 1)
    r2 = lax.broadcasted_iota(jnp.int32, (c, 2 * c), 0)
    s2 = lax.broadcasted_iota(jnp.int32, (c, 2 * c), 1)
    causal2 = jnp.where(s2 >= c, s2 - c, s2) <= r2
    lane_hi = lax.broadcasted_iota(jnp.int32, (c, HG_PAIR), 1) >= HG_DK
    same_head = ((lax.broadcasted_iota(jnp.int32, (HG_PAIR, HG_PAIR), 0) >= HG_DK)
                 == (lax.broadcasted_iota(jnp.int32, (HG_PAIR, HG_PAIR), 1) >= HG_DK))
    return (s <= r).astype(BF16), (s >= r).astype(BF16), causal2, lane_hi, same_head


def _head_rows(x, lane_hi):
    zero = jnp.zeros_like(x)
    return jnp.concatenate([jnp.where(lane_hi, zero, x), jnp.where(lane_hi, x, zero)], axis=0)


def _own_rows(y, lane_hi):
    return jnp.where(lane_hi, y[HG_CHUNK:], y[:HG_CHUNK])


def _split3(x):
    hi = x.astype(BF16)
    r1 = x - hi.astype(F32)
    mid = r1.astype(BF16)
    lo = (r1 - mid.astype(F32)).astype(BF16)
    return jnp.concatenate([hi, mid, lo], axis=1)


def _cumsum_rows(tri, x):
    w = x.shape[1]
    y = _dot(tri, _split3(x), NN)
    return y[:, :w] + y[:, w:2 * w] + y[:, 2 * w:]


def _hg_gates(zq, zf, lb):
    sq = _sigmoid(zq)
    sg = _sigmoid(zf)
    f = lb + (1.0 - lb) * sg
    return sq, zq * sq, sg, f, jnp.log(jnp.maximum(f, MIN_F)), 1.0 - f


def _hg_decays(ball_ref, ci, cols):
    c = HG_CHUNK
    b = ball_ref[pl.ds(ci * c, c), cols]
    mid = ball_ref[pl.ds(ci * c + c // 2 - 1, 1), cols]
    bc = ball_ref[pl.ds(ci * c + c - 1, 1), cols]
    return b, jnp.exp(jnp.minimum(b - mid, HG_EXP_CLAMP)), jnp.exp(jnp.minimum(mid - b, HG_EXP_CLAMP)), bc


def _hg_gate(o, zg, gn):
    return o * lax.rsqrt(jnp.mean(o * o, axis=-1, keepdims=True) + EPS) * gn * (zg * _sigmoid(zg))


def _hgrn2_fwd(proj, lb, gn, name, ex=None):
    t = proj.shape[0]
    bs_tok = min(HG_BLOCK, t)
    n_chunks = bs_tok // HG_CHUNK
    w = HG_HEADS * HG_DK

    def body(hq_ref, hf_ref, hi_ref, hg_ref, lb_ref, gn_ref, o_ref, og_ref, sall_ref, ball_ref, aall_ref, st_ref):
        @pl.when(pl.program_id(0) == 0)
        def _():
            st_ref[...] = jnp.zeros_like(st_ref)

        tril, _, causal2, lane_hi, same_head = _hg_consts()

        for ci in range(n_chunks):
            rows = pl.ds(ci * HG_CHUNK, HG_CHUNK)
            for p in range(HG_HEADS // 2):
                cols = slice(p * HG_PAIR, (p + 1) * HG_PAIR)
                v = hi_ref[rows, cols]
                zg = hg_ref[rows, cols]
                _, q, _, _, logf, k = _hg_gates(hq_ref[rows, cols], hf_ref[rows, cols], lb_ref[:, cols])
                ball_ref[rows, cols] = _cumsum_rows(tril, logf)
                b, em, en, bc = _hg_decays(ball_ref, ci, cols)
                st0 = st_ref[p]
                sall_ref[ci, 2 * p] = st0[:HG_DK, :HG_DK]
                sall_ref[ci, 2 * p + 1] = st0[HG_DK:, HG_DK:]
                vb = v.astype(BF16)
                o = _dot((q * jnp.exp(b)).astype(BF16), st0.astype(BF16), NT)
                a = jnp.where(causal2, _dot((q * em).astype(BF16), _head_rows((k * en).astype(BF16), lane_hi), NT),
                              0.0).astype(BF16)
                aall_ref[ci, p] = a
                o = o + _dot(a, _head_rows(vb, lane_hi), NN)
                kdec = (k * jnp.exp(bc - b)).astype(BF16)
                st_ref[p] = st0 * jnp.exp(bc) + jnp.where(same_head, _dot(vb, kdec, TN), 0.0)
                o_ref[rows, cols] = o
                for hh in range(2):
                    sl = slice(hh * HG_DK, (hh + 1) * HG_DK)
                    hcols = slice(p * HG_PAIR + hh * HG_DK, p * HG_PAIR + (hh + 1) * HG_DK)
                    og_ref[rows, hcols] = _hg_gate(o[:, sl], zg[:, sl], gn_ref[:, hcols]).astype(BF16)

    def col(j):
        return pl.BlockSpec((bs_tok, w), lambda n, j=j: (n, j))

    vec = pl.BlockSpec((1, w), lambda n: (0, 0))
    ex_in, ex_in_specs, ex_out, ex_out_specs, ex_scratch = _carried(ex)
    outs = pl.pallas_call(
        _carry(body, ex, 6, 5, 1, (t // bs_tok,)), name=name, grid=(t // bs_tok,),
        in_specs=[col(COL_HQ // w), col(COL_HF // w), col(COL_HI // w), col(COL_HG // w), vec, vec] + ex_in_specs,
        out_specs=[col(0), col(0),
                   pl.BlockSpec((n_chunks, HG_HEADS, HG_DK, HG_DK), lambda n: (n, 0, 0, 0)), col(0),
                   pl.BlockSpec((n_chunks, HG_HEADS // 2, HG_CHUNK, 2 * HG_CHUNK), lambda n: (n, 0, 0, 0))] + ex_out_specs,
        out_shape=[jax.ShapeDtypeStruct((t, w), F32), jax.ShapeDtypeStruct((t, w), BF16),
                   jax.ShapeDtypeStruct((t // HG_CHUNK, HG_HEADS, HG_DK, HG_DK), F32), jax.ShapeDtypeStruct((t, w), F32),
                   jax.ShapeDtypeStruct((t // HG_CHUNK, HG_HEADS // 2, HG_CHUNK, 2 * HG_CHUNK), BF16)] + ex_out,
        scratch_shapes=[pltpu.VMEM((HG_HEADS // 2, HG_PAIR, HG_PAIR), F32)] + ex_scratch,
        compiler_params=_cp("arbitrary"))(proj, proj, proj, proj, lb, gn, *ex_in)
    return outs[:5], outs[5:]


def _hgrn2_bwd(proj, lb, gn, o_hg, sall, ball, aall, dog, name, ex=None):
    t = proj.shape[0]
    bs_tok = min(HG_BLOCK, t)
    n_chunks = bs_tok // HG_CHUNK
    n_blocks = t // bs_tok
    w = HG_HEADS * HG_DK

    def body(hq_ref, hf_ref, hi_ref, hg_ref, lb_ref, gn_ref, o_ref, sall_ref, snext_ref, ball_ref, aall_ref, dog_ref,
             da_ref, dlb_ref, dgn_ref, dst_ref):
        @pl.when(pl.program_id(0) == 0)
        def _():
            dst_ref[...] = jnp.zeros_like(dst_ref)
            dlb_ref[...] = jnp.zeros_like(dlb_ref)
            dgn_ref[...] = jnp.zeros_like(dgn_ref)

        _, rev_tril, causal2, lane_hi, same_head = _hg_consts()
        zero_block = jnp.zeros((HG_DK, HG_DK), F32)

        for ci in reversed(range(n_chunks)):
            rows = pl.ds(ci * HG_CHUNK, HG_CHUNK)
            for p in range(HG_HEADS // 2):
                cols = slice(p * HG_PAIR, (p + 1) * HG_PAIR)
                zq = hq_ref[rows, cols]
                v = hi_ref[rows, cols]
                zg = hg_ref[rows, cols]
                lbv = lb_ref[:, cols]
                sq, q, sg, f, _, k = _hg_gates(zq, hf_ref[rows, cols], lbv)
                b, em, en, bc = _hg_decays(ball_ref, ci, cols)
                st0 = jnp.concatenate([jnp.concatenate([sall_ref[ci, 2 * p], zero_block], axis=1),
                                       jnp.concatenate([zero_block, sall_ref[ci, 2 * p + 1]], axis=1)], axis=0)
                dst1 = dst_ref[p]
                vb = v.astype(BF16)
                eb = jnp.exp(b)
                qg = (q * eb).astype(BF16)
                qt = (q * em).astype(BF16)
                kref = (k * en).astype(BF16)
                ebcb = jnp.exp(bc - b)
                kdec = (k * ebcb).astype(BF16)
                ebc = jnp.exp(bc)
                dos, dzgs, dgns = [], [], []
                for hh in range(2):
                    sl = slice(hh * HG_DK, (hh + 1) * HG_DK)
                    hcols = slice(p * HG_PAIR + hh * HG_DK, p * HG_PAIR + (hh + 1) * HG_DK)
                    _, gate_vjp = jax.vjp(_hg_gate, o_ref[rows, hcols], zg[:, sl], gn_ref[:, hcols])
                    do_h, dzg_h, dgn_h = gate_vjp(dog_ref[rows, hcols])
                    dos.append(do_h)
                    dzgs.append(dzg_h)
                    dgns.append(dgn_h)
                dob = jnp.concatenate(dos, axis=1).astype(BF16)
                vrows, krows = _head_rows(vb, lane_hi), _head_rows(kref, lane_hi)
                dam = jnp.where(causal2, _dot(dob, vrows, NT), 0.0).astype(BF16)
                dk = ebcb * _dot(vb, dst1.astype(BF16), NN) + en * _own_rows(_dot(dam, qt, TN), lane_hi)
                dq = eb * _dot(dob, st0.astype(BF16), NN) + em * _dot(dam, krows, NN)
                dv = _own_rows(_dot(aall_ref[ci, p], dob, TN), lane_hi) + _dot(kdec, dst1.astype(BF16), NT)
                dst_ref[p] = dst1 * ebc + jnp.where(same_head, _dot(dob, qg, TN), 0.0)

                after = [sall_ref[ci + 1, 2 * p + hh] if ci + 1 < n_chunks else snext_ref[0, 2 * p + hh] for hh in range(2)]
                dbx = jnp.concatenate(
                    [jnp.sum(dst1[hh * HG_DK:(hh + 1) * HG_DK, hh * HG_DK:(hh + 1) * HG_DK] * after[hh], axis=0, keepdims=True)
                     for hh in range(2)], axis=1)
                dlogf = _cumsum_rows(rev_tril, q * dq - k * dk) + dbx
                df = jnp.where(f > MIN_F, dlogf / f, 0.0) - dk
                dzf = df * (1.0 - lbv) * sg * (1.0 - sg)
                dzq = dq * (sq * (1.0 + zq * (1.0 - sq)))
                da_ref[rows, pl.ds(COL_HQ + p * HG_PAIR, HG_PAIR)] = dzq.astype(BF16)
                da_ref[rows, pl.ds(COL_HF + p * HG_PAIR, HG_PAIR)] = dzf.astype(BF16)
                da_ref[rows, pl.ds(COL_HI + p * HG_PAIR, HG_PAIR)] = dv.astype(BF16)
                da_ref[rows, pl.ds(COL_HG + p * HG_PAIR, HG_PAIR)] = jnp.concatenate(dzgs, axis=1).astype(BF16)
                dlb_ref[:, cols] += jnp.sum(df * (1.0 - sg), axis=0, keepdims=True)
                dgn_ref[:, cols] += jnp.concatenate(dgns, axis=1)

    def col(j):
        return pl.BlockSpec((bs_tok, w), lambda n, j=j: (n_blocks - 1 - n, j))

    vec = pl.BlockSpec((1, w), lambda n: (0, 0))
    ex_in, ex_in_specs, ex_out, ex_out_specs, ex_scratch = _carried(ex)
    outs = pl.pallas_call(
        _carry(body, ex, 12, 3, 1, (n_blocks,)), name=name, grid=(n_blocks,),
        in_specs=[col(COL_HQ // w), col(COL_HF // w), col(COL_HI // w), col(COL_HG // w), vec, vec, col(0),
                  pl.BlockSpec((n_chunks, HG_HEADS, HG_DK, HG_DK), lambda n: (n_blocks - 1 - n, 0, 0, 0)),
                  pl.BlockSpec((1, HG_HEADS, HG_DK, HG_DK),
                               lambda n: (jnp.minimum((n_blocks - n) * n_chunks, t // HG_CHUNK - 1), 0, 0, 0)),
                  col(0),
                  pl.BlockSpec((n_chunks, HG_HEADS // 2, HG_CHUNK, 2 * HG_CHUNK), lambda n: (n_blocks - 1 - n, 0, 0, 0)),
                  col(0)] + ex_in_specs,
        out_specs=[pl.BlockSpec((bs_tok, 4 * w), lambda n: (n_blocks - 1 - n, 0)), vec, vec] + ex_out_specs,
        out_shape=[jax.ShapeDtypeStruct((t, 4 * w), BF16), jax.ShapeDtypeStruct((1, w), F32),
                   jax.ShapeDtypeStruct((1, w), F32)] + ex_out,
        scratch_shapes=[pltpu.VMEM((HG_HEADS // 2, HG_PAIR, HG_PAIR), F32)] + ex_scratch,
        compiler_params=_cp("arbitrary"))(proj, proj, proj, proj, lb, gn, o_hg, sall, sall, ball, aall, dog, *ex_in)
    return outs[:3], outs[3:]


def _rope_tables(t):
    half = ROPE_DIM // 2
    inv = ROPE_THETA ** (-jnp.arange(half, dtype=F32) * 2.0 / ROPE_DIM)
    d = jnp.arange(2 * ATT_HEAD_DIM) % ATT_HEAD_DIM
    ang = jnp.arange(t).astype(F32)[:, None] * inv[d % half][None, :]
    cos, sin = jnp.cos(ang), jnp.sin(ang)
    c = jnp.where(d < ROPE_DIM, cos, 1.0)
    su = jnp.where(d < half, -sin, 0.0)
    sd = jnp.where((d >= half) & (d < ROPE_DIM), sin, 0.0)
    return c, su, sd


def _rope(x, tabs):
    c, su, sd = tabs
    n = x.shape[1]
    half = ROPE_DIM // 2
    return x * c + pltpu.roll(x, n - half, 1) * su + pltpu.roll(x, half, 1) * sd


def _rope_t(dy, tabs):
    c, su, sd = tabs
    n = dy.shape[1]
    half = ROPE_DIM // 2
    return dy * c + pltpu.roll(dy * su, half, 1) + pltpu.roll(dy * sd, n - half, 1)


def _swa_specs(n_blocks, clamp):
    blk = ATT_BLOCK

    def cur(n):
        return jnp.minimum(n, n_blocks - 1) if clamp else n

    def prev(n):
        return jnp.maximum(cur(n) - 1, 0)

    q_spec = pl.BlockSpec((blk, 512), lambda m, n: (cur(n), COL_AQ // 512 + m))
    kv = [pl.BlockSpec((blk, 128), lambda m, n, c=c, f=f: (f(n), c + m))
          for c in (COL_AK // 128, COL_AV // 128) for f in (cur, prev)]
    tabs = [pl.BlockSpec((blk, 128), lambda m, n, f=f: (f(n), 0)) for f in (cur, prev) for _ in range(3)]
    return q_spec, kv, tabs, cur, prev


ATT_SCALE = ATT_HEAD_DIM ** -0.5


def _head_halves(x, upper):
    zero = jnp.zeros_like(x)
    return jnp.concatenate([jnp.where(upper, zero, x), jnp.where(upper, x, zero)], axis=0)


def _swa_scores(scores_t, sink, mask_t):
    s = jnp.where(mask_t, scores_t, -jnp.inf)
    mx = jnp.maximum(jnp.max(s, axis=0, keepdims=True), sink)
    p = jnp.exp(s - mx)
    es = jnp.exp(sink - mx)
    rinv = 1.0 / (jnp.sum(p, axis=0, keepdims=True) + es)
    return p * rinv, es * rinv


def _swa_window(kc_ref, kp_ref, vc_ref, vp_ref, tabs_c, tabs_p, n):
    k2 = jnp.concatenate([_rope(kp_ref[...], tabs_p), _rope(kc_ref[...], tabs_c)], axis=0)
    v2 = jnp.concatenate([vp_ref[...], vc_ref[...]], axis=0)
    blk = ATT_BLOCK
    kj = lax.broadcasted_iota(jnp.int32, (2 * blk, blk), 0)
    qi = lax.broadcasted_iota(jnp.int32, (2 * blk, blk), 1)
    delta = qi + blk - kj
    mask_t = (delta >= 0) & (delta < blk) & ((kj >= blk) | (n > 0))
    return k2, v2, mask_t


def _swa_fwd(proj, sinks, tabs, name, ex=None):
    t = proj.shape[0]
    n_blocks = t // ATT_BLOCK
    q_spec, kv_specs, tab_specs, _, _ = _swa_specs(n_blocks, clamp=False)

    def body(q_ref, kc_ref, kp_ref, vc_ref, vp_ref, c0, c1, c2, p0, p1, p2, sink_ref, o_ref):
        m, n = pl.program_id(0), pl.program_id(1)
        tabs_c = (c0[...], c1[...], c2[...])
        tabs_p = (p0[...], p1[...], p2[...])
        k2, v2, mask_t = _swa_window(kc_ref, kp_ref, vc_ref, vp_ref, tabs_c, tabs_p, n)
        k2r, v2r = pltpu.roll(k2, 64, 1), pltpu.roll(v2, 64, 1)
        upper_k = lax.broadcasted_iota(jnp.int32, k2.shape, 1) >= 64
        upper_q = lax.broadcasted_iota(jnp.int32, (ATT_BLOCK, 128), 1) >= 64
        for jj in range(2):
            own = upper_k if jj else ~upper_k
            kd = jnp.where(own, k2, k2r).astype(BF16)
            vd = jnp.where(own, v2, v2r).astype(BF16)
            for pi in range(2):
                cols = slice(256 * jj + 128 * pi, 256 * jj + 128 * pi + 128)
                qp = _rope(q_ref[:, cols], tabs_c) * ATT_SCALE
                outs = []
                for e in range(2):
                    sink = sink_ref[0, 8 * m + 4 * jj + 2 * pi + e]
                    qm = jnp.where(upper_q if e else ~upper_q, qp, 0.0).astype(BF16)
                    pn, _ = _swa_scores(_dot(kd, qm, NT), sink, mask_t)
                    outs.append(_dot(pn.astype(BF16), vd, TN))
                o_ref[:, cols] = jnp.where(upper_q, outs[1], outs[0]).astype(BF16)

    ex_in, ex_in_specs, ex_out, ex_out_specs, ex_scratch = _carried(ex)
    outs = pl.pallas_call(
        _carry(body, ex, 12, 1, 0, (2, n_blocks)), name=name, grid=(2, n_blocks),
        in_specs=[q_spec] + kv_specs + tab_specs + [pl.BlockSpec(memory_space=pltpu.SMEM)] + ex_in_specs,
        out_specs=[pl.BlockSpec((ATT_BLOCK, 512), lambda m, n: (n, m))] + ex_out_specs,
        out_shape=[jax.ShapeDtypeStruct((t, ATT_Q_HEADS * ATT_HEAD_DIM), BF16)] + ex_out,
        scratch_shapes=ex_scratch,
        compiler_params=_cp("arbitrary", "arbitrary"))(proj, proj, proj, proj, proj, *tabs, *tabs, sinks, *ex_in)
    return outs[0], outs[1:]


def _swa_bwd(proj, sinks, tabs, o_att, do_att, name, ex=None):
    t = proj.shape[0]
    n_blocks = t // ATT_BLOCK
    blk = ATT_BLOCK
    q_spec, kv_specs, tab_specs, cur, prev = _swa_specs(n_blocks, clamp=True)

    def body(q_ref, kc_ref, kp_ref, vc_ref, vp_ref, c0, c1, c2, p0, p1, p2, sink_ref, o_ref, do_ref,
             dq_ref, dk_ref, dv_ref, ds_ref, ck_ref, cv_ref):
        m, n = pl.program_id(0), pl.program_id(1)

        @pl.when(n == 0)
        def _():
            ds_ref[...] = jnp.zeros_like(ds_ref)
            ck_ref[...] = jnp.zeros_like(ck_ref)
            cv_ref[...] = jnp.zeros_like(cv_ref)

        @pl.when(n < n_blocks)
        def _():
            tabs_c = (c0[...], c1[...], c2[...])
            tabs_p = (p0[...], p1[...], p2[...])
            k2, v2, mask_t = _swa_window(kc_ref, kp_ref, vc_ref, vp_ref, tabs_c, tabs_p, n)
            k2r, v2r = pltpu.roll(k2, 64, 1), pltpu.roll(v2, 64, 1)
            upper_k = lax.broadcasted_iota(jnp.int32, k2.shape, 1) >= 64
            lane = lax.broadcasted_iota(jnp.int32, (8, 128), 1)
            head_of_row = lax.broadcasted_iota(jnp.int32, (16, 128), 0) >= 8
            head_rows = (head_of_row == (lax.broadcasted_iota(jnp.int32, (16, 128), 1) >= 64)).astype(F32)
            dk2 = jnp.zeros(k2.shape, F32)
            dv2 = jnp.zeros(k2.shape, F32)
            dsv = jnp.zeros((8, 128), F32)
            nk = 2 * blk
            for jj in range(2):
                own = upper_k if jj else ~upper_k
                kh = _head_halves(jnp.where(own, k2, k2r).astype(BF16), upper_k)
                vh = _head_halves(jnp.where(own, v2, v2r).astype(BF16), upper_k)
                dkd = jnp.zeros(k2.shape, F32)
                dvd = jnp.zeros(k2.shape, F32)
                for pi in range(2):
                    cols = slice(256 * jj + 128 * pi, 256 * jj + 128 * pi + 128)
                    qp = (_rope(q_ref[:, cols], tabs_c) * ATT_SCALE).astype(BF16)
                    do_pair = do_ref[:, cols]
                    dob = do_pair.astype(BF16)
                    delta2 = lax.dot_general(head_rows, do_pair * o_ref[:, cols].astype(F32), ((NT), ((), ())),
                                             precision=lax.Precision.HIGHEST, preferred_element_type=F32)
                    st = _dot(kh, qp, NT)
                    dpt = _dot(vh, dob, NT)
                    pns, dss = [], []
                    for e in range(2):
                        hl = 4 * jj + 2 * pi + e
                        pn, ps = _swa_scores(st[e * nk:(e + 1) * nk], sink_ref[0, 8 * m + hl], mask_t)
                        delta = jnp.max(delta2[8 * e:8 * e + 8], axis=0, keepdims=True)
                        pns.append(pn.astype(BF16))
                        dss.append((pn * (dpt[e * nk:(e + 1) * nk] - delta)).astype(BF16))
                        dsv = dsv + jnp.where(lane == hl, -jnp.sum(ps * delta), 0.0)
                    dsb = jnp.concatenate(dss, axis=0)
                    dq_ref[:, cols] = _rope_t(_dot(dsb, kh, TN) * ATT_SCALE, tabs_c).astype(BF16)
                    rk = _dot(dsb, qp, NN)
                    rv = _dot(jnp.concatenate(pns, axis=0), dob, NN)
                    dkd = dkd + jnp.where(upper_k, rk[nk:], rk[:nk])
                    dvd = dvd + jnp.where(upper_k, rv[nk:], rv[:nk])
                dk2 = dk2 + jnp.where(own, dkd + pltpu.roll(dkd, 64, 1), 0.0)
                dv2 = dv2 + jnp.where(own, dvd + pltpu.roll(dvd, 64, 1), 0.0)
            dk_ref[...] = (ck_ref[...] + _rope_t(dk2[:blk], tabs_p)).astype(BF16)
            dv_ref[...] = (cv_ref[...] + dv2[:blk]).astype(BF16)
            ck_ref[...] = _rope_t(dk2[blk:], tabs_c)
            cv_ref[...] = dv2[blk:]
            ds_ref[...] += dsv

        @pl.when(n == n_blocks)
        def _():
            dk_ref[...] = ck_ref[...].astype(BF16)
            dv_ref[...] = cv_ref[...].astype(BF16)

    wide = pl.BlockSpec((blk, 512), lambda m, n: (cur(n), m))
    lagged = pl.BlockSpec((blk, 128), lambda m, n: (jnp.maximum(n - 1, 0), m))
    ex_in, ex_in_specs, ex_out, ex_out_specs, ex_scratch = _carried(ex)
    outs = pl.pallas_call(
        _carry(body, ex, 14, 4, 2, (2, n_blocks + 1)), name=name, grid=(2, n_blocks + 1),
        in_specs=[q_spec] + kv_specs + tab_specs + [pl.BlockSpec(memory_space=pltpu.SMEM), wide, wide] + ex_in_specs,
        out_specs=[wide, lagged, lagged, pl.BlockSpec((None, 8, 128), lambda m, n: (m, 0, 0))] + ex_out_specs,
        out_shape=[jax.ShapeDtypeStruct((t, 1024), BF16), jax.ShapeDtypeStruct((t, 256), BF16),
                   jax.ShapeDtypeStruct((t, 256), BF16), jax.ShapeDtypeStruct((2, 8, 128), F32)] + ex_out,
        scratch_shapes=[pltpu.VMEM((blk, 128), F32), pltpu.VMEM((blk, 128), F32)] + ex_scratch,
        compiler_params=_cp("arbitrary", "arbitrary"))(proj, proj, proj, proj, proj, *tabs, *tabs, sinks, o_att, do_att,
                                                       *ex_in)
    return outs[:4], outs[4:]


def _local_step(x, target, shards, norm1, lb_logits, hg_norm, attn_sinks, norm2, final_norm):
    t = x.shape[0]
    tabs = _rope_tables(t)
    lb_all = _lb_fwd(lb_logits)
    saved = []

    def shards_of(l):
        return {ti: shards[ti][l] for ti in ALL_KINDS}

    win_next = _run_exchange(_gather_exchange(shards_of(0), KINDS_W_IN), "gather_w_in")
    rest_next = None
    for l in range(DEPTH):
        n1, n2 = norm1[l][None, :], norm2[l][None, :]
        lb, gn, sinks = lb_all[l][None, :], hg_norm[l][None, :], attn_sinks[l][None, :]
        (win_t,) = win_next
        if l == 0:
            h = _rms_fwd(x, n1, "rms1_fwd")
        if l == 0:
            proj, rest_next = _matmul_nt(h, win_t, 0, IN_COLS, F32, "proj_fwd", tn=1280,
                                         ex=_gather_exchange(shards_of(0), KINDS_REST))
        else:
            proj = _matmul_nt(h, win_t, 0, IN_COLS, F32, "proj_fwd", tn=1280)
        w_pa, w_pb, w_o, wgu_t, w_d = rest_next
        more = l + 1 < DEPTH
        (o_hg, o_g, sall, ball, aall), rest_next = _hgrn2_fwd(
            proj, lb, gn, "hgrn2_fwd", _gather_exchange(shards_of(l + 1), KINDS_REST) if more else None)
        o_att, win_next = _swa_fwd(
            proj, sinks, tabs, "swa_fwd", _gather_exchange(shards_of(l + 1), KINDS_W_IN) if more else None)
        ya, yb, mix, h2, x1 = _merge_fwd(o_g, o_att, proj, x, w_pa, w_pb, w_o, n2, "merge_fwd")
        gu, act = _ffn_up_fwd(h2, wgu_t, "ffn_up_fwd")
        if more:
            x2, h_next = _matmul_nn(act, w_d, 0, x1, "wd_fwd", gain=norm1[l + 1][None, :])
        else:
            x2, h_next = _matmul_nn(act, w_d, 0, x1, "wd_fwd_last"), None
        saved.append((x, h, proj, o_hg, o_g, (sall, ball, aall), o_att, ya, yb, mix, x1, h2, gu, act, n1, n2, lb, gn, sinks,
                      (win_t, w_pa, w_pb, w_o, wgu_t, w_d)))
        x, h = x2, h_next

    dx, d_fn, loss = _loss_head(x, final_norm[None, :], target, "loss_head")

    owned = [None] * DEPTH
    pending = None
    d_n1, d_n2, d_lb, d_gn, d_sinks = ([None] * DEPTH for _ in range(5))
    for l in reversed(range(DEPTH)):
        x0, h, proj, o_hg, o_g, hg_saved, o_att, ya, yb, mix, x1, h2, gu, act, n1, n2, lb, gn, sinks, weights = saved[l]
        win_t, w_pa, w_pb, w_o, wgu_t, w_d = weights
        dgu = _ffn_down_bwd(dx, w_d, gu, "ffn_down_bwd")
        g_wd = _matmul_tn(act, dx, "wd_grad", tm=1408)
        g_wgu = _matmul_tn(dgu, h2, "wgu_grad", tm=1408)
        late = pending is not None
        dx1, d_n2[l], *land_w = _rows_bwd([dgu], wgu_t, x1, n2, dx, "ffn_up_bwd", tm=512,
                                          ex=_scatter_exchange(pending, KINDS_W_IN) if late else None)
        g_wo = _matmul_tn(mix, dx1, "wo_grad")
        dya, dyb, dgab, dog, doatt = _merge_bwd(dx1, ya, yb, proj, w_pa, w_pb, w_o, "merge_bwd")
        g_wpa = _matmul_tn(o_g, dya, "wpa_grad")
        g_wpb = _matmul_tn(o_att, dyb, "wpb_grad")
        (dhg, d_lb[l], d_gn[l]), land_r = _hgrn2_bwd(proj, lb, gn, o_hg, *hg_saved, dog, "hgrn2_bwd",
                                                     _scatter_exchange(pending, KINDS_REST) if late else None)
        if late:
            owned[l + 1] = jnp.concatenate([_sum_slots(land_w[0][0], "sum_slots_w_in"),
                                            _sum_slots(land_r[0], "sum_slots_rest")], axis=0)
        ex = _scatter_exchange((None, g_wpa, g_wpb, g_wo, g_wgu, g_wd), KINDS_REST) if l == 0 else None
        (daq, dak, dav, d_sinks[l]), land_rest = _swa_bwd(proj, sinks, tabs, o_att, doatt, "swa_bwd", ex)
        g_win = None
        for piece, off, tm, tag in ((dhg, COL_HQ, 512, "hg"), (daq, COL_AQ, 512, "aq"), (dak, COL_AK, 256, "ak"),
                                    (dav, COL_AV, 256, "av"), (dgab, COL_GA, 512, "gates")):
            g_win = _matmul_tn(piece, h, "win_grad_" + tag, tm=tm, rows=IN_COLS, row_off=off, into=g_win)
        if l > 0:
            dx, d_n1[l] = _rows_bwd([dhg, daq, dak, dav, dgab], win_t, x0, n1, dx1, "win_bwd")
        else:
            dx, d_n1[l], land_win = _rows_bwd([dhg, daq, dak, dav, dgab], win_t, x0, n1, dx1, "win_bwd",
                                              ex=_scatter_exchange((g_win,), KINDS_W_IN))
        pending = (g_win, g_wpa, g_wpb, g_wo, g_wgu, g_wd)
    owned[0] = jnp.concatenate([_sum_slots(land_win[0], "sum_slots_w_in"), _sum_slots(land_rest[0], "sum_slots_rest")],
                               axis=0)

    d_sink_rows = [jnp.concatenate([d[0, 0, :8], d[1, 0, :8]]) for d in d_sinks]
    small = (jnp.concatenate(d_n1, axis=0), jnp.concatenate(d_lb, axis=0), jnp.concatenate(d_gn, axis=0),
             jnp.concatenate(d_n2, axis=0), d_fn, jnp.stack(d_sink_rows, axis=0))
    return loss, dx, jnp.stack(owned, axis=0), small


def _sum_slots(land, name, tr=480):
    _, rows, d = land.shape

    def body(l_ref, o_ref):
        acc = l_ref[0].astype(F32)
        for k in range(1, N_DEV):
            acc = acc + l_ref[k].astype(F32)
        o_ref[...] = acc

    return pl.pallas_call(
        body, name=name, grid=(rows // tr,),
        in_specs=[pl.BlockSpec((N_DEV, tr, d), lambda i: (0, i, 0))],
        out_specs=pl.BlockSpec((tr, d), lambda i: (i, 0)),
        out_shape=jax.ShapeDtypeStruct((rows, d), F32),
        compiler_params=_cp("parallel"))(land)


def _adamw(w, g, m, v, name):
    shape = w.shape
    c = shape[-1]
    rows = w.size // c
    tr = rows
    for cand in (512, 352, 128):
        if rows % cand == 0:
            tr = cand
            break
    c1 = 1.0 / (1.0 - ADAM_B1 ** ADAM_STEP)
    c2 = 1.0 / (1.0 - ADAM_B2 ** ADAM_STEP)

    def body(w_ref, g_ref, m_ref, v_ref, d_ref, nm_ref, nv_ref):
        gv = g_ref[...]
        nm = ADAM_B1 * m_ref[...] + (1.0 - ADAM_B1) * gv
        nv = ADAM_B2 * v_ref[...] + (1.0 - ADAM_B2) * (gv * gv)
        d_ref[...] = -ADAM_LR * ((nm * c1) / (jnp.sqrt(nv * c2) + ADAM_EPS) + ADAM_WD * w_ref[...])
        nm_ref[...] = nm
        nv_ref[...] = nv

    spec = pl.BlockSpec((tr, c), lambda i: (i, 0))
    outs = pl.pallas_call(
        body, name=name, grid=(rows // tr,), in_specs=[spec] * 4, out_specs=[spec] * 3,
        out_shape=[jax.ShapeDtypeStruct((rows, c), F32)] * 3,
        compiler_params=_cp("parallel"))(*[a.reshape(rows, c) for a in (w, g, m, v)])
    return tuple(o.reshape(shape) for o in outs)


def kernel(x, norm1, w_in, lb_logits, hg_norm, attn_sinks, w_pa, w_pb, w_o, norm2, w_gate, w_up, w_down, final_norm, loss_target, m_norm1, m_w_in, m_lb_logits, m_hg_norm, m_attn_sinks, m_w_pa, m_w_pb, m_w_o, m_norm2, m_w_gate, m_w_up, m_w_down, m_final_norm, v_norm1, v_w_in, v_lb_logits, v_hg_norm, v_attn_sinks, v_w_pa, v_w_pb, v_w_o, v_norm2, v_w_gate, v_w_up, v_w_down, v_final_norm):
    t = x.shape[1]
    shards = [jnp.swapaxes(w_in, 1, 2).astype(BF16), w_pa.astype(BF16), w_pb.astype(BF16), w_o.astype(BF16),
              jnp.swapaxes(w_gate, 1, 2).astype(BF16), jnp.swapaxes(w_up, 1, 2).astype(BF16), w_down.astype(BF16)]
    loss_lanes, grad_x, owned, small = _local_step(
        x.reshape(t, D_MODEL), loss_target.reshape(t, D_MODEL), shards,
        norm1, lb_logits, hg_norm, attn_sinks, norm2, final_norm)

    def rows_of(ti, transpose):
        g = owned[:, SLOT_OFF[ti]:SLOT_OFF[ti] + SHARD_ROWS[ti], :]
        return jnp.swapaxes(g, 1, 2) if transpose else g

    g_big = {"w_in": rows_of(0, True), "w_pa": rows_of(1, False), "w_pb": rows_of(2, False), "w_o": rows_of(3, False),
             "w_gate": rows_of(4, True), "w_up": rows_of(5, True), "w_down": rows_of(6, False)}

    d_n1, d_lb, d_gn, d_n2, d_fn, d_sinks = small
    pad = jnp.zeros((DEPTH, D_MODEL - ATT_Q_HEADS), F32)
    packed = jnp.concatenate([
        d_n1, d_lb, d_gn, d_n2, d_fn, jnp.concatenate([d_sinks, pad], axis=1),
        jnp.concatenate([loss_lanes, jnp.zeros((1, D_MODEL - 128), F32)], axis=1),
        jnp.zeros((SMALL_ROWS - 22, D_MODEL), F32)], axis=0)
    total = _all_reduce_small(packed)
    loss = total[21, 0]
    g_small = {"norm1": total[0:4], "lb_logits": _lb_bwd(lb_logits, total[4:8]), "hg_norm": total[8:12],
               "norm2": total[12:16], "final_norm": total[16], "attn_sinks": total[17:21, :ATT_Q_HEADS]}

    params = {"norm1": (norm1, m_norm1, v_norm1), "w_in": (w_in, m_w_in, v_w_in),
              "lb_logits": (lb_logits, m_lb_logits, v_lb_logits), "hg_norm": (hg_norm, m_hg_norm, v_hg_norm),
              "attn_sinks": (attn_sinks, m_attn_sinks, v_attn_sinks), "w_pa": (w_pa, m_w_pa, v_w_pa),
              "w_pb": (w_pb, m_w_pb, v_w_pb), "w_o": (w_o, m_w_o, v_w_o), "norm2": (norm2, m_norm2, v_norm2),
              "w_gate": (w_gate, m_w_gate, v_w_gate), "w_up": (w_up, m_w_up, v_w_up),
              "w_down": (w_down, m_w_down, v_w_down), "final_norm": (final_norm, m_final_norm, v_final_norm)}
    order = ["norm1", "w_in", "lb_logits", "hg_norm", "attn_sinks", "w_pa", "w_pb", "w_o", "norm2",
             "w_gate", "w_up", "w_down", "final_norm"]
    grads, deltas, new_m, new_v = [], [], [], []
    for name in order:
        w, m, v = params[name]
        g = (g_big[name] if name in g_big else g_small[name]).reshape(w.shape)
        w2 = w.reshape(1, -1) if w.ndim == 1 else w
        d, nm, nv = _adamw(w2, g.reshape(w2.shape), m.reshape(w2.shape), v.reshape(w2.shape), "adamw_" + name)
        grads.append(g)
        deltas.append(d.reshape(w.shape))
        new_m.append(nm.reshape(w.shape))
        new_v.append(nv.reshape(w.shape))
    return (loss, grad_x.reshape(x.shape), *grads, *deltas, *new_m, *new_v)
```

```python
import functools
from typing import Callable, NamedTuple

import jax
import jax.numpy as jnp
from jax import lax
from jax.experimental import pallas as pl
from jax.experimental.pallas import tpu as pltpu

F32, BF16 = jnp.float32, jnp.bfloat16

D_MODEL = 1024
DEPTH = 4
N_DEV = 8
HG_HEADS = 8
HG_DK = 128
HG_CHUNK = 64
HG_BLOCK = 512
HG_EXP_CLAMP = 60.0
ATT_Q_HEADS = 16
ATT_HEAD_DIM = 64
ATT_BLOCK = 128
ROPE_THETA = 500000.0
ROPE_DIM = 16
FFN_HIDDEN = 2816
EPS = 1e-6
MIN_F = 1e-30
ADAM_LR, ADAM_B1, ADAM_B2, ADAM_EPS, ADAM_WD, ADAM_STEP = 0.001, 0.9, 0.999, 1e-08, 0.01, 10

COL_HQ, COL_HF, COL_HI, COL_HG = 0, 1024, 2048, 3072
COL_AQ, COL_AK, COL_AV, COL_GA, COL_GB = 4096, 5120, 5376, 5632, 6656
IN_COLS = 7680

SHARD_ROWS = (960, 128, 128, 128, 352, 352, 352)
SLOT_OFF = (0, 960, 1088, 1216, 1344, 1696, 2048)
SLOT_ROWS = 2400
SMALL_ROWS = 24

VMEM_LIMIT_BYTES = 56 * 1024 * 1024

NN = ((1,), (0,))
NT = ((1,), (1,))
TN = ((0,), (0,))


def _dot(a, b, dims):
    return lax.dot_general(a, b, (dims, ((), ())), preferred_element_type=F32)


def _cp(*sem):
    return pltpu.CompilerParams(dimension_semantics=sem if sem else None, vmem_limit_bytes=VMEM_LIMIT_BYTES)


def _sigmoid(x):
    return 0.5 * jnp.tanh(0.5 * x) + 0.5


def _matmul_nt(a, w, row_off, n, out_dtype, name, tm=1024, tn=512, ex=None):
    t, k = a.shape
    tm = min(tm, t)
    assert n % tn == 0 and row_off % tn == 0 and t % tm == 0
    grid = (n // tn, t // tm)

    def body(a_ref, w_ref, o_ref):
        o_ref[...] = _dot(a_ref[...].astype(BF16), w_ref[...], NT).astype(o_ref.dtype)

    ex_in, ex_in_specs, ex_out, ex_out_specs, ex_scratch = _carried(ex)
    outs = pl.pallas_call(
        _carry(body, ex, 2, 1, 0, grid), name=name, grid=grid,
        in_specs=[pl.BlockSpec((tm, k), lambda j, i: (i, 0)),
                  pl.BlockSpec((tn, k), lambda j, i: (row_off // tn + j, 0))] + ex_in_specs,
        out_specs=[pl.BlockSpec((tm, tn), lambda j, i: (i, j))] + ex_out_specs,
        out_shape=[jax.ShapeDtypeStruct((t, n), out_dtype)] + ex_out,
        scratch_shapes=ex_scratch,
        compiler_params=_cp("arbitrary", "arbitrary") if ex else _cp("parallel", "parallel"))(a, w, *ex_in)
    return (outs[0], outs[1:]) if ex else outs[0]


def _matmul_nn(a, w, row_off, res, name, tm=512, tk=None, gain=None):
    t, k = a.shape
    n = w.shape[1]
    tm = min(tm, t)
    tk = tk or k
    nk = k // tk
    assert k % tk == 0 and row_off % tk == 0 and t % tm == 0

    def body(*refs):
        refs = list(refs)
        a_ref, w_ref = refs[:2]
        r_ref = refs[2] if res is not None else None
        g_ref = refs[2 + (res is not None)] if gain is not None else None
        acc = refs[-1]
        o_ref = refs[-3] if gain is not None else refs[-2]
        kk = pl.program_id(1)
        part = _dot(a_ref[...].astype(BF16), w_ref[...], NN)

        @pl.when(kk == 0)
        def _():
            acc[...] = part

        @pl.when(kk > 0)
        def _():
            acc[...] += part

        @pl.when(kk == nk - 1)
        def _():
            y = acc[...] if res is None else acc[...] + r_ref[...]
            o_ref[...] = y
            if gain is not None:
                refs[-2][...] = _rms(y, g_ref[...]).astype(BF16)

    row = pl.BlockSpec((tm, n), lambda i, kk: (i, 0))
    in_specs = [pl.BlockSpec((tm, tk), lambda i, kk: (i, kk)),
                pl.BlockSpec((tk, n), lambda i, kk: (row_off // tk + kk, 0))]
    args = [a, w]
    if res is not None:
        in_specs.append(row)
        args.append(res)
    if gain is not None:
        in_specs.append(pl.BlockSpec((1, n), lambda i, kk: (0, 0)))
        args.append(gain)
    outs = pl.pallas_call(
        body, name=name, grid=(t // tm, nk), in_specs=in_specs,
        out_specs=[row, row] if gain is not None else [row],
        out_shape=[jax.ShapeDtypeStruct((t, n), F32)] + ([jax.ShapeDtypeStruct((t, n), BF16)] if gain is not None else []),
        scratch_shapes=[pltpu.VMEM((tm, n), F32)],
        compiler_params=_cp("parallel", "arbitrary"))(*args)
    return tuple(outs) if gain is not None else outs[0]


def _matmul_tn(a, b, name, tm=512, tk=2048, rows=None, row_off=0, into=None):
    t, m = a.shape
    n = b.shape[1]
    tk = min(tk, t)
    nk = t // tk
    rows = rows or m
    assert m % tm == 0 and t % tk == 0 and row_off % tm == 0

    def body(*refs):
        a_ref, b_ref, o_ref, acc = refs[0], refs[1], refs[-2], refs[-1]
        kk = pl.program_id(1)
        part = _dot(a_ref[...].astype(BF16), b_ref[...].astype(BF16), TN)

        @pl.when(kk == 0)
        def _():
            acc[...] = part

        @pl.when(kk > 0)
        def _():
            acc[...] += part

        @pl.when(kk == nk - 1)
        def _():
            o_ref[...] = acc[...].astype(BF16)

    return pl.pallas_call(
        body, name=name, grid=(m // tm, nk),
        in_specs=[pl.BlockSpec((tk, tm), lambda i, kk: (kk, i)),
                  pl.BlockSpec((tk, n), lambda i, kk: (kk, 0))] + ([ANY] if into is not None else []),
        out_specs=pl.BlockSpec((tm, n), lambda i, kk: (row_off // tm + i, 0)),
        out_shape=jax.ShapeDtypeStruct((rows, n), BF16),
        scratch_shapes=[pltpu.VMEM((tm, n), F32)],
        input_output_aliases={2: 0} if into is not None else {},
        compiler_params=_cp("parallel", "arbitrary"))(a, b, *([into] if into is not None else []))


def _rms(x, g):
    return x * lax.rsqrt(jnp.mean(x * x, axis=-1, keepdims=True) + EPS) * g


def _rms_fwd(x, g, name, tm=512):
    t, d = x.shape
    tm = min(tm, t)

    def body(x_ref, g_ref, o_ref):
        o_ref[...] = _rms(x_ref[...], g_ref[...]).astype(BF16)

    return pl.pallas_call(
        body, name=name, grid=(t // tm,),
        in_specs=[pl.BlockSpec((tm, d), lambda i: (i, 0)), pl.BlockSpec((1, d), lambda i: (0, 0))],
        out_specs=pl.BlockSpec((tm, d), lambda i: (i, 0)),
        out_shape=jax.ShapeDtypeStruct((t, d), BF16),
        compiler_params=_cp("parallel"))(x, g)


def _mix(ya, yb, ga, gb):
    return _sigmoid(ga) * ya + _sigmoid(gb) * yb


def _gate_specs(tm):
    half = D_MODEL // 2
    return [pl.BlockSpec((tm, half), lambda i, c=c: (i, c))
            for c in (COL_GA // half, COL_GA // half + 1, COL_GB // half, COL_GB // half + 1)]


def _merge_bwd(dx1, ya, yb, proj, w_pa, w_pb, w_o, name, tm=512):
    t, d = dx1.shape
    tm = min(tm, t)

    def body(dx_ref, ya_ref, yb_ref, ga0, ga1, gb0, gb1, wpa_ref, wpb_ref, wo_ref,
             dya_ref, dyb_ref, dg_ref, dog_ref, doa_ref):
        dmix = _dot(dx_ref[...].astype(BF16), wo_ref[...], NT)
        ga = jnp.concatenate([ga0[...], ga1[...]], axis=1)
        gb = jnp.concatenate([gb0[...], gb1[...]], axis=1)
        _, vjp = jax.vjp(_mix, ya_ref[...].astype(F32), yb_ref[...].astype(F32), ga, gb)
        dya, dyb, dga, dgb = vjp(dmix)
        dya, dyb = dya.astype(BF16), dyb.astype(BF16)
        dya_ref[...] = dya
        dyb_ref[...] = dyb
        dg_ref[:, :d] = dga.astype(BF16)
        dg_ref[:, d:] = dgb.astype(BF16)
        dog_ref[...] = _dot(dya, wpa_ref[...], NT)
        doa_ref[...] = _dot(dyb, wpb_ref[...], NT)

    row = pl.BlockSpec((tm, d), lambda i: (i, 0))
    wide = pl.BlockSpec((tm, 2 * d), lambda i: (i, 0))
    mat = pl.BlockSpec((d, d), lambda i: (0, 0))
    return pl.pallas_call(
        body, name=name, grid=(t // tm,), in_specs=[row, row, row] + _gate_specs(tm) + [mat, mat, mat],
        out_specs=[row, row, wide, row, row],
        out_shape=[jax.ShapeDtypeStruct((t, d), BF16), jax.ShapeDtypeStruct((t, d), BF16),
                   jax.ShapeDtypeStruct((t, 2 * d), BF16), jax.ShapeDtypeStruct((t, d), F32),
                   jax.ShapeDtypeStruct((t, d), F32)],
        compiler_params=_cp("parallel"))(dx1, ya, yb, proj, proj, proj, proj, w_pa, w_pb, w_o)


def _swiglu(g, u):
    return g * _sigmoid(g) * u


def _swiglu_bwd(g, u, dact):
    sg = _sigmoid(g)
    gs = g * sg
    return dact * u * (sg + gs * (1.0 - sg)), dact * gs


def _ffn_up_fwd(h2, wgu_t, name, tm=512):
    t, d = h2.shape
    tm = min(tm, t)
    fh = FFN_HIDDEN // 2

    def body(a_ref, w_ref, gu_ref, act_ref):
        r = _dot(a_ref[...], w_ref[...], NT)
        gu_ref[...] = r.astype(BF16)
        act_ref[...] = _swiglu(r[:, :fh], r[:, fh:]).astype(BF16)

    return pl.pallas_call(
        body, name=name, grid=(2, t // tm),
        in_specs=[pl.BlockSpec((tm, d), lambda j, i: (i, 0)), pl.BlockSpec((2 * fh, d), lambda j, i: (j, 0))],
        out_specs=[pl.BlockSpec((tm, 2 * fh), lambda j, i: (i, j)), pl.BlockSpec((tm, fh), lambda j, i: (i, j))],
        out_shape=[jax.ShapeDtypeStruct((t, 4 * fh), BF16), jax.ShapeDtypeStruct((t, 2 * fh), BF16)],
        compiler_params=_cp("parallel", "parallel"))(h2, wgu_t)


def _ffn_down_bwd(dx, w_d, gu, name, tm=512):
    t, d = dx.shape
    tm = min(tm, t)
    fh = FFN_HIDDEN // 2

    def body(a_ref, w_ref, gu_ref, o_ref):
        dact = _dot(a_ref[...].astype(BF16), w_ref[...], NT)
        dg, du = _swiglu_bwd(gu_ref[:, :fh].astype(F32), gu_ref[:, fh:].astype(F32), dact)
        o_ref[:, :fh] = dg.astype(BF16)
        o_ref[:, fh:] = du.astype(BF16)

    wide = pl.BlockSpec((tm, 2 * fh), lambda j, i: (i, j))
    return pl.pallas_call(
        body, name=name, grid=(2, t // tm),
        in_specs=[pl.BlockSpec((tm, d), lambda j, i: (i, 0)), pl.BlockSpec((fh, d), lambda j, i: (j, 0)), wide],
        out_specs=wide,
        out_shape=jax.ShapeDtypeStruct((t, 4 * fh), BF16),
        compiler_params=_cp("parallel", "parallel"))(dx, w_d, gu)


def _rows_bwd(pieces, w, x, g, dres, name, tm=256, ex=None):
    t, d = x.shape
    tm = min(tm, t)
    widths = [p.shape[1] for p in pieces]
    starts = [sum(widths[:i]) for i in range(len(widths))]
    assert sum(widths) == w.shape[0]
    n_p = len(pieces)

    def body(*refs):
        p_refs, (w_ref, x_ref, g_ref, dres_ref, dx_ref, dg_ref) = refs[:n_p], refs[n_p:]
        dh = _dot(p_refs[0][...], w_ref[pl.ds(starts[0], widths[0]), :], NN)
        for i in range(1, n_p):
            dh = dh + _dot(p_refs[i][...], w_ref[pl.ds(starts[i], widths[i]), :], NN)
        _, vjp = jax.vjp(_rms, x_ref[...], g_ref[...])
        dx, dg = vjp(dh)
        dx_ref[...] = dres_ref[...] + dx

        @pl.when(pl.program_id(0) == 0)
        def _():
            dg_ref[...] = jnp.zeros_like(dg_ref)

        dg_ref[...] += dg

    row = pl.BlockSpec((tm, d), lambda i: (i, 0))
    vec = pl.BlockSpec((1, d), lambda i: (0, 0))
    ex_in, ex_in_specs, ex_out, ex_out_specs, ex_scratch = _carried(ex)
    outs = pl.pallas_call(
        _carry(body, ex, n_p + 4, 2, 0, (t // tm,)), name=name, grid=(t // tm,),
        in_specs=[pl.BlockSpec((tm, k), lambda i: (i, 0)) for k in widths]
        + [pl.BlockSpec(w.shape, lambda i: (0, 0)), row, vec, row] + ex_in_specs,
        out_specs=[row, vec] + ex_out_specs,
        out_shape=[jax.ShapeDtypeStruct((t, d), F32), jax.ShapeDtypeStruct((1, d), F32)] + ex_out,
        scratch_shapes=ex_scratch,
        compiler_params=_cp("arbitrary"))(*pieces, w, x, g, dres, *ex_in)
    return (outs[0], outs[1], outs[2:]) if ex else (outs[0], outs[1])


def _merge_fwd(o_g, o_att, proj, x, w_pa, w_pb, w_o, gain, name, tm=512):
    t, d = x.shape
    tm = min(tm, t)

    def body(og_ref, oa_ref, ga0, ga1, gb0, gb1, x_ref, wpa_ref, wpb_ref, wo_ref, g_ref,
             ya_ref, yb_ref, mix_ref, h2_ref, x1_ref):
        ya = _dot(og_ref[...], wpa_ref[...], NN)
        yb = _dot(oa_ref[...], wpb_ref[...], NN)
        ga = jnp.concatenate([ga0[...], ga1[...]], axis=1)
        gb = jnp.concatenate([gb0[...], gb1[...]], axis=1)
        mix = _mix(ya, yb, ga, gb).astype(BF16)
        ya_ref[...] = ya.astype(BF16)
        yb_ref[...] = yb.astype(BF16)
        mix_ref[...] = mix
        x1 = x_ref[...] + _dot(mix, wo_ref[...], NN)
        x1_ref[...] = x1
        h2_ref[...] = _rms(x1, g_ref[...]).astype(BF16)

    row = pl.BlockSpec((tm, d), lambda i: (i, 0))
    mat = pl.BlockSpec((d, d), lambda i: (0, 0))
    return pl.pallas_call(
        body, name=name, grid=(t // tm,),
        in_specs=[row, row] + _gate_specs(tm) + [row, mat, mat, mat, pl.BlockSpec((1, d), lambda i: (0, 0))],
        out_specs=[row] * 5,
        out_shape=[jax.ShapeDtypeStruct((t, d), BF16)] * 4 + [jax.ShapeDtypeStruct((t, d), F32)],
        compiler_params=_cp("parallel"))(o_g, o_att, proj, proj, proj, proj, x, w_pa, w_pb, w_o, gain)


def _loss_head(x, g, target, name, tm=512):
    t, d = x.shape
    tm = min(tm, t)

    def body(x_ref, g_ref, t_ref, dx_ref, dg_ref, loss_ref):
        tgt = t_ref[...]

        def f(xv, gv):
            err = _rms(xv, gv) - tgt
            return 0.5 * jnp.sum(jnp.mean(err * err, axis=-1, keepdims=True))

        loss, vjp = jax.vjp(f, x_ref[...], g_ref[...])
        dx, dg = vjp(jnp.ones((), F32))
        dx_ref[...] = dx

        @pl.when(pl.program_id(0) == 0)
        def _():
            dg_ref[...] = jnp.zeros_like(dg_ref)
            loss_ref[...] = jnp.zeros_like(loss_ref)

        dg_ref[...] += dg
        loss_ref[...] += jnp.full(loss_ref.shape, loss, F32)

    row = pl.BlockSpec((tm, d), lambda i: (i, 0))
    vec = pl.BlockSpec((1, d), lambda i: (0, 0))
    lane = pl.BlockSpec((1, 128), lambda i: (0, 0))
    return pl.pallas_call(
        body, name=name, grid=(t // tm,), in_specs=[row, vec, row], out_specs=[row, vec, lane],
        out_shape=[jax.ShapeDtypeStruct((t, d), F32), jax.ShapeDtypeStruct((1, d), F32),
                   jax.ShapeDtypeStruct((1, 128), F32)],
        compiler_params=_cp("arbitrary"))(x, g, target)


def _lb_rows(l0, l1, l2, l3):
    mx = jnp.maximum(jnp.maximum(l0, l1), jnp.maximum(l2, l3))
    e0, e1, e2, e3 = jnp.exp(l0 - mx), jnp.exp(l1 - mx), jnp.exp(l2 - mx), jnp.exp(l3 - mx)
    s = e0 + e1 + e2 + e3
    p0, p1, p2, p3 = e0 / s, e1 / s, e2 / s, e3 / s
    c1 = p0 + p1
    c2 = c1 + p2
    c3 = c2 + p3
    return p0 - p0, c1 - p0, c2 - p0, c3 - p0


def _lb_fwd(lb_logits):
    def body(l_ref, o_ref):
        rows = _lb_rows(*[l_ref[pl.ds(i, 1), :] for i in range(DEPTH)])
        for i in range(DEPTH):
            o_ref[pl.ds(i, 1), :] = rows[i]

    return pl.pallas_call(body, name="lb_fwd", out_shape=jax.ShapeDtypeStruct(lb_logits.shape, F32))(lb_logits)


def _lb_bwd(lb_logits, dlb):
    def body(l_ref, d_ref, o_ref):
        _, vjp = jax.vjp(_lb_rows, *[l_ref[pl.ds(i, 1), :] for i in range(DEPTH)])
        grads = vjp(tuple(d_ref[pl.ds(i, 1), :] for i in range(DEPTH)))
        for i in range(DEPTH):
            o_ref[pl.ds(i, 1), :] = grads[i]

    return pl.pallas_call(body, name="lb_bwd", out_shape=jax.ShapeDtypeStruct(lb_logits.shape, F32))(lb_logits, dlb)


MESH = pl.DeviceIdType.MESH
ANY = pl.BlockSpec(memory_space=pl.ANY)
N_KINDS = len(SHARD_ROWS)
FFN_HALF = FFN_HIDDEN // 2
KIND_PLACE = ((0, 0), (1, 0), (2, 0), (3, 0), (4, 0), (4, FFN_HALF), (5, 0))
KIND_HALF_SKIP = (0, 0, 0, 0, FFN_HALF, FFN_HALF, 0)
FULL_ROWS = (N_DEV * SHARD_ROWS[0], D_MODEL, D_MODEL, D_MODEL, 2 * N_DEV * SHARD_ROWS[4], N_DEV * SHARD_ROWS[6])


def _kind_rows(ti, dev):
    oi, base = KIND_PLACE[ti]
    start = base + dev * SHARD_ROWS[ti]
    if KIND_HALF_SKIP[ti]:
        start = start + (dev // (N_DEV // 2)) * KIND_HALF_SKIP[ti]
    return oi, pl.ds(start, SHARD_ROWS[ti])


def _position():
    x, y, c = lax.axis_index("x"), lax.axis_index("y"), lax.axis_index("c")
    return x, y, c, 4 * x + 2 * y + c


def _peer(x, y, c, r):
    px = 1 - x if r & 4 else x
    py = 1 - y if r & 2 else y
    pc = 1 - c if r & 1 else c
    return (px, py, pc), 4 * px + 2 * py + pc


class _Exchange(NamedTuple):
    operands: tuple
    out_shape: tuple
    copies: Callable
    n_local: int


EXCHANGE_SCRATCH = (pltpu.SemaphoreType.DMA((N_DEV, N_KINDS)), pltpu.SemaphoreType.DMA((N_DEV, N_KINDS)),
                    pltpu.SemaphoreType.DMA((N_KINDS,)))
ALL_KINDS = tuple(range(N_KINDS))
KINDS_W_IN = (0,)
KINDS_REST = ALL_KINDS[1:]


def _all_pairs(kinds, ends, send_sems, recv_sems):
    x, y, c, me = _position()
    out = []
    for r in range(1, N_DEV):
        peer, pid = _peer(x, y, c, r)
        for ti in kinds:
            src, dst = ends(ti, me, pid)
            out.append(pltpu.make_async_remote_copy(
                src_ref=src, dst_ref=dst, send_sem=send_sems.at[r, ti], recv_sem=recv_sems.at[r, ti],
                device_id=peer, device_id_type=MESH))
    return out


def _gather_exchange(shards, kinds):
    arrays = sorted({KIND_PLACE[ti][0] for ti in kinds})

    def copies(ins, outs, send_sems, recv_sems, local_sems, arrivals):
        src = dict(zip(kinds, ins))

        def window(ti, dev):
            oi, rows = _kind_rows(ti, dev)
            return outs[arrays.index(oi)].at[rows, :]

        if arrivals:
            return _all_pairs(kinds, lambda ti, me, pid: (src[ti], window(ti, pid)), send_sems, recv_sems)
        _, _, _, me = _position()
        local = [pltpu.make_async_copy(src[ti], window(ti, me), local_sems.at[ti]) for ti in kinds]
        return local + _all_pairs(kinds, lambda ti, me, pid: (src[ti], window(ti, me)), send_sems, recv_sems)

    return _Exchange(tuple(shards[ti] for ti in kinds),
                     tuple(jax.ShapeDtypeStruct((FULL_ROWS[oi], D_MODEL), BF16) for oi in arrays), copies, len(kinds))


def _scatter_exchange(grads, kinds):
    arrays = sorted({KIND_PLACE[ti][0] for ti in kinds})
    offsets, total = {}, 0
    for ti in kinds:
        offsets[ti], total = total, total + SHARD_ROWS[ti]

    def copies(ins, outs, send_sems, recv_sems, local_sems, arrivals):
        land = outs[0]

        def piece(ti, dev):
            ii, rows = _kind_rows(ti, dev)
            return ins[arrays.index(ii)].at[rows, :]

        def slot(ti, dev):
            return land.at[dev, pl.ds(offsets[ti], SHARD_ROWS[ti]), :]

        if arrivals:
            return _all_pairs(kinds, lambda ti, me, pid: (piece(ti, me), slot(ti, pid)), send_sems, recv_sems)
        _, _, _, me = _position()
        local = [pltpu.make_async_copy(piece(ti, me), slot(ti, me), local_sems.at[ti]) for ti in kinds]
        return local + _all_pairs(kinds, lambda ti, me, pid: (piece(ti, pid), slot(ti, me)), send_sems, recv_sems)

    return _Exchange(tuple(grads[oi] for oi in arrays), (jax.ShapeDtypeStruct((N_DEV, total, D_MODEL), BF16),),
                     copies, len(kinds))


def _exchange_start(ex, ins, outs, sems):
    for cp in ex.copies(ins, outs, *sems, False):
        cp.start()


def _exchange_finish(ex, ins, outs, sems):
    for cp in ex.copies(ins, outs, *sems, True):
        cp.wait_recv()
    mine = ex.copies(ins, outs, *sems, False)
    for cp in mine[:ex.n_local]:
        cp.wait()
    for cp in mine[ex.n_local:]:
        cp.wait_send()


def _run_exchange(ex, name):
    n_in, n_out = len(ex.operands), len(ex.out_shape)

    def body(*refs):
        ins, outs, sems = refs[:n_in], refs[n_in:n_in + n_out], refs[n_in + n_out:]
        _exchange_start(ex, ins, outs, sems)
        _exchange_finish(ex, ins, outs, sems)

    return pl.pallas_call(body, name=name, in_specs=[ANY] * n_in, out_specs=[ANY] * n_out,
                          out_shape=list(ex.out_shape), scratch_shapes=list(EXCHANGE_SCRATCH))(*ex.operands)


def _carry(body, ex, n_in, n_out, n_scratch, grid):
    if ex is None:
        return body
    e_in, e_out = len(ex.operands), len(ex.out_shape)

    def at(step):
        hit = pl.program_id(0) == step[0]
        for axis in range(1, len(grid)):
            hit = hit & (pl.program_id(axis) == step[axis])
        return hit

    def carrying(*refs):
        own_in, ex_in = refs[:n_in], refs[n_in:n_in + e_in]
        rest = refs[n_in + e_in:]
        own_out, ex_out = rest[:n_out], rest[n_out:n_out + e_out]
        own_scratch, sems = rest[n_out + e_out:n_out + e_out + n_scratch], rest[n_out + e_out + n_scratch:]

        @pl.when(at([0] * len(grid)))
        def _():
            _exchange_start(ex, ex_in, ex_out, sems)

        body(*own_in, *own_out, *own_scratch)

        @pl.when(at([g - 1 for g in grid]))
        def _():
            _exchange_finish(ex, ex_in, ex_out, sems)

    return carrying


def _carried(ex):
    if ex is None:
        return (), [], [], [], []
    return (ex.operands, [ANY] * len(ex.operands), list(ex.out_shape), [ANY] * len(ex.out_shape),
            list(EXCHANGE_SCRATCH))


def _small_sum_body(p_ref, o_ref, buf, send_sems, recv_sems):
    x, y, c, me = _position()
    buf[me] = p_ref[...]
    sends = []
    for r in range(1, N_DEV):
        peer, _ = _peer(x, y, c, r)
        sends.append(pltpu.make_async_remote_copy(
            src_ref=p_ref, dst_ref=buf.at[me], send_sem=send_sems.at[r], recv_sem=recv_sems.at[r],
            device_id=peer, device_id_type=MESH))
    for cp in sends:
        cp.start()
    for r in range(1, N_DEV):
        peer, pid = _peer(x, y, c, r)
        pltpu.make_async_remote_copy(
            src_ref=p_ref, dst_ref=buf.at[pid], send_sem=send_sems.at[r], recv_sem=recv_sems.at[r],
            device_id=peer, device_id_type=MESH).wait_recv()
    for cp in sends:
        cp.wait_send()
    acc = buf[0]
    for k in range(1, N_DEV):
        acc = acc + buf[k]
    o_ref[...] = acc


def _all_reduce_small(part):
    rows, d = part.shape
    vmem = pl.BlockSpec(memory_space=pltpu.VMEM)
    return pl.pallas_call(
        functools.partial(_small_sum_body), name="all_reduce_small", in_specs=[vmem], out_specs=vmem,
        out_shape=jax.ShapeDtypeStruct((rows, d), F32),
        scratch_shapes=[pltpu.VMEM((N_DEV, rows, d), F32), pltpu.SemaphoreType.DMA((N_DEV,)),
                        pltpu.SemaphoreType.DMA((N_DEV,))],
    )(part)


HG_PAIR = 2 * HG_DK


def _hg_consts():
    c = HG_CHUNK
    r = lax.broadcasted_iota(jnp.int32, (c, c), 0)
    s = lax.broadcasted_iota(jnp.int32, (c, c), 1)
    r2 = lax.broadcasted_iota(jnp.int32, (c, 2 * c), 0)
    s2 = lax.broadcasted_iota(jnp.int32, (c, 2 * c), 1)
    causal2 = jnp.where(s2 >= c, s2 - c, s2) <= r2
    lane_hi = lax.broadcasted_iota(jnp.int32, (c, HG_PAIR), 1) >= HG_DK
    same_head = ((lax.broadcasted_iota(jnp.int32, (HG_PAIR, HG_PAIR), 0) >= HG_DK)
                 == (lax.broadcasted_iota(jnp.int32, (HG_PAIR, HG_PAIR), 1) >= HG_DK))
    return (s <= r).astype(BF16), (s >= r).astype(BF16), causal2, lane_hi, same_head


def _head_rows(x, lane_hi):
    zero = jnp.zeros_like(x)
    return jnp.concatenate([jnp.where(lane_hi, zero, x), jnp.where(lane_hi, x, zero)], axis=0)


def _own_rows(y, lane_hi):
    return jnp.where(lane_hi, y[HG_CHUNK:], y[:HG_CHUNK])


def _split3(x):
    hi = x.astype(BF16)
    r1 = x - hi.astype(F32)
    mid = r1.astype(BF16)
    lo = (r1 - mid.astype(F32)).astype(BF16)
    return jnp.concatenate([hi, mid, lo], axis=1)


def _cumsum_rows(tri, x):
    w = x.shape[1]
    y = _dot(tri, _split3(x), NN)
    return y[:, :w] + y[:, w:2 * w] + y[:, 2 * w:]


def _hg_gates(zq, zf, lb):
    sq = _sigmoid(zq)
    sg = _sigmoid(zf)
    f = lb + (1.0 - lb) * sg
    return sq, zq * sq, sg, f, jnp.log(jnp.maximum(f, MIN_F)), 1.0 - f


def _hg_decays(ball_ref, ci, cols):
    c = HG_CHUNK
    b = ball_ref[pl.ds(ci * c, c), cols]
    mid = ball_ref[pl.ds(ci * c + c // 2 - 1, 1), cols]
    bc = ball_ref[pl.ds(ci * c + c - 1, 1), cols]
    return b, jnp.exp(jnp.minimum(b - mid, HG_EXP_CLAMP)), jnp.exp(jnp.minimum(mid - b, HG_EXP_CLAMP)), bc


def _hg_gate(o, zg, gn):
    return o * lax.rsqrt(jnp.mean(o * o, axis=-1, keepdims=True) + EPS) * gn * (zg * _sigmoid(zg))


def _hgrn2_fwd(proj, lb, gn, name, ex=None):
    t = proj.shape[0]
    bs_tok = min(HG_BLOCK, t)
    n_chunks = bs_tok // HG_CHUNK
    w = HG_HEADS * HG_DK

    def body(hq_ref, hf_ref, hi_ref, hg_ref, lb_ref, gn_ref, o_ref, og_ref, sall_ref, ball_ref, aall_ref, st_ref):
        @pl.when(pl.program_id(0) == 0)
        def _():
            st_ref[...] = jnp.zeros_like(st_ref)

        tril, _, causal2, lane_hi, same_head = _hg_consts()

        for ci in range(n_chunks):
            rows = pl.ds(ci * HG_CHUNK, HG_CHUNK)
            for p in range(HG_HEADS // 2):
                cols = slice(p * HG_PAIR, (p + 1) * HG_PAIR)
                v = hi_ref[rows, cols]
                zg = hg_ref[rows, cols]
                _, q, _, _, logf, k = _hg_gates(hq_ref[rows, cols], hf_ref[rows, cols], lb_ref[:, cols])
                ball_ref[rows, cols] = _cumsum_rows(tril, logf)
                b, em, en, bc = _hg_decays(ball_ref, ci, cols)
                st0 = st_ref[p]
                sall_ref[ci, 2 * p] = st0[:HG_DK, :HG_DK]
                sall_ref[ci, 2 * p + 1] = st0[HG_DK:, HG_DK:]
                vb = v.astype(BF16)
                o = _dot((q * jnp.exp(b)).astype(BF16), st0.astype(BF16), NT)
                a = jnp.where(causal2, _dot((q * em).astype(BF16), _head_rows((k * en).astype(BF16), lane_hi), NT),
                              0.0).astype(BF16)
                aall_ref[ci, p] = a
                o = o + _dot(a, _head_rows(vb, lane_hi), NN)
                kdec = (k * jnp.exp(bc - b)).astype(BF16)
                st_ref[p] = st0 * jnp.exp(bc) + jnp.where(same_head, _dot(vb, kdec, TN), 0.0)
                o_ref[rows, cols] = o
                for hh in range(2):
                    sl = slice(hh * HG_DK, (hh + 1) * HG_DK)
                    hcols = slice(p * HG_PAIR + hh * HG_DK, p * HG_PAIR + (hh + 1) * HG_DK)
                    og_ref[rows, hcols] = _hg_gate(o[:, sl], zg[:, sl], gn_ref[:, hcols]).astype(BF16)

    def col(j):
        return pl.BlockSpec((bs_tok, w), lambda n, j=j: (n, j))

    vec = pl.BlockSpec((1, w), lambda n: (0, 0))
    ex_in, ex_in_specs, ex_out, ex_out_specs, ex_scratch = _carried(ex)
    outs = pl.pallas_call(
        _carry(body, ex, 6, 5, 1, (t // bs_tok,)), name=name, grid=(t // bs_tok,),
        in_specs=[col(COL_HQ // w), col(COL_HF // w), col(COL_HI // w), col(COL_HG // w), vec, vec] + ex_in_specs,
        out_specs=[col(0), col(0),
                   pl.BlockSpec((n_chunks, HG_HEADS, HG_DK, HG_DK), lambda n: (n, 0, 0, 0)), col(0),
                   pl.BlockSpec((n_chunks, HG_HEADS // 2, HG_CHUNK, 2 * HG_CHUNK), lambda n: (n, 0, 0, 0))] + ex_out_specs,
        out_shape=[jax.ShapeDtypeStruct((t, w), F32), jax.ShapeDtypeStruct((t, w), BF16),
                   jax.ShapeDtypeStruct((t // HG_CHUNK, HG_HEADS, HG_DK, HG_DK), F32), jax.ShapeDtypeStruct((t, w), F32),
                   jax.ShapeDtypeStruct((t // HG_CHUNK, HG_HEADS // 2, HG_CHUNK, 2 * HG_CHUNK), BF16)] + ex_out,
        scratch_shapes=[pltpu.VMEM((HG_HEADS // 2, HG_PAIR, HG_PAIR), F32)] + ex_scratch,
        compiler_params=_cp("arbitrary"))(proj, proj, proj, proj, lb, gn, *ex_in)
    return outs[:5], outs[5:]


def _hgrn2_bwd(proj, lb, gn, o_hg, sall, ball, aall, dog, name, ex=None):
    t = proj.shape[0]
    bs_tok = min(HG_BLOCK, t)
    n_chunks = bs_tok // HG_CHUNK
    n_blocks = t // bs_tok
    w = HG_HEADS * HG_DK

    def body(hq_ref, hf_ref, hi_ref, hg_ref, lb_ref, gn_ref, o_ref, sall_ref, snext_ref, ball_ref, aall_ref, dog_ref,
             da_ref, dlb_ref, dgn_ref, dst_ref):
        @pl.when(pl.program_id(0) == 0)
        def _():
            dst_ref[...] = jnp.zeros_like(dst_ref)
            dlb_ref[...] = jnp.zeros_like(dlb_ref)
            dgn_ref[...] = jnp.zeros_like(dgn_ref)

        _, rev_tril, causal2, lane_hi, same_head = _hg_consts()
        zero_block = jnp.zeros((HG_DK, HG_DK), F32)

        for ci in reversed(range(n_chunks)):
            rows = pl.ds(ci * HG_CHUNK, HG_CHUNK)
            for p in range(HG_HEADS // 2):
                cols = slice(p * HG_PAIR, (p + 1) * HG_PAIR)
                zq = hq_ref[rows, cols]
                v = hi_ref[rows, cols]
                zg = hg_ref[rows, cols]
                lbv = lb_ref[:, cols]
                sq, q, sg, f, _, k = _hg_gates(zq, hf_ref[rows, cols], lbv)
                b, em, en, bc = _hg_decays(ball_ref, ci, cols)
                st0 = jnp.concatenate([jnp.concatenate([sall_ref[ci, 2 * p], zero_block], axis=1),
                                       jnp.concatenate([zero_block, sall_ref[ci, 2 * p + 1]], axis=1)], axis=0)
                dst1 = dst_ref[p]
                vb = v.astype(BF16)
                eb = jnp.exp(b)
                qg = (q * eb).astype(BF16)
                qt = (q * em).astype(BF16)
                kref = (k * en).astype(BF16)
                ebcb = jnp.exp(bc - b)
                kdec = (k * ebcb).astype(BF16)
                ebc = jnp.exp(bc)
                dos, dzgs, dgns = [], [], []
                for hh in range(2):
                    sl = slice(hh * HG_DK, (hh + 1) * HG_DK)
                    hcols = slice(p * HG_PAIR + hh * HG_DK, p * HG_PAIR + (hh + 1) * HG_DK)
                    _, gate_vjp = jax.vjp(_hg_gate, o_ref[rows, hcols], zg[:, sl], gn_ref[:, hcols])
                    do_h, dzg_h, dgn_h = gate_vjp(dog_ref[rows, hcols])
                    dos.append(do_h)
                    dzgs.append(dzg_h)
                    dgns.append(dgn_h)
                dob = jnp.concatenate(dos, axis=1).astype(BF16)
                vrows, krows = _head_rows(vb, lane_hi), _head_rows(kref, lane_hi)
                dam = jnp.where(causal2, _dot(dob, vrows, NT), 0.0).astype(BF16)
                dk = ebcb * _dot(vb, dst1.astype(BF16), NN) + en * _own_rows(_dot(dam, qt, TN), lane_hi)
                dq = eb * _dot(dob, st0.astype(BF16), NN) + em * _dot(dam, krows, NN)
                dv = _own_rows(_dot(aall_ref[ci, p], dob, TN), lane_hi) + _dot(kdec, dst1.astype(BF16), NT)
                dst_ref[p] = dst1 * ebc + jnp.where(same_head, _dot(dob, qg, TN), 0.0)

                after = [sall_ref[ci + 1, 2 * p + hh] if ci + 1 < n_chunks else snext_ref[0, 2 * p + hh] for hh in range(2)]
                dbx = jnp.concatenate(
                    [jnp.sum(dst1[hh * HG_DK:(hh + 1) * HG_DK, hh * HG_DK:(hh + 1) * HG_DK] * after[hh], axis=0, keepdims=True)
                     for hh in range(2)], axis=1)
                dlogf = _cumsum_rows(rev_tril, q * dq - k * dk) + dbx
                df = jnp.where(f > MIN_F, dlogf / f, 0.0) - dk
                dzf = df * (1.0 - lbv) * sg * (1.0 - sg)
                dzq = dq * (sq * (1.0 + zq * (1.0 - sq)))
                da_ref[rows, pl.ds(COL_HQ + p * HG_PAIR, HG_PAIR)] = dzq.astype(BF16)
                da_ref[rows, pl.ds(COL_HF + p * HG_PAIR, HG_PAIR)] = dzf.astype(BF16)
                da_ref[rows, pl.ds(COL_HI + p * HG_PAIR, HG_PAIR)] = dv.astype(BF16)
                da_ref[rows, pl.ds(COL_HG + p * HG_PAIR, HG_PAIR)] = jnp.concatenate(dzgs, axis=1).astype(BF16)
                dlb_ref[:, cols] += jnp.sum(df * (1.0 - sg), axis=0, keepdims=True)
                dgn_ref[:, cols] += jnp.concatenate(dgns, axis=1)

    def col(j):
        return pl.BlockSpec((bs_tok, w), lambda n, j=j: (n_blocks - 1 - n, j))

    vec = pl.BlockSpec((1, w), lambda n: (0, 0))
    ex_in, ex_in_specs, ex_out, ex_out_specs, ex_scratch = _carried(ex)
    outs = pl.pallas_call(
        _carry(body, ex, 12, 3, 1, (n_blocks,)), name=name, grid=(n_blocks,),
        in_specs=[col(COL_HQ // w), col(COL_HF // w), col(COL_HI // w), col(COL_HG // w), vec, vec, col(0),
                  pl.BlockSpec((n_chunks, HG_HEADS, HG_DK, HG_DK), lambda n: (n_blocks - 1 - n, 0, 0, 0)),
                  pl.BlockSpec((1, HG_HEADS, HG_DK, HG_DK),
                               lambda n: (jnp.minimum((n_blocks - n) * n_chunks, t // HG_CHUNK - 1), 0, 0, 0)),
                  col(0),
                  pl.BlockSpec((n_chunks, HG_HEADS // 2, HG_CHUNK, 2 * HG_CHUNK), lambda n: (n_blocks - 1 - n, 0, 0, 0)),
                  col(0)] + ex_in_specs,
        out_specs=[pl.BlockSpec((bs_tok, 4 * w), lambda n: (n_blocks - 1 - n, 0)), vec, vec] + ex_out_specs,
        out_shape=[jax.ShapeDtypeStruct((t, 4 * w), BF16), jax.ShapeDtypeStruct((1, w), F32),
                   jax.ShapeDtypeStruct((1, w), F32)] + ex_out,
        scratch_shapes=[pltpu.VMEM((HG_HEADS // 2, HG_PAIR, HG_PAIR), F32)] + ex_scratch,
        compiler_params=_cp("arbitrary"))(proj, proj, proj, proj, lb, gn, o_hg, sall, sall, ball, aall, dog, *ex_in)
    return outs[:3], outs[3:]


def _rope_tables(t):
    half = ROPE_DIM // 2
    inv = ROPE_THETA ** (-jnp.arange(half, dtype=F32) * 2.0 / ROPE_DIM)
    d = jnp.arange(2 * ATT_HEAD_DIM) % ATT_HEAD_DIM
    ang = jnp.arange(t).astype(F32)[:, None] * inv[d % half][None, :]
    cos, sin = jnp.cos(ang), jnp.sin(ang)
    c = jnp.where(d < ROPE_DIM, cos, 1.0)
    su = jnp.where(d < half, -sin, 0.0)
    sd = jnp.where((d >= half) & (d < ROPE_DIM), sin, 0.0)
    return c, su, sd


def _rope(x, tabs):
    c, su, sd = tabs
    n = x.shape[1]
    half = ROPE_DIM // 2
    return x * c + pltpu.roll(x, n - half, 1) * su + pltpu.roll(x, half, 1) * sd


def _rope_t(dy, tabs):
    c, su, sd = tabs
    n = dy.shape[1]
    half = ROPE_DIM // 2
    return dy * c + pltpu.roll(dy * su, half, 1) + pltpu.roll(dy * sd, n - half, 1)


def _swa_specs(n_blocks, clamp):
    blk = ATT_BLOCK

    def cur(n):
        return jnp.minimum(n, n_blocks - 1) if clamp else n

    def prev(n):
        return jnp.maximum(cur(n) - 1, 0)

    q_spec = pl.BlockSpec((blk, 512), lambda m, n: (cur(n), COL_AQ // 512 + m))
    kv = [pl.BlockSpec((blk, 128), lambda m, n, c=c, f=f: (f(n), c + m))
          for c in (COL_AK // 128, COL_AV // 128) for f in (cur, prev)]
    tabs = [pl.BlockSpec((blk, 128), lambda m, n, f=f: (f(n), 0)) for f in (cur, prev) for _ in range(3)]
    return q_spec, kv, tabs, cur, prev


ATT_SCALE = ATT_HEAD_DIM ** -0.5


def _head_halves(x, upper):
    zero = jnp.zeros_like(x)
    return jnp.concatenate([jnp.where(upper, zero, x), jnp.where(upper, x, zero)], axis=0)


def _swa_scores(scores_t, sink, mask_t):
    s = jnp.where(mask_t, scores_t, -jnp.inf)
    mx = jnp.maximum(jnp.max(s, axis=0, keepdims=True), sink)
    p = jnp.exp(s - mx)
    es = jnp.exp(sink - mx)
    rinv = 1.0 / (jnp.sum(p, axis=0, keepdims=True) + es)
    return p * rinv, es * rinv


def _swa_window(kc_ref, kp_ref, vc_ref, vp_ref, tabs_c, tabs_p, n):
    k2 = jnp.concatenate([_rope(kp_ref[...], tabs_p), _rope(kc_ref[...], tabs_c)], axis=0)
    v2 = jnp.concatenate([vp_ref[...], vc_ref[...]], axis=0)
    blk = ATT_BLOCK
    kj = lax.broadcasted_iota(jnp.int32, (2 * blk, blk), 0)
    qi = lax.broadcasted_iota(jnp.int32, (2 * blk, blk), 1)
    delta = qi + blk - kj
    mask_t = (delta >= 0) & (delta < blk) & ((kj >= blk) | (n > 0))
    return k2, v2, mask_t


def _swa_fwd(proj, sinks, tabs, name, ex=None):
    t = proj.shape[0]
    n_blocks = t // ATT_BLOCK
    q_spec, kv_specs, tab_specs, _, _ = _swa_specs(n_blocks, clamp=False)

    def body(q_ref, kc_ref, kp_ref, vc_ref, vp_ref, c0, c1, c2, p0, p1, p2, sink_ref, o_ref):
        m, n = pl.program_id(0), pl.program_id(1)
        tabs_c = (c0[...], c1[...], c2[...])
        tabs_p = (p0[...], p1[...], p2[...])
        k2, v2, mask_t = _swa_window(kc_ref, kp_ref, vc_ref, vp_ref, tabs_c, tabs_p, n)
        k2r, v2r = pltpu.roll(k2, 64, 1), pltpu.roll(v2, 64, 1)
        upper_k = lax.broadcasted_iota(jnp.int32, k2.shape, 1) >= 64
        upper_q = lax.broadcasted_iota(jnp.int32, (ATT_BLOCK, 128), 1) >= 64
        for jj in range(2):
            own = upper_k if jj else ~upper_k
            kd = jnp.where(own, k2, k2r).astype(BF16)
            vd = jnp.where(own, v2, v2r).astype(BF16)
            for pi in range(2):
                cols = slice(256 * jj + 128 * pi, 256 * jj + 128 * pi + 128)
                qp = _rope(q_ref[:, cols], tabs_c) * ATT_SCALE
                outs = []
                for e in range(2):
                    sink = sink_ref[0, 8 * m + 4 * jj + 2 * pi + e]
                    qm = jnp.where(upper_q if e else ~upper_q, qp, 0.0).astype(BF16)
                    pn, _ = _swa_scores(_dot(kd, qm, NT), sink, mask_t)
                    outs.append(_dot(pn.astype(BF16), vd, TN))
                o_ref[:, cols] = jnp.where(upper_q, outs[1], outs[0]).astype(BF16)

    ex_in, ex_in_specs, ex_out, ex_out_specs, ex_scratch = _carried(ex)
    outs = pl.pallas_call(
        _carry(body, ex, 12, 1, 0, (2, n_blocks)), name=name, grid=(2, n_blocks),
        in_specs=[q_spec] + kv_specs + tab_specs + [pl.BlockSpec(memory_space=pltpu.SMEM)] + ex_in_specs,
        out_specs=[pl.BlockSpec((ATT_BLOCK, 512), lambda m, n: (n, m))] + ex_out_specs,
        out_shape=[jax.ShapeDtypeStruct((t, ATT_Q_HEADS * ATT_HEAD_DIM), BF16)] + ex_out,
        scratch_shapes=ex_scratch,
        compiler_params=_cp("arbitrary", "arbitrary"))(proj, proj, proj, proj, proj, *tabs, *tabs, sinks, *ex_in)
    return outs[0], outs[1:]


def _swa_bwd(proj, sinks, tabs, o_att, do_att, name, ex=None):
    t = proj.shape[0]
    n_blocks = t // ATT_BLOCK
    blk = ATT_BLOCK
    q_spec, kv_specs, tab_specs, cur, prev = _swa_specs(n_blocks, clamp=True)

    def body(q_ref, kc_ref, kp_ref, vc_ref, vp_ref, c0, c1, c2, p0, p1, p2, sink_ref, o_ref, do_ref,
             dq_ref, dk_ref, dv_ref, ds_ref, ck_ref, cv_ref):
        m, n = pl.program_id(0), pl.program_id(1)

        @pl.when(n == 0)
        def _():
            ds_ref[...] = jnp.zeros_like(ds_ref)
            ck_ref[...] = jnp.zeros_like(ck_ref)
            cv_ref[...] = jnp.zeros_like(cv_ref)

        @pl.when(n < n_blocks)
        def _():
            tabs_c = (c0[...], c1[...], c2[...])
            tabs_p = (p0[...], p1[...], p2[...])
            k2, v2, mask_t = _swa_window(kc_ref, kp_ref, vc_ref, vp_ref, tabs_c, tabs_p, n)
            k2r, v2r = pltpu.roll(k2, 64, 1), pltpu.roll(v2, 64, 1)
            upper_k = lax.broadcasted_iota(jnp.int32, k2.shape, 1) >= 64
            lane = lax.broadcasted_iota(jnp.int32, (8, 128), 1)
            head_of_row = lax.broadcasted_iota(jnp.int32, (16, 128), 0) >= 8
            head_rows = (head_of_row == (lax.broadcasted_iota(jnp.int32, (16, 128), 1) >= 64)).astype(F32)
            dk2 = jnp.zeros(k2.shape, F32)
            dv2 = jnp.zeros(k2.shape, F32)
            dsv = jnp.zeros((8, 128), F32)
            nk = 2 * blk
            for jj in range(2):
                own = upper_k if jj else ~upper_k
                kh = _head_halves(jnp.where(own, k2, k2r).astype(BF16), upper_k)
                vh = _head_halves(jnp.where(own, v2, v2r).astype(BF16), upper_k)
                dkd = jnp.zeros(k2.shape, F32)
                dvd = jnp.zeros(k2.shape, F32)
                for pi in range(2):
                    cols = slice(256 * jj + 128 * pi, 256 * jj + 128 * pi + 128)
                    qp = (_rope(q_ref[:, cols], tabs_c) * ATT_SCALE).astype(BF16)
                    do_pair = do_ref[:, cols]
                    dob = do_pair.astype(BF16)
                    delta2 = lax.dot_general(head_rows, do_pair * o_ref[:, cols].astype(F32), ((NT), ((), ())),
                                             precision=lax.Precision.HIGHEST, preferred_element_type=F32)
                    st = _dot(kh, qp, NT)
                    dpt = _dot(vh, dob, NT)
                    pns, dss = [], []
                    for e in range(2):
                        hl = 4 * jj + 2 * pi + e
                        pn, ps = _swa_scores(st[e * nk:(e + 1) * nk], sink_ref[0, 8 * m + hl], mask_t)
                        delta = jnp.max(delta2[8 * e:8 * e + 8], axis=0, keepdims=True)
                        pns.append(pn.astype(BF16))
                        dss.append((pn * (dpt[e * nk:(e + 1) * nk] - delta)).astype(BF16))
                        dsv = dsv + jnp.where(lane == hl, -jnp.sum(ps * delta), 0.0)
                    dsb = jnp.concatenate(dss, axis=0)
                    dq_ref[:, cols] = _rope_t(_dot(dsb, kh, TN) * ATT_SCALE, tabs_c).astype(BF16)
                    rk = _dot(dsb, qp, NN)
                    rv = _dot(jnp.concatenate(pns, axis=0), dob, NN)
                    dkd = dkd + jnp.where(upper_k, rk[nk:], rk[:nk])
                    dvd = dvd + jnp.where(upper_k, rv[nk:], rv[:nk])
                dk2 = dk2 + jnp.where(own, dkd + pltpu.roll(dkd, 64, 1), 0.0)
                dv2 = dv2 + jnp.where(own, dvd + pltpu.roll(dvd, 64, 1), 0.0)
            dk_ref[...] = (ck_ref[...] + _rope_t(dk2[:blk], tabs_p)).astype(BF16)
            dv_ref[...] = (cv_ref[...] + dv2[:blk]).astype(BF16)
            ck_ref[...] = _rope_t(dk2[blk:], tabs_c)
            cv_ref[...] = dv2[blk:]
            ds_ref[...] += dsv

        @pl.when(n == n_blocks)
        def _():
            dk_ref[...] = ck_ref[...].astype(BF16)
            dv_ref[...] = cv_ref[...].astype(BF16)

    wide = pl.BlockSpec((blk, 512), lambda m, n: (cur(n), m))
    lagged = pl.BlockSpec((blk, 128), lambda m, n: (jnp.maximum(n - 1, 0), m))
    ex_in, ex_in_specs, ex_out, ex_out_specs, ex_scratch = _carried(ex)
    outs = pl.pallas_call(
        _carry(body, ex, 14, 4, 2, (2, n_blocks + 1)), name=name, grid=(2, n_blocks + 1),
        in_specs=[q_spec] + kv_specs + tab_specs + [pl.BlockSpec(memory_space=pltpu.SMEM), wide, wide] + ex_in_specs,
        out_specs=[wide, lagged, lagged, pl.BlockSpec((None, 8, 128), lambda m, n: (m, 0, 0))] + ex_out_specs,
        out_shape=[jax.ShapeDtypeStruct((t, 1024), BF16), jax.ShapeDtypeStruct((t, 256), BF16),
                   jax.ShapeDtypeStruct((t, 256), BF16), jax.ShapeDtypeStruct((2, 8, 128), F32)] + ex_out,
        scratch_shapes=[pltpu.VMEM((blk, 128), F32), pltpu.VMEM((blk, 128), F32)] + ex_scratch,
        compiler_params=_cp("arbitrary", "arbitrary"))(proj, proj, proj, proj, proj, *tabs, *tabs, sinks, o_att, do_att,
                                                       *ex_in)
    return outs[:4], outs[4:]


def _local_step(x, target, shards, norm1, lb_logits, hg_norm, attn_sinks, norm2, final_norm):
    t = x.shape[0]
    tabs = _rope_tables(t)
    lb_all = _lb_fwd(lb_logits)
    saved = []

    def shards_of(l):
        return {ti: shards[ti][l] for ti in ALL_KINDS}

    win_next = _run_exchange(_gather_exchange(shards_of(0), KINDS_W_IN), "gather_w_in")
    rest_next = None
    for l in range(DEPTH):
        n1, n2 = norm1[l][None, :], norm2[l][None, :]
        lb, gn, sinks = lb_all[l][None, :], hg_norm[l][None, :], attn_sinks[l][None, :]
        (win_t,) = win_next
        if l == 0:
            h = _rms_fwd(x, n1, "rms1_fwd")
        if l == 0:
            proj, rest_next = _matmul_nt(h, win_t, 0, IN_COLS, F32, "proj_fwd", tn=1280,
                                         ex=_gather_exchange(shards_of(0), KINDS_REST))
        else:
            proj = _matmul_nt(h, win_t, 0, IN_COLS, F32, "proj_fwd", tn=1280)
        w_pa, w_pb, w_o, wgu_t, w_d = rest_next
        more = l + 1 < DEPTH
        (o_hg, o_g, sall, ball, aall), rest_next = _hgrn2_fwd(
            proj, lb, gn, "hgrn2_fwd", _gather_exchange(shards_of(l + 1), KINDS_REST) if more else None)
        o_att, win_next = _swa_fwd(
            proj, sinks, tabs, "swa_fwd", _gather_exchange(shards_of(l + 1), KINDS_W_IN) if more else None)
        ya, yb, mix, h2, x1 = _merge_fwd(o_g, o_att, proj, x, w_pa, w_pb, w_o, n2, "merge_fwd")
        gu, act = _ffn_up_fwd(h2, wgu_t, "ffn_up_fwd")
        if more:
            x2, h_next = _matmul_nn(act, w_d, 0, x1, "wd_fwd", gain=norm1[l + 1][None, :])
        else:
            x2, h_next = _matmul_nn(act, w_d, 0, x1, "wd_fwd_last"), None
        saved.append((x, h, proj, o_hg, o_g, (sall, ball, aall), o_att, ya, yb, mix, x1, h2, gu, act, n1, n2, lb, gn, sinks,
                      (win_t, w_pa, w_pb, w_o, wgu_t, w_d)))
        x, h = x2, h_next

    dx, d_fn, loss = _loss_head(x, final_norm[None, :], target, "loss_head")

    owned = [None] * DEPTH
    pending = None
    d_n1, d_n2, d_lb, d_gn, d_sinks = ([None] * DEPTH for _ in range(5))
    for l in reversed(range(DEPTH)):
        x0, h, proj, o_hg, o_g, hg_saved, o_att, ya, yb, mix, x1, h2, gu, act, n1, n2, lb, gn, sinks, weights = saved[l]
        win_t, w_pa, w_pb, w_o, wgu_t, w_d = weights
        dgu = _ffn_down_bwd(dx, w_d, gu, "ffn_down_bwd")
        g_wd = _matmul_tn(act, dx, "wd_grad", tm=1408)
        g_wgu = _matmul_tn(dgu, h2, "wgu_grad", tm=1408)
        late = pending is not None
        dx1, d_n2[l], *land_w = _rows_bwd([dgu], wgu_t, x1, n2, dx, "ffn_up_bwd", tm=512,
                                          ex=_scatter_exchange(pending, KINDS_W_IN) if late else None)
        g_wo = _matmul_tn(mix, dx1, "wo_grad")
        dya, dyb, dgab, dog, doatt = _merge_bwd(dx1, ya, yb, proj, w_pa, w_pb, w_o, "merge_bwd")
        g_wpa = _matmul_tn(o_g, dya, "wpa_grad")
        g_wpb = _matmul_tn(o_att, dyb, "wpb_grad")
        (dhg, d_lb[l], d_gn[l]), land_r = _hgrn2_bwd(proj, lb, gn, o_hg, *hg_saved, dog, "hgrn2_bwd",
                                                     _scatter_exchange(pending, KINDS_REST) if late else None)
        if late:
            owned[l + 1] = jnp.concatenate([_sum_slots(land_w[0][0], "sum_slots_w_in"),
                                            _sum_slots(land_r[0], "sum_slots_rest")], axis=0)
        ex = _scatter_exchange((None, g_wpa, g_wpb, g_wo, g_wgu, g_wd), KINDS_REST) if l == 0 else None
        (daq, dak, dav, d_sinks[l]), land_rest = _swa_bwd(proj, sinks, tabs, o_att, doatt, "swa_bwd", ex)
        g_win = None
        for piece, off, tm, tag in ((dhg, COL_HQ, 512, "hg"), (daq, COL_AQ, 512, "aq"), (dak, COL_AK, 256, "ak"),
                                    (dav, COL_AV, 256, "av"), (dgab, COL_GA, 512, "gates")):
            g_win = _matmul_tn(piece, h, "win_grad_" + tag, tm=tm, rows=IN_COLS, row_off=off, into=g_win)
        if l > 0:
            dx, d_n1[l] = _rows_bwd([dhg, daq, dak, dav, dgab], win_t, x0, n1, dx1, "win_bwd")
        else:
            dx, d_n1[l], land_win = _rows_bwd([dhg, daq, dak, dav, dgab], win_t, x0, n1, dx1, "win_bwd",
                                              ex=_scatter_exchange((g_win,), KINDS_W_IN))
        pending = (g_win, g_wpa, g_wpb, g_wo, g_wgu, g_wd)
    owned[0] = jnp.concatenate([_sum_slots(land_win[0], "sum_slots_w_in"), _sum_slots(land_rest[0], "sum_slots_rest")],
                               axis=0)

    d_sink_rows = [jnp.concatenate([d[0, 0, :8], d[1, 0, :8]]) for d in d_sinks]
    small = (jnp.concatenate(d_n1, axis=0), jnp.concatenate(d_lb, axis=0), jnp.concatenate(d_gn, axis=0),
             jnp.concatenate(d_n2, axis=0), d_fn, jnp.stack(d_sink_rows, axis=0))
    return loss, dx, jnp.stack(owned, axis=0), small


def _sum_slots(land, name, tr=480):
    _, rows, d = land.shape

    def body(l_ref, o_ref):
        acc = l_ref[0].astype(F32)
        for k in range(1, N_DEV):
            acc = acc + l_ref[k].astype(F32)
        o_ref[...] = acc

    return pl.pallas_call(
        body, name=name, grid=(rows // tr,),
        in_specs=[pl.BlockSpec((N_DEV, tr, d), lambda i: (0, i, 0))],
        out_specs=pl.BlockSpec((tr, d), lambda i: (i, 0)),
        out_shape=jax.ShapeDtypeStruct((rows, d), F32),
        compiler_params=_cp("parallel"))(land)


def _adamw(w, g, m, v, name):
    shape = w.shape
    c = shape[-1]
    rows = w.size // c
    tr = rows
    for cand in (512, 352, 128):
        if rows % cand == 0:
            tr = cand
            break
    c1 = 1.0 / (1.0 - ADAM_B1 ** ADAM_STEP)
    c2 = 1.0 / (1.0 - ADAM_B2 ** ADAM_STEP)

    def body(w_ref, g_ref, m_ref, v_ref, d_ref, nm_ref, nv_ref):
        gv = g_ref[...]
        nm = ADAM_B1 * m_ref[...] + (1.0 - ADAM_B1) * gv
        nv = ADAM_B2 * v_ref[...] + (1.0 - ADAM_B2) * (gv * gv)
        d_ref[...] = -ADAM_LR * ((nm * c1) / (jnp.sqrt(nv * c2) + ADAM_EPS) + ADAM_WD * w_ref[...])
        nm_ref[...] = nm
        nv_ref[...] = nv

    spec = pl.BlockSpec((tr, c), lambda i: (i, 0))
    outs = pl.pallas_call(
        body, name=name, grid=(rows // tr,), in_specs=[spec] * 4, out_specs=[spec] * 3,
        out_shape=[jax.ShapeDtypeStruct((rows, c), F32)] * 3,
        compiler_params=_cp("parallel"))(*[a.reshape(rows, c) for a in (w, g, m, v)])
    return tuple(o.reshape(shape) for o in outs)


def kernel(x, norm1, w_in, lb_logits, hg_norm, attn_sinks, w_pa, w_pb, w_o, norm2, w_gate, w_up, w_down, final_norm, loss_target, m_norm1, m_w_in, m_lb_logits, m_hg_norm, m_attn_sinks, m_w_pa, m_w_pb, m_w_o, m_norm2, m_w_gate, m_w_up, m_w_down, m_final_norm, v_norm1, v_w_in, v_lb_logits, v_hg_norm, v_attn_sinks, v_w_pa, v_w_pb, v_w_o, v_norm2, v_w_gate, v_w_up, v_w_down, v_final_norm):
    t = x.shape[1]
    shards = [jnp.swapaxes(w_in, 1, 2).astype(BF16), w_pa.astype(BF16), w_pb.astype(BF16), w_o.astype(BF16),
              jnp.swapaxes(w_gate, 1, 2).astype(BF16), jnp.swapaxes(w_up, 1, 2).astype(BF16), w_down.astype(BF16)]
    loss_lanes, grad_x, owned, small = _local_step(
        x.reshape(t, D_MODEL), loss_target.reshape(t, D_MODEL), shards,
        norm1, lb_logits, hg_norm, attn_sinks, norm2, final_norm)

    def rows_of(ti, transpose):
        g = owned[:, SLOT_OFF[ti]:SLOT_OFF[ti] + SHARD_ROWS[ti], :]
        return jnp.swapaxes(g, 1, 2) if transpose else g

    g_big = {"w_in": rows_of(0, True), "w_pa": rows_of(1, False), "w_pb": rows_of(2, False), "w_o": rows_of(3, False),
             "w_gate": rows_of(4, True), "w_up": rows_of(5, True), "w_down": rows_of(6, False)}

    d_n1, d_lb, d_gn, d_n2, d_fn, d_sinks = small
    pad = jnp.zeros((DEPTH, D_MODEL - ATT_Q_HEADS), F32)
    packed = jnp.concatenate([
        d_n1, d_lb, d_gn, d_n2, d_fn, jnp.concatenate([d_sinks, pad], axis=1),
        jnp.concatenate([loss_lanes, jnp.zeros((1, D_MODEL - 128), F32)], axis=1),
        jnp.zeros((SMALL_ROWS - 22, D_MODEL), F32)], axis=0)
    total = _all_reduce_small(packed)
    loss = total[21, 0]
    g_small = {"norm1": total[0:4], "lb_logits": _lb_bwd(lb_logits, total[4:8]), "hg_norm": total[8:12],
               "norm2": total[12:16], "final_norm": total[16], "attn_sinks": total[17:21, :ATT_Q_HEADS]}

    params = {"norm1": (norm1, m_norm1, v_norm1), "w_in": (w_in, m_w_in, v_w_in),
              "lb_logits": (lb_logits, m_lb_logits, v_lb_logits), "hg_norm": (hg_norm, m_hg_norm, v_hg_norm),
              "attn_sinks": (attn_sinks, m_attn_sinks, v_attn_sinks), "w_pa": (w_pa, m_w_pa, v_w_pa),
              "w_pb": (w_pb, m_w_pb, v_w_pb), "w_o": (w_o, m_w_o, v_w_o), "norm2": (norm2, m_norm2, v_norm2),
              "w_gate": (w_gate, m_w_gate, v_w_gate), "w_up": (w_up, m_w_up, v_w_up),
              "w_down": (w_down, m_w_down, v_w_down), "final_norm": (final_norm, m_final_norm, v_final_norm)}
    order = ["norm1", "w_in", "lb_logits", "hg_norm", "attn_sinks", "w_pa", "w_pb", "w_o", "norm2",
             "w_gate", "w_up", "w_down", "final_norm"]
    grads, deltas, new_m, new_v = [], [], [], []
    for name in order:
        w, m, v = params[name]
        g = (g_big[name] if name in g_big else g_small[name]).reshape(w.shape)
        w2 = w.reshape(1, -1) if w.ndim == 1 else w
        d, nm, nv = _adamw(w2, g.reshape(w2.shape), m.reshape(w2.shape), v.reshape(w2.shape), "adamw_" + name)
        grads.append(g)
        deltas.append(d.reshape(w.shape))
        new_m.append(nm.reshape(w.shape))
        new_v.append(nv.reshape(w.shape))
    return (loss, grad_x.reshape(x.shape), *grads, *deltas, *new_m, *new_v)
```

```python
import functools
from typing import Callable, NamedTuple

import jax
import jax.numpy as jnp
from jax import lax
from jax.experimental import pallas as pl
from jax.experimental.pallas import tpu as pltpu

F32, BF16 = jnp.float32, jnp.bfloat16

D_MODEL = 1024
DEPTH = 4
N_DEV = 8
HG_HEADS = 8
HG_DK = 128
HG_CHUNK = 64
HG_BLOCK = 512
HG_EXP_CLAMP = 60.0
ATT_Q_HEADS = 16
ATT_HEAD_DIM = 64
ATT_BLOCK = 128
ROPE_THETA = 500000.0
ROPE_DIM = 16
FFN_HIDDEN = 2816
EPS = 1e-6
MIN_F = 1e-30
ADAM_LR, ADAM_B1, ADAM_B2, ADAM_EPS, ADAM_WD, ADAM_STEP = 0.001, 0.9, 0.999, 1e-08, 0.01, 10

COL_HQ, COL_HF, COL_HI, COL_HG = 0, 1024, 2048, 3072
COL_AQ, COL_AK, COL_AV, COL_GA, COL_GB = 4096, 5120, 5376, 5632, 6656
IN_COLS = 7680

SHARD_ROWS = (960, 128, 128, 128, 352, 352, 352)
SLOT_OFF = (0, 960, 1088, 1216, 1344, 1696, 2048)
SLOT_ROWS = 2400
SMALL_ROWS = 24

VMEM_LIMIT_BYTES = 56 * 1024 * 1024

NN = ((1,), (0,))
NT = ((1,), (1,))
TN = ((0,), (0,))


def _dot(a, b, dims):
    return lax.dot_general(a, b, (dims, ((), ())), preferred_element_type=F32)


def _cp(*sem):
    return pltpu.CompilerParams(dimension_semantics=sem if sem else None, vmem_limit_bytes=VMEM_LIMIT_BYTES)


def _sigmoid(x):
    return 0.5 * jnp.tanh(0.5 * x) + 0.5


def _matmul_nt(a, w, row_off, n, out_dtype, name, tm=1024, tn=512, ex=None):
    t, k = a.shape
    tm = min(tm, t)
    assert n % tn == 0 and row_off % tn == 0 and t % tm == 0
    grid = (n // tn, t // tm)

    def body(a_ref, w_ref, o_ref):
        o_ref[...] = _dot(a_ref[...].astype(BF16), w_ref[...], NT).astype(o_ref.dtype)

    ex_in, ex_in_specs, ex_out, ex_out_specs, ex_scratch = _carried(ex)
    outs = pl.pallas_call(
        _carry(body, ex, 2, 1, 0, grid), name=name, grid=grid,
        in_specs=[pl.BlockSpec((tm, k), lambda j, i: (i, 0)),
                  pl.BlockSpec((tn, k), lambda j, i: (row_off // tn + j, 0))] + ex_in_specs,
        out_specs=[pl.BlockSpec((tm, tn), lambda j, i: (i, j))] + ex_out_specs,
        out_shape=[jax.ShapeDtypeStruct((t, n), out_dtype)] + ex_out,
        scratch_shapes=ex_scratch,
        compiler_params=_cp("arbitrary", "arbitrary") if ex else _cp("parallel", "parallel"))(a, w, *ex_in)
    return (outs[0], outs[1:]) if ex else outs[0]


def _matmul_nn(a, w, row_off, res, name, tm=512, tk=None, gain=None):
    t, k = a.shape
    n = w.shape[1]
    tm = min(tm, t)
    tk = tk or k
    nk = k // tk
    assert k % tk == 0 and row_off % tk == 0 and t % tm == 0

    def body(*refs):
        refs = list(refs)
        a_ref, w_ref = refs[:2]
        r_ref = refs[2] if res is not None else None
        g_ref = refs[2 + (res is not None)] if gain is not None else None
        acc = refs[-1]
        o_ref = refs[-3] if gain is not None else refs[-2]
        kk = pl.program_id(1)
        part = _dot(a_ref[...].astype(BF16), w_ref[...], NN)

        @pl.when(kk == 0)
        def _():
            acc[...] = part

        @pl.when(kk > 0)
        def _():
            acc[...] += part

        @pl.when(kk == nk - 1)
        def _():
            y = acc[...] if res is None else acc[...] + r_ref[...]
            o_ref[...] = y
            if gain is not None:
                refs[-2][...] = _rms(y, g_ref[...]).astype(BF16)

    row = pl.BlockSpec((tm, n), lambda i, kk: (i, 0))
    in_specs = [pl.BlockSpec((tm, tk), lambda i, kk: (i, kk)),
                pl.BlockSpec((tk, n), lambda i, kk: (row_off // tk + kk, 0))]
    args = [a, w]
    if res is not None:
        in_specs.append(row)
        args.append(res)
    if gain is not None:
        in_specs.append(pl.BlockSpec((1, n), lambda i, kk: (0, 0)))
        args.append(gain)
    outs = pl.pallas_call(
        body, name=name, grid=(t // tm, nk), in_specs=in_specs,
        out_specs=[row, row] if gain is not None else [row],
        out_shape=[jax.ShapeDtypeStruct((t, n), F32)] + ([jax.ShapeDtypeStruct((t, n), BF16)] if gain is not None else []),
        scratch_shapes=[pltpu.VMEM((tm, n), F32)],
        compiler_params=_cp("parallel", "arbitrary"))(*args)
    return tuple(outs) if gain is not None else outs[0]


def _matmul_tn(a, b, name, tm=512, tk=2048, rows=None, row_off=0, into=None):
    t, m = a.shape
    n = b.shape[1]
    tk = min(tk, t)
    nk = t // tk
    rows = rows or m
    assert m % tm == 0 and t % tk == 0 and row_off % tm == 0

    def body(*refs):
        a_ref, b_ref, o_ref, acc = refs[0], refs[1], refs[-2], refs[-1]
        kk = pl.program_id(1)
        part = _dot(a_ref[...].astype(BF16), b_ref[...].astype(BF16), TN)

        @pl.when(kk == 0)
        def _():
            acc[...] = part

        @pl.when(kk > 0)
        def _():
            acc[...] += part

        @pl.when(kk == nk - 1)
        def _():
            o_ref[...] = acc[...].astype(BF16)

    return pl.pallas_call(
        body, name=name, grid=(m // tm, nk),
        in_specs=[pl.BlockSpec((tk, tm), lambda i, kk: (kk, i)),
                  pl.BlockSpec((tk, n), lambda i, kk: (kk, 0))] + ([ANY] if into is not None else []),
        out_specs=pl.BlockSpec((tm, n), lambda i, kk: (row_off // tm + i, 0)),
        out_shape=jax.ShapeDtypeStruct((rows, n), BF16),
        scratch_shapes=[pltpu.VMEM((tm, n), F32)],
        input_output_aliases={2: 0} if into is not None else {},
        compiler_params=_cp("parallel", "arbitrary"))(a, b, *([into] if into is not None else []))


def _rms(x, g):
    return x * lax.rsqrt(jnp.mean(x * x, axis=-1, keepdims=True) + EPS) * g


def _rms_fwd(x, g, name, tm=512):
    t, d = x.shape
    tm = min(tm, t)

    def body(x_ref, g_ref, o_ref):
        o_ref[...] = _rms(x_ref[...], g_ref[...]).astype(BF16)

    return pl.pallas_call(
        body, name=name, grid=(t // tm,),
        in_specs=[pl.BlockSpec((tm, d), lambda i: (i, 0)), pl.BlockSpec((1, d), lambda i: (0, 0))],
        out_specs=pl.BlockSpec((tm, d), lambda i: (i, 0)),
        out_shape=jax.ShapeDtypeStruct((t, d), BF16),
        compiler_params=_cp("parallel"))(x, g)


def _mix(ya, yb, ga, gb):
    return _sigmoid(ga) * ya + _sigmoid(gb) * yb


def _gate_specs(tm):
    half = D_MODEL // 2
    return [pl.BlockSpec((tm, half), lambda i, c=c: (i, c))
            for c in (COL_GA // half, COL_GA // half + 1, COL_GB // half, COL_GB // half + 1)]


def _merge_bwd(dx1, ya, yb, proj, w_pa, w_pb, w_o, name, tm=512):
    t, d = dx1.shape
    tm = min(tm, t)

    def body(dx_ref, ya_ref, yb_ref, ga0, ga1, gb0, gb1, wpa_ref, wpb_ref, wo_ref,
             dya_ref, dyb_ref, dg_ref, dog_ref, doa_ref):
        dmix = _dot(dx_ref[...].astype(BF16), wo_ref[...], NT)
        ga = jnp.concatenate([ga0[...], ga1[...]], axis=1)
        gb = jnp.concatenate([gb0[...], gb1[...]], axis=1)
        _, vjp = jax.vjp(_mix, ya_ref[...].astype(F32), yb_ref[...].astype(F32), ga, gb)
        dya, dyb, dga, dgb = vjp(dmix)
        dya, dyb = dya.astype(BF16), dyb.astype(BF16)
        dya_ref[...] = dya
        dyb_ref[...] = dyb
        dg_ref[:, :d] = dga.astype(BF16)
        dg_ref[:, d:] = dgb.astype(BF16)
        dog_ref[...] = _dot(dya, wpa_ref[...], NT)
        doa_ref[...] = _dot(dyb, wpb_ref[...], NT)

    row = pl.BlockSpec((tm, d), lambda i: (i, 0))
    wide = pl.BlockSpec((tm, 2 * d), lambda i: (i, 0))
    mat = pl.BlockSpec((d, d), lambda i: (0, 0))
    return pl.pallas_call(
        body, name=name, grid=(t // tm,), in_specs=[row, row, row] + _gate_specs(tm) + [mat, mat, mat],
        out_specs=[row, row, wide, row, row],
        out_shape=[jax.ShapeDtypeStruct((t, d), BF16), jax.ShapeDtypeStruct((t, d), BF16),
                   jax.ShapeDtypeStruct((t, 2 * d), BF16), jax.ShapeDtypeStruct((t, d), F32),
                   jax.ShapeDtypeStruct((t, d), F32)],
        compiler_params=_cp("parallel"))(dx1, ya, yb, proj, proj, proj, proj, w_pa, w_pb, w_o)


def _swiglu(g, u):
    return g * _sigmoid(g) * u


def _swiglu_bwd(g, u, dact):
    sg = _sigmoid(g)
    gs = g * sg
    return dact * u * (sg + gs * (1.0 - sg)), dact * gs


def _ffn_up_fwd(h2, wgu_t, name, tm=512):
    t, d = h2.shape
    tm = min(tm, t)
    fh = FFN_HIDDEN // 2

    def body(a_ref, w_ref, gu_ref, act_ref):
        r = _dot(a_ref[...], w_ref[...], NT)
        gu_ref[...] = r.astype(BF16)
        act_ref[...] = _swiglu(r[:, :fh], r[:, fh:]).astype(BF16)

    return pl.pallas_call(
        body, name=name, grid=(2, t // tm),
        in_specs=[pl.BlockSpec((tm, d), lambda j, i: (i, 0)), pl.BlockSpec((2 * fh, d), lambda j, i: (j, 0))],
        out_specs=[pl.BlockSpec((tm, 2 * fh), lambda j, i: (i, j)), pl.BlockSpec((tm, fh), lambda j, i: (i, j))],
        out_shape=[jax.ShapeDtypeStruct((t, 4 * fh), BF16), jax.ShapeDtypeStruct((t, 2 * fh), BF16)],
        compiler_params=_cp("parallel", "parallel"))(h2, wgu_t)


def _ffn_down_bwd(dx, w_d, gu, name, tm=512):
    t, d = dx.shape
    tm = min(tm, t)
    fh = FFN_HIDDEN // 2

    def body(a_ref, w_ref, gu_ref, o_ref):
        dact = _dot(a_ref[...].astype(BF16), w_ref[...], NT)
        dg, du = _swiglu_bwd(gu_ref[:, :fh].astype(F32), gu_ref[:, fh:].astype(F32), dact)
        o_ref[:, :fh] = dg.astype(BF16)
        o_ref[:, fh:] = du.astype(BF16)

    wide = pl.BlockSpec((tm, 2 * fh), lambda j, i: (i, j))
    return pl.pallas_call(
        body, name=name, grid=(2, t // tm),
        in_specs=[pl.BlockSpec((tm, d), lambda j, i: (i, 0)), pl.BlockSpec((fh, d), lambda j, i: (j, 0)), wide],
        out_specs=wide,
        out_shape=jax.ShapeDtypeStruct((t, 4 * fh), BF16),
        compiler_params=_cp("parallel", "parallel"))(dx, w_d, gu)


def _rows_bwd(pieces, w, x, g, dres, name, tm=256, ex=None):
    t, d = x.shape
    tm = min(tm, t)
    widths = [p.shape[1] for p in pieces]
    starts = [sum(widths[:i]) for i in range(len(widths))]
    assert sum(widths) == w.shape[0]
    n_p = len(pieces)

    def body(*refs):
        p_refs, (w_ref, x_ref, g_ref, dres_ref, dx_ref, dg_ref) = refs[:n_p], refs[n_p:]
        dh = _dot(p_refs[0][...], w_ref[pl.ds(starts[0], widths[0]), :], NN)
        for i in range(1, n_p):
            dh = dh + _dot(p_refs[i][...], w_ref[pl.ds(starts[i], widths[i]), :], NN)
        _, vjp = jax.vjp(_rms, x_ref[...], g_ref[...])
        dx, dg = vjp(dh)
        dx_ref[...] = dres_ref[...] + dx

        @pl.when(pl.program_id(0) == 0)
        def _():
            dg_ref[...] = jnp.zeros_like(dg_ref)

        dg_ref[...] += dg

    row = pl.BlockSpec((tm, d), lambda i: (i, 0))
    vec = pl.BlockSpec((1, d), lambda i: (0, 0))
    ex_in, ex_in_specs, ex_out, ex_out_specs, ex_scratch = _carried(ex)
    outs = pl.pallas_call(
        _carry(body, ex, n_p + 4, 2, 0, (t // tm,)), name=name, grid=(t // tm,),
        in_specs=[pl.BlockSpec((tm, k), lambda i: (i, 0)) for k in widths]
        + [pl.BlockSpec(w.shape, lambda i: (0, 0)), row, vec, row] + ex_in_specs,
        out_specs=[row, vec] + ex_out_specs,
        out_shape=[jax.ShapeDtypeStruct((t, d), F32), jax.ShapeDtypeStruct((1, d), F32)] + ex_out,
        scratch_shapes=ex_scratch,
        compiler_params=_cp("arbitrary"))(*pieces, w, x, g, dres, *ex_in)
    return (outs[0], outs[1], outs[2:]) if ex else (outs[0], outs[1])


def _merge_fwd(o_g, o_att, proj, x, w_pa, w_pb, w_o, gain, name, tm=512):
    t, d = x.shape
    tm = min(tm, t)

    def body(og_ref, oa_ref, ga0, ga1, gb0, gb1, x_ref, wpa_ref, wpb_ref, wo_ref, g_ref,
             ya_ref, yb_ref, mix_ref, h2_ref, x1_ref):
        ya = _dot(og_ref[...], wpa_ref[...], NN)
        yb = _dot(oa_ref[...], wpb_ref[...], NN)
        ga = jnp.concatenate([ga0[...], ga1[...]], axis=1)
        gb = jnp.concatenate([gb0[...], gb1[...]], axis=1)
        mix = _mix(ya, yb, ga, gb).astype(BF16)
        ya_ref[...] = ya.astype(BF16)
        yb_ref[...] = yb.astype(BF16)
        mix_ref[...] = mix
        x1 = x_ref[...] + _dot(mix, wo_ref[...], NN)
        x1_ref[...] = x1
        h2_ref[...] = _rms(x1, g_ref[...]).astype(BF16)

    row = pl.BlockSpec((tm, d), lambda i: (i, 0))
    mat = pl.BlockSpec((d, d), lambda i: (0, 0))
    return pl.pallas_call(
        body, name=name, grid=(t // tm,),
        in_specs=[row, row] + _gate_specs(tm) + [row, mat, mat, mat, pl.BlockSpec((1, d), lambda i: (0, 0))],
        out_specs=[row] * 5,
        out_shape=[jax.ShapeDtypeStruct((t, d), BF16)] * 4 + [jax.ShapeDtypeStruct((t, d), F32)],
        compiler_params=_cp("parallel"))(o_g, o_att, proj, proj, proj, proj, x, w_pa, w_pb, w_o, gain)


def _loss_head(x, g, target, name, tm=512):
    t, d = x.shape
    tm = min(tm, t)

    def body(x_ref, g_ref, t_ref, dx_ref, dg_ref, loss_ref):
        tgt = t_ref[...]

        def f(xv, gv):
            err = _rms(xv, gv) - tgt
            return 0.5 * jnp.sum(jnp.mean(err * err, axis=-1, keepdims=True))

        loss, vjp = jax.vjp(f, x_ref[...], g_ref[...])
        dx, dg = vjp(jnp.ones((), F32))
        dx_ref[...] = dx

        @pl.when(pl.program_id(0) == 0)
        def _():
            dg_ref[...] = jnp.zeros_like(dg_ref)
            loss_ref[...] = jnp.zeros_like(loss_ref)

        dg_ref[...] += dg
        loss_ref[...] += jnp.full(loss_ref.shape, loss, F32)

    row = pl.BlockSpec((tm, d), lambda i: (i, 0))
    vec = pl.BlockSpec((1, d), lambda i: (0, 0))
    lane = pl.BlockSpec((1, 128), lambda i: (0, 0))
    return pl.pallas_call(
        body, name=name, grid=(t // tm,), in_specs=[row, vec, row], out_specs=[row, vec, lane],
        out_shape=[jax.ShapeDtypeStruct((t, d), F32), jax.ShapeDtypeStruct((1, d), F32),
                   jax.ShapeDtypeStruct((1, 128), F32)],
        compiler_params=_cp("arbitrary"))(x, g, target)


def _lb_rows(l0, l1, l2, l3):
    mx = jnp.maximum(jnp.maximum(l0, l1), jnp.maximum(l2, l3))
    e0, e1, e2, e3 = jnp.exp(l0 - mx), jnp.exp(l1 - mx), jnp.exp(l2 - mx), jnp.exp(l3 - mx)
    s = e0 + e1 + e2 + e3
    p0, p1, p2, p3 = e0 / s, e1 / s, e2 / s, e3 / s
    c1 = p0 + p1
    c2 = c1 + p2
    c3 = c2 + p3
    return p0 - p0, c1 - p0, c2 - p0, c3 - p0


def _lb_fwd(lb_logits):
    def body(l_ref, o_ref):
        rows = _lb_rows(*[l_ref[pl.ds(i, 1), :] for i in range(DEPTH)])
        for i in range(DEPTH):
            o_ref[pl.ds(i, 1), :] = rows[i]

    return pl.pallas_call(body, name="lb_fwd", out_shape=jax.ShapeDtypeStruct(lb_logits.shape, F32))(lb_logits)


def _lb_bwd(lb_logits, dlb):
    def body(l_ref, d_ref, o_ref):
        _, vjp = jax.vjp(_lb_rows, *[l_ref[pl.ds(i, 1), :] for i in range(DEPTH)])
        grads = vjp(tuple(d_ref[pl.ds(i, 1), :] for i in range(DEPTH)))
        for i in range(DEPTH):
            o_ref[pl.ds(i, 1), :] = grads[i]

    return pl.pallas_call(body, name="lb_bwd", out_shape=jax.ShapeDtypeStruct(lb_logits.shape, F32))(lb_logits, dlb)


MESH = pl.DeviceIdType.MESH
ANY = pl.BlockSpec(memory_space=pl.ANY)
N_KINDS = len(SHARD_ROWS)
FFN_HALF = FFN_HIDDEN // 2
KIND_PLACE = ((0, 0), (1, 0), (2, 0), (3, 0), (4, 0), (4, FFN_HALF), (5, 0))
KIND_HALF_SKIP = (0, 0, 0, 0, FFN_HALF, FFN_HALF, 0)
FULL_ROWS = (N_DEV * SHARD_ROWS[0], D_MODEL, D_MODEL, D_MODEL, 2 * N_DEV * SHARD_ROWS[4], N_DEV * SHARD_ROWS[6])


def _kind_rows(ti, dev):
    oi, base = KIND_PLACE[ti]
    start = base + dev * SHARD_ROWS[ti]
    if KIND_HALF_SKIP[ti]:
        start = start + (dev // (N_DEV // 2)) * KIND_HALF_SKIP[ti]
    return oi, pl.ds(start, SHARD_ROWS[ti])


def _position():
    x, y, c = lax.axis_index("x"), lax.axis_index("y"), lax.axis_index("c")
    return x, y, c, 4 * x + 2 * y + c


def _peer(x, y, c, r):
    px = 1 - x if r & 4 else x
    py = 1 - y if r & 2 else y
    pc = 1 - c if r & 1 else c
    return (px, py, pc), 4 * px + 2 * py + pc


class _Exchange(NamedTuple):
    operands: tuple
    out_shape: tuple
    copies: Callable
    n_local: int


EXCHANGE_SCRATCH = (pltpu.SemaphoreType.DMA((N_DEV, N_KINDS)), pltpu.SemaphoreType.DMA((N_DEV, N_KINDS)),
                    pltpu.SemaphoreType.DMA((N_KINDS,)))
ALL_KINDS = tuple(range(N_KINDS))
KINDS_W_IN = (0,)
KINDS_REST = ALL_KINDS[1:]


def _all_pairs(kinds, ends, send_sems, recv_sems):
    x, y, c, me = _position()
    out = []
    for r in range(1, N_DEV):
        peer, pid = _peer(x, y, c, r)
        for ti in kinds:
            src, dst = ends(ti, me, pid)
            out.append(pltpu.make_async_remote_copy(
                src_ref=src, dst_ref=dst, send_sem=send_sems.at[r, ti], recv_sem=recv_sems.at[r, ti],
                device_id=peer, device_id_type=MESH))
    return out


def _gather_exchange(shards, kinds):
    arrays = sorted({KIND_PLACE[ti][0] for ti in kinds})

    def copies(ins, outs, send_sems, recv_sems, local_sems, arrivals):
        src = dict(zip(kinds, ins))

        def window(ti, dev):
            oi, rows = _kind_rows(ti, dev)
            return outs[arrays.index(oi)].at[rows, :]

        if arrivals:
            return _all_pairs(kinds, lambda ti, me, pid: (src[ti], window(ti, pid)), send_sems, recv_sems)
        _, _, _, me = _position()
        local = [pltpu.make_async_copy(src[ti], window(ti, me), local_sems.at[ti]) for ti in kinds]
        return local + _all_pairs(kinds, lambda ti, me, pid: (src[ti], window(ti, me)), send_sems, recv_sems)

    return _Exchange(tuple(shards[ti] for ti in kinds),
                     tuple(jax.ShapeDtypeStruct((FULL_ROWS[oi], D_MODEL), BF16) for oi in arrays), copies, len(kinds))


def _scatter_exchange(grads, kinds):
    arrays = sorted({KIND_PLACE[ti][0] for ti in kinds})
    offsets, total = {}, 0
    for ti in kinds:
        offsets[ti], total = total, total + SHARD_ROWS[ti]

    def copies(ins, outs, send_sems, recv_sems, local_sems, arrivals):
        land = outs[0]

        def piece(ti, dev):
            ii, rows = _kind_rows(ti, dev)
            return ins[arrays.index(ii)].at[rows, :]

        def slot(ti, dev):
            return land.at[dev, pl.ds(offsets[ti], SHARD_ROWS[ti]), :]

        if arrivals:
            return _all_pairs(kinds, lambda ti, me, pid: (piece(ti, me), slot(ti, pid)), send_sems, recv_sems)
        _, _, _, me = _position()
        local = [pltpu.make_async_copy(piece(ti, me), slot(ti, me), local_sems.at[ti]) for ti in kinds]
        return local + _all_pairs(kinds, lambda ti, me, pid: (piece(ti, pid), slot(ti, me)), send_sems, recv_sems)

    return _Exchange(tuple(grads[oi] for oi in arrays), (jax.ShapeDtypeStruct((N_DEV, total, D_MODEL), BF16),),
                     copies, len(kinds))


def _exchange_start(ex, ins, outs, sems):
    for cp in ex.copies(ins, outs, *sems, False):
        cp.start()


def _exchange_finish(ex, ins, outs, sems):
    for cp in ex.copies(ins, outs, *sems, True):
        cp.wait_recv()
    mine = ex.copies(ins, outs, *sems, False)
    for cp in mine[:ex.n_local]:
        cp.wait()
    for cp in mine[ex.n_local:]:
        cp.wait_send()


def _run_exchange(ex, name):
    n_in, n_out = len(ex.operands), len(ex.out_shape)

    def body(*refs):
        ins, outs, sems = refs[:n_in], refs[n_in:n_in + n_out], refs[n_in + n_out:]
        _exchange_start(ex, ins, outs, sems)
        _exchange_finish(ex, ins, outs, sems)

    return pl.pallas_call(body, name=name, in_specs=[ANY] * n_in, out_specs=[ANY] * n_out,
                          out_shape=list(ex.out_shape), scratch_shapes=list(EXCHANGE_SCRATCH))(*ex.operands)


def _gather_two_level(shard, ti, name):
    oi = KIND_PLACE[ti][0]

    def body(x_ref, out_ref, send_sems, recv_sems, local_sem):
        x, y, c, _ = _position()
        me, sibling = (x, y, c), (x, y, 1 - c)
        chips = [(1 - x, y), (x, 1 - y), (1 - x, 1 - y)]

        def rows(px, py, pc):
            return out_ref.at[_kind_rows(ti, 4 * px + 2 * py + pc)[1], :]

        def copy(k, block, to, src=None):
            return pltpu.make_async_remote_copy(
                src_ref=rows(*block) if src is None else src, dst_ref=rows(*block),
                send_sem=send_sems.at[k], recv_sem=recv_sems.at[k], device_id=to, device_id_type=MESH)

        mine = pltpu.make_async_copy(x_ref, rows(*me), local_sem)
        mine.start()
        first = [copy(0, me, sibling, src=x_ref)] + [copy(1 + j, me, (*chip, c), src=x_ref) for j, chip in enumerate(chips)]
        for cp in first:
            cp.start()
        passed = [copy(4 + j, (*chip, c), sibling) for j, chip in enumerate(chips)]
        for j, chip in enumerate(chips):
            copy(1 + j, (*chip, c), me).wait_recv()
            passed[j].start()
        copy(0, sibling, me).wait_recv()
        for j, chip in enumerate(chips):
            copy(4 + j, (*chip, 1 - c), me).wait_recv()
        for cp in first + passed:
            cp.wait_send()
        mine.wait()

    return pl.pallas_call(
        body, name=name, in_specs=[ANY], out_specs=ANY,
        out_shape=jax.ShapeDtypeStruct((FULL_ROWS[oi], D_MODEL), BF16),
        scratch_shapes=[pltpu.SemaphoreType.DMA((N_DEV - 1,)), pltpu.SemaphoreType.DMA((N_DEV - 1,)),
                        pltpu.SemaphoreType.DMA])(shard)


def _carry(body, ex, n_in, n_out, n_scratch, grid):
    if ex is None:
        return body
    e_in, e_out = len(ex.operands), len(ex.out_shape)

    def at(step):
        hit = pl.program_id(0) == step[0]
        for axis in range(1, len(grid)):
            hit = hit & (pl.program_id(axis) == step[axis])
        return hit

    def carrying(*refs):
        own_in, ex_in = refs[:n_in], refs[n_in:n_in + e_in]
        rest = refs[n_in + e_in:]
        own_out, ex_out = rest[:n_out], rest[n_out:n_out + e_out]
        own_scratch, sems = rest[n_out + e_out:n_out + e_out + n_scratch], rest[n_out + e_out + n_scratch:]

        @pl.when(at([0] * len(grid)))
        def _():
            _exchange_start(ex, ex_in, ex_out, sems)

        body(*own_in, *own_out, *own_scratch)

        @pl.when(at([g - 1 for g in grid]))
        def _():
            _exchange_finish(ex, ex_in, ex_out, sems)

    return carrying


def _carried(ex):
    if ex is None:
        return (), [], [], [], []
    return (ex.operands, [ANY] * len(ex.operands), list(ex.out_shape), [ANY] * len(ex.out_shape),
            list(EXCHANGE_SCRATCH))


def _small_sum_body(p_ref, o_ref, buf, send_sems, recv_sems):
    x, y, c, me = _position()
    buf[me] = p_ref[...]
    sends = []
    for r in range(1, N_DEV):
        peer, _ = _peer(x, y, c, r)
        sends.append(pltpu.make_async_remote_copy(
            src_ref=p_ref, dst_ref=buf.at[me], send_sem=send_sems.at[r], recv_sem=recv_sems.at[r],
            device_id=peer, device_id_type=MESH))
    for cp in sends:
        cp.start()
    for r in range(1, N_DEV):
        peer, pid = _peer(x, y, c, r)
        pltpu.make_async_remote_copy(
            src_ref=p_ref, dst_ref=buf.at[pid], send_sem=send_sems.at[r], recv_sem=recv_sems.at[r],
            device_id=peer, device_id_type=MESH).wait_recv()
    for cp in sends:
        cp.wait_send()
    acc = buf[0]
    for k in range(1, N_DEV):
        acc = acc + buf[k]
    o_ref[...] = acc


def _all_reduce_small(part):
    rows, d = part.shape
    vmem = pl.BlockSpec(memory_space=pltpu.VMEM)
    return pl.pallas_call(
        functools.partial(_small_sum_body), name="all_reduce_small", in_specs=[vmem], out_specs=vmem,
        out_shape=jax.ShapeDtypeStruct((rows, d), F32),
        scratch_shapes=[pltpu.VMEM((N_DEV, rows, d), F32), pltpu.SemaphoreType.DMA((N_DEV,)),
                        pltpu.SemaphoreType.DMA((N_DEV,))],
    )(part)


HG_PAIR = 2 * HG_DK


def _hg_consts():
    c = HG_CHUNK
    r = lax.broadcasted_iota(jnp.int32, (c, c), 0)
    s = lax.broadcasted_iota(jnp.int32, (c, c), 1)
    r2 = lax.broadcasted_iota(jnp.int32, (c, 2 * c), 0)
    s2 = lax.broadcasted_iota(jnp.int32, (c, 2 * c), 1)
    causal2 = jnp.where(s2 >= c, s2 - c, s2) <= r2
    lane_hi = lax.broadcasted_iota(jnp.int32, (c, HG_PAIR), 1) >= HG_DK
    same_head = ((lax.broadcasted_iota(jnp.int32, (HG_PAIR, HG_PAIR), 0) >= HG_DK)
                 == (lax.broadcasted_iota(jnp.int32, (HG_PAIR, HG_PAIR), 1) >= HG_DK))
    return (s <= r).astype(BF16), (s >= r).astype(BF16), causal2, lane_hi, same_head


def _head_rows(x, lane_hi):
    zero = jnp.zeros_like(x)
    return jnp.concatenate([jnp.where(lane_hi, zero, x), jnp.where(lane_hi, x, zero)], axis=0)


def _own_rows(y, lane_hi):
    return jnp.where(lane_hi, y[HG_CHUNK:], y[:HG_CHUNK])


def _split3(x):
    hi = x.astype(BF16)
    r1 = x - hi.astype(F32)
    mid = r1.astype(BF16)
    lo = (r1 - mid.astype(F32)).astype(BF16)
    return jnp.concatenate([hi, mid, lo], axis=1)


def _cumsum_rows(tri, x):
    w = x.shape[1]
    y = _dot(tri, _split3(x), NN)
    return y[:, :w] + y[:, w:2 * w] + y[:, 2 * w:]


def _hg_gates(zq, zf, lb):
    sq = _sigmoid(zq)
    sg = _sigmoid(zf)
    f = lb + (1.0 - lb) * sg
    return sq, zq * sq, sg, f, jnp.log(jnp.maximum(f, MIN_F)), 1.0 - f


def _hg_decays(ball_ref, ci, cols):
    c = HG_CHUNK
    b = ball_ref[pl.ds(ci * c, c), cols]
    mid = ball_ref[pl.ds(ci * c + c // 2 - 1, 1), cols]
    bc = ball_ref[pl.ds(ci * c + c - 1, 1), cols]
    return b, jnp.exp(jnp.minimum(b - mid, HG_EXP_CLAMP)), jnp.exp(jnp.minimum(mid - b, HG_EXP_CLAMP)), bc


def _hg_gate(o, zg, gn):
    return o * lax.rsqrt(jnp.mean(o * o, axis=-1, keepdims=True) + EPS) * gn * (zg * _sigmoid(zg))


def _hgrn2_fwd(proj, lb, gn, name, ex=None):
    t = proj.shape[0]
    bs_tok = min(HG_BLOCK, t)
    n_chunks = bs_tok // HG_CHUNK
    w = HG_HEADS * HG_DK

    def body(hq_ref, hf_ref, hi_ref, hg_ref, lb_ref, gn_ref, o_ref, og_ref, sall_ref, ball_ref, aall_ref, st_ref):
        @pl.when(pl.program_id(0) == 0)
        def _():
            st_ref[...] = jnp.zeros_like(st_ref)

        tril, _, causal2, lane_hi, same_head = _hg_consts()

        for ci in range(n_chunks):
            rows = pl.ds(ci * HG_CHUNK, HG_CHUNK)
            for p in range(HG_HEADS // 2):
                cols = slice(p * HG_PAIR, (p + 1) * HG_PAIR)
                v = hi_ref[rows, cols]
                zg = hg_ref[rows, cols]
                _, q, _, _, logf, k = _hg_gates(hq_ref[rows, cols], hf_ref[rows, cols], lb_ref[:, cols])
                ball_ref[rows, cols] = _cumsum_rows(tril, logf)
                b, em, en, bc = _hg_decays(ball_ref, ci, cols)
                st0 = st_ref[p]
                sall_ref[ci, 2 * p] = st0[:HG_DK, :HG_DK]
                sall_ref[ci, 2 * p + 1] = st0[HG_DK:, HG_DK:]
                vb = v.astype(BF16)
                o = _dot((q * jnp.exp(b)).astype(BF16), st0.astype(BF16), NT)
                a = jnp.where(causal2, _dot((q * em).astype(BF16), _head_rows((k * en).astype(BF16), lane_hi), NT),
                              0.0).astype(BF16)
                aall_ref[ci, p] = a
                o = o + _dot(a, _head_rows(vb, lane_hi), NN)
                kdec = (k * jnp.exp(bc - b)).astype(BF16)
                st_ref[p] = st0 * jnp.exp(bc) + jnp.where(same_head, _dot(vb, kdec, TN), 0.0)
                o_ref[rows, cols] = o
                for hh in range(2):
                    sl = slice(hh * HG_DK, (hh + 1) * HG_DK)
                    hcols = slice(p * HG_PAIR + hh * HG_DK, p * HG_PAIR + (hh + 1) * HG_DK)
                    og_ref[rows, hcols] = _hg_gate(o[:, sl], zg[:, sl], gn_ref[:, hcols]).astype(BF16)

    def col(j):
        return pl.BlockSpec((bs_tok, w), lambda n, j=j: (n, j))

    vec = pl.BlockSpec((1, w), lambda n: (0, 0))
    ex_in, ex_in_specs, ex_out, ex_out_specs, ex_scratch = _carried(ex)
    outs = pl.pallas_call(
        _carry(body, ex, 6, 5, 1, (t // bs_tok,)), name=name, grid=(t // bs_tok,),
        in_specs=[col(COL_HQ // w), col(COL_HF // w), col(COL_HI // w), col(COL_HG // w), vec, vec] + ex_in_specs,
        out_specs=[col(0), col(0),
                   pl.BlockSpec((n_chunks, HG_HEADS, HG_DK, HG_DK), lambda n: (n, 0, 0, 0)), col(0),
                   pl.BlockSpec((n_chunks, HG_HEADS // 2, HG_CHUNK, 2 * HG_CHUNK), lambda n: (n, 0, 0, 0))] + ex_out_specs,
        out_shape=[jax.ShapeDtypeStruct((t, w), F32), jax.ShapeDtypeStruct((t, w), BF16),
                   jax.ShapeDtypeStruct((t // HG_CHUNK, HG_HEADS, HG_DK, HG_DK), F32), jax.ShapeDtypeStruct((t, w), F32),
                   jax.ShapeDtypeStruct((t // HG_CHUNK, HG_HEADS // 2, HG_CHUNK, 2 * HG_CHUNK), BF16)] + ex_out,
        scratch_shapes=[pltpu.VMEM((HG_HEADS // 2, HG_PAIR, HG_PAIR), F32)] + ex_scratch,
        compiler_params=_cp("arbitrary"))(proj, proj, proj, proj, lb, gn, *ex_in)
    return outs[:5], outs[5:]


def _hgrn2_bwd(proj, lb, gn, o_hg, sall, ball, aall, dog, name, ex=None):
    t = proj.shape[0]
    bs_tok = min(HG_BLOCK, t)
    n_chunks = bs_tok // HG_CHUNK
    n_blocks = t // bs_tok
    w = HG_HEADS * HG_DK

    def body(hq_ref, hf_ref, hi_ref, hg_ref, lb_ref, gn_ref, o_ref, sall_ref, snext_ref, ball_ref, aall_ref, dog_ref,
             da_ref, dlb_ref, dgn_ref, dst_ref):
        @pl.when(pl.program_id(0) == 0)
        def _():
            dst_ref[...] = jnp.zeros_like(dst_ref)
            dlb_ref[...] = jnp.zeros_like(dlb_ref)
            dgn_ref[...] = jnp.zeros_like(dgn_ref)

        _, rev_tril, causal2, lane_hi, same_head = _hg_consts()
        zero_block = jnp.zeros((HG_DK, HG_DK), F32)

        for ci in reversed(range(n_chunks)):
            rows = pl.ds(ci * HG_CHUNK, HG_CHUNK)
            for p in range(HG_HEADS // 2):
                cols = slice(p * HG_PAIR, (p + 1) * HG_PAIR)
                zq = hq_ref[rows, cols]
                v = hi_ref[rows, cols]
                zg = hg_ref[rows, cols]
                lbv = lb_ref[:, cols]
                sq, q, sg, f, _, k = _hg_gates(zq, hf_ref[rows, cols], lbv)
                b, em, en, bc = _hg_decays(ball_ref, ci, cols)
                st0 = jnp.concatenate([jnp.concatenate([sall_ref[ci, 2 * p], zero_block], axis=1),
                                       jnp.concatenate([zero_block, sall_ref[ci, 2 * p + 1]], axis=1)], axis=0)
                dst1 = dst_ref[p]
                vb = v.astype(BF16)
                eb = jnp.exp(b)
                qg = (q * eb).astype(BF16)
                qt = (q * em).astype(BF16)
                kref = (k * en).astype(BF16)
                ebcb = jnp.exp(bc - b)
                kdec = (k * ebcb).astype(BF16)
                ebc = jnp.exp(bc)
                dos, dzgs, dgns = [], [], []
                for hh in range(2):
                    sl = slice(hh * HG_DK, (hh + 1) * HG_DK)
                    hcols = slice(p * HG_PAIR + hh * HG_DK, p * HG_PAIR + (hh + 1) * HG_DK)
                    _, gate_vjp = jax.vjp(_hg_gate, o_ref[rows, hcols], zg[:, sl], gn_ref[:, hcols])
                    do_h, dzg_h, dgn_h = gate_vjp(dog_ref[rows, hcols])
                    dos.append(do_h)
                    dzgs.append(dzg_h)
                    dgns.append(dgn_h)
                dob = jnp.concatenate(dos, axis=1).astype(BF16)
                vrows, krows = _head_rows(vb, lane_hi), _head_rows(kref, lane_hi)
                dam = jnp.where(causal2, _dot(dob, vrows, NT), 0.0).astype(BF16)
                dk = ebcb * _dot(vb, dst1.astype(BF16), NN) + en * _own_rows(_dot(dam, qt, TN), lane_hi)
                dq = eb * _dot(dob, st0.astype(BF16), NN) + em * _dot(dam, krows, NN)
                dv = _own_rows(_dot(aall_ref[ci, p], dob, TN), lane_hi) + _dot(kdec, dst1.astype(BF16), NT)
                dst_ref[p] = dst1 * ebc + jnp.where(same_head, _dot(dob, qg, TN), 0.0)

                after = [sall_ref[ci + 1, 2 * p + hh] if ci + 1 < n_chunks else snext_ref[0, 2 * p + hh] for hh in range(2)]
                dbx = jnp.concatenate(
                    [jnp.sum(dst1[hh * HG_DK:(hh + 1) * HG_DK, hh * HG_DK:(hh + 1) * HG_DK] * after[hh], axis=0, keepdims=True)
                     for hh in range(2)], axis=1)
                dlogf = _cumsum_rows(rev_tril, q * dq - k * dk) + dbx
                df = jnp.where(f > MIN_F, dlogf / f, 0.0) - dk
                dzf = df * (1.0 - lbv) * sg * (1.0 - sg)
                dzq = dq * (sq * (1.0 + zq * (1.0 - sq)))
                da_ref[rows, pl.ds(COL_HQ + p * HG_PAIR, HG_PAIR)] = dzq.astype(BF16)
                da_ref[rows, pl.ds(COL_HF + p * HG_PAIR, HG_PAIR)] = dzf.astype(BF16)
                da_ref[rows, pl.ds(COL_HI + p * HG_PAIR, HG_PAIR)] = dv.astype(BF16)
                da_ref[rows, pl.ds(COL_HG + p * HG_PAIR, HG_PAIR)] = jnp.concatenate(dzgs, axis=1).astype(BF16)
                dlb_ref[:, cols] += jnp.sum(df * (1.0 - sg), axis=0, keepdims=True)
                dgn_ref[:, cols] += jnp.concatenate(dgns, axis=1)

    def col(j):
        return pl.BlockSpec((bs_tok, w), lambda n, j=j: (n_blocks - 1 - n, j))

    vec = pl.BlockSpec((1, w), lambda n: (0, 0))
    ex_in, ex_in_specs, ex_out, ex_out_specs, ex_scratch = _carried(ex)
    outs = pl.pallas_call(
        _carry(body, ex, 12, 3, 1, (n_blocks,)), name=name, grid=(n_blocks,),
        in_specs=[col(COL_HQ // w), col(COL_HF // w), col(COL_HI // w), col(COL_HG // w), vec, vec, col(0),
                  pl.BlockSpec((n_chunks, HG_HEADS, HG_DK, HG_DK), lambda n: (n_blocks - 1 - n, 0, 0, 0)),
                  pl.BlockSpec((1, HG_HEADS, HG_DK, HG_DK),
                               lambda n: (jnp.minimum((n_blocks - n) * n_chunks, t // HG_CHUNK - 1), 0, 0, 0)),
                  col(0),
                  pl.BlockSpec((n_chunks, HG_HEADS // 2, HG_CHUNK, 2 * HG_CHUNK), lambda n: (n_blocks - 1 - n, 0, 0, 0)),
                  col(0)] + ex_in_specs,
        out_specs=[pl.BlockSpec((bs_tok, 4 * w), lambda n: (n_blocks - 1 - n, 0)), vec, vec] + ex_out_specs,
        out_shape=[jax.ShapeDtypeStruct((t, 4 * w), BF16), jax.ShapeDtypeStruct((1, w), F32),
                   jax.ShapeDtypeStruct((1, w), F32)] + ex_out,
        scratch_shapes=[pltpu.VMEM((HG_HEADS // 2, HG_PAIR, HG_PAIR), F32)] + ex_scratch,
        compiler_params=_cp("arbitrary"))(proj, proj, proj, proj, lb, gn, o_hg, sall, sall, ball, aall, dog, *ex_in)
    return outs[:3], outs[3:]


def _rope_tables(t):
    half = ROPE_DIM // 2
    inv = ROPE_THETA ** (-jnp.arange(half, dtype=F32) * 2.0 / ROPE_DIM)
    d = jnp.arange(2 * ATT_HEAD_DIM) % ATT_HEAD_DIM
    ang = jnp.arange(t).astype(F32)[:, None] * inv[d % half][None, :]
    cos, sin = jnp.cos(ang), jnp.sin(ang)
    c = jnp.where(d < ROPE_DIM, cos, 1.0)
    su = jnp.where(d < half, -sin, 0.0)
    sd = jnp.where((d >= half) & (d < ROPE_DIM), sin, 0.0)
    return c, su, sd


def _rope(x, tabs):
    c, su, sd = tabs
    n = x.shape[1]
    half = ROPE_DIM // 2
    return x * c + pltpu.roll(x, n - half, 1) * su + pltpu.roll(x, half, 1) * sd


def _rope_t(dy, tabs):
    c, su, sd = tabs
    n = dy.shape[1]
    half = ROPE_DIM // 2
    return dy * c + pltpu.roll(dy * su, half, 1) + pltpu.roll(dy * sd, n - half, 1)


def _swa_specs(n_blocks, clamp):
    blk = ATT_BLOCK

    def cur(n):
        return jnp.minimum(n, n_blocks - 1) if clamp else n

    def prev(n):
        return jnp.maximum(cur(n) - 1, 0)

    q_spec = pl.BlockSpec((blk, 512), lambda m, n: (cur(n), COL_AQ // 512 + m))
    kv = [pl.BlockSpec((blk, 128), lambda m, n, c=c, f=f: (f(n), c + m))
          for c in (COL_AK // 128, COL_AV // 128) for f in (cur, prev)]
    tabs = [pl.BlockSpec((blk, 128), lambda m, n, f=f: (f(n), 0)) for f in (cur, prev) for _ in range(3)]
    return q_spec, kv, tabs, cur, prev


ATT_SCALE = ATT_HEAD_DIM ** -0.5


def _head_halves(x, upper):
    zero = jnp.zeros_like(x)
    return jnp.concatenate([jnp.where(upper, zero, x), jnp.where(upper, x, zero)], axis=0)


def _swa_scores(scores_t, sink, mask_t):
    s = jnp.where(mask_t, scores_t, -jnp.inf)
    mx = jnp.maximum(jnp.max(s, axis=0, keepdims=True), sink)
    p = jnp.exp(s - mx)
    es = jnp.exp(sink - mx)
    rinv = 1.0 / (jnp.sum(p, axis=0, keepdims=True) + es)
    return p * rinv, es * rinv


def _swa_window(kc_ref, kp_ref, vc_ref, vp_ref, tabs_c, tabs_p, n):
    k2 = jnp.concatenate([_rope(kp_ref[...], tabs_p), _rope(kc_ref[...], tabs_c)], axis=0)
    v2 = jnp.concatenate([vp_ref[...], vc_ref[...]], axis=0)
    blk = ATT_BLOCK
    kj = lax.broadcasted_iota(jnp.int32, (2 * blk, blk), 0)
    qi = lax.broadcasted_iota(jnp.int32, (2 * blk, blk), 1)
    delta = qi + blk - kj
    mask_t = (delta >= 0) & (delta < blk) & ((kj >= blk) | (n > 0))
    return k2, v2, mask_t


def _swa_fwd(proj, sinks, tabs, name, ex=None):
    t = proj.shape[0]
    n_blocks = t // ATT_BLOCK
    q_spec, kv_specs, tab_specs, _, _ = _swa_specs(n_blocks, clamp=False)

    def body(q_ref, kc_ref, kp_ref, vc_ref, vp_ref, c0, c1, c2, p0, p1, p2, sink_ref, o_ref):
        m, n = pl.program_id(0), pl.program_id(1)
        tabs_c = (c0[...], c1[...], c2[...])
        tabs_p = (p0[...], p1[...], p2[...])
        k2, v2, mask_t = _swa_window(kc_ref, kp_ref, vc_ref, vp_ref, tabs_c, tabs_p, n)
        k2r, v2r = pltpu.roll(k2, 64, 1), pltpu.roll(v2, 64, 1)
        upper_k = lax.broadcasted_iota(jnp.int32, k2.shape, 1) >= 64
        upper_q = lax.broadcasted_iota(jnp.int32, (ATT_BLOCK, 128), 1) >= 64
        for jj in range(2):
            own = upper_k if jj else ~upper_k
            kd = jnp.where(own, k2, k2r).astype(BF16)
            vd = jnp.where(own, v2, v2r).astype(BF16)
            for pi in range(2):
                cols = slice(256 * jj + 128 * pi, 256 * jj + 128 * pi + 128)
                qp = _rope(q_ref[:, cols], tabs_c) * ATT_SCALE
                outs = []
                for e in range(2):
                    sink = sink_ref[0, 8 * m + 4 * jj + 2 * pi + e]
                    qm = jnp.where(upper_q if e else ~upper_q, qp, 0.0).astype(BF16)
                    pn, _ = _swa_scores(_dot(kd, qm, NT), sink, mask_t)
                    outs.append(_dot(pn.astype(BF16), vd, TN))
                o_ref[:, cols] = jnp.where(upper_q, outs[1], outs[0]).astype(BF16)

    ex_in, ex_in_specs, ex_out, ex_out_specs, ex_scratch = _carried(ex)
    outs = pl.pallas_call(
        _carry(body, ex, 12, 1, 0, (2, n_blocks)), name=name, grid=(2, n_blocks),
        in_specs=[q_spec] + kv_specs + tab_specs + [pl.BlockSpec(memory_space=pltpu.SMEM)] + ex_in_specs,
        out_specs=[pl.BlockSpec((ATT_BLOCK, 512), lambda m, n: (n, m))] + ex_out_specs,
        out_shape=[jax.ShapeDtypeStruct((t, ATT_Q_HEADS * ATT_HEAD_DIM), BF16)] + ex_out,
        scratch_shapes=ex_scratch,
        compiler_params=_cp("arbitrary", "arbitrary"))(proj, proj, proj, proj, proj, *tabs, *tabs, sinks, *ex_in)
    return outs[0], outs[1:]


def _swa_bwd(proj, sinks, tabs, o_att, do_att, name, ex=None):
    t = proj.shape[0]
    n_blocks = t // ATT_BLOCK
    blk = ATT_BLOCK
    q_spec, kv_specs, tab_specs, cur, prev = _swa_specs(n_blocks, clamp=True)

    def body(q_ref, kc_ref, kp_ref, vc_ref, vp_ref, c0, c1, c2, p0, p1, p2, sink_ref, o_ref, do_ref,
             dq_ref, dk_ref, dv_ref, ds_ref, ck_ref, cv_ref):
        m, n = pl.program_id(0), pl.program_id(1)

        @pl.when(n == 0)
        def _():
            ds_ref[...] = jnp.zeros_like(ds_ref)
            ck_ref[...] = jnp.zeros_like(ck_ref)
            cv_ref[...] = jnp.zeros_like(cv_ref)

        @pl.when(n < n_blocks)
        def _():
            tabs_c = (c0[...], c1[...], c2[...])
            tabs_p = (p0[...], p1[...], p2[...])
            k2, v2, mask_t = _swa_window(kc_ref, kp_ref, vc_ref, vp_ref, tabs_c, tabs_p, n)
            k2r, v2r = pltpu.roll(k2, 64, 1), pltpu.roll(v2, 64, 1)
            upper_k = lax.broadcasted_iota(jnp.int32, k2.shape, 1) >= 64
            lane = lax.broadcasted_iota(jnp.int32, (8, 128), 1)
            head_of_row = lax.broadcasted_iota(jnp.int32, (16, 128), 0) >= 8
            head_rows = (head_of_row == (lax.broadcasted_iota(jnp.int32, (16, 128), 1) >= 64)).astype(F32)
            dk2 = jnp.zeros(k2.shape, F32)
            dv2 = jnp.zeros(k2.shape, F32)
            dsv = jnp.zeros((8, 128), F32)
            nk = 2 * blk
            for jj in range(2):
                own = upper_k if jj else ~upper_k
                kh = _head_halves(jnp.where(own, k2, k2r).astype(BF16), upper_k)
                vh = _head_halves(jnp.where(own, v2, v2r).astype(BF16), upper_k)
                dkd = jnp.zeros(k2.shape, F32)
                dvd = jnp.zeros(k2.shape, F32)
                for pi in range(2):
                    cols = slice(256 * jj + 128 * pi, 256 * jj + 128 * pi + 128)
                    qp = (_rope(q_ref[:, cols], tabs_c) * ATT_SCALE).astype(BF16)
                    do_pair = do_ref[:, cols]
                    dob = do_pair.astype(BF16)
                    delta2 = lax.dot_general(head_rows, do_pair * o_ref[:, cols].astype(F32), ((NT), ((), ())),
                                             precision=lax.Precision.HIGHEST, preferred_element_type=F32)
                    st = _dot(kh, qp, NT)
                    dpt = _dot(vh, dob, NT)
                    pns, dss = [], []
                    for e in range(2):
                        hl = 4 * jj + 2 * pi + e
                        pn, ps = _swa_scores(st[e * nk:(e + 1) * nk], sink_ref[0, 8 * m + hl], mask_t)
                        delta = jnp.max(delta2[8 * e:8 * e + 8], axis=0, keepdims=True)
                        pns.append(pn.astype(BF16))
                        dss.append((pn * (dpt[e * nk:(e + 1) * nk] - delta)).astype(BF16))
                        dsv = dsv + jnp.where(lane == hl, -jnp.sum(ps * delta), 0.0)
                    dsb = jnp.concatenate(dss, axis=0)
                    dq_ref[:, cols] = _rope_t(_dot(dsb, kh, TN) * ATT_SCALE, tabs_c).astype(BF16)
                    rk = _dot(dsb, qp, NN)
                    rv = _dot(jnp.concatenate(pns, axis=0), dob, NN)
                    dkd = dkd + jnp.where(upper_k, rk[nk:], rk[:nk])
                    dvd = dvd + jnp.where(upper_k, rv[nk:], rv[:nk])
                dk2 = dk2 + jnp.where(own, dkd + pltpu.roll(dkd, 64, 1), 0.0)
                dv2 = dv2 + jnp.where(own, dvd + pltpu.roll(dvd, 64, 1), 0.0)
            dk_ref[...] = (ck_ref[...] + _rope_t(dk2[:blk], tabs_p)).astype(BF16)
            dv_ref[...] = (cv_ref[...] + dv2[:blk]).astype(BF16)
            ck_ref[...] = _rope_t(dk2[blk:], tabs_c)
            cv_ref[...] = dv2[blk:]
            ds_ref[...] += dsv

        @pl.when(n == n_blocks)
        def _():
            dk_ref[...] = ck_ref[...].astype(BF16)
            dv_ref[...] = cv_ref[...].astype(BF16)

    wide = pl.BlockSpec((blk, 512), lambda m, n: (cur(n), m))
    lagged = pl.BlockSpec((blk, 128), lambda m, n: (jnp.maximum(n - 1, 0), m))
    ex_in, ex_in_specs, ex_out, ex_out_specs, ex_scratch = _carried(ex)
    outs = pl.pallas_call(
        _carry(body, ex, 14, 4, 2, (2, n_blocks + 1)), name=name, grid=(2, n_blocks + 1),
        in_specs=[q_spec] + kv_specs + tab_specs + [pl.BlockSpec(memory_space=pltpu.SMEM), wide, wide] + ex_in_specs,
        out_specs=[wide, lagged, lagged, pl.BlockSpec((None, 8, 128), lambda m, n: (m, 0, 0))] + ex_out_specs,
        out_shape=[jax.ShapeDtypeStruct((t, 1024), BF16), jax.ShapeDtypeStruct((t, 256), BF16),
                   jax.ShapeDtypeStruct((t, 256), BF16), jax.ShapeDtypeStruct((2, 8, 128), F32)] + ex_out,
        scratch_shapes=[pltpu.VMEM((blk, 128), F32), pltpu.VMEM((blk, 128), F32)] + ex_scratch,
        compiler_params=_cp("arbitrary", "arbitrary"))(proj, proj, proj, proj, proj, *tabs, *tabs, sinks, o_att, do_att,
                                                       *ex_in)
    return outs[:4], outs[4:]


def _local_step(x, target, shards, norm1, lb_logits, hg_norm, attn_sinks, norm2, final_norm):
    t = x.shape[0]
    tabs = _rope_tables(t)
    lb_all = _lb_fwd(lb_logits)
    saved = []

    def shards_of(l):
        return {ti: shards[ti][l] for ti in ALL_KINDS}

    win_next = (_gather_two_level(shards[0][0], 0, "gather_w_in"),)
    rest_next = None
    for l in range(DEPTH):
        n1, n2 = norm1[l][None, :], norm2[l][None, :]
        lb, gn, sinks = lb_all[l][None, :], hg_norm[l][None, :], attn_sinks[l][None, :]
        (win_t,) = win_next
        if l == 0:
            h = _rms_fwd(x, n1, "rms1_fwd")
        if l == 0:
            proj, rest_next = _matmul_nt(h, win_t, 0, IN_COLS, F32, "proj_fwd", tn=1280,
                                         ex=_gather_exchange(shards_of(0), KINDS_REST))
        else:
            proj = _matmul_nt(h, win_t, 0, IN_COLS, F32, "proj_fwd", tn=1280)
        w_pa, w_pb, w_o, wgu_t, w_d = rest_next
        more = l + 1 < DEPTH
        (o_hg, o_g, sall, ball, aall), rest_next = _hgrn2_fwd(
            proj, lb, gn, "hgrn2_fwd", _gather_exchange(shards_of(l + 1), KINDS_REST) if more else None)
        o_att, win_next = _swa_fwd(
            proj, sinks, tabs, "swa_fwd", _gather_exchange(shards_of(l + 1), KINDS_W_IN) if more else None)
        ya, yb, mix, h2, x1 = _merge_fwd(o_g, o_att, proj, x, w_pa, w_pb, w_o, n2, "merge_fwd")
        gu, act = _ffn_up_fwd(h2, wgu_t, "ffn_up_fwd")
        if more:
            x2, h_next = _matmul_nn(act, w_d, 0, x1, "wd_fwd", gain=norm1[l + 1][None, :])
        else:
            x2, h_next = _matmul_nn(act, w_d, 0, x1, "wd_fwd_last"), None
        saved.append((x, h, proj, o_hg, o_g, (sall, ball, aall), o_att, ya, yb, mix, x1, h2, gu, act, n1, n2, lb, gn, sinks,
                      (win_t, w_pa, w_pb, w_o, wgu_t, w_d)))
        x, h = x2, h_next

    dx, d_fn, loss = _loss_head(x, final_norm[None, :], target, "loss_head")

    owned = [None] * DEPTH
    pending = None
    d_n1, d_n2, d_lb, d_gn, d_sinks = ([None] * DEPTH for _ in range(5))
    for l in reversed(range(DEPTH)):
        x0, h, proj, o_hg, o_g, hg_saved, o_att, ya, yb, mix, x1, h2, gu, act, n1, n2, lb, gn, sinks, weights = saved[l]
        win_t, w_pa, w_pb, w_o, wgu_t, w_d = weights
        dgu = _ffn_down_bwd(dx, w_d, gu, "ffn_down_bwd")
        g_wd = _matmul_tn(act, dx, "wd_grad", tm=1408)
        g_wgu = _matmul_tn(dgu, h2, "wgu_grad", tm=1408)
        late = pending is not None
        dx1, d_n2[l], *land_w = _rows_bwd([dgu], wgu_t, x1, n2, dx, "ffn_up_bwd", tm=512,
                                          ex=_scatter_exchange(pending, KINDS_W_IN) if late else None)
        g_wo = _matmul_tn(mix, dx1, "wo_grad")
        dya, dyb, dgab, dog, doatt = _merge_bwd(dx1, ya, yb, proj, w_pa, w_pb, w_o, "merge_bwd")
        g_wpa = _matmul_tn(o_g, dya, "wpa_grad")
        g_wpb = _matmul_tn(o_att, dyb, "wpb_grad")
        (dhg, d_lb[l], d_gn[l]), land_r = _hgrn2_bwd(proj, lb, gn, o_hg, *hg_saved, dog, "hgrn2_bwd",
                                                     _scatter_exchange(pending, KINDS_REST) if late else None)
        if late:
            owned[l + 1] = jnp.concatenate([_sum_slots(land_w[0][0], "sum_slots_w_in"),
                                            _sum_slots(land_r[0], "sum_slots_rest")], axis=0)
        ex = _scatter_exchange((None, g_wpa, g_wpb, g_wo, g_wgu, g_wd), KINDS_REST) if l == 0 else None
        (daq, dak, dav, d_sinks[l]), land_rest = _swa_bwd(proj, sinks, tabs, o_att, doatt, "swa_bwd", ex)
        g_win = None
        for piece, off, tm, tag in ((dhg, COL_HQ, 512, "hg"), (daq, COL_AQ, 512, "aq"), (dak, COL_AK, 256, "ak"),
                                    (dav, COL_AV, 256, "av"), (dgab, COL_GA, 512, "gates")):
            g_win = _matmul_tn(piece, h, "win_grad_" + tag, tm=tm, rows=IN_COLS, row_off=off, into=g_win)
        if l > 0:
            dx, d_n1[l] = _rows_bwd([dhg, daq, dak, dav, dgab], win_t, x0, n1, dx1, "win_bwd")
        else:
            dx, d_n1[l], land_win = _rows_bwd([dhg, daq, dak, dav, dgab], win_t, x0, n1, dx1, "win_bwd",
                                              ex=_scatter_exchange((g_win,), KINDS_W_IN))
        pending = (g_win, g_wpa, g_wpb, g_wo, g_wgu, g_wd)
    owned[0] = jnp.concatenate([_sum_slots(land_win[0], "sum_slots_w_in"), _sum_slots(land_rest[0], "sum_slots_rest")],
                               axis=0)

    d_sink_rows = [jnp.concatenate([d[0, 0, :8], d[1, 0, :8]]) for d in d_sinks]
    small = (jnp.concatenate(d_n1, axis=0), jnp.concatenate(d_lb, axis=0), jnp.concatenate(d_gn, axis=0),
             jnp.concatenate(d_n2, axis=0), d_fn, jnp.stack(d_sink_rows, axis=0))
    return loss, dx, jnp.stack(owned, axis=0), small


def _sum_slots(land, name, tr=480):
    _, rows, d = land.shape

    def body(l_ref, o_ref):
        acc = l_ref[0].astype(F32)
        for k in range(1, N_DEV):
            acc = acc + l_ref[k].astype(F32)
        o_ref[...] = acc

    return pl.pallas_call(
        body, name=name, grid=(rows // tr,),
        in_specs=[pl.BlockSpec((N_DEV, tr, d), lambda i: (0, i, 0))],
        out_specs=pl.BlockSpec((tr, d), lambda i: (i, 0)),
        out_shape=jax.ShapeDtypeStruct((rows, d), F32),
        compiler_params=_cp("parallel"))(land)


def _adamw(w, g, m, v, name):
    shape = w.shape
    c = shape[-1]
    rows = w.size // c
    tr = rows
    for cand in (512, 352, 128):
        if rows % cand == 0:
            tr = cand
            break
    c1 = 1.0 / (1.0 - ADAM_B1 ** ADAM_STEP)
    c2 = 1.0 / (1.0 - ADAM_B2 ** ADAM_STEP)

    def body(w_ref, g_ref, m_ref, v_ref, d_ref, nm_ref, nv_ref):
        gv = g_ref[...]
        nm = ADAM_B1 * m_ref[...] + (1.0 - ADAM_B1) * gv
        nv = ADAM_B2 * v_ref[...] + (1.0 - ADAM_B2) * (gv * gv)
        d_ref[...] = -ADAM_LR * ((nm * c1) / (jnp.sqrt(nv * c2) + ADAM_EPS) + ADAM_WD * w_ref[...])
        nm_ref[...] = nm
        nv_ref[...] = nv

    spec = pl.BlockSpec((tr, c), lambda i: (i, 0))
    outs = pl.pallas_call(
        body, name=name, grid=(rows // tr,), in_specs=[spec] * 4, out_specs=[spec] * 3,
        out_shape=[jax.ShapeDtypeStruct((rows, c), F32)] * 3,
        compiler_params=_cp("parallel"))(*[a.reshape(rows, c) for a in (w, g, m, v)])
    return tuple(o.reshape(shape) for o in outs)


def kernel(x, norm1, w_in, lb_logits, hg_norm, attn_sinks, w_pa, w_pb, w_o, norm2, w_gate, w_up, w_down, final_norm, loss_target, m_norm1, m_w_in, m_lb_logits, m_hg_norm, m_attn_sinks, m_w_pa, m_w_pb, m_w_o, m_norm2, m_w_gate, m_w_up, m_w_down, m_final_norm, v_norm1, v_w_in, v_lb_logits, v_hg_norm, v_attn_sinks, v_w_pa, v_w_pb, v_w_o, v_norm2, v_w_gate, v_w_up, v_w_down, v_final_norm):
    t = x.shape[1]
    shards = [jnp.swapaxes(w_in, 1, 2).astype(BF16), w_pa.astype(BF16), w_pb.astype(BF16), w_o.astype(BF16),
              jnp.swapaxes(w_gate, 1, 2).astype(BF16), jnp.swapaxes(w_up, 1, 2).astype(BF16), w_down.astype(BF16)]
    loss_lanes, grad_x, owned, small = _local_step(
        x.reshape(t, D_MODEL), loss_target.reshape(t, D_MODEL), shards,
        norm1, lb_logits, hg_norm, attn_sinks, norm2, final_norm)

    def rows_of(ti, transpose):
        g = owned[:, SLOT_OFF[ti]:SLOT_OFF[ti] + SHARD_ROWS[ti], :]
        return jnp.swapaxes(g, 1, 2) if transpose else g

    g_big = {"w_in": rows_of(0, True), "w_pa": rows_of(1, False), "w_pb": rows_of(2, False), "w_o": rows_of(3, False),
             "w_gate": rows_of(4, True), "w_up": rows_of(5, True), "w_down": rows_of(6, False)}

    d_n1, d_lb, d_gn, d_n2, d_fn, d_sinks = small
    pad = jnp.zeros((DEPTH, D_MODEL - ATT_Q_HEADS), F32)
    packed = jnp.concatenate([
        d_n1, d_lb, d_gn, d_n2, d_fn, jnp.concatenate([d_sinks, pad], axis=1),
        jnp.concatenate([loss_lanes, jnp.zeros((1, D_MODEL - 128), F32)], axis=1),
        jnp.zeros((SMALL_ROWS - 22, D_MODEL), F32)], axis=0)
    total = _all_reduce_small(packed)
    loss = total[21, 0]
    g_small = {"norm1": total[0:4], "lb_logits": _lb_bwd(lb_logits, total[4:8]), "hg_norm": total[8:12],
               "norm2": total[12:16], "final_norm": total[16], "attn_sinks": total[17:21, :ATT_Q_HEADS]}

    params = {"norm1": (norm1, m_norm1, v_norm1), "w_in": (w_in, m_w_in, v_w_in),
              "lb_logits": (lb_logits, m_lb_logits, v_lb_logits), "hg_norm": (hg_norm, m_hg_norm, v_hg_norm),
              "attn_sinks": (attn_sinks, m_attn_sinks, v_attn_sinks), "w_pa": (w_pa, m_w_pa, v_w_pa),
              "w_pb": (w_pb, m_w_pb, v_w_pb), "w_o": (w_o, m_w_o, v_w_o), "norm2": (norm2, m_norm2, v_norm2),
              "w_gate": (w_gate, m_w_gate, v_w_gate), "w_up": (w_up, m_w_up, v_w_up),
              "w_down": (w_down, m_w_down, v_w_down), "final_norm": (final_norm, m_final_norm, v_final_norm)}
    order = ["norm1", "w_in", "lb_logits", "hg_norm", "attn_sinks", "w_pa", "w_pb", "w_o", "norm2",
             "w_gate", "w_up", "w_down", "final_norm"]
    grads, deltas, new_m, new_v = [], [], [], []
    for name in order:
        w, m, v = params[name]
        g = (g_big[name] if name in g_big else g_small[name]).reshape(w.shape)
        w2 = w.reshape(1, -1) if w.ndim == 1 else w
        d, nm, nv = _adamw(w2, g.reshape(w2.shape), m.reshape(w2.shape), v.reshape(w2.shape), "adamw_" + name)
        grads.append(g)
        deltas.append(d.reshape(w.shape))
        new_m.append(nm.reshape(w.shape))
        new_v.append(nv.reshape(w.shape))
    return (loss, grad_x.reshape(x.shape), *grads, *deltas, *new_m, *new_v)
```

```python
import functools
from typing import Callable, NamedTuple

import jax
import jax.numpy as jnp
from jax import lax
from jax.experimental import pallas as pl
from jax.experimental.pallas import tpu as pltpu

F32, BF16 = jnp.float32, jnp.bfloat16

D_MODEL = 1024
DEPTH = 4
N_DEV = 8
HG_HEADS = 8
HG_DK = 128
HG_CHUNK = 64
HG_BLOCK = 512
HG_EXP_CLAMP = 60.0
ATT_Q_HEADS = 16
ATT_HEAD_DIM = 64
ATT_BLOCK = 128
ROPE_THETA = 500000.0
ROPE_DIM = 16
FFN_HIDDEN = 2816
EPS = 1e-6
MIN_F = 1e-30
ADAM_LR, ADAM_B1, ADAM_B2, ADAM_EPS, ADAM_WD, ADAM_STEP = 0.001, 0.9, 0.999, 1e-08, 0.01, 10

COL_HQ, COL_HF, COL_HI, COL_HG = 0, 1024, 2048, 3072
COL_AQ, COL_AK, COL_AV, COL_GA, COL_GB = 4096, 5120, 5376, 5632, 6656
IN_COLS = 7680

SHARD_ROWS = (960, 128, 128, 128, 352, 352, 352)
SLOT_OFF = (0, 960, 1088, 1216, 1344, 1696, 2048)
SLOT_ROWS = 2400
SMALL_ROWS = 24

VMEM_LIMIT_BYTES = 56 * 1024 * 1024

NN = ((1,), (0,))
NT = ((1,), (1,))
TN = ((0,), (0,))


def _dot(a, b, dims):
    return lax.dot_general(a, b, (dims, ((), ())), preferred_element_type=F32)


def _cp(*sem):
    return pltpu.CompilerParams(dimension_semantics=sem if sem else None, vmem_limit_bytes=VMEM_LIMIT_BYTES)


def _sigmoid(x):
    return 0.5 * jnp.tanh(0.5 * x) + 0.5


def _matmul_nt(a, w, row_off, n, out_dtype, name, tm=1024, tn=512, ex=None):
    t, k = a.shape
    tm = min(tm, t)
    assert n % tn == 0 and row_off % tn == 0 and t % tm == 0
    grid = (n // tn, t // tm)

    def body(a_ref, w_ref, o_ref):
        o_ref[...] = _dot(a_ref[...].astype(BF16), w_ref[...], NT).astype(o_ref.dtype)

    ex_in, ex_in_specs, ex_out, ex_out_specs, ex_scratch = _carried(ex)
    outs = pl.pallas_call(
        _carry(body, ex, 2, 1, 0, grid), name=name, grid=grid,
        in_specs=[pl.BlockSpec((tm, k), lambda j, i: (i, 0)),
                  pl.BlockSpec((tn, k), lambda j, i: (row_off // tn + j, 0))] + ex_in_specs,
        out_specs=[pl.BlockSpec((tm, tn), lambda j, i: (i, j))] + ex_out_specs,
        out_shape=[jax.ShapeDtypeStruct((t, n), out_dtype)] + ex_out,
        scratch_shapes=ex_scratch,
        compiler_params=_cp("arbitrary", "arbitrary") if ex else _cp("parallel", "parallel"))(a, w, *ex_in)
    return (outs[0], outs[1:]) if ex else outs[0]


def _matmul_nn(a, w, row_off, res, name, tm=512, tk=None, gain=None):
    t, k = a.shape
    n = w.shape[1]
    tm = min(tm, t)
    tk = tk or k
    nk = k // tk
    assert k % tk == 0 and row_off % tk == 0 and t % tm == 0

    def body(*refs):
        refs = list(refs)
        a_ref, w_ref = refs[:2]
        r_ref = refs[2] if res is not None else None
        g_ref = refs[2 + (res is not None)] if gain is not None else None
        acc = refs[-1]
        o_ref = refs[-3] if gain is not None else refs[-2]
        kk = pl.program_id(1)
        part = _dot(a_ref[...].astype(BF16), w_ref[...], NN)

        @pl.when(kk == 0)
        def _():
            acc[...] = part

        @pl.when(kk > 0)
        def _():
            acc[...] += part

        @pl.when(kk == nk - 1)
        def _():
            y = acc[...] if res is None else acc[...] + r_ref[...]
            o_ref[...] = y
            if gain is not None:
                refs[-2][...] = _rms(y, g_ref[...]).astype(BF16)

    row = pl.BlockSpec((tm, n), lambda i, kk: (i, 0))
    in_specs = [pl.BlockSpec((tm, tk), lambda i, kk: (i, kk)),
                pl.BlockSpec((tk, n), lambda i, kk: (row_off // tk + kk, 0))]
    args = [a, w]
    if res is not None:
        in_specs.append(row)
        args.append(res)
    if gain is not None:
        in_specs.append(pl.BlockSpec((1, n), lambda i, kk: (0, 0)))
        args.append(gain)
    outs = pl.pallas_call(
        body, name=name, grid=(t // tm, nk), in_specs=in_specs,
        out_specs=[row, row] if gain is not None else [row],
        out_shape=[jax.ShapeDtypeStruct((t, n), F32)] + ([jax.ShapeDtypeStruct((t, n), BF16)] if gain is not None else []),
        scratch_shapes=[pltpu.VMEM((tm, n), F32)],
        compiler_params=_cp("parallel", "arbitrary"))(*args)
    return tuple(outs) if gain is not None else outs[0]


def _matmul_tn(a, b, name, tm=512, tk=2048, rows=None, row_off=0, into=None):
    t, m = a.shape
    n = b.shape[1]
    tk = min(tk, t)
    nk = t // tk
    rows = rows or m
    assert m % tm == 0 and t % tk == 0 and row_off % tm == 0

    def body(*refs):
        a_ref, b_ref, o_ref, acc = refs[0], refs[1], refs[-2], refs[-1]
        kk = pl.program_id(1)
        part = _dot(a_ref[...].astype(BF16), b_ref[...].astype(BF16), TN)

        @pl.when(kk == 0)
        def _():
            acc[...] = part

        @pl.when(kk > 0)
        def _():
            acc[...] += part

        @pl.when(kk == nk - 1)
        def _():
            o_ref[...] = acc[...].astype(BF16)

    return pl.pallas_call(
        body, name=name, grid=(m // tm, nk),
        in_specs=[pl.BlockSpec((tk, tm), lambda i, kk: (kk, i)),
                  pl.BlockSpec((tk, n), lambda i, kk: (kk, 0))] + ([ANY] if into is not None else []),
        out_specs=pl.BlockSpec((tm, n), lambda i, kk: (row_off // tm + i, 0)),
        out_shape=jax.ShapeDtypeStruct((rows, n), BF16),
        scratch_shapes=[pltpu.VMEM((tm, n), F32)],
        input_output_aliases={2: 0} if into is not None else {},
        compiler_params=_cp("parallel", "arbitrary"))(a, b, *([into] if into is not None else []))


def _rms(x, g):
    return x * lax.rsqrt(jnp.mean(x * x, axis=-1, keepdims=True) + EPS) * g


def _rms_fwd(x, g, name, tm=512):
    t, d = x.shape
    tm = min(tm, t)

    def body(x_ref, g_ref, o_ref):
        o_ref[...] = _rms(x_ref[...], g_ref[...]).astype(BF16)

    return pl.pallas_call(
        body, name=name, grid=(t // tm,),
        in_specs=[pl.BlockSpec((tm, d), lambda i: (i, 0)), pl.BlockSpec((1, d), lambda i: (0, 0))],
        out_specs=pl.BlockSpec((tm, d), lambda i: (i, 0)),
        out_shape=jax.ShapeDtypeStruct((t, d), BF16),
        compiler_params=_cp("parallel"))(x, g)


def _mix(ya, yb, ga, gb):
    return _sigmoid(ga) * ya + _sigmoid(gb) * yb


def _gate_specs(tm):
    half = D_MODEL // 2
    return [pl.BlockSpec((tm, half), lambda i, c=c: (i, c))
            for c in (COL_GA // half, COL_GA // half + 1, COL_GB // half, COL_GB // half + 1)]


def _merge_bwd(dx1, ya, yb, proj, w_pa, w_pb, w_o, name, tm=512):
    t, d = dx1.shape
    tm = min(tm, t)

    def body(dx_ref, ya_ref, yb_ref, ga0, ga1, gb0, gb1, wpa_ref, wpb_ref, wo_ref,
             dya_ref, dyb_ref, dg_ref, dog_ref, doa_ref):
        dmix = _dot(dx_ref[...].astype(BF16), wo_ref[...], NT)
        ga = jnp.concatenate([ga0[...], ga1[...]], axis=1)
        gb = jnp.concatenate([gb0[...], gb1[...]], axis=1)
        _, vjp = jax.vjp(_mix, ya_ref[...].astype(F32), yb_ref[...].astype(F32), ga, gb)
        dya, dyb, dga, dgb = vjp(dmix)
        dya, dyb = dya.astype(BF16), dyb.astype(BF16)
        dya_ref[...] = dya
        dyb_ref[...] = dyb
        dg_ref[:, :d] = dga.astype(BF16)
        dg_ref[:, d:] = dgb.astype(BF16)
        dog_ref[...] = _dot(dya, wpa_ref[...], NT)
        doa_ref[...] = _dot(dyb, wpb_ref[...], NT)

    row = pl.BlockSpec((tm, d), lambda i: (i, 0))
    wide = pl.BlockSpec((tm, 2 * d), lambda i: (i, 0))
    mat = pl.BlockSpec((d, d), lambda i: (0, 0))
    return pl.pallas_call(
        body, name=name, grid=(t // tm,), in_specs=[row, row, row] + _gate_specs(tm) + [mat, mat, mat],
        out_specs=[row, row, wide, row, row],
        out_shape=[jax.ShapeDtypeStruct((t, d), BF16), jax.ShapeDtypeStruct((t, d), BF16),
                   jax.ShapeDtypeStruct((t, 2 * d), BF16), jax.ShapeDtypeStruct((t, d), F32),
                   jax.ShapeDtypeStruct((t, d), F32)],
        compiler_params=_cp("parallel"))(dx1, ya, yb, proj, proj, proj, proj, w_pa, w_pb, w_o)


def _swiglu(g, u):
    return g * _sigmoid(g) * u


def _swiglu_bwd(g, u, dact):
    sg = _sigmoid(g)
    gs = g * sg
    return dact * u * (sg + gs * (1.0 - sg)), dact * gs


def _ffn_up_fwd(h2, wgu_t, name, tm=512):
    t, d = h2.shape
    tm = min(tm, t)
    fh = FFN_HIDDEN // 2

    def body(a_ref, w_ref, gu_ref, act_ref):
        r = _dot(a_ref[...], w_ref[...], NT)
        gu_ref[...] = r.astype(BF16)
        act_ref[...] = _swiglu(r[:, :fh], r[:, fh:]).astype(BF16)

    return pl.pallas_call(
        body, name=name, grid=(2, t // tm),
        in_specs=[pl.BlockSpec((tm, d), lambda j, i: (i, 0)), pl.BlockSpec((2 * fh, d), lambda j, i: (j, 0))],
        out_specs=[pl.BlockSpec((tm, 2 * fh), lambda j, i: (i, j)), pl.BlockSpec((tm, fh), lambda j, i: (i, j))],
        out_shape=[jax.ShapeDtypeStruct((t, 4 * fh), BF16), jax.ShapeDtypeStruct((t, 2 * fh), BF16)],
        compiler_params=_cp("parallel", "parallel"))(h2, wgu_t)


def _ffn_down_bwd(dx, w_d, gu, name, tm=512):
    t, d = dx.shape
    tm = min(tm, t)
    fh = FFN_HIDDEN // 2

    def body(a_ref, w_ref, gu_ref, o_ref):
        dact = _dot(a_ref[...].astype(BF16), w_ref[...], NT)
        dg, du = _swiglu_bwd(gu_ref[:, :fh].astype(F32), gu_ref[:, fh:].astype(F32), dact)
        o_ref[:, :fh] = dg.astype(BF16)
        o_ref[:, fh:] = du.astype(BF16)

    wide = pl.BlockSpec((tm, 2 * fh), lambda j, i: (i, j))
    return pl.pallas_call(
        body, name=name, grid=(2, t // tm),
        in_specs=[pl.BlockSpec((tm, d), lambda j, i: (i, 0)), pl.BlockSpec((fh, d), lambda j, i: (j, 0)), wide],
        out_specs=wide,
        out_shape=jax.ShapeDtypeStruct((t, 4 * fh), BF16),
        compiler_params=_cp("parallel", "parallel"))(dx, w_d, gu)


def _rows_bwd(pieces, w, x, g, dres, name, tm=256, ex=None):
    t, d = x.shape
    tm = min(tm, t)
    widths = [p.shape[1] for p in pieces]
    starts = [sum(widths[:i]) for i in range(len(widths))]
    assert sum(widths) == w.shape[0]
    n_p = len(pieces)

    def body(*refs):
        p_refs, (w_ref, x_ref, g_ref, dres_ref, dx_ref, dg_ref) = refs[:n_p], refs[n_p:]
        dh = _dot(p_refs[0][...], w_ref[pl.ds(starts[0], widths[0]), :], NN)
        for i in range(1, n_p):
            dh = dh + _dot(p_refs[i][...], w_ref[pl.ds(starts[i], widths[i]), :], NN)
        _, vjp = jax.vjp(_rms, x_ref[...], g_ref[...])
        dx, dg = vjp(dh)
        dx_ref[...] = dres_ref[...] + dx

        @pl.when(pl.program_id(0) == 0)
        def _():
            dg_ref[...] = jnp.zeros_like(dg_ref)

        dg_ref[...] += dg

    row = pl.BlockSpec((tm, d), lambda i: (i, 0))
    vec = pl.BlockSpec((1, d), lambda i: (0, 0))
    ex_in, ex_in_specs, ex_out, ex_out_specs, ex_scratch = _carried(ex)
    outs = pl.pallas_call(
        _carry(body, ex, n_p + 4, 2, 0, (t // tm,)), name=name, grid=(t // tm,),
        in_specs=[pl.BlockSpec((tm, k), lambda i: (i, 0)) for k in widths]
        + [pl.BlockSpec(w.shape, lambda i: (0, 0)), row, vec, row] + ex_in_specs,
        out_specs=[row, vec] + ex_out_specs,
        out_shape=[jax.ShapeDtypeStruct((t, d), F32), jax.ShapeDtypeStruct((1, d), F32)] + ex_out,
        scratch_shapes=ex_scratch,
        compiler_params=_cp("arbitrary"))(*pieces, w, x, g, dres, *ex_in)
    return (outs[0], outs[1], outs[2:]) if ex else (outs[0], outs[1])


def _merge_fwd(o_g, o_att, proj, x, w_pa, w_pb, w_o, gain, name, tm=512):
    t, d = x.shape
    tm = min(tm, t)

    def body(og_ref, oa_ref, ga0, ga1, gb0, gb1, x_ref, wpa_ref, wpb_ref, wo_ref, g_ref,
             ya_ref, yb_ref, mix_ref, h2_ref, x1_ref):
        ya = _dot(og_ref[...], wpa_ref[...], NN)
        yb = _dot(oa_ref[...], wpb_ref[...], NN)
        ga = jnp.concatenate([ga0[...], ga1[...]], axis=1)
        gb = jnp.concatenate([gb0[...], gb1[...]], axis=1)
        mix = _mix(ya, yb, ga, gb).astype(BF16)
        ya_ref[...] = ya.astype(BF16)
        yb_ref[...] = yb.astype(BF16)
        mix_ref[...] = mix
        x1 = x_ref[...] + _dot(mix, wo_ref[...], NN)
        x1_ref[...] = x1
        h2_ref[...] = _rms(x1, g_ref[...]).astype(BF16)

    row = pl.BlockSpec((tm, d), lambda i: (i, 0))
    mat = pl.BlockSpec((d, d), lambda i: (0, 0))
    return pl.pallas_call(
        body, name=name, grid=(t // tm,),
        in_specs=[row, row] + _gate_specs(tm) + [row, mat, mat, mat, pl.BlockSpec((1, d), lambda i: (0, 0))],
        out_specs=[row] * 5,
        out_shape=[jax.ShapeDtypeStruct((t, d), BF16)] * 4 + [jax.ShapeDtypeStruct((t, d), F32)],
        compiler_params=_cp("parallel"))(o_g, o_att, proj, proj, proj, proj, x, w_pa, w_pb, w_o, gain)


def _loss_head(x, g, target, name, tm=512):
    t, d = x.shape
    tm = min(tm, t)

    def body(x_ref, g_ref, t_ref, dx_ref, dg_ref, loss_ref):
        tgt = t_ref[...]

        def f(xv, gv):
            err = _rms(xv, gv) - tgt
            return 0.5 * jnp.sum(jnp.mean(err * err, axis=-1, keepdims=True))

        loss, vjp = jax.vjp(f, x_ref[...], g_ref[...])
        dx, dg = vjp(jnp.ones((), F32))
        dx_ref[...] = dx

        @pl.when(pl.program_id(0) == 0)
        def _():
            dg_ref[...] = jnp.zeros_like(dg_ref)
            loss_ref[...] = jnp.zeros_like(loss_ref)

        dg_ref[...] += dg
        loss_ref[...] += jnp.full(loss_ref.shape, loss, F32)

    row = pl.BlockSpec((tm, d), lambda i: (i, 0))
    vec = pl.BlockSpec((1, d), lambda i: (0, 0))
    lane = pl.BlockSpec((1, 128), lambda i: (0, 0))
    return pl.pallas_call(
        body, name=name, grid=(t // tm,), in_specs=[row, vec, row], out_specs=[row, vec, lane],
        out_shape=[jax.ShapeDtypeStruct((t, d), F32), jax.ShapeDtypeStruct((1, d), F32),
                   jax.ShapeDtypeStruct((1, 128), F32)],
        compiler_params=_cp("arbitrary"))(x, g, target)


def _lb_rows(l0, l1, l2, l3):
    mx = jnp.maximum(jnp.maximum(l0, l1), jnp.maximum(l2, l3))
    e0, e1, e2, e3 = jnp.exp(l0 - mx), jnp.exp(l1 - mx), jnp.exp(l2 - mx), jnp.exp(l3 - mx)
    s = e0 + e1 + e2 + e3
    p0, p1, p2, p3 = e0 / s, e1 / s, e2 / s, e3 / s
    c1 = p0 + p1
    c2 = c1 + p2
    c3 = c2 + p3
    return p0 - p0, c1 - p0, c2 - p0, c3 - p0


def _lb_fwd(lb_logits):
    def body(l_ref, o_ref):
        rows = _lb_rows(*[l_ref[pl.ds(i, 1), :] for i in range(DEPTH)])
        for i in range(DEPTH):
            o_ref[pl.ds(i, 1), :] = rows[i]

    return pl.pallas_call(body, name="lb_fwd", out_shape=jax.ShapeDtypeStruct(lb_logits.shape, F32))(lb_logits)


def _lb_bwd(lb_logits, dlb):
    def body(l_ref, d_ref, o_ref):
        _, vjp = jax.vjp(_lb_rows, *[l_ref[pl.ds(i, 1), :] for i in range(DEPTH)])
        grads = vjp(tuple(d_ref[pl.ds(i, 1), :] for i in range(DEPTH)))
        for i in range(DEPTH):
            o_ref[pl.ds(i, 1), :] = grads[i]

    return pl.pallas_call(body, name="lb_bwd", out_shape=jax.ShapeDtypeStruct(lb_logits.shape, F32))(lb_logits, dlb)


MESH = pl.DeviceIdType.MESH
ANY = pl.BlockSpec(memory_space=pl.ANY)
N_KINDS = len(SHARD_ROWS)
FFN_HALF = FFN_HIDDEN // 2
KIND_PLACE = ((0, 0), (1, 0), (2, 0), (3, 0), (4, 0), (4, FFN_HALF), (5, 0))
KIND_HALF_SKIP = (0, 0, 0, 0, FFN_HALF, FFN_HALF, 0)
FULL_ROWS = (N_DEV * SHARD_ROWS[0], D_MODEL, D_MODEL, D_MODEL, 2 * N_DEV * SHARD_ROWS[4], N_DEV * SHARD_ROWS[6])


def _kind_rows(ti, dev):
    oi, base = KIND_PLACE[ti]
    start = base + dev * SHARD_ROWS[ti]
    if KIND_HALF_SKIP[ti]:
        start = start + (dev // (N_DEV // 2)) * KIND_HALF_SKIP[ti]
    return oi, pl.ds(start, SHARD_ROWS[ti])


def _position():
    x, y, c = lax.axis_index("x"), lax.axis_index("y"), lax.axis_index("c")
    return x, y, c, 4 * x + 2 * y + c


def _peer(x, y, c, r):
    px = 1 - x if r & 4 else x
    py = 1 - y if r & 2 else y
    pc = 1 - c if r & 1 else c
    return (px, py, pc), 4 * px + 2 * py + pc


class _Exchange(NamedTuple):
    operands: tuple
    out_shape: tuple
    copies: Callable
    n_local: int


EXCHANGE_SCRATCH = (pltpu.SemaphoreType.DMA((N_DEV, N_KINDS)), pltpu.SemaphoreType.DMA((N_DEV, N_KINDS)),
                    pltpu.SemaphoreType.DMA((N_KINDS,)))
ALL_KINDS = tuple(range(N_KINDS))
KINDS_W_IN = (0,)
KINDS_REST = ALL_KINDS[1:]


def _all_pairs(kinds, ends, send_sems, recv_sems):
    x, y, c, me = _position()
    out = []
    for r in range(1, N_DEV):
        peer, pid = _peer(x, y, c, r)
        for ti in kinds:
            src, dst = ends(ti, me, pid)
            out.append(pltpu.make_async_remote_copy(
                src_ref=src, dst_ref=dst, send_sem=send_sems.at[r, ti], recv_sem=recv_sems.at[r, ti],
                device_id=peer, device_id_type=MESH))
    return out


def _gather_exchange(shards, kinds):
    arrays = sorted({KIND_PLACE[ti][0] for ti in kinds})

    def copies(ins, outs, send_sems, recv_sems, local_sems, arrivals):
        src = dict(zip(kinds, ins))

        def window(ti, dev):
            oi, rows = _kind_rows(ti, dev)
            return outs[arrays.index(oi)].at[rows, :]

        if arrivals:
            return _all_pairs(kinds, lambda ti, me, pid: (src[ti], window(ti, pid)), send_sems, recv_sems)
        _, _, _, me = _position()
        local = [pltpu.make_async_copy(src[ti], window(ti, me), local_sems.at[ti]) for ti in kinds]
        return local + _all_pairs(kinds, lambda ti, me, pid: (src[ti], window(ti, me)), send_sems, recv_sems)

    return _Exchange(tuple(shards[ti] for ti in kinds),
                     tuple(jax.ShapeDtypeStruct((FULL_ROWS[oi], D_MODEL), BF16) for oi in arrays), copies, len(kinds))


def _scatter_exchange(grads, kinds):
    arrays = sorted({KIND_PLACE[ti][0] for ti in kinds})
    offsets, total = {}, 0
    for ti in kinds:
        offsets[ti], total = total, total + SHARD_ROWS[ti]

    def copies(ins, outs, send_sems, recv_sems, local_sems, arrivals):
        land = outs[0]

        def piece(ti, dev):
            ii, rows = _kind_rows(ti, dev)
            return ins[arrays.index(ii)].at[rows, :]

        def slot(ti, dev):
            return land.at[dev, pl.ds(offsets[ti], SHARD_ROWS[ti]), :]

        if arrivals:
            return _all_pairs(kinds, lambda ti, me, pid: (piece(ti, me), slot(ti, pid)), send_sems, recv_sems)
        _, _, _, me = _position()
        local = [pltpu.make_async_copy(piece(ti, me), slot(ti, me), local_sems.at[ti]) for ti in kinds]
        return local + _all_pairs(kinds, lambda ti, me, pid: (piece(ti, pid), slot(ti, me)), send_sems, recv_sems)

    return _Exchange(tuple(grads[oi] for oi in arrays), (jax.ShapeDtypeStruct((N_DEV, total, D_MODEL), BF16),),
                     copies, len(kinds))


def _exchange_start(ex, ins, outs, sems):
    for cp in ex.copies(ins, outs, *sems, False):
        cp.start()


def _exchange_finish(ex, ins, outs, sems):
    for cp in ex.copies(ins, outs, *sems, True):
        cp.wait_recv()
    mine = ex.copies(ins, outs, *sems, False)
    for cp in mine[:ex.n_local]:
        cp.wait()
    for cp in mine[ex.n_local:]:
        cp.wait_send()


def _run_exchange(ex, name):
    n_in, n_out = len(ex.operands), len(ex.out_shape)

    def body(*refs):
        ins, outs, sems = refs[:n_in], refs[n_in:n_in + n_out], refs[n_in + n_out:]
        _exchange_start(ex, ins, outs, sems)
        _exchange_finish(ex, ins, outs, sems)

    return pl.pallas_call(body, name=name, in_specs=[ANY] * n_in, out_specs=[ANY] * n_out,
                          out_shape=list(ex.out_shape), scratch_shapes=list(EXCHANGE_SCRATCH))(*ex.operands)


def _gather_two_level(shard, ti, name, x_in, gain, tm=512):
    oi = KIND_PLACE[ti][0]
    t, d = x_in.shape
    tm = min(tm, t)
    n_steps = t // tm

    def body(s_ref, x_ref, g_ref, out_ref, h_ref, send_sems, recv_sems, local_sem):
        x, y, c, _ = _position()
        me, sibling = (x, y, c), (x, y, 1 - c)
        chips = [(1 - x, y), (x, 1 - y), (1 - x, 1 - y)]

        def rows(px, py, pc):
            return out_ref.at[_kind_rows(ti, 4 * px + 2 * py + pc)[1], :]

        def copy(k, block, to, src=None):
            return pltpu.make_async_remote_copy(
                src_ref=rows(*block) if src is None else src, dst_ref=rows(*block),
                send_sem=send_sems.at[k], recv_sem=recv_sems.at[k], device_id=to, device_id_type=MESH)

        def first():
            return [copy(0, me, sibling, src=s_ref)] + [copy(1 + j, me, (*chip, c), src=s_ref) for j, chip in enumerate(chips)]

        @pl.when(pl.program_id(0) == 0)
        def _():
            pltpu.make_async_copy(s_ref, rows(*me), local_sem).start()
            for cp in first():
                cp.start()

        h_ref[...] = _rms(x_ref[...], g_ref[...]).astype(BF16)

        @pl.when(pl.program_id(0) == n_steps - 1)
        def _():
            passed = [copy(4 + j, (*chip, c), sibling) for j, chip in enumerate(chips)]
            for j, chip in enumerate(chips):
                copy(1 + j, (*chip, c), me).wait_recv()
                passed[j].start()
            copy(0, sibling, me).wait_recv()
            for j, chip in enumerate(chips):
                copy(4 + j, (*chip, 1 - c), me).wait_recv()
            for cp in first() + passed:
                cp.wait_send()
            pltpu.make_async_copy(s_ref, rows(*me), local_sem).wait()

    row = pl.BlockSpec((tm, d), lambda i: (i, 0))
    return pl.pallas_call(
        body, name=name, grid=(n_steps,),
        in_specs=[ANY, row, pl.BlockSpec((1, d), lambda i: (0, 0))], out_specs=[ANY, row],
        out_shape=[jax.ShapeDtypeStruct((FULL_ROWS[oi], D_MODEL), BF16), jax.ShapeDtypeStruct((t, d), BF16)],
        scratch_shapes=[pltpu.SemaphoreType.DMA((N_DEV - 1,)), pltpu.SemaphoreType.DMA((N_DEV - 1,)),
                        pltpu.SemaphoreType.DMA],
        compiler_params=_cp("arbitrary"))(shard, x_in, gain)


def _carry(body, ex, n_in, n_out, n_scratch, grid):
    if ex is None:
        return body
    e_in, e_out = len(ex.operands), len(ex.out_shape)

    def at(step):
        hit = pl.program_id(0) == step[0]
        for axis in range(1, len(grid)):
            hit = hit & (pl.program_id(axis) == step[axis])
        return hit

    def carrying(*refs):
        own_in, ex_in = refs[:n_in], refs[n_in:n_in + e_in]
        rest = refs[n_in + e_in:]
        own_out, ex_out = rest[:n_out], rest[n_out:n_out + e_out]
        own_scratch, sems = rest[n_out + e_out:n_out + e_out + n_scratch], rest[n_out + e_out + n_scratch:]

        @pl.when(at([0] * len(grid)))
        def _():
            _exchange_start(ex, ex_in, ex_out, sems)

        body(*own_in, *own_out, *own_scratch)

        @pl.when(at([g - 1 for g in grid]))
        def _():
            _exchange_finish(ex, ex_in, ex_out, sems)

    return carrying


def _carried(ex):
    if ex is None:
        return (), [], [], [], []
    return (ex.operands, [ANY] * len(ex.operands), list(ex.out_shape), [ANY] * len(ex.out_shape),
            list(EXCHANGE_SCRATCH))


def _small_sum_body(p_ref, o_ref, buf, send_sems, recv_sems):
    x, y, c, me = _position()
    buf[me] = p_ref[...]
    sends = []
    for r in range(1, N_DEV):
        peer, _ = _peer(x, y, c, r)
        sends.append(pltpu.make_async_remote_copy(
            src_ref=p_ref, dst_ref=buf.at[me], send_sem=send_sems.at[r], recv_sem=recv_sems.at[r],
            device_id=peer, device_id_type=MESH))
    for cp in sends:
        cp.start()
    for r in range(1, N_DEV):
        peer, pid = _peer(x, y, c, r)
        pltpu.make_async_remote_copy(
            src_ref=p_ref, dst_ref=buf.at[pid], send_sem=send_sems.at[r], recv_sem=recv_sems.at[r],
            device_id=peer, device_id_type=MESH).wait_recv()
    for cp in sends:
        cp.wait_send()
    acc = buf[0]
    for k in range(1, N_DEV):
        acc = acc + buf[k]
    o_ref[...] = acc


def _all_reduce_small(part):
    rows, d = part.shape
    vmem = pl.BlockSpec(memory_space=pltpu.VMEM)
    return pl.pallas_call(
        functools.partial(_small_sum_body), name="all_reduce_small", in_specs=[vmem], out_specs=vmem,
        out_shape=jax.ShapeDtypeStruct((rows, d), F32),
        scratch_shapes=[pltpu.VMEM((N_DEV, rows, d), F32), pltpu.SemaphoreType.DMA((N_DEV,)),
                        pltpu.SemaphoreType.DMA((N_DEV,))],
    )(part)


HG_PAIR = 2 * HG_DK


def _hg_consts():
    c = HG_CHUNK
    r = lax.broadcasted_iota(jnp.int32, (c, c), 0)
    s = lax.broadcasted_iota(jnp.int32, (c, c), 1)
    r2 = lax.broadcasted_iota(jnp.int32, (c, 2 * c), 0)
    s2 = lax.broadcasted_iota(jnp.int32, (c, 2 * c), 1)
    causal2 = jnp.where(s2 >= c, s2 - c, s2) <= r2
    lane_hi = lax.broadcasted_iota(jnp.int32, (c, HG_PAIR), 1) >= HG_DK
    same_head = ((lax.broadcasted_iota(jnp.int32, (HG_PAIR, HG_PAIR), 0) >= HG_DK)
                 == (lax.broadcasted_iota(jnp.int32, (HG_PAIR, HG_PAIR), 1) >= HG_DK))
    return (s <= r).astype(BF16), (s >= r).astype(BF16), causal2, lane_hi, same_head


def _head_rows(x, lane_hi):
    zero = jnp.zeros_like(x)
    return jnp.concatenate([jnp.where(lane_hi, zero, x), jnp.where(lane_hi, x, zero)], axis=0)


def _own_rows(y, lane_hi):
    return jnp.where(lane_hi, y[HG_CHUNK:], y[:HG_CHUNK])


def _split3(x):
    hi = x.astype(BF16)
    r1 = x - hi.astype(F32)
    mid = r1.astype(BF16)
    lo = (r1 - mid.astype(F32)).astype(BF16)
    return jnp.concatenate([hi, mid, lo], axis=1)


def _cumsum_rows(tri, x):
    w = x.shape[1]
    y = _dot(tri, _split3(x), NN)
    return y[:, :w] + y[:, w:2 * w] + y[:, 2 * w:]


def _hg_gates(zq, zf, lb):
    sq = _sigmoid(zq)
    sg = _sigmoid(zf)
    f = lb + (1.0 - lb) * sg
    return sq, zq * sq, sg, f, jnp.log(jnp.maximum(f, MIN_F)), 1.0 - f


def _hg_decays(ball_ref, ci, cols):
    c = HG_CHUNK
    b = ball_ref[pl.ds(ci * c, c), cols]
    mid = ball_ref[pl.ds(ci * c + c // 2 - 1, 1), cols]
    bc = ball_ref[pl.ds(ci * c + c - 1, 1), cols]
    return b, jnp.exp(jnp.minimum(b - mid, HG_EXP_CLAMP)), jnp.exp(jnp.minimum(mid - b, HG_EXP_CLAMP)), bc


def _hg_gate(o, zg, gn):
    return o * lax.rsqrt(jnp.mean(o * o, axis=-1, keepdims=True) + EPS) * gn * (zg * _sigmoid(zg))


def _hgrn2_fwd(proj, lb, gn, name, ex=None):
    t = proj.shape[0]
    bs_tok = min(HG_BLOCK, t)
    n_chunks = bs_tok // HG_CHUNK
    w = HG_HEADS * HG_DK

    def body(hq_ref, hf_ref, hi_ref, hg_ref, lb_ref, gn_ref, o_ref, og_ref, sall_ref, ball_ref, aall_ref, st_ref):
        @pl.when(pl.program_id(0) == 0)
        def _():
            st_ref[...] = jnp.zeros_like(st_ref)

        tril, _, causal2, lane_hi, same_head = _hg_consts()

        for ci in range(n_chunks):
            rows = pl.ds(ci * HG_CHUNK, HG_CHUNK)
            for p in range(HG_HEADS // 2):
                cols = slice(p * HG_PAIR, (p + 1) * HG_PAIR)
                v = hi_ref[rows, cols]
                zg = hg_ref[rows, cols]
                _, q, _, _, logf, k = _hg_gates(hq_ref[rows, cols], hf_ref[rows, cols], lb_ref[:, cols])
                ball_ref[rows, cols] = _cumsum_rows(tril, logf)
                b, em, en, bc = _hg_decays(ball_ref, ci, cols)
                st0 = st_ref[p]
                sall_ref[ci, 2 * p] = st0[:HG_DK, :HG_DK]
                sall_ref[ci, 2 * p + 1] = st0[HG_DK:, HG_DK:]
                vb = v.astype(BF16)
                o = _dot((q * jnp.exp(b)).astype(BF16), st0.astype(BF16), NT)
                a = jnp.where(causal2, _dot((q * em).astype(BF16), _head_rows((k * en).astype(BF16), lane_hi), NT),
                              0.0).astype(BF16)
                aall_ref[ci, p] = a
                o = o + _dot(a, _head_rows(vb, lane_hi), NN)
                kdec = (k * jnp.exp(bc - b)).astype(BF16)
                st_ref[p] = st0 * jnp.exp(bc) + jnp.where(same_head, _dot(vb, kdec, TN), 0.0)
                o_ref[rows, cols] = o
                for hh in range(2):
                    sl = slice(hh * HG_DK, (hh + 1) * HG_DK)
                    hcols = slice(p * HG_PAIR + hh * HG_DK, p * HG_PAIR + (hh + 1) * HG_DK)
                    og_ref[rows, hcols] = _hg_gate(o[:, sl], zg[:, sl], gn_ref[:, hcols]).astype(BF16)

    def col(j):
        return pl.BlockSpec((bs_tok, w), lambda n, j=j: (n, j))

    vec = pl.BlockSpec((1, w), lambda n: (0, 0))
    ex_in, ex_in_specs, ex_out, ex_out_specs, ex_scratch = _carried(ex)
    outs = pl.pallas_call(
        _carry(body, ex, 6, 5, 1, (t // bs_tok,)), name=name, grid=(t // bs_tok,),
        in_specs=[col(COL_HQ // w), col(COL_HF // w), col(COL_HI // w), col(COL_HG // w), vec, vec] + ex_in_specs,
        out_specs=[col(0), col(0),
                   pl.BlockSpec((n_chunks, HG_HEADS, HG_DK, HG_DK), lambda n: (n, 0, 0, 0)), col(0),
                   pl.BlockSpec((n_chunks, HG_HEADS // 2, HG_CHUNK, 2 * HG_CHUNK), lambda n: (n, 0, 0, 0))] + ex_out_specs,
        out_shape=[jax.ShapeDtypeStruct((t, w), F32), jax.ShapeDtypeStruct((t, w), BF16),
                   jax.ShapeDtypeStruct((t // HG_CHUNK, HG_HEADS, HG_DK, HG_DK), F32), jax.ShapeDtypeStruct((t, w), F32),
                   jax.ShapeDtypeStruct((t // HG_CHUNK, HG_HEADS // 2, HG_CHUNK, 2 * HG_CHUNK), BF16)] + ex_out,
        scratch_shapes=[pltpu.VMEM((HG_HEADS // 2, HG_PAIR, HG_PAIR), F32)] + ex_scratch,
        compiler_params=_cp("arbitrary"))(proj, proj, proj, proj, lb, gn, *ex_in)
    return outs[:5], outs[5:]


def _hgrn2_bwd(proj, lb, gn, o_hg, sall, ball, aall, dog, name, ex=None):
    t = proj.shape[0]
    bs_tok = min(HG_BLOCK, t)
    n_chunks = bs_tok // HG_CHUNK
    n_blocks = t // bs_tok
    w = HG_HEADS * HG_DK

    def body(hq_ref, hf_ref, hi_ref, hg_ref, lb_ref, gn_ref, o_ref, sall_ref, snext_ref, ball_ref, aall_ref, dog_ref,
             da_ref, dlb_ref, dgn_ref, dst_ref):
        @pl.when(pl.program_id(0) == 0)
        def _():
            dst_ref[...] = jnp.zeros_like(dst_ref)
            dlb_ref[...] = jnp.zeros_like(dlb_ref)
            dgn_ref[...] = jnp.zeros_like(dgn_ref)

        _, rev_tril, causal2, lane_hi, same_head = _hg_consts()
        zero_block = jnp.zeros((HG_DK, HG_DK), F32)

        for ci in reversed(range(n_chunks)):
            rows = pl.ds(ci * HG_CHUNK, HG_CHUNK)
            for p in range(HG_HEADS // 2):
                cols = slice(p * HG_PAIR, (p + 1) * HG_PAIR)
                zq = hq_ref[rows, cols]
                v = hi_ref[rows, cols]
                zg = hg_ref[rows, cols]
                lbv = lb_ref[:, cols]
                sq, q, sg, f, _, k = _hg_gates(zq, hf_ref[rows, cols], lbv)
                b, em, en, bc = _hg_decays(ball_ref, ci, cols)
                st0 = jnp.concatenate([jnp.concatenate([sall_ref[ci, 2 * p], zero_block], axis=1),
                                       jnp.concatenate([zero_block, sall_ref[ci, 2 * p + 1]], axis=1)], axis=0)
                dst1 = dst_ref[p]
                vb = v.astype(BF16)
                eb = jnp.exp(b)
                qg = (q * eb).astype(BF16)
                qt = (q * em).astype(BF16)
                kref = (k * en).astype(BF16)
                ebcb = jnp.exp(bc - b)
                kdec = (k * ebcb).astype(BF16)
                ebc = jnp.exp(bc)
                dos, dzgs, dgns = [], [], []
                for hh in range(2):
                    sl = slice(hh * HG_DK, (hh + 1) * HG_DK)
                    hcols = slice(p * HG_PAIR + hh * HG_DK, p * HG_PAIR + (hh + 1) * HG_DK)
                    _, gate_vjp = jax.vjp(_hg_gate, o_ref[rows, hcols], zg[:, sl], gn_ref[:, hcols])
                    do_h, dzg_h, dgn_h = gate_vjp(dog_ref[rows, hcols])
                    dos.append(do_h)
                    dzgs.append(dzg_h)
                    dgns.append(dgn_h)
                dob = jnp.concatenate(dos, axis=1).astype(BF16)
                vrows, krows = _head_rows(vb, lane_hi), _head_rows(kref, lane_hi)
                dam = jnp.where(causal2, _dot(dob, vrows, NT), 0.0).astype(BF16)
                dk = ebcb * _dot(vb, dst1.astype(BF16), NN) + en * _own_rows(_dot(dam, qt, TN), lane_hi)
                dq = eb * _dot(dob, st0.astype(BF16), NN) + em * _dot(dam, krows, NN)
                dv = _own_rows(_dot(aall_ref[ci, p], dob, TN), lane_hi) + _dot(kdec, dst1.astype(BF16), NT)
                dst_ref[p] = dst1 * ebc + jnp.where(same_head, _dot(dob, qg, TN), 0.0)

                after = [sall_ref[ci + 1, 2 * p + hh] if ci + 1 < n_chunks else snext_ref[0, 2 * p + hh] for hh in range(2)]
                dbx = jnp.concatenate(
                    [jnp.sum(dst1[hh * HG_DK:(hh + 1) * HG_DK, hh * HG_DK:(hh + 1) * HG_DK] * after[hh], axis=0, keepdims=True)
                     for hh in range(2)], axis=1)
                dlogf = _cumsum_rows(rev_tril, q * dq - k * dk) + dbx
                df = jnp.where(f > MIN_F, dlogf / f, 0.0) - dk
                dzf = df * (1.0 - lbv) * sg * (1.0 - sg)
                dzq = dq * (sq * (1.0 + zq * (1.0 - sq)))
                da_ref[rows, pl.ds(COL_HQ + p * HG_PAIR, HG_PAIR)] = dzq.astype(BF16)
                da_ref[rows, pl.ds(COL_HF + p * HG_PAIR, HG_PAIR)] = dzf.astype(BF16)
                da_ref[rows, pl.ds(COL_HI + p * HG_PAIR, HG_PAIR)] = dv.astype(BF16)
                da_ref[rows, pl.ds(COL_HG + p * HG_PAIR, HG_PAIR)] = jnp.concatenate(dzgs, axis=1).astype(BF16)
                dlb_ref[:, cols] += jnp.sum(df * (1.0 - sg), axis=0, keepdims=True)
                dgn_ref[:, cols] += jnp.concatenate(dgns, axis=1)

    def col(j):
        return pl.BlockSpec((bs_tok, w), lambda n, j=j: (n_blocks - 1 - n, j))

    vec = pl.BlockSpec((1, w), lambda n: (0, 0))
    ex_in, ex_in_specs, ex_out, ex_out_specs, ex_scratch = _carried(ex)
    outs = pl.pallas_call(
        _carry(body, ex, 12, 3, 1, (n_blocks,)), name=name, grid=(n_blocks,),
        in_specs=[col(COL_HQ // w), col(COL_HF // w), col(COL_HI // w), col(COL_HG // w), vec, vec, col(0),
                  pl.BlockSpec((n_chunks, HG_HEADS, HG_DK, HG_DK), lambda n: (n_blocks - 1 - n, 0, 0, 0)),
                  pl.BlockSpec((1, HG_HEADS, HG_DK, HG_DK),
                               lambda n: (jnp.minimum((n_blocks - n) * n_chunks, t // HG_CHUNK - 1), 0, 0, 0)),
                  col(0),
                  pl.BlockSpec((n_chunks, HG_HEADS // 2, HG_CHUNK, 2 * HG_CHUNK), lambda n: (n_blocks - 1 - n, 0, 0, 0)),
                  col(0)] + ex_in_specs,
        out_specs=[pl.BlockSpec((bs_tok, 4 * w), lambda n: (n_blocks - 1 - n, 0)), vec, vec] + ex_out_specs,
        out_shape=[jax.ShapeDtypeStruct((t, 4 * w), BF16), jax.ShapeDtypeStruct((1, w), F32),
                   jax.ShapeDtypeStruct((1, w), F32)] + ex_out,
        scratch_shapes=[pltpu.VMEM((HG_HEADS // 2, HG_PAIR, HG_PAIR), F32)] + ex_scratch,
        compiler_params=_cp("arbitrary"))(proj, proj, proj, proj, lb, gn, o_hg, sall, sall, ball, aall, dog, *ex_in)
    return outs[:3], outs[3:]


def _rope_tables(t):
    half = ROPE_DIM // 2
    inv = ROPE_THETA ** (-jnp.arange(half, dtype=F32) * 2.0 / ROPE_DIM)
    d = jnp.arange(2 * ATT_HEAD_DIM) % ATT_HEAD_DIM
    ang = jnp.arange(t).astype(F32)[:, None] * inv[d % half][None, :]
    cos, sin = jnp.cos(ang), jnp.sin(ang)
    c = jnp.where(d < ROPE_DIM, cos, 1.0)
    su = jnp.where(d < half, -sin, 0.0)
    sd = jnp.where((d >= half) & (d < ROPE_DIM), sin, 0.0)
    return c, su, sd


def _rope(x, tabs):
    c, su, sd = tabs
    n = x.shape[1]
    half = ROPE_DIM // 2
    return x * c + pltpu.roll(x, n - half, 1) * su + pltpu.roll(x, half, 1) * sd


def _rope_t(dy, tabs):
    c, su, sd = tabs
    n = dy.shape[1]
    half = ROPE_DIM // 2
    return dy * c + pltpu.roll(dy * su, half, 1) + pltpu.roll(dy * sd, n - half, 1)


def _swa_specs(n_blocks, clamp):
    blk = ATT_BLOCK

    def cur(n):
        return jnp.minimum(n, n_blocks - 1) if clamp else n

    def prev(n):
        return jnp.maximum(cur(n) - 1, 0)

    q_spec = pl.BlockSpec((blk, 512), lambda m, n: (cur(n), COL_AQ // 512 + m))
    kv = [pl.BlockSpec((blk, 128), lambda m, n, c=c, f=f: (f(n), c + m))
          for c in (COL_AK // 128, COL_AV // 128) for f in (cur, prev)]
    tabs = [pl.BlockSpec((blk, 128), lambda m, n, f=f: (f(n), 0)) for f in (cur, prev) for _ in range(3)]
    return q_spec, kv, tabs, cur, prev


ATT_SCALE = ATT_HEAD_DIM ** -0.5


def _head_halves(x, upper):
    zero = jnp.zeros_like(x)
    return jnp.concatenate([jnp.where(upper, zero, x), jnp.where(upper, x, zero)], axis=0)


def _swa_scores(scores_t, sink, mask_t):
    s = jnp.where(mask_t, scores_t, -jnp.inf)
    mx = jnp.maximum(jnp.max(s, axis=0, keepdims=True), sink)
    p = jnp.exp(s - mx)
    es = jnp.exp(sink - mx)
    rinv = 1.0 / (jnp.sum(p, axis=0, keepdims=True) + es)
    return p * rinv, es * rinv


def _swa_window(kc_ref, kp_ref, vc_ref, vp_ref, tabs_c, tabs_p, n):
    k2 = jnp.concatenate([_rope(kp_ref[...], tabs_p), _rope(kc_ref[...], tabs_c)], axis=0)
    v2 = jnp.concatenate([vp_ref[...], vc_ref[...]], axis=0)
    blk = ATT_BLOCK
    kj = lax.broadcasted_iota(jnp.int32, (2 * blk, blk), 0)
    qi = lax.broadcasted_iota(jnp.int32, (2 * blk, blk), 1)
    delta = qi + blk - kj
    mask_t = (delta >= 0) & (delta < blk) & ((kj >= blk) | (n > 0))
    return k2, v2, mask_t


def _swa_fwd(proj, sinks, tabs, name, ex=None):
    t = proj.shape[0]
    n_blocks = t // ATT_BLOCK
    q_spec, kv_specs, tab_specs, _, _ = _swa_specs(n_blocks, clamp=False)

    def body(q_ref, kc_ref, kp_ref, vc_ref, vp_ref, c0, c1, c2, p0, p1, p2, sink_ref, o_ref):
        m, n = pl.program_id(0), pl.program_id(1)
        tabs_c = (c0[...], c1[...], c2[...])
        tabs_p = (p0[...], p1[...], p2[...])
        k2, v2, mask_t = _swa_window(kc_ref, kp_ref, vc_ref, vp_ref, tabs_c, tabs_p, n)
        k2r, v2r = pltpu.roll(k2, 64, 1), pltpu.roll(v2, 64, 1)
        upper_k = lax.broadcasted_iota(jnp.int32, k2.shape, 1) >= 64
        upper_q = lax.broadcasted_iota(jnp.int32, (ATT_BLOCK, 128), 1) >= 64
        for jj in range(2):
            own = upper_k if jj else ~upper_k
            kd = jnp.where(own, k2, k2r).astype(BF16)
            vd = jnp.where(own, v2, v2r).astype(BF16)
            for pi in range(2):
                cols = slice(256 * jj + 128 * pi, 256 * jj + 128 * pi + 128)
                qp = _rope(q_ref[:, cols], tabs_c) * ATT_SCALE
                outs = []
                for e in range(2):
                    sink = sink_ref[0, 8 * m + 4 * jj + 2 * pi + e]
                    qm = jnp.where(upper_q if e else ~upper_q, qp, 0.0).astype(BF16)
                    pn, _ = _swa_scores(_dot(kd, qm, NT), sink, mask_t)
                    outs.append(_dot(pn.astype(BF16), vd, TN))
                o_ref[:, cols] = jnp.where(upper_q, outs[1], outs[0]).astype(BF16)

    ex_in, ex_in_specs, ex_out, ex_out_specs, ex_scratch = _carried(ex)
    outs = pl.pallas_call(
        _carry(body, ex, 12, 1, 0, (2, n_blocks)), name=name, grid=(2, n_blocks),
        in_specs=[q_spec] + kv_specs + tab_specs + [pl.BlockSpec(memory_space=pltpu.SMEM)] + ex_in_specs,
        out_specs=[pl.BlockSpec((ATT_BLOCK, 512), lambda m, n: (n, m))] + ex_out_specs,
        out_shape=[jax.ShapeDtypeStruct((t, ATT_Q_HEADS * ATT_HEAD_DIM), BF16)] + ex_out,
        scratch_shapes=ex_scratch,
        compiler_params=_cp("arbitrary", "arbitrary"))(proj, proj, proj, proj, proj, *tabs, *tabs, sinks, *ex_in)
    return outs[0], outs[1:]


def _swa_bwd(proj, sinks, tabs, o_att, do_att, name, ex=None):
    t = proj.shape[0]
    n_blocks = t // ATT_BLOCK
    blk = ATT_BLOCK
    q_spec, kv_specs, tab_specs, cur, prev = _swa_specs(n_blocks, clamp=True)

    def body(q_ref, kc_ref, kp_ref, vc_ref, vp_ref, c0, c1, c2, p0, p1, p2, sink_ref, o_ref, do_ref,
             dq_ref, dk_ref, dv_ref, ds_ref, ck_ref, cv_ref):
        m, n = pl.program_id(0), pl.program_id(1)

        @pl.when(n == 0)
        def _():
            ds_ref[...] = jnp.zeros_like(ds_ref)
            ck_ref[...] = jnp.zeros_like(ck_ref)
            cv_ref[...] = jnp.zeros_like(cv_ref)

        @pl.when(n < n_blocks)
        def _():
            tabs_c = (c0[...], c1[...], c2[...])
            tabs_p = (p0[...], p1[...], p2[...])
            k2, v2, mask_t = _swa_window(kc_ref, kp_ref, vc_ref, vp_ref, tabs_c, tabs_p, n)
            k2r, v2r = pltpu.roll(k2, 64, 1), pltpu.roll(v2, 64, 1)
            upper_k = lax.broadcasted_iota(jnp.int32, k2.shape, 1) >= 64
            lane = lax.broadcasted_iota(jnp.int32, (8, 128), 1)
            head_of_row = lax.broadcasted_iota(jnp.int32, (16, 128), 0) >= 8
            head_rows = (head_of_row == (lax.broadcasted_iota(jnp.int32, (16, 128), 1) >= 64)).astype(F32)
            dk2 = jnp.zeros(k2.shape, F32)
            dv2 = jnp.zeros(k2.shape, F32)
            dsv = jnp.zeros((8, 128), F32)
            nk = 2 * blk
            for jj in range(2):
                own = upper_k if jj else ~upper_k
                kh = _head_halves(jnp.where(own, k2, k2r).astype(BF16), upper_k)
                vh = _head_halves(jnp.where(own, v2, v2r).astype(BF16), upper_k)
                dkd = jnp.zeros(k2.shape, F32)
                dvd = jnp.zeros(k2.shape, F32)
                for pi in range(2):
                    cols = slice(256 * jj + 128 * pi, 256 * jj + 128 * pi + 128)
                    qp = (_rope(q_ref[:, cols], tabs_c) * ATT_SCALE).astype(BF16)
                    do_pair = do_ref[:, cols]
                    dob = do_pair.astype(BF16)
                    delta2 = lax.dot_general(head_rows, do_pair * o_ref[:, cols].astype(F32), ((NT), ((), ())),
                                             precision=lax.Precision.HIGHEST, preferred_element_type=F32)
                    st = _dot(kh, qp, NT)
                    dpt = _dot(vh, dob, NT)
                    pns, dss = [], []
                    for e in range(2):
                        hl = 4 * jj + 2 * pi + e
                        pn, ps = _swa_scores(st[e * nk:(e + 1) * nk], sink_ref[0, 8 * m + hl], mask_t)
                        delta = jnp.max(delta2[8 * e:8 * e + 8], axis=0, keepdims=True)
                        pns.append(pn.astype(BF16))
                        dss.append((pn * (dpt[e * nk:(e + 1) * nk] - delta)).astype(BF16))
                        dsv = dsv + jnp.where(lane == hl, -jnp.sum(ps * delta), 0.0)
                    dsb = jnp.concatenate(dss, axis=0)
                    dq_ref[:, cols] = _rope_t(_dot(dsb, kh, TN) * ATT_SCALE, tabs_c).astype(BF16)
                    rk = _dot(dsb, qp, NN)
                    rv = _dot(jnp.concatenate(pns, axis=0), dob, NN)
                    dkd = dkd + jnp.where(upper_k, rk[nk:], rk[:nk])
                    dvd = dvd + jnp.where(upper_k, rv[nk:], rv[:nk])
                dk2 = dk2 + jnp.where(own, dkd + pltpu.roll(dkd, 64, 1), 0.0)
                dv2 = dv2 + jnp.where(own, dvd + pltpu.roll(dvd, 64, 1), 0.0)
            dk_ref[...] = (ck_ref[...] + _rope_t(dk2[:blk], tabs_p)).astype(BF16)
            dv_ref[...] = (cv_ref[...] + dv2[:blk]).astype(BF16)
            ck_ref[...] = _rope_t(dk2[blk:], tabs_c)
            cv_ref[...] = dv2[blk:]
            ds_ref[...] += dsv

        @pl.when(n == n_blocks)
        def _():
            dk_ref[...] = ck_ref[...].astype(BF16)
            dv_ref[...] = cv_ref[...].astype(BF16)

    wide = pl.BlockSpec((blk, 512), lambda m, n: (cur(n), m))
    lagged = pl.BlockSpec((blk, 128), lambda m, n: (jnp.maximum(n - 1, 0), m))
    ex_in, ex_in_specs, ex_out, ex_out_specs, ex_scratch = _carried(ex)
    outs = pl.pallas_call(
        _carry(body, ex, 14, 4, 2, (2, n_blocks + 1)), name=name, grid=(2, n_blocks + 1),
        in_specs=[q_spec] + kv_specs + tab_specs + [pl.BlockSpec(memory_space=pltpu.SMEM), wide, wide] + ex_in_specs,
        out_specs=[wide, lagged, lagged, pl.BlockSpec((None, 8, 128), lambda m, n: (m, 0, 0))] + ex_out_specs,
        out_shape=[jax.ShapeDtypeStruct((t, 1024), BF16), jax.ShapeDtypeStruct((t, 256), BF16),
                   jax.ShapeDtypeStruct((t, 256), BF16), jax.ShapeDtypeStruct((2, 8, 128), F32)] + ex_out,
        scratch_shapes=[pltpu.VMEM((blk, 128), F32), pltpu.VMEM((blk, 128), F32)] + ex_scratch,
        compiler_params=_cp("arbitrary", "arbitrary"))(proj, proj, proj, proj, proj, *tabs, *tabs, sinks, o_att, do_att,
                                                       *ex_in)
    return outs[:4], outs[4:]


def _local_step(x, target, shards, norm1, lb_logits, hg_norm, attn_sinks, norm2, final_norm):
    t = x.shape[0]
    tabs = _rope_tables(t)
    lb_all = _lb_fwd(lb_logits)
    saved = []

    def shards_of(l):
        return {ti: shards[ti][l] for ti in ALL_KINDS}

    win_0, h = _gather_two_level(shards[0][0], 0, "gather_w_in_rms1", x, norm1[0][None, :])
    win_next = (win_0,)
    rest_next = None
    for l in range(DEPTH):
        n1, n2 = norm1[l][None, :], norm2[l][None, :]
        lb, gn, sinks = lb_all[l][None, :], hg_norm[l][None, :], attn_sinks[l][None, :]
        (win_t,) = win_next
        if l == 0:
            proj, rest_next = _matmul_nt(h, win_t, 0, IN_COLS, F32, "proj_fwd", tn=1280,
                                         ex=_gather_exchange(shards_of(0), KINDS_REST))
        else:
            proj = _matmul_nt(h, win_t, 0, IN_COLS, F32, "proj_fwd", tn=1280)
        w_pa, w_pb, w_o, wgu_t, w_d = rest_next
        more = l + 1 < DEPTH
        (o_hg, o_g, sall, ball, aall), rest_next = _hgrn2_fwd(
            proj, lb, gn, "hgrn2_fwd", _gather_exchange(shards_of(l + 1), KINDS_REST) if more else None)
        o_att, win_next = _swa_fwd(
            proj, sinks, tabs, "swa_fwd", _gather_exchange(shards_of(l + 1), KINDS_W_IN) if more else None)
        ya, yb, mix, h2, x1 = _merge_fwd(o_g, o_att, proj, x, w_pa, w_pb, w_o, n2, "merge_fwd")
        gu, act = _ffn_up_fwd(h2, wgu_t, "ffn_up_fwd")
        if more:
            x2, h_next = _matmul_nn(act, w_d, 0, x1, "wd_fwd", gain=norm1[l + 1][None, :])
        else:
            x2, h_next = _matmul_nn(act, w_d, 0, x1, "wd_fwd_last"), None
        saved.append((x, h, proj, o_hg, o_g, (sall, ball, aall), o_att, ya, yb, mix, x1, h2, gu, act, n1, n2, lb, gn, sinks,
                      (win_t, w_pa, w_pb, w_o, wgu_t, w_d)))
        x, h = x2, h_next

    dx, d_fn, loss = _loss_head(x, final_norm[None, :], target, "loss_head")

    owned = [None] * DEPTH
    pending = None
    d_n1, d_n2, d_lb, d_gn, d_sinks = ([None] * DEPTH for _ in range(5))
    for l in reversed(range(DEPTH)):
        x0, h, proj, o_hg, o_g, hg_saved, o_att, ya, yb, mix, x1, h2, gu, act, n1, n2, lb, gn, sinks, weights = saved[l]
        win_t, w_pa, w_pb, w_o, wgu_t, w_d = weights
        dgu = _ffn_down_bwd(dx, w_d, gu, "ffn_down_bwd")
        g_wd = _matmul_tn(act, dx, "wd_grad", tm=1408)
        g_wgu = _matmul_tn(dgu, h2, "wgu_grad", tm=1408)
        late = pending is not None
        dx1, d_n2[l], *land_w = _rows_bwd([dgu], wgu_t, x1, n2, dx, "ffn_up_bwd", tm=512,
                                          ex=_scatter_exchange(pending, KINDS_W_IN) if late else None)
        g_wo = _matmul_tn(mix, dx1, "wo_grad")
        dya, dyb, dgab, dog, doatt = _merge_bwd(dx1, ya, yb, proj, w_pa, w_pb, w_o, "merge_bwd")
        g_wpa = _matmul_tn(o_g, dya, "wpa_grad")
        g_wpb = _matmul_tn(o_att, dyb, "wpb_grad")
        (dhg, d_lb[l], d_gn[l]), land_r = _hgrn2_bwd(proj, lb, gn, o_hg, *hg_saved, dog, "hgrn2_bwd",
                                                     _scatter_exchange(pending, KINDS_REST) if late else None)
        if late:
            owned[l + 1] = jnp.concatenate([_sum_slots(land_w[0][0], "sum_slots_w_in"),
                                            _sum_slots(land_r[0], "sum_slots_rest")], axis=0)
        ex = _scatter_exchange((None, g_wpa, g_wpb, g_wo, g_wgu, g_wd), KINDS_REST) if l == 0 else None
        (daq, dak, dav, d_sinks[l]), land_rest = _swa_bwd(proj, sinks, tabs, o_att, doatt, "swa_bwd", ex)
        g_win = None
        for piece, off, tm, tag in ((dhg, COL_HQ, 512, "hg"), (daq, COL_AQ, 512, "aq"), (dak, COL_AK, 256, "ak"),
                                    (dav, COL_AV, 256, "av"), (dgab, COL_GA, 512, "gates")):
            g_win = _matmul_tn(piece, h, "win_grad_" + tag, tm=tm, rows=IN_COLS, row_off=off, into=g_win)
        if l > 0:
            dx, d_n1[l] = _rows_bwd([dhg, daq, dak, dav, dgab], win_t, x0, n1, dx1, "win_bwd")
        else:
            dx, d_n1[l], land_win = _rows_bwd([dhg, daq, dak, dav, dgab], win_t, x0, n1, dx1, "win_bwd",
                                              ex=_scatter_exchange((g_win,), KINDS_W_IN))
        pending = (g_win, g_wpa, g_wpb, g_wo, g_wgu, g_wd)
    owned[0] = jnp.concatenate([_sum_slots(land_win[0], "sum_slots_w_in"), _sum_slots(land_rest[0], "sum_slots_rest")],
                               axis=0)

    d_sink_rows = [jnp.concatenate([d[0, 0, :8], d[1, 0, :8]]) for d in d_sinks]
    small = (jnp.concatenate(d_n1, axis=0), jnp.concatenate(d_lb, axis=0), jnp.concatenate(d_gn, axis=0),
             jnp.concatenate(d_n2, axis=0), d_fn, jnp.stack(d_sink_rows, axis=0))
    return loss, dx, jnp.stack(owned, axis=0), small


def _sum_slots(land, name, tr=480):
    _, rows, d = land.shape

    def body(l_ref, o_ref):
        acc = l_ref[0].astype(F32)
        for k in range(1, N_DEV):
            acc = acc + l_ref[k].astype(F32)
        o_ref[...] = acc

    return pl.pallas_call(
        body, name=name, grid=(rows // tr,),
        in_specs=[pl.BlockSpec((N_DEV, tr, d), lambda i: (0, i, 0))],
        out_specs=pl.BlockSpec((tr, d), lambda i: (i, 0)),
        out_shape=jax.ShapeDtypeStruct((rows, d), F32),
        compiler_params=_cp("parallel"))(land)


def _adamw(w, g, m, v, name):
    shape = w.shape
    c = shape[-1]
    rows = w.size // c
    tr = rows
    for cand in (512, 352, 128):
        if rows % cand == 0:
            tr = cand
            break
    c1 = 1.0 / (1.0 - ADAM_B1 ** ADAM_STEP)
    c2 = 1.0 / (1.0 - ADAM_B2 ** ADAM_STEP)

    def body(w_ref, g_ref, m_ref, v_ref, d_ref, nm_ref, nv_ref):
        gv = g_ref[...]
        nm = ADAM_B1 * m_ref[...] + (1.0 - ADAM_B1) * gv
        nv = ADAM_B2 * v_ref[...] + (1.0 - ADAM_B2) * (gv * gv)
        d_ref[...] = -ADAM_LR * ((nm * c1) / (jnp.sqrt(nv * c2) + ADAM_EPS) + ADAM_WD * w_ref[...])
        nm_ref[...] = nm
        nv_ref[...] = nv

    spec = pl.BlockSpec((tr, c), lambda i: (i, 0))
    outs = pl.pallas_call(
        body, name=name, grid=(rows // tr,), in_specs=[spec] * 4, out_specs=[spec] * 3,
        out_shape=[jax.ShapeDtypeStruct((rows, c), F32)] * 3,
        compiler_params=_cp("parallel"))(*[a.reshape(rows, c) for a in (w, g, m, v)])
    return tuple(o.reshape(shape) for o in outs)


def kernel(x, norm1, w_in, lb_logits, hg_norm, attn_sinks, w_pa, w_pb, w_o, norm2, w_gate, w_up, w_down, final_norm, loss_target, m_norm1, m_w_in, m_lb_logits, m_hg_norm, m_attn_sinks, m_w_pa, m_w_pb, m_w_o, m_norm2, m_w_gate, m_w_up, m_w_down, m_final_norm, v_norm1, v_w_in, v_lb_logits, v_hg_norm, v_attn_sinks, v_w_pa, v_w_pb, v_w_o, v_norm2, v_w_gate, v_w_up, v_w_down, v_final_norm):
    t = x.shape[1]
    shards = [jnp.swapaxes(w_in, 1, 2).astype(BF16), w_pa.astype(BF16), w_pb.astype(BF16), w_o.astype(BF16),
              jnp.swapaxes(w_gate, 1, 2).astype(BF16), jnp.swapaxes(w_up, 1, 2).astype(BF16), w_down.astype(BF16)]
    loss_lanes, grad_x, owned, small = _local_step(
        x.reshape(t, D_MODEL), loss_target.reshape(t, D_MODEL), shards,
        norm1, lb_logits, hg_norm, attn_sinks, norm2, final_norm)

    def rows_of(ti, transpose):
        g = owned[:, SLOT_OFF[ti]:SLOT_OFF[ti] + SHARD_ROWS[ti], :]
        return jnp.swapaxes(g, 1, 2) if transpose else g

    g_big = {"w_in": rows_of(0, True), "w_pa": rows_of(1, False), "w_pb": rows_of(2, False), "w_o": rows_of(3, False),
             "w_gate": rows_of(4, True), "w_up": rows_of(5, True), "w_down": rows_of(6, False)}

    d_n1, d_lb, d_gn, d_n2, d_fn, d_sinks = small
    pad = jnp.zeros((DEPTH, D_MODEL - ATT_Q_HEADS), F32)
    packed = jnp.concatenate([
        d_n1, d_lb, d_gn, d_n2, d_fn, jnp.concatenate([d_sinks, pad], axis=1),
        jnp.concatenate([loss_lanes, jnp.zeros((1, D_MODEL - 128), F32)], axis=1),
        jnp.zeros((SMALL_ROWS - 22, D_MODEL), F32)], axis=0)
    total = _all_reduce_small(packed)
    loss = total[21, 0]
    g_small = {"norm1": total[0:4], "lb_logits": _lb_bwd(lb_logits, total[4:8]), "hg_norm": total[8:12],
               "norm2": total[12:16], "final_norm": total[16], "attn_sinks": total[17:21, :ATT_Q_HEADS]}

    params = {"norm1": (norm1, m_norm1, v_norm1), "w_in": (w_in, m_w_in, v_w_in),
              "lb_logits": (lb_logits, m_lb_logits, v_lb_logits), "hg_norm": (hg_norm, m_hg_norm, v_hg_norm),
              "attn_sinks": (attn_sinks, m_attn_sinks, v_attn_sinks), "w_pa": (w_pa, m_w_pa, v_w_pa),
              "w_pb": (w_pb, m_w_pb, v_w_pb), "w_o": (w_o, m_w_o, v_w_o), "norm2": (norm2, m_norm2, v_norm2),
              "w_gate": (w_gate, m_w_gate, v_w_gate), "w_up": (w_up, m_w_up, v_w_up),
              "w_down": (w_down, m_w_down, v_w_down), "final_norm": (final_norm, m_final_norm, v_final_norm)}
    order = ["norm1", "w_in", "lb_logits", "hg_norm", "attn_sinks", "w_pa", "w_pb", "w_o", "norm2",
             "w_gate", "w_up", "w_down", "final_norm"]
    grads, deltas, new_m, new_v = [], [], [], []
    for name in order:
        w, m, v = params[name]
        g = (g_big[name] if name in g_big else g_small[name]).reshape(w.shape)
        w2 = w.reshape(1, -1) if w.ndim == 1 else w
        d, nm, nv = _adamw(w2, g.reshape(w2.shape), m.reshape(w2.shape), v.reshape(w2.shape), "adamw_" + name)
        grads.append(g)
        deltas.append(d.reshape(w.shape))
        new_m.append(nm.reshape(w.shape))
        new_v.append(nv.reshape(w.shape))
    return (loss, grad_x.reshape(x.shape), *grads, *deltas, *new_m, *new_v)
```
